```python
import math
import jax
import jax.numpy as jnp
from jax import lax
import numpy as np

D_MODEL = 1024
BATCH = 4
SEQ = 8192
DEPTH = 2

GRID_W = 64
CTX_LEN = 256
EPS = 1e-6

N_BRANCH = 4
W_MIX = D_MODEL // N_BRANCH

GLA_HEADS = 4
GLA_DK = W_MIX // GLA_HEADS
GLA_DV = W_MIX // GLA_HEADS
GLA_RANK = 16
GLA_TAU = 16.0
GLA_CHUNK = 64

S5_GROUP = 16
S5_GROUPS = W_MIX // S5_GROUP
S5_STATE = 64
S5_MAX_RE = -1e-4

HY_BANDS = 16
HY_EMB = 1 + 2 * HY_BANDS
HY_FFN = 64
HY_SHORT = 3
HY_DECAY_SHORT = 0.3
HY_DECAY_LONG = 1.5
HY_TARGET = 1e-2

RG_BLOCKS = 4
RG_BLOCK = W_MIX // RG_BLOCKS
RG_CONV = 4
RG_C = 8.0

D_FF = 2816
N_EXPERTS = 8
TOP_K = 2
D_EXPERT = 3584
N_DENSE = (DEPTH + 1) // 2
N_MOE = DEPTH // 2

IN_SIZES = (GLA_HEADS * GLA_DK, GLA_HEADS * GLA_DK, GLA_HEADS * GLA_DV, GLA_HEADS * GLA_DV,
            2 * GLA_RANK, W_MIX, 3 * W_MIX, W_MIX, W_MIX, N_BRANCH * D_MODEL)
IN_COLS = sum(IN_SIZES)

kernel_name = "hybrid_dit_gla_s5_hyena_rglru_moe"


def rmsnorm(x, g):
    x32 = x.astype(jnp.float32)
    y = x32 * lax.rsqrt(jnp.mean(x32 * x32, axis=-1, keepdims=True) + EPS)
    return (y * g.astype(jnp.float32)).astype(x.dtype)


def modulate(h, shift, scale):
    return (h * (1 + scale) + shift).astype(h.dtype)


def adaln(cond, w, b):
    return jax.nn.silu(cond) @ w + b


def split_cols(p):
    out, start = [], 0
    for n in IN_SIZES:
        out.append(p[..., start:start + n])
        start += n
    return out


def grid_pos_embed(n_tokens, dim):
    rows = n_tokens // GRID_W
    q = dim // 4
    omega = 1.0 / (10000.0 ** (jnp.arange(q, dtype=jnp.float32) / q))
    r = jnp.arange(rows, dtype=jnp.float32)[:, None] * omega
    cc = jnp.arange(GRID_W, dtype=jnp.float32)[:, None] * omega
    er = jnp.concatenate([jnp.sin(r), jnp.cos(r)], axis=-1)
    ec = jnp.concatenate([jnp.sin(cc), jnp.cos(cc)], axis=-1)
    emb = jnp.concatenate([jnp.broadcast_to(er[:, None], (rows, GRID_W, dim // 2)),
                           jnp.broadcast_to(ec[None], (rows, GRID_W, dim // 2))], axis=-1)
    return emb.reshape(rows * GRID_W, dim)


def dwconv(x, w, pad_l, pad_r):
    n = x.shape[1]
    xp = jnp.pad(x, ((0, 0), (pad_l, pad_r), (0, 0)))
    return sum(xp[:, k:k + n] * w[k] for k in range(w.shape[0]))


def to_heads(t, n):
    b, n_tok, w = t.shape
    return t.reshape(b, n_tok, n, w // n).transpose(0, 2, 1, 3)


def _lin_combine(e1, e2):
    a1, b1 = e1
    a2, b2 = e2
    return a1 * a2, a2 * b1 + b2


def linear_scan(a, b, h0):
    b = b.at[:, 0].add(a[:, 0] * h0)
    _, h = lax.associative_scan(_lin_combine, (a, b), axis=1)
    return h


def gla_chunked(q, k, v, la, s0):
    b_, h_, n_tok, dk = q.shape
    dv = v.shape[-1]
    n_ch = n_tok // GLA_CHUNK
    q = q.reshape(b_, h_, n_ch, GLA_CHUNK, dk)
    k = k.reshape(b_, h_, n_ch, GLA_CHUNK, dk)
    v = v.reshape(b_, h_, n_ch, GLA_CHUNK, dv)
    cum = jnp.cumsum(la.reshape(b_, h_, n_ch, GLA_CHUNK, dk), axis=3)
    cum_last = cum[:, :, :, -1:, :]
    q_in = q * jnp.exp(cum)
    k_in = k * jnp.exp(-cum)
    k_out = k * jnp.exp(cum_last - cum)
    mask = jnp.tril(jnp.ones((GLA_CHUNK, GLA_CHUNK), dtype=bool))
    att = jnp.where(mask, jnp.einsum('bhnid,bhnjd->bhnij', q_in, k_in), 0.0)
    o_intra = jnp.einsum('bhnij,bhnjv->bhniv', att, v)
    d_state = jnp.einsum('bhnjd,bhnjv->bhndv', k_out, v)
    decay = jnp.exp(cum_last[:, :, :, 0, :])

    def step(s, inp):
        dec, ds = inp
        return dec[..., None] * s + ds, s

    s_fin, s_in = lax.scan(step, s0, (jnp.moveaxis(decay, 2, 0), jnp.moveaxis(d_state, 2, 0)))
    s_in = jnp.moveaxis(s_in, 0, 2)
    o_inter = jnp.einsum('bhnid,bhndv->bhniv', q_in, s_in)
    return (o_intra + o_inter).reshape(b_, h_, n_tok, dv), s_fin


def gla_mixer(q, k, v, og, gdn, w_up, b_up, s0):
    f32 = jnp.float32
    b_, n_tok, _ = q.shape
    qh = to_heads(q.astype(f32), GLA_HEADS) * (GLA_DK ** -0.5)
    kh = to_heads(k.astype(f32), GLA_HEADS)
    vh = to_heads(v.astype(f32), GLA_HEADS)
    g_dirs = jnp.split(gdn.astype(f32), 2, axis=-1)
    o = 0.0
    finals = []
    for d in range(2):
        la = jax.nn.log_sigmoid(g_dirs[d] @ w_up[d].astype(f32) + b_up[d].astype(f32)) / GLA_TAU
        la = to_heads(la, GLA_HEADS)
        if d == 0:
            od, sd = gla_chunked(qh, kh, vh, la, s0[d])
        else:
            od, sd = gla_chunked(*(jnp.flip(t, 2) for t in (qh, kh, vh, la)), s0[d])
            od = jnp.flip(od, 2)
        o = o + od
        finals.append(sd)
    o = o * lax.rsqrt(jnp.mean(o * o, axis=-1, keepdims=True) + EPS)
    o = o.transpose(0, 2, 1, 3).reshape(b_, n_tok, GLA_HEADS * GLA_DV)
    out = o * jax.nn.silu(og.astype(f32))
    return out.astype(q.dtype), jnp.stack(finals)


def s5_mixer(u, lam_re, lam_im, log_dt, b_re, b_im, c_re, c_im, d_skip, w_glu, b_glu, s0):
    f32 = jnp.float32
    b_, n_tok, _ = u.shape
    ug = u.astype(f32).reshape(b_, n_tok, S5_GROUPS, S5_GROUP)
    y = ug * d_skip.astype(f32)
    finals = []
    for d in range(2):
        lam = lax.complex(jnp.minimum(lam_re[d].astype(f32), S5_MAX_RE), lam_im[d].astype(f32))
        dt = jnp.exp(log_dt[d].astype(f32))[:, None]
        lbar = jnp.exp(lam * dt)
        bbar = ((lbar - 1.0) / lam)[..., None] * lax.complex(b_re[d].astype(f32), b_im[d].astype(f32))
        cmat = lax.complex(c_re[d].astype(f32), c_im[d].astype(f32))
        src = ug if d == 0 else jnp.flip(ug, 1)
        bu = jnp.einsum('blgh,gph->blgp', src.astype(jnp.complex64), bbar)
        a = jnp.broadcast_to(lbar[None, None], (1, n_tok) + lbar.shape)
        h = linear_scan(a, bu, s0[d])
        finals.append(h[:, -1])
        yd = jnp.einsum('blgp,ghp->blgh', h, cmat).real
        y = y + (yd if d == 0 else jnp.flip(yd, 1))
    y = y.reshape(b_, n_tok, W_MIX)
    g = jax.nn.gelu(y)
    out = g * jax.nn.sigmoid(g @ w_glu.astype(f32) + b_glu.astype(f32))
    return out.astype(u.dtype), jnp.stack(finals)


def hyena_filters(n_tok, w1, b1, w2, b2, w3, freq):
    f32 = jnp.float32
    t = jnp.arange(n_tok, dtype=f32)[:, None]
    bands = jnp.linspace(1e-4, HY_BANDS - 1, HY_BANDS, dtype=f32)[None]
    ang = 2.0 * math.pi * bands * t / n_tok
    z = jnp.concatenate([t / n_tok, jnp.cos(ang), jnp.sin(ang)], axis=-1)
    fr = freq.astype(f32)
    h = jnp.sin(fr * (z @ w1.astype(f32) + b1.astype(f32)))
    h = jnp.sin(fr * (h @ w2.astype(f32) + b2.astype(f32)))
    h = h @ w3.astype(f32)
    t01 = t / max(n_tok - 1, 1)
    deltas = jnp.abs(jnp.linspace(math.log(HY_TARGET) / HY_DECAY_SHORT,
                                  math.log(HY_TARGET) / HY_DECAY_LONG, W_MIX, dtype=f32))
    h = h * jnp.exp(-t01 * jnp.tile(deltas, 2))
    return h / (jnp.sum(jnp.abs(h), axis=0, keepdims=True) + EPS)


def hyena_mixer(p, w_short, w1, b1, w2, b2, w3, freq, bias):
    f32 = jnp.float32
    n_tok = p.shape[1]
    pc = dwconv(p.astype(f32), w_short.astype(f32), 1, 1)
    v, x0, x1 = jnp.split(pc, 3, axis=-1)
    hf, hb = jnp.split(hyena_filters(n_tok, w1, b1, w2, b2, w3, freq), 2, axis=-1)
    n_fft = 2 * n_tok
    h_freq = jnp.fft.rfft(hf, n=n_fft, axis=0) + jnp.conj(jnp.fft.rfft(hb, n=n_fft, axis=0))
    z = x1 * v
    conv = jnp.fft.irfft(jnp.fft.rfft(z, n=n_fft, axis=1) * h_freq, n=n_fft, axis=1)[:, :n_tok]
    y = x0 * (conv + z * bias.astype(f32))
    return y.astype(p.dtype)


def rglru_mixer(xr, gate, w_conv, w_a, b_a, w_x, b_x, lam, s0):
    f32 = jnp.float32
    b_, n_tok, _ = xr.shape
    xc = dwconv(xr.astype(f32), w_conv.astype(f32), 2, 1)
    xb = xc.reshape(b_, n_tok, RG_BLOCKS, RG_BLOCK)
    y = 0.0
    finals = []
    for d in range(2):
        src = xb if d == 0 else jnp.flip(xb, 1)
        r = jax.nn.sigmoid(jnp.einsum('blhi,hij->blhj', src, w_a[d].astype(f32)) + b_a[d].astype(f32))
        i = jax.nn.sigmoid(jnp.einsum('blhi,hij->blhj', src, w_x[d].astype(f32)) + b_x[d].astype(f32))
        log_a = -RG_C * r * jax.nn.softplus(-lam[d].astype(f32).reshape(RG_BLOCKS, RG_BLOCK))
        a = jnp.exp(log_a)
        bt = jnp.sqrt(-jnp.expm1(2.0 * log_a)) * (i * src)
        h = linear_scan(a, bt, s0[d])
        finals.append(h[:, -1])
        y = y + (h if d == 0 else jnp.flip(h, 1))
    y = y.reshape(b_, n_tok, W_MIX) * jax.nn.gelu(gate.astype(f32))
    return y.astype(xr.dtype), jnp.stack(finals)


def merge_branches(branches, gates, w_branch, w_out):
    gk = jnp.split(gates, N_BRANCH, axis=-1)
    y = 0.0
    for k in range(N_BRANCH):
        y = y + jax.nn.sigmoid(gk[k]) * (branches[k] @ w_branch[k])
    return y @ w_out


def swiglu(h, w_gu, w_down):
    g, u = jnp.split(h @ w_gu, 2, axis=-1)
    return (jax.nn.silu(g) * u) @ w_down


def moe_swiglu(h, router, router_b, w_gu, w_down):
    shp = h.shape
    ht = h.reshape(-1, shp[-1])
    logits = ht.astype(jnp.float32) @ router.astype(jnp.float32) + router_b.astype(jnp.float32)
    top_v, top_i = lax.top_k(logits, TOP_K)
    w = jax.nn.softmax(top_v, axis=-1)
    gates = jnp.sum(jax.nn.one_hot(top_i, N_EXPERTS, dtype=jnp.float32) * w[..., None], axis=1)
    y = jnp.zeros_like(ht)
    for e in range(N_EXPERTS):
        y = y + gates[:, e:e + 1].astype(ht.dtype) * swiglu(ht, w_gu[e], w_down[e])
    return y.reshape(shp)


def channel_mix(h, layer, ffn_w_gu, ffn_w_down, moe_router, moe_router_b, moe_w_gu, moe_w_down):
    j = layer // 2
    if layer % 2 == 0:
        return swiglu(h, ffn_w_gu[j], ffn_w_down[j])
    return moe_swiglu(h, moe_router[j], moe_router_b[j], moe_w_gu[j], moe_w_down[j])


def setup_inputs(seed: int = 0) -> dict:
    key = jax.random.key(seed)
    ks = iter(jax.random.split(key, 64))
    f32 = jnp.float32

    def nrm(shape, scale):
        return jax.random.normal(next(ks), shape, f32) * scale

    def uni(shape, lo, hi):
        return jax.random.uniform(next(ks), shape, f32, lo, hi)

    l2 = (DEPTH, 2)
    x = nrm((BATCH, SEQ, D_MODEL), 1.0)
    c = nrm((BATCH, D_MODEL), 1.0)
    ctx = nrm((BATCH, CTX_LEN, D_MODEL), 1.0)
    c_ctx = nrm((D_MODEL,), 1.0)
    mod_w = nrm((DEPTH, D_MODEL, 6 * D_MODEL), 0.5 * D_MODEL ** -0.5)
    mod_b = nrm((DEPTH, 6 * D_MODEL), 0.02)
    norm1_g = 1.0 + nrm((DEPTH, D_MODEL), 0.02)
    norm2_g = 1.0 + nrm((DEPTH, D_MODEL), 0.02)
    w_in = nrm((DEPTH, D_MODEL, IN_COLS), D_MODEL ** -0.5)
    gla_w_up = nrm(l2 + (GLA_RANK, W_MIX), GLA_RANK ** -0.5)
    gla_b_up = nrm(l2 + (W_MIX,), 0.1)
    n_idx = jnp.arange(S5_STATE, dtype=f32)
    s5_lam_re = -0.5 + nrm(l2 + (S5_GROUPS, S5_STATE), 0.01)
    s5_lam_im = math.pi * n_idx + nrm(l2 + (S5_GROUPS, S5_STATE), 0.01)
    s5_log_dt = uni(l2 + (S5_GROUPS,), math.log(1e-3), math.log(1e-1))
    s5_b_re = nrm(l2 + (S5_GROUPS, S5_STATE, S5_GROUP), (2 * S5_GROUP) ** -0.5)
    s5_b_im = nrm(l2 + (S5_GROUPS, S5_STATE, S5_GROUP), (2 * S5_GROUP) ** -0.5)
    s5_c_re = nrm(l2 + (S5_GROUPS, S5_GROUP, S5_STATE), (2 * S5_STATE) ** -0.5)
    s5_c_im = nrm(l2 + (S5_GROUPS, S5_GROUP, S5_STATE), (2 * S5_STATE) ** -0.5)
    s5_d = nrm((DEPTH, S5_GROUPS, S5_GROUP), 0.5)
    s5_w_glu = nrm((DEPTH, W_MIX, W_MIX), W_MIX ** -0.5)
    s5_b_glu = nrm((DEPTH, W_MIX), 0.02)
    hy_w_short = nrm((DEPTH, HY_SHORT, 3 * W_MIX), HY_SHORT ** -0.5)
    hy_w1 = nrm((DEPTH, HY_EMB, HY_FFN), HY_EMB ** -0.5)
    hy_b1 = nrm((DEPTH, HY_FFN), 0.1)
    hy_w2 = nrm((DEPTH, HY_FFN, HY_FFN), HY_FFN ** -0.5)
    hy_b2 = nrm((DEPTH, HY_FFN), 0.1)
    hy_w3 = nrm((DEPTH, HY_FFN, 2 * W_MIX), HY_FFN ** -0.5)
    hy_freq = 1.0 + nrm((DEPTH, HY_FFN), 0.05)
    hy_bias = nrm((DEPTH, W_MIX), 0.1)
    rg_w_conv = nrm((DEPTH, RG_CONV, W_MIX), RG_CONV ** -0.5)
    rg_w_a = nrm(l2 + (RG_BLOCKS, RG_BLOCK, RG_BLOCK), RG_BLOCK ** -0.5)
    rg_b_a = nrm(l2 + (RG_BLOCKS, RG_BLOCK), 0.1)
    rg_w_x = nrm(l2 + (RG_BLOCKS, RG_BLOCK, RG_BLOCK), RG_BLOCK ** -0.5)
    rg_b_x = nrm(l2 + (RG_BLOCKS, RG_BLOCK), 0.1)
    a0 = uni(l2 + (W_MIX,), 0.9, 0.999)
    s_root = a0 ** (1.0 / RG_C)
    rg_lam = jnp.log(s_root) - jnp.log1p(-s_root)
    w_branch = nrm((DEPTH, N_BRANCH, W_MIX, D_MODEL), W_MIX ** -0.5)
    w_out = nrm((DEPTH, D_MODEL, D_MODEL), D_MODEL ** -0.5)
    ffn_w_gu = nrm((N_DENSE, D_MODEL, 2 * D_FF), D_MODEL ** -0.5)
    ffn_w_down = nrm((N_DENSE, D_FF, D_MODEL), D_FF ** -0.5)
    moe_router = nrm((N_MOE, D_MODEL, N_EXPERTS), D_MODEL ** -0.5)
    moe_router_b = nrm((N_MOE, N_EXPERTS), 0.01)
    moe_w_gu = nrm((N_MOE, N_EXPERTS, D_MODEL, 2 * D_EXPERT), D_MODEL ** -0.5)
    moe_w_down = nrm((N_MOE, N_EXPERTS, D_EXPERT, D_MODEL), D_EXPERT ** -0.5)
    final_g = 1.0 + nrm((D_MODEL,), 0.02)
    return {"x": x, "c": c, "ctx": ctx, "c_ctx": c_ctx, "mod_w": mod_w, "mod_b": mod_b,
            "norm1_g": norm1_g, "norm2_g": norm2_g, "w_in": w_in,
            "gla_w_up": gla_w_up, "gla_b_up": gla_b_up,
            "s5_lam_re": s5_lam_re, "s5_lam_im": s5_lam_im, "s5_log_dt": s5_log_dt,
            "s5_b_re": s5_b_re, "s5_b_im": s5_b_im, "s5_c_re": s5_c_re, "s5_c_im": s5_c_im,
            "s5_d": s5_d, "s5_w_glu": s5_w_glu, "s5_b_glu": s5_b_glu,
            "hy_w_short": hy_w_short, "hy_w1": hy_w1, "hy_b1": hy_b1, "hy_w2": hy_w2,
            "hy_b2": hy_b2, "hy_w3": hy_w3, "hy_freq": hy_freq, "hy_bias": hy_bias,
            "rg_w_conv": rg_w_conv, "rg_w_a": rg_w_a, "rg_b_a": rg_b_a, "rg_w_x": rg_w_x,
            "rg_b_x": rg_b_x, "rg_lam": rg_lam, "w_branch": w_branch, "w_out": w_out,
            "ffn_w_gu": ffn_w_gu, "ffn_w_down": ffn_w_down, "moe_router": moe_router,
            "moe_router_b": moe_router_b, "moe_w_gu": moe_w_gu, "moe_w_down": moe_w_down,
            "final_g": final_g}


def reference(x, c, ctx, c_ctx, mod_w, mod_b, norm1_g, norm2_g, w_in, gla_w_up, gla_b_up,
              s5_lam_re, s5_lam_im, s5_log_dt, s5_b_re, s5_b_im, s5_c_re, s5_c_im, s5_d,
              s5_w_glu, s5_b_glu, hy_w_short, hy_w1, hy_b1, hy_w2, hy_b2, hy_w3, hy_freq,
              hy_bias, rg_w_conv, rg_w_a, rg_b_a, rg_w_x, rg_b_x, rg_lam, w_branch, w_out,
              ffn_w_gu, ffn_w_down, moe_router, moe_router_b, moe_w_gu, moe_w_down, final_g):
    f32 = jnp.float32
    n_b, n_lat, _ = x.shape
    x = x + grid_pos_embed(n_lat, D_MODEL).astype(x.dtype)[None]
    y_ctx = ctx
    for l in range(DEPTH):
        last = l == DEPTH - 1
        m_lat = jnp.split(adaln(c, mod_w[l], mod_b[l])[:, None, :], 6, axis=-1)
        m_ctx = jnp.split(adaln(c_ctx, mod_w[l], mod_b[l])[None, None, :], 6, axis=-1)

        h_lat = modulate(rmsnorm(x, norm1_g[l]), m_lat[0], m_lat[1])
        h_ctx = modulate(rmsnorm(y_ctx, norm1_g[l]), m_ctx[0], m_ctx[1])
        p_lat = split_cols(h_lat @ w_in[l])
        p_ctx = split_cols(h_ctx @ w_in[l])
        gla_p = (gla_w_up[l], gla_b_up[l])
        s5_p = (s5_lam_re[l], s5_lam_im[l], s5_log_dt[l], s5_b_re[l], s5_b_im[l],
                s5_c_re[l], s5_c_im[l], s5_d[l], s5_w_glu[l], s5_b_glu[l])
        hy_p = (hy_w_short[l], hy_w1[l], hy_b1[l], hy_w2[l], hy_b2[l], hy_w3[l], hy_freq[l], hy_bias[l])
        rg_p = (rg_w_conv[l], rg_w_a[l], rg_b_a[l], rg_w_x[l], rg_b_x[l], rg_lam[l])

        gla_c, gla_state = gla_mixer(*p_ctx[0:5], *gla_p,
                                     jnp.zeros((2, n_b, GLA_HEADS, GLA_DK, GLA_DV), f32))
        gla_l, _ = gla_mixer(*p_lat[0:5], *gla_p, gla_state)
        s5_c, s5_state = s5_mixer(p_ctx[5], *s5_p,
                                  jnp.zeros((2, n_b, S5_GROUPS, S5_STATE), jnp.complex64))
        s5_l, _ = s5_mixer(p_lat[5], *s5_p, s5_state)
        rg_c, rg_state = rglru_mixer(p_ctx[7], p_ctx[8], *rg_p,
                                     jnp.zeros((2, n_b, RG_BLOCKS, RG_BLOCK), f32))
        rg_l, _ = rglru_mixer(p_lat[7], p_lat[8], *rg_p, rg_state)
        hy_l = hyena_mixer(p_lat[6], *hy_p)

        x = x + m_lat[2] * merge_branches((gla_l, s5_l, hy_l, rg_l), p_lat[9], w_branch[l], w_out[l])
        if not last:
            hy_c = hyena_mixer(p_ctx[6], *hy_p)
            y_ctx = y_ctx + m_ctx[2] * merge_branches((gla_c, s5_c, hy_c, rg_c), p_ctx[9],
                                                      w_branch[l], w_out[l])

        h2_lat = modulate(rmsnorm(x, norm2_g[l]), m_lat[3], m_lat[4])
        x = x + m_lat[5] * channel_mix(h2_lat, l, ffn_w_gu, ffn_w_down, moe_router,
                                       moe_router_b, moe_w_gu, moe_w_down)
        if not last:
            h2_ctx = modulate(rmsnorm(y_ctx, norm2_g[l]), m_ctx[3], m_ctx[4])
            y_ctx = y_ctx + m_ctx[5] * channel_mix(h2_ctx, l, ffn_w_gu, ffn_w_down, moe_router,
                                                   moe_router_b, moe_w_gu, moe_w_down)
    return rmsnorm(x, final_g)
```

```python
import functools
import math

import jax
import jax.numpy as jnp
from jax import lax
from jax.experimental import pallas as pl
from jax.experimental.pallas import tpu as pltpu

D_MODEL = 1024
DEPTH = 2
GRID_W = 64
EPS = 1e-6
N_BRANCH = 4
W_MIX = D_MODEL // N_BRANCH
GLA_HEADS = 4
GLA_DK = W_MIX // GLA_HEADS
GLA_DV = W_MIX // GLA_HEADS
GLA_RANK = 16
GLA_TAU = 16.0
GLA_CHUNK = 64
S5_GROUP = 16
S5_GROUPS = W_MIX // S5_GROUP
S5_STATE = 64
S5_MAX_RE = -1e-4
HY_BANDS = 16
HY_DECAY_SHORT = 0.3
HY_DECAY_LONG = 1.5
HY_TARGET = 1e-2
RG_BLOCKS = 4
RG_BLOCK = W_MIX // RG_BLOCKS
RG_C = 8.0
N_EXPERTS = 8
TOP_K = 2
IN_SIZES = (GLA_HEADS * GLA_DK, GLA_HEADS * GLA_DK, GLA_HEADS * GLA_DV, GLA_HEADS * GLA_DV,
            2 * GLA_RANK, W_MIX, 3 * W_MIX, W_MIX, W_MIX, N_BRANCH * D_MODEL)

VMEM_LIMIT_BYTES = 48 * 1024 * 1024


def _mm_kernel(x_ref, w_ref, o_ref):
    o_ref[...] = jnp.dot(x_ref[...].astype(jnp.bfloat16), w_ref[...],
                         preferred_element_type=jnp.float32)


def _pick_tile(n, cap):
    best = None
    for t in range(128, cap + 1, 128):
        if n % t == 0:
            best = t
    return best if best is not None else n


def pmm(x, w):
    lead = x.shape[:-1]
    k = x.shape[-1]
    n = w.shape[-1]
    x2 = x.reshape(-1, k)
    m = x2.shape[0]
    tm = 512 if m % 512 == 0 else m
    if k > 2048 and m % 256 == 0:
        tm = 256
    tn = n if k * n * 2 <= 6 * 1024 * 1024 else _pick_tile(n, 1024)
    out = pl.pallas_call(
        _mm_kernel,
        grid=(m // tm, n // tn),
        in_specs=[pl.BlockSpec((tm, k), lambda i, j: (i, 0)),
                  pl.BlockSpec((k, tn), lambda i, j: (0, j))],
        out_specs=pl.BlockSpec((tm, tn), lambda i, j: (i, j)),
        out_shape=jax.ShapeDtypeStruct((m, n), jnp.float32),
        compiler_params=pltpu.CompilerParams(
            dimension_semantics=("arbitrary", "arbitrary"),
            vmem_limit_bytes=VMEM_LIMIT_BYTES),
    )(x2, w.astype(jnp.bfloat16))
    return out.reshape(lead + (n,))


def rmsnorm(x, g):
    y = x * lax.rsqrt(jnp.mean(x * x, axis=-1, keepdims=True) + EPS)
    return y * g


def modulate(h, shift, scale):
    return h * (1 + scale) + shift


def adaln(cond, w, b):
    return jax.nn.silu(cond) @ w + b


def split_cols(p):
    out, start = [], 0
    for n in IN_SIZES:
        out.append(p[..., start:start + n])
        start += n
    return out


def grid_pos_embed(n_tokens, dim):
    rows = n_tokens // GRID_W
    q = dim // 4
    omega = 1.0 / (10000.0 ** (jnp.arange(q, dtype=jnp.float32) / q))
    r = jnp.arange(rows, dtype=jnp.float32)[:, None] * omega
    cc = jnp.arange(GRID_W, dtype=jnp.float32)[:, None] * omega
    er = jnp.concatenate([jnp.sin(r), jnp.cos(r)], axis=-1)
    ec = jnp.concatenate([jnp.sin(cc), jnp.cos(cc)], axis=-1)
    emb = jnp.concatenate([jnp.broadcast_to(er[:, None], (rows, GRID_W, dim // 2)),
                           jnp.broadcast_to(ec[None], (rows, GRID_W, dim // 2))], axis=-1)
    return emb.reshape(rows * GRID_W, dim)


def dwconv(x, w, pad_l, pad_r):
    n = x.shape[1]
    xp = jnp.pad(x, ((0, 0), (pad_l, pad_r), (0, 0)))
    return sum(xp[:, k:k + n] * w[k] for k in range(w.shape[0]))


def to_heads(t, n):
    b, n_tok, w = t.shape
    return t.reshape(b, n_tok, n, w // n).transpose(0, 2, 1, 3)


def _lin_combine(e1, e2):
    a1, b1 = e1
    a2, b2 = e2
    return a1 * a2, a2 * b1 + b2


def linear_scan(a, b, h0):
    b = b.at[:, 0].add(a[:, 0] * h0)
    _, h = lax.associative_scan(_lin_combine, (a, b), axis=1)
    return h


def gla_chunked(q, k, v, la, s0):
    b_, h_, n_tok, dk = q.shape
    dv = v.shape[-1]
    n_ch = n_tok // GLA_CHUNK
    q = q.reshape(b_, h_, n_ch, GLA_CHUNK, dk)
    k = k.reshape(b_, h_, n_ch, GLA_CHUNK, dk)
    v = v.reshape(b_, h_, n_ch, GLA_CHUNK, dv)
    cum = jnp.cumsum(la.reshape(b_, h_, n_ch, GLA_CHUNK, dk), axis=3)
    cum_last = cum[:, :, :, -1:, :]
    q_in = q * jnp.exp(cum)
    k_in = k * jnp.exp(-cum)
    k_out = k * jnp.exp(cum_last - cum)
    mask = jnp.tril(jnp.ones((GLA_CHUNK, GLA_CHUNK), dtype=bool))
    att = jnp.where(mask, jnp.einsum('bhnid,bhnjd->bhnij', q_in, k_in), 0.0)
    o_intra = jnp.einsum('bhnij,bhnjv->bhniv', att, v)
    d_state = jnp.einsum('bhnjd,bhnjv->bhndv', k_out, v)
    decay = jnp.exp(cum_last[:, :, :, 0, :])

    def step(s, inp):
        dec, ds = inp
        return dec[..., None] * s + ds, s

    s_fin, s_in = lax.scan(step, s0, (jnp.moveaxis(decay, 2, 0), jnp.moveaxis(d_state, 2, 0)))
    s_in = jnp.moveaxis(s_in, 0, 2)
    o_inter = jnp.einsum('bhnid,bhndv->bhniv', q_in, s_in)
    return (o_intra + o_inter).reshape(b_, h_, n_tok, dv), s_fin


def gla_mixer(q, k, v, og, gdn, w_up, b_up, s0):
    b_, n_tok, _ = q.shape
    qh = to_heads(q, GLA_HEADS) * (GLA_DK ** -0.5)
    kh = to_heads(k, GLA_HEADS)
    vh = to_heads(v, GLA_HEADS)
    g_dirs = jnp.split(gdn, 2, axis=-1)
    o = 0.0
    finals = []
    for d in range(2):
        la = jax.nn.log_sigmoid(g_dirs[d] @ w_up[d] + b_up[d]) / GLA_TAU
        la = to_heads(la, GLA_HEADS)
        if d == 0:
            od, sd = gla_chunked(qh, kh, vh, la, s0[d])
        else:
            od, sd = gla_chunked(*(jnp.flip(t, 2) for t in (qh, kh, vh, la)), s0[d])
            od = jnp.flip(od, 2)
        o = o + od
        finals.append(sd)
    o = o * lax.rsqrt(jnp.mean(o * o, axis=-1, keepdims=True) + EPS)
    o = o.transpose(0, 2, 1, 3).reshape(b_, n_tok, GLA_HEADS * GLA_DV)
    return o * jax.nn.silu(og), jnp.stack(finals)


def s5_mixer(u, lam_re, lam_im, log_dt, b_re, b_im, c_re, c_im, d_skip, w_glu, b_glu, s0):
    b_, n_tok, _ = u.shape
    ug = u.reshape(b_, n_tok, S5_GROUPS, S5_GROUP)
    y = ug * d_skip
    finals = []
    for d in range(2):
        lam = lax.complex(jnp.minimum(lam_re[d], S5_MAX_RE), lam_im[d])
        dt = jnp.exp(log_dt[d])[:, None]
        lbar = jnp.exp(lam * dt)
        bbar = ((lbar - 1.0) / lam)[..., None] * lax.complex(b_re[d], b_im[d])
        cmat = lax.complex(c_re[d], c_im[d])
        src = ug if d == 0 else jnp.flip(ug, 1)
        bu = jnp.einsum('blgh,gph->blgp', src.astype(jnp.complex64), bbar)
        a = jnp.broadcast_to(lbar[None, None], (1, n_tok) + lbar.shape)
        h = linear_scan(a, bu, s0[d])
        finals.append(h[:, -1])
        yd = jnp.einsum('blgp,ghp->blgh', h, cmat).real
        y = y + (yd if d == 0 else jnp.flip(yd, 1))
    y = y.reshape(b_, n_tok, W_MIX)
    g = jax.nn.gelu(y)
    out = g * jax.nn.sigmoid(g @ w_glu + b_glu)
    return out, jnp.stack(finals)


def hyena_filters(n_tok, w1, b1, w2, b2, w3, freq):
    f32 = jnp.float32
    t = jnp.arange(n_tok, dtype=f32)[:, None]
    bands = jnp.linspace(1e-4, HY_BANDS - 1, HY_BANDS, dtype=f32)[None]
    ang = 2.0 * math.pi * bands * t / n_tok
    z = jnp.concatenate([t / n_tok, jnp.cos(ang), jnp.sin(ang)], axis=-1)
    h = jnp.sin(freq * (z @ w1 + b1))
    h = jnp.sin(freq * (h @ w2 + b2))
    h = h @ w3
    t01 = t / max(n_tok - 1, 1)
    deltas = jnp.abs(jnp.linspace(math.log(HY_TARGET) / HY_DECAY_SHORT,
                                  math.log(HY_TARGET) / HY_DECAY_LONG, W_MIX, dtype=f32))
    h = h * jnp.exp(-t01 * jnp.tile(deltas, 2))
    return h / (jnp.sum(jnp.abs(h), axis=0, keepdims=True) + EPS)


def hyena_mixer(p, w_short, w1, b1, w2, b2, w3, freq, bias):
    n_tok = p.shape[1]
    pc = dwconv(p, w_short, 1, 1)
    v, x0, x1 = jnp.split(pc, 3, axis=-1)
    hf, hb = jnp.split(hyena_filters(n_tok, w1, b1, w2, b2, w3, freq), 2, axis=-1)
    n_fft = 2 * n_tok
    h_freq = jnp.fft.rfft(hf, n=n_fft, axis=0) + jnp.conj(jnp.fft.rfft(hb, n=n_fft, axis=0))
    z = x1 * v
    conv = jnp.fft.irfft(jnp.fft.rfft(z, n=n_fft, axis=1) * h_freq, n=n_fft, axis=1)[:, :n_tok]
    return x0 * (conv + z * bias)


def rglru_mixer(xr, gate, w_conv, w_a, b_a, w_x, b_x, lam, s0):
    b_, n_tok, _ = xr.shape
    xc = dwconv(xr, w_conv, 2, 1)
    xb = xc.reshape(b_, n_tok, RG_BLOCKS, RG_BLOCK)
    y = 0.0
    finals = []
    for d in range(2):
        src = xb if d == 0 else jnp.flip(xb, 1)
        r = jax.nn.sigmoid(jnp.einsum('blhi,hij->blhj', src, w_a[d]) + b_a[d])
        i = jax.nn.sigmoid(jnp.einsum('blhi,hij->blhj', src, w_x[d]) + b_x[d])
        log_a = -RG_C * r * jax.nn.softplus(-lam[d].reshape(RG_BLOCKS, RG_BLOCK))
        a = jnp.exp(log_a)
        bt = jnp.sqrt(-jnp.expm1(2.0 * log_a)) * (i * src)
        h = linear_scan(a, bt, s0[d])
        finals.append(h[:, -1])
        y = y + (h if d == 0 else jnp.flip(h, 1))
    y = y.reshape(b_, n_tok, W_MIX) * jax.nn.gelu(gate)
    return y, jnp.stack(finals)


def merge_branches(branches, gates, w_branch, w_out):
    gk = jnp.split(gates, N_BRANCH, axis=-1)
    y = 0.0
    for k in range(N_BRANCH):
        y = y + jax.nn.sigmoid(gk[k]) * pmm(branches[k], w_branch[k])
    return pmm(y, w_out)


def swiglu(h, w_gu, w_down):
    g, u = jnp.split(pmm(h, w_gu), 2, axis=-1)
    return pmm(jax.nn.silu(g) * u, w_down)


def moe_swiglu(h, router, router_b, w_gu, w_down):
    shp = h.shape
    ht = h.reshape(-1, shp[-1])
    logits = jnp.dot(ht, router, precision=lax.Precision.HIGHEST) + router_b
    top_v, top_i = lax.top_k(logits, TOP_K)
    w = jax.nn.softmax(top_v, axis=-1)
    gates = jnp.sum(jax.nn.one_hot(top_i, N_EXPERTS, dtype=jnp.float32) * w[..., None], axis=1)
    y = jnp.zeros_like(ht)
    for e in range(N_EXPERTS):
        y = y + gates[:, e:e + 1] * swiglu(ht, w_gu[e], w_down[e])
    return y.reshape(shp)


def channel_mix(h, layer, ffn_w_gu, ffn_w_down, moe_router, moe_router_b, moe_w_gu, moe_w_down):
    j = layer // 2
    if layer % 2 == 0:
        return swiglu(h, ffn_w_gu[j], ffn_w_down[j])
    return moe_swiglu(h, moe_router[j], moe_router_b[j], moe_w_gu[j], moe_w_down[j])


def kernel(x, c, ctx, c_ctx, mod_w, mod_b, norm1_g, norm2_g, w_in, gla_w_up, gla_b_up,
           s5_lam_re, s5_lam_im, s5_log_dt, s5_b_re, s5_b_im, s5_c_re, s5_c_im, s5_d,
           s5_w_glu, s5_b_glu, hy_w_short, hy_w1, hy_b1, hy_w2, hy_b2, hy_w3, hy_freq,
           hy_bias, rg_w_conv, rg_w_a, rg_b_a, rg_w_x, rg_b_x, rg_lam, w_branch, w_out,
           ffn_w_gu, ffn_w_down, moe_router, moe_router_b, moe_w_gu, moe_w_down, final_g):
    f32 = jnp.float32
    n_b, n_lat, _ = x.shape
    x = x + grid_pos_embed(n_lat, D_MODEL)[None]
    y_ctx = ctx
    for l in range(DEPTH):
        last = l == DEPTH - 1
        m_lat = jnp.split(adaln(c, mod_w[l], mod_b[l])[:, None, :], 6, axis=-1)
        m_ctx = jnp.split(adaln(c_ctx, mod_w[l], mod_b[l])[None, None, :], 6, axis=-1)

        h_lat = modulate(rmsnorm(x, norm1_g[l]), m_lat[0], m_lat[1])
        h_ctx = modulate(rmsnorm(y_ctx, norm1_g[l]), m_ctx[0], m_ctx[1])
        p_lat = split_cols(pmm(h_lat, w_in[l]))
        p_ctx = split_cols(pmm(h_ctx, w_in[l]))
        gla_p = (gla_w_up[l], gla_b_up[l])
        s5_p = (s5_lam_re[l], s5_lam_im[l], s5_log_dt[l], s5_b_re[l], s5_b_im[l],
                s5_c_re[l], s5_c_im[l], s5_d[l], s5_w_glu[l], s5_b_glu[l])
        hy_p = (hy_w_short[l], hy_w1[l], hy_b1[l], hy_w2[l], hy_b2[l], hy_w3[l], hy_freq[l], hy_bias[l])
        rg_p = (rg_w_conv[l], rg_w_a[l], rg_b_a[l], rg_w_x[l], rg_b_x[l], rg_lam[l])

        gla_c, gla_state = gla_mixer(*p_ctx[0:5], *gla_p,
                                     jnp.zeros((2, n_b, GLA_HEADS, GLA_DK, GLA_DV), f32))
        gla_l, _ = gla_mixer(*p_lat[0:5], *gla_p, gla_state)
        s5_c, s5_state = s5_mixer(p_ctx[5], *s5_p,
                                  jnp.zeros((2, n_b, S5_GROUPS, S5_STATE), jnp.complex64))
        s5_l, _ = s5_mixer(p_lat[5], *s5_p, s5_state)
        rg_c, rg_state = rglru_mixer(p_ctx[7], p_ctx[8], *rg_p,
                                     jnp.zeros((2, n_b, RG_BLOCKS, RG_BLOCK), f32))
        rg_l, _ = rglru_mixer(p_lat[7], p_lat[8], *rg_p, rg_state)
        hy_l = hyena_mixer(p_lat[6], *hy_p)

        x = x + m_lat[2] * merge_branches((gla_l, s5_l, hy_l, rg_l), p_lat[9], w_branch[l], w_out[l])
        if not last:
            hy_c = hyena_mixer(p_ctx[6], *hy_p)
            y_ctx = y_ctx + m_ctx[2] * merge_branches((gla_c, s5_c, hy_c, rg_c), p_ctx[9],
                                                      w_branch[l], w_out[l])

        h2_lat = modulate(rmsnorm(x, norm2_g[l]), m_lat[3], m_lat[4])
        x = x + m_lat[5] * channel_mix(h2_lat, l, ffn_w_gu, ffn_w_down, moe_router,
                                       moe_router_b, moe_w_gu, moe_w_down)
        if not last:
            h2_ctx = modulate(rmsnorm(y_ctx, norm2_g[l]), m_ctx[3], m_ctx[4])
            y_ctx = y_ctx + m_ctx[5] * channel_mix(h2_ctx, l, ffn_w_gu, ffn_w_down, moe_router,
                                                   moe_router_b, moe_w_gu, moe_w_down)
    return rmsnorm(x, final_g)
```

```python
import functools
import math

import jax
import jax.numpy as jnp
from jax import lax
from jax.experimental import pallas as pl
from jax.experimental.pallas import tpu as pltpu

D_MODEL = 1024
DEPTH = 2
GRID_W = 64
EPS = 1e-6
N_BRANCH = 4
W_MIX = D_MODEL // N_BRANCH
GLA_HEADS = 4
GLA_DK = W_MIX // GLA_HEADS
GLA_DV = W_MIX // GLA_HEADS
GLA_RANK = 16
GLA_TAU = 16.0
GLA_CHUNK = 64
S5_GROUP = 16
S5_GROUPS = W_MIX // S5_GROUP
S5_STATE = 64
S5_MAX_RE = -1e-4
HY_BANDS = 16
HY_DECAY_SHORT = 0.3
HY_DECAY_LONG = 1.5
HY_TARGET = 1e-2
RG_BLOCKS = 4
RG_BLOCK = W_MIX // RG_BLOCKS
RG_C = 8.0
N_EXPERTS = 8
TOP_K = 2
IN_SIZES = (GLA_HEADS * GLA_DK, GLA_HEADS * GLA_DK, GLA_HEADS * GLA_DV, GLA_HEADS * GLA_DV,
            2 * GLA_RANK, W_MIX, 3 * W_MIX, W_MIX, W_MIX, N_BRANCH * D_MODEL)

VMEM_LIMIT_BYTES = 48 * 1024 * 1024


def _mm_kernel(x_ref, w_ref, o_ref):
    o_ref[...] = jnp.dot(x_ref[...].astype(jnp.bfloat16), w_ref[...],
                         preferred_element_type=jnp.float32)


def _pick_tile(n, cap):
    best = None
    for t in range(128, cap + 1, 128):
        if n % t == 0:
            best = t
    return best if best is not None else n


def pmm(x, w):
    lead = x.shape[:-1]
    k = x.shape[-1]
    n = w.shape[-1]
    x2 = x.reshape(-1, k)
    m = x2.shape[0]
    tm = 512 if m % 512 == 0 else m
    if k > 2048 and m % 256 == 0:
        tm = 256
    tn = n if k * n * 2 <= 6 * 1024 * 1024 else _pick_tile(n, 1024)
    out = pl.pallas_call(
        _mm_kernel,
        grid=(m // tm, n // tn),
        in_specs=[pl.BlockSpec((tm, k), lambda i, j: (i, 0)),
                  pl.BlockSpec((k, tn), lambda i, j: (0, j))],
        out_specs=pl.BlockSpec((tm, tn), lambda i, j: (i, j)),
        out_shape=jax.ShapeDtypeStruct((m, n), jnp.float32),
        compiler_params=pltpu.CompilerParams(
            dimension_semantics=("arbitrary", "arbitrary"),
            vmem_limit_bytes=VMEM_LIMIT_BYTES),
    )(x2, w.astype(jnp.bfloat16))
    return out.reshape(lead + (n,))


def _mm_multi_kernel(*refs):
    o_ref = refs[-1]
    n = (len(refs) - 1) // 2
    acc = None
    for i in range(n):
        t = jnp.dot(refs[i][...].astype(jnp.bfloat16), refs[n + i][...],
                    preferred_element_type=jnp.float32)
        acc = t if acc is None else acc + t
    o_ref[...] = acc


def pmm_multi(xs, ws, tm=256, tn=512):
    m = xs[0].shape[0]
    n = ws[0].shape[1]
    tm = tm if m % tm == 0 else m
    tn = tn if n % tn == 0 else n
    in_specs = ([pl.BlockSpec((tm, x.shape[1]), lambda i, j: (i, 0)) for x in xs]
                + [pl.BlockSpec((w.shape[0], tn), lambda i, j: (0, j)) for w in ws])
    return pl.pallas_call(
        _mm_multi_kernel,
        grid=(m // tm, n // tn),
        in_specs=in_specs,
        out_specs=pl.BlockSpec((tm, tn), lambda i, j: (i, j)),
        out_shape=jax.ShapeDtypeStruct((m, n), jnp.float32),
        compiler_params=pltpu.CompilerParams(
            dimension_semantics=("arbitrary", "arbitrary"),
            vmem_limit_bytes=VMEM_LIMIT_BYTES),
    )(*xs, *[w.astype(jnp.bfloat16) for w in ws])


RG_SCAN_ROWS = 256


def _rg_scan_kernel(a_ref, b_ref, s0_ref, h_ref, fin_ref, st_ref, *, reverse, tb, nb):
    @pl.when(pl.program_id(0) == 0)
    def _():
        st_ref[...] = s0_ref[...]

    def body(r, hs):
        rr = (tb - 1 - r) if reverse else r
        out = []
        for i in range(nb):
            h = a_ref[i, pl.ds(rr, 1), :] * hs[i] + b_ref[i, pl.ds(rr, 1), :]
            h_ref[i, pl.ds(rr, 1), :] = h
            out.append(h)
        return tuple(out)

    hs = lax.fori_loop(0, tb, body, tuple(st_ref[i:i + 1, :] for i in range(nb)), unroll=8)
    for i in range(nb):
        st_ref[i:i + 1, :] = hs[i]
        fin_ref[i:i + 1, :] = hs[i]


def rg_scan(a, b, s0, reverse):
    nb, n_tok, ch = a.shape
    tb = min(RG_SCAN_ROWS, n_tok)
    nblk = n_tok // tb
    imap = (lambda k: (0, nblk - 1 - k, 0)) if reverse else (lambda k: (0, k, 0))
    return pl.pallas_call(
        functools.partial(_rg_scan_kernel, reverse=reverse, tb=tb, nb=nb),
        grid=(nblk,),
        in_specs=[pl.BlockSpec((nb, tb, ch), imap), pl.BlockSpec((nb, tb, ch), imap),
                  pl.BlockSpec((nb, ch), lambda k: (0, 0))],
        out_specs=[pl.BlockSpec((nb, tb, ch), imap), pl.BlockSpec((nb, ch), lambda k: (0, 0))],
        out_shape=[jax.ShapeDtypeStruct((nb, n_tok, ch), jnp.float32),
                   jax.ShapeDtypeStruct((nb, ch), jnp.float32)],
        scratch_shapes=[pltpu.VMEM((nb, ch), jnp.float32)],
        compiler_params=pltpu.CompilerParams(dimension_semantics=("arbitrary",)),
    )(a, b, s0)


S5_T = 16
S5_NS = S5_GROUPS * S5_STATE
S5_SCAN_CHUNKS = 64


def _s5_scan_kernel(d_ref, s0_ref, a_ref, h_ref, fin_ref, st_ref, *, rc, nb):
    d = pl.program_id(0)

    @pl.when(pl.program_id(1) == 0)
    def _():
        st_ref[...] = s0_ref[0]

    ar = jnp.broadcast_to(a_ref[0, :, 0:S5_NS], (nb, S5_NS))
    ai = jnp.broadcast_to(a_ref[0, :, S5_NS:2 * S5_NS], (nb, S5_NS))

    def body(r, carry):
        hr, hi = carry
        rr = r + d * (rc - 1 - 2 * r)
        h_ref[rr, :, 0:S5_NS] = hr
        h_ref[rr, :, S5_NS:2 * S5_NS] = hi
        dr = d_ref[rr, :, 0:S5_NS]
        di = d_ref[rr, :, S5_NS:2 * S5_NS]
        return ar * hr - ai * hi + dr, ar * hi + ai * hr + di

    hr, hi = lax.fori_loop(0, rc, body, (st_ref[:, 0:S5_NS], st_ref[:, S5_NS:2 * S5_NS]))
    st_ref[:, 0:S5_NS] = hr
    st_ref[:, S5_NS:2 * S5_NS] = hi
    fin_ref[0, :, 0:S5_NS] = hr
    fin_ref[0, :, S5_NS:2 * S5_NS] = hi


def s5_scan(dmat, s0, a_t):
    n, nb, _ = dmat.shape
    rc = min(S5_SCAN_CHUNKS, n)
    nblk = n // rc
    w = 2 * S5_NS
    imap = lambda d, k: (k + d * (nblk - 1 - 2 * k), 0, d)
    return pl.pallas_call(
        functools.partial(_s5_scan_kernel, rc=rc, nb=nb),
        grid=(2, nblk),
        in_specs=[pl.BlockSpec((rc, nb, w), imap),
                  pl.BlockSpec((1, nb, w), lambda d, k: (d, 0, 0)),
                  pl.BlockSpec((1, 1, w), lambda d, k: (d, 0, 0))],
        out_specs=[pl.BlockSpec((rc, nb, w), imap),
                   pl.BlockSpec((1, nb, w), lambda d, k: (d, 0, 0))],
        out_shape=[jax.ShapeDtypeStruct((n, nb, 2 * w), jnp.float32),
                   jax.ShapeDtypeStruct((2, nb, w), jnp.float32)],
        scratch_shapes=[pltpu.VMEM((nb, w), jnp.float32)],
        compiler_params=pltpu.CompilerParams(dimension_semantics=("arbitrary", "arbitrary"),
                                             vmem_limit_bytes=VMEM_LIMIT_BYTES),
    )(dmat, s0, a_t)


def _cmul(ar, ai, br, bi):
    return ar * br - ai * bi, ar * bi + ai * br


def s5_prepare(lam_re, lam_im, log_dt, b_re, b_im, c_re, c_im, d_skip):
    f32 = jnp.float32
    hp = lax.Precision.HIGHEST
    t_len, g_n, p_n, h_n = S5_T, S5_GROUPS, S5_STATE, S5_GROUP
    eye_g = jnp.eye(g_n, dtype=f32)
    ar_t = jnp.arange(t_len)
    wd, wc, kk, a_t = [], [], [], []
    for d in range(2):
        lr = jnp.minimum(lam_re[d], S5_MAX_RE)
        li = lam_im[d]
        dt = jnp.exp(log_dt[d])[:, None]
        tt = jnp.arange(t_len + 1, dtype=f32)[:, None, None]
        mag = jnp.exp(lr * dt * tt)
        ang = li * dt * tt
        pr, pi = mag * jnp.cos(ang), mag * jnp.sin(ang)
        nr, ni = pr[1] - 1.0, pi[1]
        den = lr * lr + li * li
        qr, qi = (nr * lr + ni * li) / den, (ni * lr - nr * li) / den
        bbr, bbi = _cmul(qr[..., None], qi[..., None], b_re[d], b_im[d])
        cr, ci = c_re[d], c_im[d]

        idx = (t_len - 1 - ar_t) if d == 0 else ar_t
        wr, wi = _cmul(pr[idx][..., None], pi[idx][..., None], bbr[None], bbi[None])

        def place_d(w):
            wt = w.transpose(0, 3, 1, 2)
            full = wt[:, None] * eye_g[None, :, None, :, None]
            return full.reshape(t_len * W_MIX, g_n * p_n)

        wd.append(jnp.concatenate([place_d(wr), place_d(wi)], axis=1))

        idx2 = (ar_t + 1) if d == 0 else (t_len - ar_t)
        cwr, cwi = _cmul(cr[None], ci[None], pr[idx2][:, :, None, :], pi[idx2][:, :, None, :])

        def place_c(w):
            wt = w.transpose(1, 3, 0, 2)
            full = wt[:, :, :, None, :] * eye_g[:, None, None, :, None]
            return full.reshape(g_n * p_n, t_len * W_MIX)

        wc.append(jnp.concatenate([place_c(cwr), place_c(-cwi)], axis=0))

        er, ei = _cmul(pr[:t_len][:, :, None, :], pi[:t_len][:, :, None, :], cr[None], ci[None])
        kk.append(jnp.einsum('tghp,gpk->tghk', er, bbr, precision=hp)
                  - jnp.einsum('tghp,gpk->tghk', ei, bbi, precision=hp))
        a_t.append(jnp.concatenate([pr[t_len].reshape(1, -1), pi[t_len].reshape(1, -1)], axis=1))

    lag = ar_t[None, :] - ar_t[:, None]
    mf = jnp.where((lag >= 0)[..., None, None, None], kk[0][jnp.clip(lag, 0, t_len - 1)], 0.0)
    mb = jnp.where((lag <= 0)[..., None, None, None], kk[1][jnp.clip(-lag, 0, t_len - 1)], 0.0)
    skip = (jnp.eye(t_len, dtype=f32)[:, :, None, None, None] * d_skip[None, None, :, :, None]
            * jnp.eye(h_n, dtype=f32)[None, None, None])
    m = mf + mb + skip
    mt = m.transpose(0, 4, 1, 2, 3)
    full = mt[:, None, :, :, :, :] * eye_g[None, :, None, None, :, None]
    wk = full.reshape(t_len * W_MIX, t_len * W_MIX)
    return (jnp.concatenate(wd, axis=1), wk, jnp.concatenate(wc, axis=0), jnp.stack(a_t))


def rmsnorm(x, g):
    y = x * lax.rsqrt(jnp.mean(x * x, axis=-1, keepdims=True) + EPS)
    return y * g


def modulate(h, shift, scale):
    return h * (1 + scale) + shift


def adaln(cond, w, b):
    return jax.nn.silu(cond) @ w + b


def split_cols(p):
    out, start = [], 0
    for n in IN_SIZES:
        out.append(p[..., start:start + n])
        start += n
    return out


def grid_pos_embed(n_tokens, dim):
    rows = n_tokens // GRID_W
    q = dim // 4
    omega = 1.0 / (10000.0 ** (jnp.arange(q, dtype=jnp.float32) / q))
    r = jnp.arange(rows, dtype=jnp.float32)[:, None] * omega
    cc = jnp.arange(GRID_W, dtype=jnp.float32)[:, None] * omega
    er = jnp.concatenate([jnp.sin(r), jnp.cos(r)], axis=-1)
    ec = jnp.concatenate([jnp.sin(cc), jnp.cos(cc)], axis=-1)
    emb = jnp.concatenate([jnp.broadcast_to(er[:, None], (rows, GRID_W, dim // 2)),
                           jnp.broadcast_to(ec[None], (rows, GRID_W, dim // 2))], axis=-1)
    return emb.reshape(rows * GRID_W, dim)


def dwconv(x, w, pad_l, pad_r):
    n = x.shape[1]
    xp = jnp.pad(x, ((0, 0), (pad_l, pad_r), (0, 0)))
    return sum(xp[:, k:k + n] * w[k] for k in range(w.shape[0]))


def to_heads(t, n):
    b, n_tok, w = t.shape
    return t.reshape(b, n_tok, n, w // n).transpose(0, 2, 1, 3)


def _lin_combine(e1, e2):
    a1, b1 = e1
    a2, b2 = e2
    return a1 * a2, a2 * b1 + b2


def linear_scan(a, b, h0):
    b = b.at[:, 0].add(a[:, 0] * h0)
    _, h = lax.associative_scan(_lin_combine, (a, b), axis=1)
    return h


def gla_chunked(q, k, v, la, s0):
    b_, h_, n_tok, dk = q.shape
    dv = v.shape[-1]
    n_ch = n_tok // GLA_CHUNK
    q = q.reshape(b_, h_, n_ch, GLA_CHUNK, dk)
    k = k.reshape(b_, h_, n_ch, GLA_CHUNK, dk)
    v = v.reshape(b_, h_, n_ch, GLA_CHUNK, dv)
    cum = jnp.cumsum(la.reshape(b_, h_, n_ch, GLA_CHUNK, dk), axis=3)
    cum_last = cum[:, :, :, -1:, :]
    q_in = q * jnp.exp(cum)
    k_in = k * jnp.exp(-cum)
    k_out = k * jnp.exp(cum_last - cum)
    mask = jnp.tril(jnp.ones((GLA_CHUNK, GLA_CHUNK), dtype=bool))
    att = jnp.where(mask, jnp.einsum('bhnid,bhnjd->bhnij', q_in, k_in), 0.0)
    o_intra = jnp.einsum('bhnij,bhnjv->bhniv', att, v)
    d_state = jnp.einsum('bhnjd,bhnjv->bhndv', k_out, v)
    decay = jnp.exp(cum_last[:, :, :, 0, :])

    def step(s, inp):
        dec, ds = inp
        return dec[..., None] * s + ds, s

    s_fin, s_in = lax.scan(step, s0, (jnp.moveaxis(decay, 2, 0), jnp.moveaxis(d_state, 2, 0)))
    s_in = jnp.moveaxis(s_in, 0, 2)
    o_inter = jnp.einsum('bhnid,bhndv->bhniv', q_in, s_in)
    return (o_intra + o_inter).reshape(b_, h_, n_tok, dv), s_fin


def gla_mixer(q, k, v, og, gdn, w_up, b_up, s0):
    b_, n_tok, _ = q.shape
    qh = to_heads(q, GLA_HEADS) * (GLA_DK ** -0.5)
    kh = to_heads(k, GLA_HEADS)
    vh = to_heads(v, GLA_HEADS)
    g_dirs = jnp.split(gdn, 2, axis=-1)
    o = 0.0
    finals = []
    for d in range(2):
        la = jax.nn.log_sigmoid(g_dirs[d] @ w_up[d] + b_up[d]) / GLA_TAU
        la = to_heads(la, GLA_HEADS)
        if d == 0:
            od, sd = gla_chunked(qh, kh, vh, la, s0[d])
        else:
            od, sd = gla_chunked(*(jnp.flip(t, 2) for t in (qh, kh, vh, la)), s0[d])
            od = jnp.flip(od, 2)
        o = o + od
        finals.append(sd)
    o = o * lax.rsqrt(jnp.mean(o * o, axis=-1, keepdims=True) + EPS)
    o = o.transpose(0, 2, 1, 3).reshape(b_, n_tok, GLA_HEADS * GLA_DV)
    return o * jax.nn.silu(og), jnp.stack(finals)


def s5_mixer(u, prep, w_glu, b_glu, s0):
    wd, wk, wc, a_t = prep
    b_, n_tok, _ = u.shape
    n = n_tok // S5_T
    u2 = u.reshape(b_, n, S5_T * W_MIX).transpose(1, 0, 2).reshape(n * b_, S5_T * W_MIX)
    dmat = pmm(u2, wd)
    hmat, fin = s5_scan(dmat.reshape(n, b_, 4 * S5_NS), s0, a_t)
    y2 = pmm_multi([u2, hmat.reshape(n * b_, 4 * S5_NS)], [wk, wc])
    y = y2.reshape(n, b_, S5_T, W_MIX).transpose(1, 0, 2, 3).reshape(b_, n_tok, W_MIX)
    g = jax.nn.gelu(y)
    out = g * jax.nn.sigmoid(pmm(g, w_glu) + b_glu)
    return out, fin


def hyena_filters(n_tok, w1, b1, w2, b2, w3, freq):
    f32 = jnp.float32
    t = jnp.arange(n_tok, dtype=f32)[:, None]
    bands = jnp.linspace(1e-4, HY_BANDS - 1, HY_BANDS, dtype=f32)[None]
    ang = 2.0 * math.pi * bands * t / n_tok
    z = jnp.concatenate([t / n_tok, jnp.cos(ang), jnp.sin(ang)], axis=-1)
    h = jnp.sin(freq * (z @ w1 + b1))
    h = jnp.sin(freq * (h @ w2 + b2))
    h = h @ w3
    t01 = t / max(n_tok - 1, 1)
    deltas = jnp.abs(jnp.linspace(math.log(HY_TARGET) / HY_DECAY_SHORT,
                                  math.log(HY_TARGET) / HY_DECAY_LONG, W_MIX, dtype=f32))
    h = h * jnp.exp(-t01 * jnp.tile(deltas, 2))
    return h / (jnp.sum(jnp.abs(h), axis=0, keepdims=True) + EPS)


def hyena_mixer(p, w_short, w1, b1, w2, b2, w3, freq, bias):
    n_tok = p.shape[1]
    pc = dwconv(p, w_short, 1, 1)
    v, x0, x1 = jnp.split(pc, 3, axis=-1)
    hf, hb = jnp.split(hyena_filters(n_tok, w1, b1, w2, b2, w3, freq), 2, axis=-1)
    n_fft = 2 * n_tok
    h_freq = jnp.fft.rfft(hf, n=n_fft, axis=0) + jnp.conj(jnp.fft.rfft(hb, n=n_fft, axis=0))
    z = x1 * v
    conv = jnp.fft.irfft(jnp.fft.rfft(z, n=n_fft, axis=1) * h_freq, n=n_fft, axis=1)[:, :n_tok]
    return x0 * (conv + z * bias)


def rglru_mixer(xr, gate, w_conv, w_a, b_a, w_x, b_x, lam, s0):
    b_, n_tok, _ = xr.shape
    xc = dwconv(xr, w_conv, 2, 1)
    xb = xc.reshape(b_, n_tok, RG_BLOCKS, RG_BLOCK)
    y = 0.0
    finals = []
    for d in range(2):
        r = jax.nn.sigmoid(jnp.einsum('blhi,hij->blhj', xb, w_a[d]) + b_a[d])
        i = jax.nn.sigmoid(jnp.einsum('blhi,hij->blhj', xb, w_x[d]) + b_x[d])
        log_a = -RG_C * r * jax.nn.softplus(-lam[d].reshape(RG_BLOCKS, RG_BLOCK))
        a = jnp.exp(log_a)
        bt = jnp.sqrt(-jnp.expm1(2.0 * log_a)) * (i * xb)
        h, fin = rg_scan(a.reshape(b_, n_tok, W_MIX), bt.reshape(b_, n_tok, W_MIX), s0[d], d == 1)
        finals.append(fin)
        y = y + h
    y = y * jax.nn.gelu(gate)
    return y, jnp.stack(finals)


def merge_branches(branches, gates, w_branch, w_out):
    gk = jnp.split(gates, N_BRANCH, axis=-1)
    y = 0.0
    for k in range(N_BRANCH):
        y = y + jax.nn.sigmoid(gk[k]) * pmm(branches[k], w_branch[k])
    return pmm(y, w_out)


def swiglu(h, w_gu, w_down):
    g, u = jnp.split(pmm(h, w_gu), 2, axis=-1)
    return pmm(jax.nn.silu(g) * u, w_down)


def moe_swiglu(h, router, router_b, w_gu, w_down):
    shp = h.shape
    ht = h.reshape(-1, shp[-1])
    logits = jnp.dot(ht, router, precision=lax.Precision.HIGHEST) + router_b
    top_v, top_i = lax.top_k(logits, TOP_K)
    w = jax.nn.softmax(top_v, axis=-1)
    gates = jnp.sum(jax.nn.one_hot(top_i, N_EXPERTS, dtype=jnp.float32) * w[..., None], axis=1)
    y = jnp.zeros_like(ht)
    for e in range(N_EXPERTS):
        y = y + gates[:, e:e + 1] * swiglu(ht, w_gu[e], w_down[e])
    return y.reshape(shp)


def channel_mix(h, layer, ffn_w_gu, ffn_w_down, moe_router, moe_router_b, moe_w_gu, moe_w_down):
    j = layer // 2
    if layer % 2 == 0:
        return swiglu(h, ffn_w_gu[j], ffn_w_down[j])
    return moe_swiglu(h, moe_router[j], moe_router_b[j], moe_w_gu[j], moe_w_down[j])


def kernel(x, c, ctx, c_ctx, mod_w, mod_b, norm1_g, norm2_g, w_in, gla_w_up, gla_b_up,
           s5_lam_re, s5_lam_im, s5_log_dt, s5_b_re, s5_b_im, s5_c_re, s5_c_im, s5_d,
           s5_w_glu, s5_b_glu, hy_w_short, hy_w1, hy_b1, hy_w2, hy_b2, hy_w3, hy_freq,
           hy_bias, rg_w_conv, rg_w_a, rg_b_a, rg_w_x, rg_b_x, rg_lam, w_branch, w_out,
           ffn_w_gu, ffn_w_down, moe_router, moe_router_b, moe_w_gu, moe_w_down, final_g):
    f32 = jnp.float32
    n_b, n_lat, _ = x.shape
    x = x + grid_pos_embed(n_lat, D_MODEL)[None]
    y_ctx = ctx
    for l in range(DEPTH):
        last = l == DEPTH - 1
        m_lat = jnp.split(adaln(c, mod_w[l], mod_b[l])[:, None, :], 6, axis=-1)
        m_ctx = jnp.split(adaln(c_ctx, mod_w[l], mod_b[l])[None, None, :], 6, axis=-1)

        h_lat = modulate(rmsnorm(x, norm1_g[l]), m_lat[0], m_lat[1])
        h_ctx = modulate(rmsnorm(y_ctx, norm1_g[l]), m_ctx[0], m_ctx[1])
        p_lat = split_cols(pmm(h_lat, w_in[l]))
        p_ctx = split_cols(pmm(h_ctx, w_in[l]))
        gla_p = (gla_w_up[l], gla_b_up[l])
        s5_p = (s5_prepare(s5_lam_re[l], s5_lam_im[l], s5_log_dt[l], s5_b_re[l], s5_b_im[l],
                           s5_c_re[l], s5_c_im[l], s5_d[l]), s5_w_glu[l], s5_b_glu[l])
        hy_p = (hy_w_short[l], hy_w1[l], hy_b1[l], hy_w2[l], hy_b2[l], hy_w3[l], hy_freq[l], hy_bias[l])
        rg_p = (rg_w_conv[l], rg_w_a[l], rg_b_a[l], rg_w_x[l], rg_b_x[l], rg_lam[l])

        gla_c, gla_state = gla_mixer(*p_ctx[0:5], *gla_p,
                                     jnp.zeros((2, n_b, GLA_HEADS, GLA_DK, GLA_DV), f32))
        gla_l, _ = gla_mixer(*p_lat[0:5], *gla_p, gla_state)
        s5_c, s5_state = s5_mixer(p_ctx[5], *s5_p,
                                  jnp.zeros((2, n_b, 2 * S5_NS), f32))
        s5_l, _ = s5_mixer(p_lat[5], *s5_p, s5_state)
        rg_c, rg_state = rglru_mixer(p_ctx[7], p_ctx[8], *rg_p,
                                     jnp.zeros((2, n_b, W_MIX), f32))
        rg_l, _ = rglru_mixer(p_lat[7], p_lat[8], *rg_p, rg_state)
        hy_l = hyena_mixer(p_lat[6], *hy_p)

        x = x + m_lat[2] * merge_branches((gla_l, s5_l, hy_l, rg_l), p_lat[9], w_branch[l], w_out[l])
        if not last:
            hy_c = hyena_mixer(p_ctx[6], *hy_p)
            y_ctx = y_ctx + m_ctx[2] * merge_branches((gla_c, s5_c, hy_c, rg_c), p_ctx[9],
                                                      w_branch[l], w_out[l])

        h2_lat = modulate(rmsnorm(x, norm2_g[l]), m_lat[3], m_lat[4])
        x = x + m_lat[5] * channel_mix(h2_lat, l, ffn_w_gu, ffn_w_down, moe_router,
                                       moe_router_b, moe_w_gu, moe_w_down)
        if not last:
            h2_ctx = modulate(rmsnorm(y_ctx, norm2_g[l]), m_ctx[3], m_ctx[4])
            y_ctx = y_ctx + m_ctx[5] * channel_mix(h2_ctx, l, ffn_w_gu, ffn_w_down, moe_router,
                                                   moe_router_b, moe_w_gu, moe_w_down)
    return rmsnorm(x, final_g)
```

```python
import functools
import math

import jax
import jax.numpy as jnp
from jax import lax
from jax.experimental import pallas as pl
from jax.experimental.pallas import tpu as pltpu

D_MODEL = 1024
DEPTH = 2
GRID_W = 64
EPS = 1e-6
N_BRANCH = 4
W_MIX = D_MODEL // N_BRANCH
GLA_HEADS = 4
GLA_DK = W_MIX // GLA_HEADS
GLA_DV = W_MIX // GLA_HEADS
GLA_RANK = 16
GLA_TAU = 16.0
GLA_CHUNK = 64
S5_GROUP = 16
S5_GROUPS = W_MIX // S5_GROUP
S5_STATE = 64
S5_MAX_RE = -1e-4
HY_BANDS = 16
HY_DECAY_SHORT = 0.3
HY_DECAY_LONG = 1.5
HY_TARGET = 1e-2
RG_BLOCKS = 4
RG_BLOCK = W_MIX // RG_BLOCKS
RG_C = 8.0
N_EXPERTS = 8
TOP_K = 2
IN_SIZES = (GLA_HEADS * GLA_DK, GLA_HEADS * GLA_DK, GLA_HEADS * GLA_DV, GLA_HEADS * GLA_DV,
            2 * GLA_RANK, W_MIX, 3 * W_MIX, W_MIX, W_MIX, N_BRANCH * D_MODEL)

VMEM_LIMIT_BYTES = 48 * 1024 * 1024


def _mm_kernel(x_ref, w_ref, o_ref):
    o_ref[...] = jnp.dot(x_ref[...].astype(jnp.bfloat16), w_ref[...],
                         preferred_element_type=jnp.float32)


def _pick_tile(n, cap):
    best = None
    for t in range(128, cap + 1, 128):
        if n % t == 0:
            best = t
    return best if best is not None else n


def pmm(x, w):
    lead = x.shape[:-1]
    k = x.shape[-1]
    n = w.shape[-1]
    x2 = x.reshape(-1, k)
    m = x2.shape[0]
    tm = 512 if m % 512 == 0 else m
    if k > 2048 and m % 256 == 0:
        tm = 256
    tn = n if k * n * 2 <= 6 * 1024 * 1024 else _pick_tile(n, 1024)
    out = pl.pallas_call(
        _mm_kernel,
        grid=(m // tm, n // tn),
        in_specs=[pl.BlockSpec((tm, k), lambda i, j: (i, 0)),
                  pl.BlockSpec((k, tn), lambda i, j: (0, j))],
        out_specs=pl.BlockSpec((tm, tn), lambda i, j: (i, j)),
        out_shape=jax.ShapeDtypeStruct((m, n), jnp.float32),
        compiler_params=pltpu.CompilerParams(
            dimension_semantics=("arbitrary", "arbitrary"),
            vmem_limit_bytes=VMEM_LIMIT_BYTES),
    )(x2, w.astype(jnp.bfloat16))
    return out.reshape(lead + (n,))


def _mm_multi_kernel(*refs):
    o_ref = refs[-1]
    n = (len(refs) - 1) // 2
    acc = None
    for i in range(n):
        t = jnp.dot(refs[i][...].astype(jnp.bfloat16), refs[n + i][...],
                    preferred_element_type=jnp.float32)
        acc = t if acc is None else acc + t
    o_ref[...] = acc


def pmm_multi(xs, ws, tm=256, tn=512):
    m = xs[0].shape[0]
    n = ws[0].shape[1]
    tm = tm if m % tm == 0 else m
    tn = tn if n % tn == 0 else n
    in_specs = ([pl.BlockSpec((tm, x.shape[1]), lambda i, j: (i, 0)) for x in xs]
                + [pl.BlockSpec((w.shape[0], tn), lambda i, j: (0, j)) for w in ws])
    return pl.pallas_call(
        _mm_multi_kernel,
        grid=(m // tm, n // tn),
        in_specs=in_specs,
        out_specs=pl.BlockSpec((tm, tn), lambda i, j: (i, j)),
        out_shape=jax.ShapeDtypeStruct((m, n), jnp.float32),
        compiler_params=pltpu.CompilerParams(
            dimension_semantics=("arbitrary", "arbitrary"),
            vmem_limit_bytes=VMEM_LIMIT_BYTES),
    )(*xs, *[w.astype(jnp.bfloat16) for w in ws])


RG_SCAN_ROWS = 256


def _rg_scan_kernel(a_ref, b_ref, s0_ref, h_ref, fin_ref, st_ref, *, reverse, tb, nb):
    @pl.when(pl.program_id(0) == 0)
    def _():
        st_ref[...] = s0_ref[...]

    def body(r, hs):
        rr = (tb - 1 - r) if reverse else r
        out = []
        for i in range(nb):
            h = a_ref[i, pl.ds(rr, 1), :] * hs[i] + b_ref[i, pl.ds(rr, 1), :]
            h_ref[i, pl.ds(rr, 1), :] = h
            out.append(h)
        return tuple(out)

    hs = lax.fori_loop(0, tb, body, tuple(st_ref[i:i + 1, :] for i in range(nb)), unroll=8)
    for i in range(nb):
        st_ref[i:i + 1, :] = hs[i]
        fin_ref[i:i + 1, :] = hs[i]


def rg_scan(a, b, s0, reverse):
    nb, n_tok, ch = a.shape
    tb = min(RG_SCAN_ROWS, n_tok)
    nblk = n_tok // tb
    imap = (lambda k: (0, nblk - 1 - k, 0)) if reverse else (lambda k: (0, k, 0))
    return pl.pallas_call(
        functools.partial(_rg_scan_kernel, reverse=reverse, tb=tb, nb=nb),
        grid=(nblk,),
        in_specs=[pl.BlockSpec((nb, tb, ch), imap), pl.BlockSpec((nb, tb, ch), imap),
                  pl.BlockSpec((nb, ch), lambda k: (0, 0))],
        out_specs=[pl.BlockSpec((nb, tb, ch), imap), pl.BlockSpec((nb, ch), lambda k: (0, 0))],
        out_shape=[jax.ShapeDtypeStruct((nb, n_tok, ch), jnp.float32),
                   jax.ShapeDtypeStruct((nb, ch), jnp.float32)],
        scratch_shapes=[pltpu.VMEM((nb, ch), jnp.float32)],
        compiler_params=pltpu.CompilerParams(dimension_semantics=("arbitrary",)),
    )(a, b, s0)


S5_T = 16
S5_NS = S5_GROUPS * S5_STATE
S5_SCAN_CHUNKS = 64


def _s5_scan_kernel(d_ref, s0_ref, a_ref, h_ref, fin_ref, st_ref, *, rc, nb):
    d = pl.program_id(0)

    @pl.when(pl.program_id(1) == 0)
    def _():
        st_ref[...] = s0_ref[0]

    ar = jnp.broadcast_to(a_ref[0, :, 0:S5_NS], (nb, S5_NS))
    ai = jnp.broadcast_to(a_ref[0, :, S5_NS:2 * S5_NS], (nb, S5_NS))

    def body(r, carry):
        hr, hi = carry
        rr = r + d * (rc - 1 - 2 * r)
        h_ref[rr, :, 0:S5_NS] = hr
        h_ref[rr, :, S5_NS:2 * S5_NS] = hi
        dr = d_ref[rr, :, 0:S5_NS]
        di = d_ref[rr, :, S5_NS:2 * S5_NS]
        return ar * hr - ai * hi + dr, ar * hi + ai * hr + di

    hr, hi = lax.fori_loop(0, rc, body, (st_ref[:, 0:S5_NS], st_ref[:, S5_NS:2 * S5_NS]))
    st_ref[:, 0:S5_NS] = hr
    st_ref[:, S5_NS:2 * S5_NS] = hi
    fin_ref[0, :, 0:S5_NS] = hr
    fin_ref[0, :, S5_NS:2 * S5_NS] = hi


def s5_scan(dmat, s0, a_t):
    n, nb, _ = dmat.shape
    rc = min(S5_SCAN_CHUNKS, n)
    nblk = n // rc
    w = 2 * S5_NS
    imap = lambda d, k: (k + d * (nblk - 1 - 2 * k), 0, d)
    return pl.pallas_call(
        functools.partial(_s5_scan_kernel, rc=rc, nb=nb),
        grid=(2, nblk),
        in_specs=[pl.BlockSpec((rc, nb, w), imap),
                  pl.BlockSpec((1, nb, w), lambda d, k: (d, 0, 0)),
                  pl.BlockSpec((1, 1, w), lambda d, k: (d, 0, 0))],
        out_specs=[pl.BlockSpec((rc, nb, w), imap),
                   pl.BlockSpec((1, nb, w), lambda d, k: (d, 0, 0))],
        out_shape=[jax.ShapeDtypeStruct((n, nb, 2 * w), jnp.float32),
                   jax.ShapeDtypeStruct((2, nb, w), jnp.float32)],
        scratch_shapes=[pltpu.VMEM((nb, w), jnp.float32)],
        compiler_params=pltpu.CompilerParams(dimension_semantics=("arbitrary", "arbitrary"),
                                             vmem_limit_bytes=VMEM_LIMIT_BYTES),
    )(dmat, s0, a_t)


def _cmul(ar, ai, br, bi):
    return ar * br - ai * bi, ar * bi + ai * br


def s5_prepare(lam_re, lam_im, log_dt, b_re, b_im, c_re, c_im, d_skip):
    f32 = jnp.float32
    hp = lax.Precision.HIGHEST
    t_len, g_n, p_n, h_n = S5_T, S5_GROUPS, S5_STATE, S5_GROUP
    eye_g = jnp.eye(g_n, dtype=f32)
    ar_t = jnp.arange(t_len)
    wd, wc, kk, a_t = [], [], [], []
    for d in range(2):
        lr = jnp.minimum(lam_re[d], S5_MAX_RE)
        li = lam_im[d]
        dt = jnp.exp(log_dt[d])[:, None]
        tt = jnp.arange(t_len + 1, dtype=f32)[:, None, None]
        mag = jnp.exp(lr * dt * tt)
        ang = li * dt * tt
        pr, pi = mag * jnp.cos(ang), mag * jnp.sin(ang)
        nr, ni = pr[1] - 1.0, pi[1]
        den = lr * lr + li * li
        qr, qi = (nr * lr + ni * li) / den, (ni * lr - nr * li) / den
        bbr, bbi = _cmul(qr[..., None], qi[..., None], b_re[d], b_im[d])
        cr, ci = c_re[d], c_im[d]

        idx = (t_len - 1 - ar_t) if d == 0 else ar_t
        wr, wi = _cmul(pr[idx][..., None], pi[idx][..., None], bbr[None], bbi[None])

        def place_d(w):
            wt = w.transpose(0, 3, 1, 2)
            full = wt[:, None] * eye_g[None, :, None, :, None]
            return full.reshape(t_len * W_MIX, g_n * p_n)

        wd.append(jnp.concatenate([place_d(wr), place_d(wi)], axis=1))

        idx2 = (ar_t + 1) if d == 0 else (t_len - ar_t)
        cwr, cwi = _cmul(cr[None], ci[None], pr[idx2][:, :, None, :], pi[idx2][:, :, None, :])

        def place_c(w):
            wt = w.transpose(1, 3, 0, 2)
            full = wt[:, :, :, None, :] * eye_g[:, None, None, :, None]
            return full.reshape(g_n * p_n, t_len * W_MIX)

        wc.append(jnp.concatenate([place_c(cwr), place_c(-cwi)], axis=0))

        er, ei = _cmul(pr[:t_len][:, :, None, :], pi[:t_len][:, :, None, :], cr[None], ci[None])
        kk.append(jnp.einsum('tghp,gpk->tghk', er, bbr, precision=hp)
                  - jnp.einsum('tghp,gpk->tghk', ei, bbi, precision=hp))
        a_t.append(jnp.concatenate([pr[t_len].reshape(1, -1), pi[t_len].reshape(1, -1)], axis=1))

    lag = ar_t[None, :] - ar_t[:, None]
    mf = jnp.where((lag >= 0)[..., None, None, None], kk[0][jnp.clip(lag, 0, t_len - 1)], 0.0)
    mb = jnp.where((lag <= 0)[..., None, None, None], kk[1][jnp.clip(-lag, 0, t_len - 1)], 0.0)
    skip = (jnp.eye(t_len, dtype=f32)[:, :, None, None, None] * d_skip[None, None, :, :, None]
            * jnp.eye(h_n, dtype=f32)[None, None, None])
    m = mf + mb + skip
    mt = m.transpose(0, 4, 1, 2, 3)
    full = mt[:, None, :, :, :, :] * eye_g[None, :, None, None, :, None]
    wk = full.reshape(t_len * W_MIX, t_len * W_MIX)
    return (jnp.concatenate(wd, axis=1), wk, jnp.concatenate(wc, axis=0), jnp.stack(a_t))


GLA_BLOCK = 512


def _gla_kernel(q_ref, k_ref, v_ref, g_ref, wup_ref, bup_ref, s0_ref, o_ref, fin_ref, st_ref,
                *, reverse, tb):
    f32, bf16 = jnp.float32, jnp.bfloat16
    hp = lax.Precision.HIGHEST
    cc = GLA_CHUNK

    @pl.when(pl.program_id(1) == 0)
    def _():
        st_ref[...] = s0_ref[0]

    r_i = lax.broadcasted_iota(jnp.int32, (cc, cc), 0)
    c_i = lax.broadcasted_iota(jnp.int32, (cc, cc), 1)
    keep = (c_i >= r_i) if reverse else (c_i <= r_i)
    tri = keep.astype(f32)
    nt = (((1,), (1,)), ((), ()))
    tn = (((0,), (0,)), ((), ()))
    n_ch = tb // cc
    for ci in (range(n_ch - 1, -1, -1) if reverse else range(n_ch)):
        rows = slice(ci * cc, (ci + 1) * cc)
        z = jnp.dot(g_ref[0, rows, :], wup_ref[...], precision=hp,
                    preferred_element_type=f32) + bup_ref[...]
        la = (jnp.minimum(z, 0.0) - jnp.log1p(jnp.exp(-jnp.abs(z)))) * (1.0 / GLA_TAU)
        cum = jnp.dot(tri, la, precision=hp, preferred_element_type=f32)
        last = cum[0:1, :] if reverse else cum[cc - 1:cc, :]
        k = k_ref[0, rows, :]
        q_in = (q_ref[0, rows, :] * (GLA_DK ** -0.5) * jnp.exp(cum)).astype(bf16)
        k_in = (k * jnp.exp(-cum)).astype(bf16)
        k_out = (k * jnp.exp(last - cum)).astype(bf16)
        dec = jnp.exp(last)
        vb = v_ref[0, rows, :].astype(bf16)
        for h in range(GLA_HEADS):
            sl = slice(h * GLA_DK, (h + 1) * GLA_DK)
            att = lax.dot_general(q_in[:, sl], k_in[:, sl], nt, preferred_element_type=f32)
            att = jnp.where(keep, att, 0.0).astype(bf16)
            st = st_ref[h]
            o_h = (jnp.dot(att, vb[:, sl], preferred_element_type=f32)
                   + lax.dot_general(q_in[:, sl], st.astype(bf16), nt, preferred_element_type=f32))
            st_ref[h] = st * dec[:, sl] + lax.dot_general(vb[:, sl], k_out[:, sl], tn,
                                                          preferred_element_type=f32)
            o_ref[0, rows, sl] = o_h
    fin_ref[0] = st_ref[...]


def gla_dir(p, gd, w_up, b_up, s0, reverse):
    nb, n_tok, _ = p.shape
    tb = min(GLA_BLOCK, n_tok)
    nblk = n_tok // tb
    blk = (lambda k: nblk - 1 - k) if reverse else (lambda k: k)
    col = lambda c: pl.BlockSpec((1, tb, W_MIX), lambda b, k: (b, blk(k), c))
    st_shape = (GLA_HEADS, GLA_DV, GLA_DK)
    return pl.pallas_call(
        functools.partial(_gla_kernel, reverse=reverse, tb=tb),
        grid=(nb, nblk),
        in_specs=[col(0), col(1), col(2),
                  pl.BlockSpec((1, tb, GLA_RANK), lambda b, k: (b, blk(k), 0)),
                  pl.BlockSpec((GLA_RANK, W_MIX), lambda b, k: (0, 0)),
                  pl.BlockSpec((1, W_MIX), lambda b, k: (0, 0)),
                  pl.BlockSpec((1,) + st_shape, lambda b, k: (b, 0, 0, 0))],
        out_specs=[pl.BlockSpec((1, tb, W_MIX), lambda b, k: (b, blk(k), 0)),
                   pl.BlockSpec((1,) + st_shape, lambda b, k: (b, 0, 0, 0))],
        out_shape=[jax.ShapeDtypeStruct((nb, n_tok, W_MIX), jnp.float32),
                   jax.ShapeDtypeStruct((nb,) + st_shape, jnp.float32)],
        scratch_shapes=[pltpu.VMEM(st_shape, jnp.float32)],
        compiler_params=pltpu.CompilerParams(dimension_semantics=("arbitrary", "arbitrary")),
    )(p, p, p, gd, w_up, b_up.reshape(1, W_MIX), s0)


FFN_TM = 512


def _swiglu_kernel(te_ref, nu_ref, x_ref, wg_ref, wu_ref, wd_ref, o_ref):
    del te_ref
    j = pl.program_id(1)

    @pl.when(pl.program_id(0) < nu_ref[0])
    def _():
        x = x_ref[...]
        g = jnp.dot(x, wg_ref[0], preferred_element_type=jnp.float32)
        u = jnp.dot(x, wu_ref[0], preferred_element_type=jnp.float32)
        a = (g * jax.nn.sigmoid(g) * u).astype(jnp.bfloat16)
        part = jnp.dot(a, wd_ref[0], preferred_element_type=jnp.float32)

        @pl.when(j == 0)
        def _():
            o_ref[...] = part

        @pl.when(j > 0)
        def _():
            o_ref[...] += part


def grouped_swiglu(tile_expert, n_used, xs, w_gu, w_down, nf):
    m, d = xs.shape
    f = w_down.shape[1]
    tf = f // nf
    n_tiles = m // FFN_TM
    grid_spec = pltpu.PrefetchScalarGridSpec(
        num_scalar_prefetch=2,
        grid=(n_tiles, nf),
        in_specs=[pl.BlockSpec((FFN_TM, d), lambda t, j, te, nu: (t, 0)),
                  pl.BlockSpec((1, d, tf), lambda t, j, te, nu: (te[t], 0, j)),
                  pl.BlockSpec((1, d, tf), lambda t, j, te, nu: (te[t], 0, nf + j)),
                  pl.BlockSpec((1, tf, d), lambda t, j, te, nu: (te[t], j, 0))],
        out_specs=pl.BlockSpec((FFN_TM, d), lambda t, j, te, nu: (t, 0)))
    return pl.pallas_call(
        _swiglu_kernel,
        grid_spec=grid_spec,
        out_shape=jax.ShapeDtypeStruct((m, d), jnp.float32),
        compiler_params=pltpu.CompilerParams(
            dimension_semantics=("arbitrary", "arbitrary"),
            vmem_limit_bytes=VMEM_LIMIT_BYTES),
    )(tile_expert, n_used, xs, w_gu, w_gu, w_down)


def rmsnorm(x, g):
    y = x * lax.rsqrt(jnp.mean(x * x, axis=-1, keepdims=True) + EPS)
    return y * g


def modulate(h, shift, scale):
    return h * (1 + scale) + shift


def adaln(cond, w, b):
    return jax.nn.silu(cond) @ w + b


def split_cols(p):
    out, start = [], 0
    for n in IN_SIZES:
        out.append(p[..., start:start + n])
        start += n
    return out


def grid_pos_embed(n_tokens, dim):
    rows = n_tokens // GRID_W
    q = dim // 4
    omega = 1.0 / (10000.0 ** (jnp.arange(q, dtype=jnp.float32) / q))
    r = jnp.arange(rows, dtype=jnp.float32)[:, None] * omega
    cc = jnp.arange(GRID_W, dtype=jnp.float32)[:, None] * omega
    er = jnp.concatenate([jnp.sin(r), jnp.cos(r)], axis=-1)
    ec = jnp.concatenate([jnp.sin(cc), jnp.cos(cc)], axis=-1)
    emb = jnp.concatenate([jnp.broadcast_to(er[:, None], (rows, GRID_W, dim // 2)),
                           jnp.broadcast_to(ec[None], (rows, GRID_W, dim // 2))], axis=-1)
    return emb.reshape(rows * GRID_W, dim)


def dwconv(x, w, pad_l, pad_r):
    n = x.shape[1]
    xp = jnp.pad(x, ((0, 0), (pad_l, pad_r), (0, 0)))
    return sum(xp[:, k:k + n] * w[k] for k in range(w.shape[0]))


def gla_mixer(p, og, gdn, w_up, b_up, s0):
    b_, n_tok, _ = p.shape
    g_dirs = jnp.split(gdn, 2, axis=-1)
    o = 0.0
    finals = []
    for d in range(2):
        od, sd = gla_dir(p, g_dirs[d], w_up[d], b_up[d], s0[d], d == 1)
        o = o + od
        finals.append(sd)
    o = o.reshape(b_, n_tok, GLA_HEADS, GLA_DV)
    o = o * lax.rsqrt(jnp.mean(o * o, axis=-1, keepdims=True) + EPS)
    o = o.reshape(b_, n_tok, GLA_HEADS * GLA_DV)
    return o * jax.nn.silu(og), jnp.stack(finals)


def s5_mixer(u, prep, w_glu, b_glu, s0):
    wd, wk, wc, a_t = prep
    b_, n_tok, _ = u.shape
    n = n_tok // S5_T
    u2 = u.reshape(b_, n, S5_T * W_MIX).transpose(1, 0, 2).reshape(n * b_, S5_T * W_MIX)
    dmat = pmm(u2, wd)
    hmat, fin = s5_scan(dmat.reshape(n, b_, 4 * S5_NS), s0, a_t)
    y2 = pmm_multi([u2, hmat.reshape(n * b_, 4 * S5_NS)], [wk, wc])
    y = y2.reshape(n, b_, S5_T, W_MIX).transpose(1, 0, 2, 3).reshape(b_, n_tok, W_MIX)
    g = jax.nn.gelu(y)
    out = g * jax.nn.sigmoid(pmm(g, w_glu) + b_glu)
    return out, fin


def hyena_filters(n_tok, w1, b1, w2, b2, w3, freq):
    f32 = jnp.float32
    t = jnp.arange(n_tok, dtype=f32)[:, None]
    bands = jnp.linspace(1e-4, HY_BANDS - 1, HY_BANDS, dtype=f32)[None]
    ang = 2.0 * math.pi * bands * t / n_tok
    z = jnp.concatenate([t / n_tok, jnp.cos(ang), jnp.sin(ang)], axis=-1)
    h = jnp.sin(freq * (z @ w1 + b1))
    h = jnp.sin(freq * (h @ w2 + b2))
    h = h @ w3
    t01 = t / max(n_tok - 1, 1)
    deltas = jnp.abs(jnp.linspace(math.log(HY_TARGET) / HY_DECAY_SHORT,
                                  math.log(HY_TARGET) / HY_DECAY_LONG, W_MIX, dtype=f32))
    h = h * jnp.exp(-t01 * jnp.tile(deltas, 2))
    return h / (jnp.sum(jnp.abs(h), axis=0, keepdims=True) + EPS)


def hyena_mixer(p, w_short, w1, b1, w2, b2, w3, freq, bias):
    n_tok = p.shape[1]
    pc = dwconv(p, w_short, 1, 1)
    v, x0, x1 = jnp.split(pc, 3, axis=-1)
    hf, hb = jnp.split(hyena_filters(n_tok, w1, b1, w2, b2, w3, freq), 2, axis=-1)
    n_fft = 2 * n_tok
    h_freq = jnp.fft.rfft(hf, n=n_fft, axis=0) + jnp.conj(jnp.fft.rfft(hb, n=n_fft, axis=0))
    z = x1 * v
    conv = jnp.fft.irfft(jnp.fft.rfft(z, n=n_fft, axis=1) * h_freq, n=n_fft, axis=1)[:, :n_tok]
    return x0 * (conv + z * bias)


def rglru_mixer(xr, gate, w_conv, w_a, b_a, w_x, b_x, lam, s0):
    b_, n_tok, _ = xr.shape
    xc = dwconv(xr, w_conv, 2, 1)
    xb = xc.reshape(b_, n_tok, RG_BLOCKS, RG_BLOCK)
    y = 0.0
    finals = []
    for d in range(2):
        r = jax.nn.sigmoid(jnp.einsum('blhi,hij->blhj', xb, w_a[d]) + b_a[d])
        i = jax.nn.sigmoid(jnp.einsum('blhi,hij->blhj', xb, w_x[d]) + b_x[d])
        log_a = -RG_C * r * jax.nn.softplus(-lam[d].reshape(RG_BLOCKS, RG_BLOCK))
        a = jnp.exp(log_a)
        bt = jnp.sqrt(-jnp.expm1(2.0 * log_a)) * (i * xb)
        h, fin = rg_scan(a.reshape(b_, n_tok, W_MIX), bt.reshape(b_, n_tok, W_MIX), s0[d], d == 1)
        finals.append(fin)
        y = y + h
    y = y * jax.nn.gelu(gate)
    return y, jnp.stack(finals)


def merge_branches(branches, gates, w_branch, w_out):
    gk = jnp.split(gates, N_BRANCH, axis=-1)
    y = 0.0
    for k in range(N_BRANCH):
        y = y + jax.nn.sigmoid(gk[k]) * pmm(branches[k], w_branch[k])
    return pmm(y, w_out)


def swiglu(h, w_gu, w_down):
    shp = h.shape
    ht = h.reshape(-1, shp[-1]).astype(jnp.bfloat16)
    n_tiles = ht.shape[0] // FFN_TM
    y = grouped_swiglu(jnp.zeros((n_tiles,), jnp.int32), jnp.full((1,), n_tiles, jnp.int32), ht,
                       w_gu[None].astype(jnp.bfloat16), w_down[None].astype(jnp.bfloat16), nf=2)
    return y.reshape(shp)


def moe_swiglu(h, router, router_b, w_gu, w_down):
    shp = h.shape
    ht = h.reshape(-1, shp[-1])
    n_tok = ht.shape[0]
    n_slot = TOP_K * n_tok
    logits = jnp.dot(ht, router, precision=lax.Precision.HIGHEST) + router_b
    top_v, top_i = lax.top_k(logits, TOP_K)
    w = jax.nn.softmax(top_v, axis=-1)
    e_flat = top_i.reshape(-1).astype(jnp.int32)
    onehot = (e_flat[:, None] == jnp.arange(N_EXPERTS, dtype=jnp.int32)[None]).astype(jnp.int32)
    csum = jnp.cumsum(onehot, axis=0)
    cnt = csum[-1]
    rank = jnp.sum(csum * onehot, axis=1) - 1
    padded = ((cnt + FFN_TM - 1) // FFN_TM) * FFN_TM
    ends = jnp.cumsum(padded)
    dest = (ends - padded)[e_flat] + rank
    n_rows = n_slot + N_EXPERTS * FFN_TM
    n_tiles = n_rows // FFN_TM
    src = jnp.zeros((n_rows,), jnp.int32).at[dest].set(jnp.arange(n_slot, dtype=jnp.int32) // TOP_K)
    tile_start = jnp.arange(n_tiles, dtype=jnp.int32) * FFN_TM
    tile_expert = jnp.minimum(jnp.sum((tile_start[:, None] >= ends[None, :]).astype(jnp.int32), axis=1),
                              N_EXPERTS - 1)
    n_used = (ends[-1:] // FFN_TM).astype(jnp.int32)
    xs = jnp.take(ht.astype(jnp.bfloat16), src, axis=0)
    ys = grouped_swiglu(tile_expert, n_used, xs, w_gu.astype(jnp.bfloat16),
                        w_down.astype(jnp.bfloat16), nf=2)
    yk = jnp.take(ys, dest, axis=0).reshape(n_tok, TOP_K, shp[-1])
    return jnp.sum(yk * w[..., None], axis=1).reshape(shp)


def channel_mix(h, layer, ffn_w_gu, ffn_w_down, moe_router, moe_router_b, moe_w_gu, moe_w_down):
    j = layer // 2
    if layer % 2 == 0:
        return swiglu(h, ffn_w_gu[j], ffn_w_down[j])
    return moe_swiglu(h, moe_router[j], moe_router_b[j], moe_w_gu[j], moe_w_down[j])


def kernel(x, c, ctx, c_ctx, mod_w, mod_b, norm1_g, norm2_g, w_in, gla_w_up, gla_b_up,
           s5_lam_re, s5_lam_im, s5_log_dt, s5_b_re, s5_b_im, s5_c_re, s5_c_im, s5_d,
           s5_w_glu, s5_b_glu, hy_w_short, hy_w1, hy_b1, hy_w2, hy_b2, hy_w3, hy_freq,
           hy_bias, rg_w_conv, rg_w_a, rg_b_a, rg_w_x, rg_b_x, rg_lam, w_branch, w_out,
           ffn_w_gu, ffn_w_down, moe_router, moe_router_b, moe_w_gu, moe_w_down, final_g):
    f32 = jnp.float32
    n_b, n_lat, _ = x.shape
    x = x + grid_pos_embed(n_lat, D_MODEL)[None]
    y_ctx = ctx
    for l in range(DEPTH):
        last = l == DEPTH - 1
        m_lat = jnp.split(adaln(c, mod_w[l], mod_b[l])[:, None, :], 6, axis=-1)
        m_ctx = jnp.split(adaln(c_ctx, mod_w[l], mod_b[l])[None, None, :], 6, axis=-1)

        h_lat = modulate(rmsnorm(x, norm1_g[l]), m_lat[0], m_lat[1])
        h_ctx = modulate(rmsnorm(y_ctx, norm1_g[l]), m_ctx[0], m_ctx[1])
        pf_lat = pmm(h_lat, w_in[l])
        pf_ctx = pmm(h_ctx, w_in[l])
        p_lat = split_cols(pf_lat)
        p_ctx = split_cols(pf_ctx)
        gla_p = (gla_w_up[l], gla_b_up[l])
        s5_p = (s5_prepare(s5_lam_re[l], s5_lam_im[l], s5_log_dt[l], s5_b_re[l], s5_b_im[l],
                           s5_c_re[l], s5_c_im[l], s5_d[l]), s5_w_glu[l], s5_b_glu[l])
        hy_p = (hy_w_short[l], hy_w1[l], hy_b1[l], hy_w2[l], hy_b2[l], hy_w3[l], hy_freq[l], hy_bias[l])
        rg_p = (rg_w_conv[l], rg_w_a[l], rg_b_a[l], rg_w_x[l], rg_b_x[l], rg_lam[l])

        gla_c, gla_state = gla_mixer(pf_ctx, p_ctx[3], p_ctx[4], *gla_p,
                                     jnp.zeros((2, n_b, GLA_HEADS, GLA_DV, GLA_DK), f32))
        gla_l, _ = gla_mixer(pf_lat, p_lat[3], p_lat[4], *gla_p, gla_state)
        s5_c, s5_state = s5_mixer(p_ctx[5], *s5_p,
                                  jnp.zeros((2, n_b, 2 * S5_NS), f32))
        s5_l, _ = s5_mixer(p_lat[5], *s5_p, s5_state)
        rg_c, rg_state = rglru_mixer(p_ctx[7], p_ctx[8], *rg_p,
                                     jnp.zeros((2, n_b, W_MIX), f32))
        rg_l, _ = rglru_mixer(p_lat[7], p_lat[8], *rg_p, rg_state)
        hy_l = hyena_mixer(p_lat[6], *hy_p)

        x = x + m_lat[2] * merge_branches((gla_l, s5_l, hy_l, rg_l), p_lat[9], w_branch[l], w_out[l])
        if not last:
            hy_c = hyena_mixer(p_ctx[6], *hy_p)
            y_ctx = y_ctx + m_ctx[2] * merge_branches((gla_c, s5_c, hy_c, rg_c), p_ctx[9],
                                                      w_branch[l], w_out[l])

        h2_lat = modulate(rmsnorm(x, norm2_g[l]), m_lat[3], m_lat[4])
        x = x + m_lat[5] * channel_mix(h2_lat, l, ffn_w_gu, ffn_w_down, moe_router,
                                       moe_router_b, moe_w_gu, moe_w_down)
        if not last:
            h2_ctx = modulate(rmsnorm(y_ctx, norm2_g[l]), m_ctx[3], m_ctx[4])
            y_ctx = y_ctx + m_ctx[5] * channel_mix(h2_ctx, l, ffn_w_gu, ffn_w_down, moe_router,
                                                   moe_router_b, moe_w_gu, moe_w_down)
    return rmsnorm(x, final_g)
```

```python
import functools
import math

import jax
import jax.numpy as jnp
from jax import lax
from jax.experimental import pallas as pl
from jax.experimental.pallas import tpu as pltpu

D_MODEL = 1024
DEPTH = 2
GRID_W = 64
EPS = 1e-6
N_BRANCH = 4
W_MIX = D_MODEL // N_BRANCH
GLA_HEADS = 4
GLA_DK = W_MIX // GLA_HEADS
GLA_DV = W_MIX // GLA_HEADS
GLA_RANK = 16
GLA_TAU = 16.0
GLA_CHUNK = 64
S5_GROUP = 16
S5_GROUPS = W_MIX // S5_GROUP
S5_STATE = 64
S5_MAX_RE = -1e-4
HY_BANDS = 16
HY_DECAY_SHORT = 0.3
HY_DECAY_LONG = 1.5
HY_TARGET = 1e-2
RG_BLOCKS = 4
RG_BLOCK = W_MIX // RG_BLOCKS
RG_C = 8.0
N_EXPERTS = 8
TOP_K = 2
IN_SIZES = (GLA_HEADS * GLA_DK, GLA_HEADS * GLA_DK, GLA_HEADS * GLA_DV, GLA_HEADS * GLA_DV,
            2 * GLA_RANK, W_MIX, 3 * W_MIX, W_MIX, W_MIX, N_BRANCH * D_MODEL)

VMEM_LIMIT_BYTES = 48 * 1024 * 1024


def _mm_kernel(x_ref, w_ref, o_ref):
    o_ref[...] = jnp.dot(x_ref[...].astype(jnp.bfloat16), w_ref[...],
                         preferred_element_type=jnp.float32)


def _pick_tile(n, cap):
    best = None
    for t in range(128, cap + 1, 128):
        if n % t == 0:
            best = t
    return best if best is not None else n


def pmm(x, w):
    lead = x.shape[:-1]
    k = x.shape[-1]
    n = w.shape[-1]
    x2 = x.reshape(-1, k)
    m = x2.shape[0]
    tm = 512 if m % 512 == 0 else m
    if k > 2048 and m % 256 == 0:
        tm = 256
    tn = n if k * n * 2 <= 6 * 1024 * 1024 else _pick_tile(n, 1024)
    out = pl.pallas_call(
        _mm_kernel,
        grid=(m // tm, n // tn),
        in_specs=[pl.BlockSpec((tm, k), lambda i, j: (i, 0)),
                  pl.BlockSpec((k, tn), lambda i, j: (0, j))],
        out_specs=pl.BlockSpec((tm, tn), lambda i, j: (i, j)),
        out_shape=jax.ShapeDtypeStruct((m, n), jnp.float32),
        compiler_params=pltpu.CompilerParams(
            dimension_semantics=("arbitrary", "arbitrary"),
            vmem_limit_bytes=VMEM_LIMIT_BYTES),
    )(x2, w.astype(jnp.bfloat16))
    return out.reshape(lead + (n,))


def _mm_multi_kernel(*refs):
    o_ref = refs[-1]
    n = (len(refs) - 1) // 2
    acc = None
    for i in range(n):
        t = jnp.dot(refs[i][...].astype(jnp.bfloat16), refs[n + i][...],
                    preferred_element_type=jnp.float32)
        acc = t if acc is None else acc + t
    o_ref[...] = acc


def pmm_multi(xs, ws, tm=256, tn=512):
    m = xs[0].shape[0]
    n = ws[0].shape[1]
    tm = tm if m % tm == 0 else m
    tn = tn if n % tn == 0 else n
    in_specs = ([pl.BlockSpec((tm, x.shape[1]), lambda i, j: (i, 0)) for x in xs]
                + [pl.BlockSpec((w.shape[0], tn), lambda i, j: (0, j)) for w in ws])
    return pl.pallas_call(
        _mm_multi_kernel,
        grid=(m // tm, n // tn),
        in_specs=in_specs,
        out_specs=pl.BlockSpec((tm, tn), lambda i, j: (i, j)),
        out_shape=jax.ShapeDtypeStruct((m, n), jnp.float32),
        compiler_params=pltpu.CompilerParams(
            dimension_semantics=("arbitrary", "arbitrary"),
            vmem_limit_bytes=VMEM_LIMIT_BYTES),
    )(*xs, *[w.astype(jnp.bfloat16) for w in ws])


RG_SCAN_ROWS = 256


def _rg_scan_kernel(a_ref, b_ref, s0_ref, h_ref, fin_ref, st_ref, *, reverse, tb, nb):
    @pl.when(pl.program_id(0) == 0)
    def _():
        st_ref[...] = s0_ref[...]

    def body(r, hs):
        rr = (tb - 1 - r) if reverse else r
        out = []
        for i in range(nb):
            h = a_ref[i, pl.ds(rr, 1), :] * hs[i] + b_ref[i, pl.ds(rr, 1), :]
            h_ref[i, pl.ds(rr, 1), :] = h
            out.append(h)
        return tuple(out)

    hs = lax.fori_loop(0, tb, body, tuple(st_ref[i:i + 1, :] for i in range(nb)), unroll=8)
    for i in range(nb):
        st_ref[i:i + 1, :] = hs[i]
        fin_ref[i:i + 1, :] = hs[i]


def rg_scan(a, b, s0, reverse):
    nb, n_tok, ch = a.shape
    tb = min(RG_SCAN_ROWS, n_tok)
    nblk = n_tok // tb
    imap = (lambda k: (0, nblk - 1 - k, 0)) if reverse else (lambda k: (0, k, 0))
    return pl.pallas_call(
        functools.partial(_rg_scan_kernel, reverse=reverse, tb=tb, nb=nb),
        grid=(nblk,),
        in_specs=[pl.BlockSpec((nb, tb, ch), imap), pl.BlockSpec((nb, tb, ch), imap),
                  pl.BlockSpec((nb, ch), lambda k: (0, 0))],
        out_specs=[pl.BlockSpec((nb, tb, ch), imap), pl.BlockSpec((nb, ch), lambda k: (0, 0))],
        out_shape=[jax.ShapeDtypeStruct((nb, n_tok, ch), jnp.float32),
                   jax.ShapeDtypeStruct((nb, ch), jnp.float32)],
        scratch_shapes=[pltpu.VMEM((nb, ch), jnp.float32)],
        compiler_params=pltpu.CompilerParams(dimension_semantics=("arbitrary",)),
    )(a, b, s0)


S5_T = 16
S5_NS = S5_GROUPS * S5_STATE
S5_SCAN_CHUNKS = 64


def _s5_scan_kernel(d_ref, s0_ref, a_ref, h_ref, fin_ref, st_ref, *, rc, nb):
    d = pl.program_id(0)

    @pl.when(pl.program_id(1) == 0)
    def _():
        st_ref[...] = s0_ref[0]

    ar = jnp.broadcast_to(a_ref[0, :, 0:S5_NS], (nb, S5_NS))
    ai = jnp.broadcast_to(a_ref[0, :, S5_NS:2 * S5_NS], (nb, S5_NS))

    def body(r, carry):
        hr, hi = carry
        rr = r + d * (rc - 1 - 2 * r)
        h_ref[rr, :, 0:S5_NS] = hr
        h_ref[rr, :, S5_NS:2 * S5_NS] = hi
        dr = d_ref[rr, :, 0:S5_NS]
        di = d_ref[rr, :, S5_NS:2 * S5_NS]
        return ar * hr - ai * hi + dr, ar * hi + ai * hr + di

    hr, hi = lax.fori_loop(0, rc, body, (st_ref[:, 0:S5_NS], st_ref[:, S5_NS:2 * S5_NS]))
    st_ref[:, 0:S5_NS] = hr
    st_ref[:, S5_NS:2 * S5_NS] = hi
    fin_ref[0, :, 0:S5_NS] = hr
    fin_ref[0, :, S5_NS:2 * S5_NS] = hi


def s5_scan(dmat, s0, a_t):
    n, nb, _ = dmat.shape
    rc = min(S5_SCAN_CHUNKS, n)
    nblk = n // rc
    w = 2 * S5_NS
    imap = lambda d, k: (k + d * (nblk - 1 - 2 * k), 0, d)
    return pl.pallas_call(
        functools.partial(_s5_scan_kernel, rc=rc, nb=nb),
        grid=(2, nblk),
        in_specs=[pl.BlockSpec((rc, nb, w), imap),
                  pl.BlockSpec((1, nb, w), lambda d, k: (d, 0, 0)),
                  pl.BlockSpec((1, 1, w), lambda d, k: (d, 0, 0))],
        out_specs=[pl.BlockSpec((rc, nb, w), imap),
                   pl.BlockSpec((1, nb, w), lambda d, k: (d, 0, 0))],
        out_shape=[jax.ShapeDtypeStruct((n, nb, 2 * w), jnp.float32),
                   jax.ShapeDtypeStruct((2, nb, w), jnp.float32)],
        scratch_shapes=[pltpu.VMEM((nb, w), jnp.float32)],
        compiler_params=pltpu.CompilerParams(dimension_semantics=("arbitrary", "arbitrary"),
                                             vmem_limit_bytes=VMEM_LIMIT_BYTES),
    )(dmat, s0, a_t)


def _cmul(ar, ai, br, bi):
    return ar * br - ai * bi, ar * bi + ai * br


def s5_prepare(lam_re, lam_im, log_dt, b_re, b_im, c_re, c_im, d_skip):
    f32 = jnp.float32
    hp = lax.Precision.HIGHEST
    t_len, g_n, p_n, h_n = S5_T, S5_GROUPS, S5_STATE, S5_GROUP
    eye_g = jnp.eye(g_n, dtype=f32)
    ar_t = jnp.arange(t_len)
    wd, wc, kk, a_t = [], [], [], []
    for d in range(2):
        lr = jnp.minimum(lam_re[d], S5_MAX_RE)
        li = lam_im[d]
        dt = jnp.exp(log_dt[d])[:, None]
        tt = jnp.arange(t_len + 1, dtype=f32)[:, None, None]
        mag = jnp.exp(lr * dt * tt)
        ang = li * dt * tt
        pr, pi = mag * jnp.cos(ang), mag * jnp.sin(ang)
        nr, ni = pr[1] - 1.0, pi[1]
        den = lr * lr + li * li
        qr, qi = (nr * lr + ni * li) / den, (ni * lr - nr * li) / den
        bbr, bbi = _cmul(qr[..., None], qi[..., None], b_re[d], b_im[d])
        cr, ci = c_re[d], c_im[d]

        idx = (t_len - 1 - ar_t) if d == 0 else ar_t
        wr, wi = _cmul(pr[idx][..., None], pi[idx][..., None], bbr[None], bbi[None])

        def place_d(w):
            wt = w.transpose(0, 3, 1, 2)
            full = wt[:, None] * eye_g[None, :, None, :, None]
            return full.reshape(t_len * W_MIX, g_n * p_n)

        wd.append(jnp.concatenate([place_d(wr), place_d(wi)], axis=1))

        idx2 = (ar_t + 1) if d == 0 else (t_len - ar_t)
        cwr, cwi = _cmul(cr[None], ci[None], pr[idx2][:, :, None, :], pi[idx2][:, :, None, :])

        def place_c(w):
            wt = w.transpose(1, 3, 0, 2)
            full = wt[:, :, :, None, :] * eye_g[:, None, None, :, None]
            return full.reshape(g_n * p_n, t_len * W_MIX)

        wc.append(jnp.concatenate([place_c(cwr), place_c(-cwi)], axis=0))

        er, ei = _cmul(pr[:t_len][:, :, None, :], pi[:t_len][:, :, None, :], cr[None], ci[None])
        kk.append(jnp.einsum('tghp,gpk->tghk', er, bbr, precision=hp)
                  - jnp.einsum('tghp,gpk->tghk', ei, bbi, precision=hp))
        a_t.append(jnp.concatenate([pr[t_len].reshape(1, -1), pi[t_len].reshape(1, -1)], axis=1))

    lag = ar_t[None, :] - ar_t[:, None]
    mf = jnp.where((lag >= 0)[..., None, None, None], kk[0][jnp.clip(lag, 0, t_len - 1)], 0.0)
    mb = jnp.where((lag <= 0)[..., None, None, None], kk[1][jnp.clip(-lag, 0, t_len - 1)], 0.0)
    skip = (jnp.eye(t_len, dtype=f32)[:, :, None, None, None] * d_skip[None, None, :, :, None]
            * jnp.eye(h_n, dtype=f32)[None, None, None])
    m = mf + mb + skip
    mt = m.transpose(0, 4, 1, 2, 3)
    full = mt[:, None, :, :, :, :] * eye_g[None, :, None, None, :, None]
    wk = full.reshape(t_len * W_MIX, t_len * W_MIX)
    return (jnp.concatenate(wd, axis=1), wk, jnp.concatenate(wc, axis=0), jnp.stack(a_t))


GLA_BLOCK = 512


def _gla_kernel(q_ref, k_ref, v_ref, g_ref, wup_ref, bup_ref, s0_ref, o_ref, fin_ref, st_ref,
                *, reverse, tb):
    f32, bf16 = jnp.float32, jnp.bfloat16
    hp = lax.Precision.HIGHEST
    cc = GLA_CHUNK

    @pl.when(pl.program_id(1) == 0)
    def _():
        st_ref[...] = s0_ref[0]

    r_i = lax.broadcasted_iota(jnp.int32, (cc, cc), 0)
    c_i = lax.broadcasted_iota(jnp.int32, (cc, cc), 1)
    keep = (c_i >= r_i) if reverse else (c_i <= r_i)
    tri = keep.astype(f32)
    nt = (((1,), (1,)), ((), ()))
    tn = (((0,), (0,)), ((), ()))
    n_ch = tb // cc
    for ci in (range(n_ch - 1, -1, -1) if reverse else range(n_ch)):
        rows = slice(ci * cc, (ci + 1) * cc)
        z = jnp.dot(g_ref[0, rows, :], wup_ref[...], precision=hp,
                    preferred_element_type=f32) + bup_ref[...]
        la = (jnp.minimum(z, 0.0) - jnp.log1p(jnp.exp(-jnp.abs(z)))) * (1.0 / GLA_TAU)
        cum = jnp.dot(tri, la, precision=hp, preferred_element_type=f32)
        last = cum[0:1, :] if reverse else cum[cc - 1:cc, :]
        k = k_ref[0, rows, :]
        q_in = (q_ref[0, rows, :] * (GLA_DK ** -0.5) * jnp.exp(cum)).astype(bf16)
        k_in = (k * jnp.exp(-cum)).astype(bf16)
        k_out = (k * jnp.exp(last - cum)).astype(bf16)
        dec = jnp.exp(last)
        vb = v_ref[0, rows, :].astype(bf16)
        for h in range(GLA_HEADS):
            sl = slice(h * GLA_DK, (h + 1) * GLA_DK)
            att = lax.dot_general(q_in[:, sl], k_in[:, sl], nt, preferred_element_type=f32)
            att = jnp.where(keep, att, 0.0).astype(bf16)
            st = st_ref[h]
            o_h = (jnp.dot(att, vb[:, sl], preferred_element_type=f32)
                   + lax.dot_general(q_in[:, sl], st.astype(bf16), nt, preferred_element_type=f32))
            st_ref[h] = st * dec[:, sl] + lax.dot_general(vb[:, sl], k_out[:, sl], tn,
                                                          preferred_element_type=f32)
            o_ref[0, rows, sl] = o_h
    fin_ref[0] = st_ref[...]


def gla_dir(p, gd, w_up, b_up, s0, reverse):
    nb, n_tok, _ = p.shape
    tb = min(GLA_BLOCK, n_tok)
    nblk = n_tok // tb
    blk = (lambda k: nblk - 1 - k) if reverse else (lambda k: k)
    col = lambda c: pl.BlockSpec((1, tb, W_MIX), lambda b, k: (b, blk(k), c))
    st_shape = (GLA_HEADS, GLA_DV, GLA_DK)
    return pl.pallas_call(
        functools.partial(_gla_kernel, reverse=reverse, tb=tb),
        grid=(nb, nblk),
        in_specs=[col(P_Q), col(P_Q + 1), col(P_Q + 2),
                  pl.BlockSpec((1, tb, GLA_RANK), lambda b, k: (b, blk(k), 0)),
                  pl.BlockSpec((GLA_RANK, W_MIX), lambda b, k: (0, 0)),
                  pl.BlockSpec((1, W_MIX), lambda b, k: (0, 0)),
                  pl.BlockSpec((1,) + st_shape, lambda b, k: (b, 0, 0, 0))],
        out_specs=[pl.BlockSpec((1, tb, W_MIX), lambda b, k: (b, blk(k), 0)),
                   pl.BlockSpec((1,) + st_shape, lambda b, k: (b, 0, 0, 0))],
        out_shape=[jax.ShapeDtypeStruct((nb, n_tok, W_MIX), jnp.float32),
                   jax.ShapeDtypeStruct((nb,) + st_shape, jnp.float32)],
        scratch_shapes=[pltpu.VMEM(st_shape, jnp.float32)],
        compiler_params=pltpu.CompilerParams(dimension_semantics=("arbitrary", "arbitrary")),
    )(p, p, p, gd, w_up, b_up.reshape(1, W_MIX), s0)


FFN_TM = 512


def _swiglu_kernel(te_ref, nu_ref, x_ref, wg_ref, wu_ref, wd_ref, *rest, nf, residual):
    del te_ref
    o_ref = rest[-1]
    j = pl.program_id(1)

    @pl.when(pl.program_id(0) < nu_ref[0])
    def _():
        x = x_ref[...]
        g = jnp.dot(x, wg_ref[0], preferred_element_type=jnp.float32)
        u = jnp.dot(x, wu_ref[0], preferred_element_type=jnp.float32)
        a = (g * jax.nn.sigmoid(g) * u).astype(jnp.bfloat16)
        part = jnp.dot(a, wd_ref[0], preferred_element_type=jnp.float32)

        @pl.when(j == 0)
        def _():
            o_ref[...] = part

        @pl.when(j > 0)
        def _():
            o_ref[...] += part

        if residual:
            xres_ref, gm_ref = rest[:2]

            @pl.when(j == nf - 1)
            def _():
                o_ref[...] = xres_ref[...] + gm_ref[0] * o_ref[...]


def grouped_swiglu(tile_expert, n_used, xs, w_gu, w_down, nf, residual=None):
    m, d = xs.shape
    f = w_down.shape[1]
    tf = f // nf
    n_tiles = m // FFN_TM
    in_specs = [pl.BlockSpec((FFN_TM, d), lambda t, j, te, nu: (t, 0)),
                pl.BlockSpec((1, d, tf), lambda t, j, te, nu: (te[t], 0, j)),
                pl.BlockSpec((1, d, tf), lambda t, j, te, nu: (te[t], 0, nf + j)),
                pl.BlockSpec((1, tf, d), lambda t, j, te, nu: (te[t], j, 0))]
    extra = ()
    if residual is not None:
        rows_per_mod = m // residual[1].shape[0]
        in_specs += [pl.BlockSpec((FFN_TM, d), lambda t, j, te, nu: (t, 0)),
                     pl.BlockSpec((1, 1, d), lambda t, j, te, nu: ((t * FFN_TM) // rows_per_mod, 0, 0))]
        extra = tuple(residual)
    grid_spec = pltpu.PrefetchScalarGridSpec(
        num_scalar_prefetch=2,
        grid=(n_tiles, nf),
        in_specs=in_specs,
        out_specs=pl.BlockSpec((FFN_TM, d), lambda t, j, te, nu: (t, 0)))
    return pl.pallas_call(
        functools.partial(_swiglu_kernel, nf=nf, residual=residual is not None),
        grid_spec=grid_spec,
        out_shape=jax.ShapeDtypeStruct((m, d), jnp.float32),
        compiler_params=pltpu.CompilerParams(
            dimension_semantics=("arbitrary", "arbitrary"),
            vmem_limit_bytes=VMEM_LIMIT_BYTES),
        name="grouped_swiglu",
    )(tile_expert, n_used, xs, w_gu, w_gu, w_down, *extra)


ROW_TILE = 512
P_HY, P_Q, P_OG, P_S5, P_RGX, P_RGG = 0, 3, 6, 7, 8, 9
P_GDN = 10 * W_MIX
P_COLS = P_GDN + 128


def _rms_mod(x, g, shift, scale):
    y = x * lax.rsqrt(jnp.mean(x * x, axis=-1, keepdims=True) + EPS) * g
    return y * (1.0 + scale) + shift


def _front_kernel(x_ref, sh_ref, sc_ref, g_ref, w_ref, o_ref):
    h = _rms_mod(x_ref[...], g_ref[...], sh_ref[0], sc_ref[0])
    o_ref[...] = jnp.dot(h.astype(jnp.bfloat16), w_ref[...], preferred_element_type=jnp.float32)


def _mod_spec(rows_per_mod):
    return pl.BlockSpec((1, 1, D_MODEL), lambda i: ((i * ROW_TILE) // rows_per_mod, 0, 0))


def front(x, shift, scale, g, w):
    m = x.shape[0]
    n_mod = shift.shape[0]
    n_out = w.shape[1]
    mod = _mod_spec(m // n_mod)
    return pl.pallas_call(
        _front_kernel,
        grid=(m // ROW_TILE,),
        in_specs=[pl.BlockSpec((ROW_TILE, D_MODEL), lambda i: (i, 0)), mod, mod,
                  pl.BlockSpec((1, D_MODEL), lambda i: (0, 0)),
                  pl.BlockSpec((D_MODEL, n_out), lambda i: (0, 0))],
        out_specs=pl.BlockSpec((ROW_TILE, n_out), lambda i: (i, 0)),
        out_shape=jax.ShapeDtypeStruct((m, n_out), jnp.float32),
        compiler_params=pltpu.CompilerParams(dimension_semantics=("arbitrary",),
                                             vmem_limit_bytes=VMEM_LIMIT_BYTES),
        name="front",
    )(x, shift, scale, g.reshape(1, D_MODEL), w)


MERGE_TILE = 256
ROUTER_PAD = 128


def _merge_kernel(*refs, with_router):
    (x_ref, sh1_ref, sc1_ref, gm_ref, sh2_ref, sc2_ref, g1_ref, g2_ref,
     of_ref, ob_ref, og_ref, s5_ref, hy_ref, rf_ref, rb_ref, rgg_ref,
     wg_ref, wb_ref, wo_ref, wglu_ref, bglu_ref, havg_ref) = refs[:22]
    f32, bf16 = jnp.float32, jnp.bfloat16
    hp = lax.Precision.HIGHEST
    x = x_ref[...]
    hb = _rms_mod(x, g1_ref[...], sh1_ref[0], sc1_ref[0]).astype(bf16)

    o = of_ref[...] + ob_ref[...]
    ms = jnp.dot(o * o, havg_ref[...], precision=hp, preferred_element_type=f32)
    og = og_ref[...]
    gla = o * lax.rsqrt(ms + EPS) * (og * jax.nn.sigmoid(og))
    g5 = jax.nn.gelu(s5_ref[...])
    s5o = g5 * jax.nn.sigmoid(jnp.dot(g5.astype(bf16), wglu_ref[...], preferred_element_type=f32)
                              + bglu_ref[...])
    rgo = (rf_ref[...] + rb_ref[...]) * jax.nn.gelu(rgg_ref[...])
    branches = (gla, s5o, hy_ref[...], rgo)

    y = None
    for k in range(N_BRANCH):
        gate = jax.nn.sigmoid(jnp.dot(hb, wg_ref[:, k * D_MODEL:(k + 1) * D_MODEL],
                                      preferred_element_type=f32))
        t = gate * jnp.dot(branches[k].astype(bf16), wb_ref[k], preferred_element_type=f32)
        y = t if y is None else y + t
    out = jnp.dot(y.astype(bf16), wo_ref[...], preferred_element_type=f32)
    xn = x + gm_ref[0] * out
    h2 = _rms_mod(xn, g2_ref[...], sh2_ref[0], sc2_ref[0])
    if with_router:
        rw_ref, rb2_ref, xo_ref, h2_ref, lg_ref = refs[22:]
        lg_ref[...] = jnp.dot(h2, rw_ref[...], precision=hp, preferred_element_type=f32) + rb2_ref[...]
    else:
        xo_ref, h2_ref = refs[22:]
    xo_ref[...] = xn
    h2_ref[...] = h2.astype(bf16)


def merge(x, mods, g1, g2, o_f, o_b, p, s5y, hy, r_f, r_b, wg, wb, wo, wglu, bglu, router=None):
    m = x.shape[0]
    tm = MERGE_TILE
    n_mod = mods[0].shape[0]
    rows_per_mod = m // n_mod
    mod = pl.BlockSpec((1, 1, D_MODEL), lambda i: ((i * tm) // rows_per_mod, 0, 0))
    row = pl.BlockSpec((tm, D_MODEL), lambda i: (i, 0))
    br = pl.BlockSpec((tm, W_MIX), lambda i: (i, 0))
    pcol = lambda c: pl.BlockSpec((tm, W_MIX), lambda i: (i, c))
    full = lambda a: pl.BlockSpec(a.shape, lambda i: (0,) * a.ndim)
    head = jnp.arange(W_MIX) // GLA_DV
    havg = (head[:, None] == head[None, :]).astype(jnp.float32) / GLA_DV
    vec = lambda v: v.reshape(1, -1)
    consts = [wg, wb, wo, wglu, vec(bglu), havg]
    out_specs = [row, pl.BlockSpec((tm, D_MODEL), lambda i: (i, 0))]
    out_shape = [jax.ShapeDtypeStruct((m, D_MODEL), jnp.float32),
                 jax.ShapeDtypeStruct((m, D_MODEL), jnp.bfloat16)]
    if router is not None:
        rw, rbias = router
        pad = ROUTER_PAD - rw.shape[1]
        consts += [jnp.pad(rw, ((0, 0), (0, pad))), jnp.pad(rbias, (0, pad)).reshape(1, -1)]
        out_specs.append(pl.BlockSpec((tm, ROUTER_PAD), lambda i: (i, 0)))
        out_shape.append(jax.ShapeDtypeStruct((m, ROUTER_PAD), jnp.float32))
    in_specs = ([row] + [mod] * 5 + [full(vec(g1)), full(vec(g2)), br, br, pcol(P_OG), br, br, br, br,
                                     pcol(P_RGG)] + [full(a) for a in consts])
    return pl.pallas_call(
        functools.partial(_merge_kernel, with_router=router is not None),
        grid=(m // tm,),
        in_specs=in_specs,
        out_specs=out_specs,
        out_shape=out_shape,
        compiler_params=pltpu.CompilerParams(dimension_semantics=("arbitrary",),
                                             vmem_limit_bytes=VMEM_LIMIT_BYTES),
        name="merge",
    )(x, *mods, vec(g1), vec(g2), o_f, o_b, p, s5y, hy, r_f, r_b, p, *consts)


def _combine_kernel(x_ref, y_ref, w_ref, gm_ref, g_ref, o_ref, *, final_norm):
    w = w_ref[...]
    y = w[:, 0:1] * y_ref[:, 0:D_MODEL] + w[:, 1:2] * y_ref[:, D_MODEL:2 * D_MODEL]
    xn = x_ref[...] + gm_ref[0] * y
    if final_norm:
        xn = xn * lax.rsqrt(jnp.mean(xn * xn, axis=-1, keepdims=True) + EPS) * g_ref[...]
    o_ref[...] = xn


def moe_combine(x, yk, w, gate_mod, final_g):
    m = x.shape[0]
    n_mod = gate_mod.shape[0]
    g = jnp.ones((1, D_MODEL), jnp.float32) if final_g is None else final_g.reshape(1, D_MODEL)
    return pl.pallas_call(
        functools.partial(_combine_kernel, final_norm=final_g is not None),
        grid=(m // ROW_TILE,),
        in_specs=[pl.BlockSpec((ROW_TILE, D_MODEL), lambda i: (i, 0)),
                  pl.BlockSpec((ROW_TILE, TOP_K * D_MODEL), lambda i: (i, 0)),
                  pl.BlockSpec((ROW_TILE, TOP_K), lambda i: (i, 0)),
                  _mod_spec(m // n_mod),
                  pl.BlockSpec((1, D_MODEL), lambda i: (0, 0))],
        out_specs=pl.BlockSpec((ROW_TILE, D_MODEL), lambda i: (i, 0)),
        out_shape=jax.ShapeDtypeStruct((m, D_MODEL), jnp.float32),
        compiler_params=pltpu.CompilerParams(dimension_semantics=("arbitrary",)),
        name="moe_combine",
    )(x, yk, w, gate_mod, g)


def rmsnorm(x, g):
    y = x * lax.rsqrt(jnp.mean(x * x, axis=-1, keepdims=True) + EPS)
    return y * g


def adaln(cond, w, b):
    return jax.nn.silu(cond) @ w + b


def grid_pos_embed(n_tokens, dim):
    rows = n_tokens // GRID_W
    q = dim // 4
    omega = 1.0 / (10000.0 ** (jnp.arange(q, dtype=jnp.float32) / q))
    r = jnp.arange(rows, dtype=jnp.float32)[:, None] * omega
    cc = jnp.arange(GRID_W, dtype=jnp.float32)[:, None] * omega
    er = jnp.concatenate([jnp.sin(r), jnp.cos(r)], axis=-1)
    ec = jnp.concatenate([jnp.sin(cc), jnp.cos(cc)], axis=-1)
    emb = jnp.concatenate([jnp.broadcast_to(er[:, None], (rows, GRID_W, dim // 2)),
                           jnp.broadcast_to(ec[None], (rows, GRID_W, dim // 2))], axis=-1)
    return emb.reshape(rows * GRID_W, dim)


def dwconv(x, w, pad_l, pad_r):
    n = x.shape[1]
    xp = jnp.pad(x, ((0, 0), (pad_l, pad_r), (0, 0)))
    return sum(xp[:, k:k + n] * w[k] for k in range(w.shape[0]))


def gla_mixer(p, w_up, b_up, s0):
    gdn = p[..., P_GDN:P_GDN + 2 * GLA_RANK]
    outs, finals = [], []
    for d in range(2):
        od, sd = gla_dir(p, gdn[..., d * GLA_RANK:(d + 1) * GLA_RANK], w_up[d], b_up[d], s0[d], d == 1)
        outs.append(od)
        finals.append(sd)
    return outs, jnp.stack(finals)


def s5_mixer(u, prep, s0):
    wd, wk, wc, a_t = prep
    b_, n_tok, _ = u.shape
    n = n_tok // S5_T
    u2 = u.reshape(b_, n, S5_T * W_MIX).transpose(1, 0, 2).reshape(n * b_, S5_T * W_MIX)
    dmat = pmm(u2, wd)
    hmat, fin = s5_scan(dmat.reshape(n, b_, 4 * S5_NS), s0, a_t)
    y2 = pmm_multi([u2, hmat.reshape(n * b_, 4 * S5_NS)], [wk, wc])
    y = y2.reshape(n, b_, S5_T, W_MIX).transpose(1, 0, 2, 3).reshape(b_, n_tok, W_MIX)
    return y, fin


def hyena_filters(n_tok, w1, b1, w2, b2, w3, freq):
    f32 = jnp.float32
    t = jnp.arange(n_tok, dtype=f32)[:, None]
    bands = jnp.linspace(1e-4, HY_BANDS - 1, HY_BANDS, dtype=f32)[None]
    ang = 2.0 * math.pi * bands * t / n_tok
    z = jnp.concatenate([t / n_tok, jnp.cos(ang), jnp.sin(ang)], axis=-1)
    h = jnp.sin(freq * (z @ w1 + b1))
    h = jnp.sin(freq * (h @ w2 + b2))
    h = h @ w3
    t01 = t / max(n_tok - 1, 1)
    deltas = jnp.abs(jnp.linspace(math.log(HY_TARGET) / HY_DECAY_SHORT,
                                  math.log(HY_TARGET) / HY_DECAY_LONG, W_MIX, dtype=f32))
    h = h * jnp.exp(-t01 * jnp.tile(deltas, 2))
    return h / (jnp.sum(jnp.abs(h), axis=0, keepdims=True) + EPS)


def hyena_mixer(p, w_short, w1, b1, w2, b2, w3, freq, bias):
    n_tok = p.shape[1]
    pc = dwconv(p, w_short, 1, 1)
    v, x0, x1 = jnp.split(pc, 3, axis=-1)
    hf, hb = jnp.split(hyena_filters(n_tok, w1, b1, w2, b2, w3, freq), 2, axis=-1)
    n_fft = 2 * n_tok
    h_freq = jnp.fft.rfft(hf, n=n_fft, axis=0) + jnp.conj(jnp.fft.rfft(hb, n=n_fft, axis=0))
    z = x1 * v
    conv = jnp.fft.irfft(jnp.fft.rfft(z, n=n_fft, axis=1) * h_freq, n=n_fft, axis=1)[:, :n_tok]
    return x0 * (conv + z * bias)


def rglru_mixer(xr, w_conv, w_a, b_a, w_x, b_x, lam, s0):
    b_, n_tok, _ = xr.shape
    xc = dwconv(xr, w_conv, 2, 1)
    xb = xc.reshape(b_, n_tok, RG_BLOCKS, RG_BLOCK)
    outs = []
    finals = []
    for d in range(2):
        r = jax.nn.sigmoid(jnp.einsum('blhi,hij->blhj', xb, w_a[d]) + b_a[d])
        i = jax.nn.sigmoid(jnp.einsum('blhi,hij->blhj', xb, w_x[d]) + b_x[d])
        log_a = -RG_C * r * jax.nn.softplus(-lam[d].reshape(RG_BLOCKS, RG_BLOCK))
        a = jnp.exp(log_a)
        bt = jnp.sqrt(-jnp.expm1(2.0 * log_a)) * (i * xb)
        h, fin = rg_scan(a.reshape(b_, n_tok, W_MIX), bt.reshape(b_, n_tok, W_MIX), s0[d], d == 1)
        finals.append(fin)
        outs.append(h)
    return outs, jnp.stack(finals)


def ffn_dense(h2, x, gate_mod, w_gu, w_down):
    n_tiles = h2.shape[0] // FFN_TM
    return grouped_swiglu(jnp.zeros((n_tiles,), jnp.int32), jnp.full((1,), n_tiles, jnp.int32), h2,
                          w_gu[None].astype(jnp.bfloat16), w_down[None].astype(jnp.bfloat16), nf=2,
                          residual=(x, gate_mod))


def ffn_moe(h2, logits, x, gate_mod, w_gu, w_down, final_g):
    n_tok = h2.shape[0]
    n_slot = TOP_K * n_tok
    top_v, top_i = lax.top_k(logits, TOP_K)
    w = jax.nn.softmax(top_v, axis=-1)
    e_flat = top_i.reshape(-1).astype(jnp.int32)
    onehot = (e_flat[:, None] == jnp.arange(N_EXPERTS, dtype=jnp.int32)[None]).astype(jnp.int32)
    csum = jnp.cumsum(onehot, axis=0)
    cnt = csum[-1]
    rank = jnp.sum(csum * onehot, axis=1) - 1
    padded = ((cnt + FFN_TM - 1) // FFN_TM) * FFN_TM
    ends = jnp.cumsum(padded)
    dest = (ends - padded)[e_flat] + rank
    n_rows = n_slot + N_EXPERTS * FFN_TM
    n_tiles = n_rows // FFN_TM
    src = jnp.zeros((n_rows,), jnp.int32).at[dest].set(jnp.arange(n_slot, dtype=jnp.int32) // TOP_K)
    tile_start = jnp.arange(n_tiles, dtype=jnp.int32) * FFN_TM
    tile_expert = jnp.minimum(jnp.sum((tile_start[:, None] >= ends[None, :]).astype(jnp.int32), axis=1),
                              N_EXPERTS - 1)
    n_used = (ends[-1:] // FFN_TM).astype(jnp.int32)
    xs = jnp.take(h2, src, axis=0)
    ys = grouped_swiglu(tile_expert, n_used, xs, w_gu.astype(jnp.bfloat16),
                        w_down.astype(jnp.bfloat16), nf=2)
    yk = jnp.take(ys, dest, axis=0).reshape(n_tok, TOP_K * D_MODEL)
    return moe_combine(x, yk, w, gate_mod, final_g)


def kernel(x, c, ctx, c_ctx, mod_w, mod_b, norm1_g, norm2_g, w_in, gla_w_up, gla_b_up,
           s5_lam_re, s5_lam_im, s5_log_dt, s5_b_re, s5_b_im, s5_c_re, s5_c_im, s5_d,
           s5_w_glu, s5_b_glu, hy_w_short, hy_w1, hy_b1, hy_w2, hy_b2, hy_w3, hy_freq,
           hy_bias, rg_w_conv, rg_w_a, rg_b_a, rg_w_x, rg_b_x, rg_lam, w_branch, w_out,
           ffn_w_gu, ffn_w_down, moe_router, moe_router_b, moe_w_gu, moe_w_down, final_g):
    f32, bf16 = jnp.float32, jnp.bfloat16
    n_b, n_lat, _ = x.shape
    n_ctx = ctx.shape[1]
    xs = (x + grid_pos_embed(n_lat, D_MODEL)[None]).reshape(n_b * n_lat, D_MODEL)
    cs = ctx.reshape(n_b * n_ctx, D_MODEL)
    offs = [0]
    for n in IN_SIZES:
        offs.append(offs[-1] + n)
    for l in range(DEPTH):
        last = l == DEPTH - 1
        dense = l % 2 == 0
        j = l // 2
        m_lat = [t[:, None, :] for t in jnp.split(adaln(c, mod_w[l], mod_b[l]), 6, axis=-1)]
        m_ctx = [t[None, None, :] for t in jnp.split(adaln(c_ctx, mod_w[l], mod_b[l]), 6, axis=-1)]
        wl = w_in[l]
        w_mix = jnp.concatenate([wl[:, offs[6]:offs[7]], wl[:, offs[0]:offs[4]], wl[:, offs[5]:offs[6]],
                                 wl[:, offs[7]:offs[9]], wl[:, offs[4]:offs[5]],
                                 jnp.zeros((D_MODEL, P_COLS - P_GDN - 2 * GLA_RANK), f32)], axis=1).astype(bf16)
        w_gate = wl[:, offs[9]:offs[10]].astype(bf16)
        wb, wo, wglu = w_branch[l].astype(bf16), w_out[l].astype(bf16), s5_w_glu[l].astype(bf16)
        s5_ops = s5_prepare(s5_lam_re[l], s5_lam_im[l], s5_log_dt[l], s5_b_re[l], s5_b_im[l],
                            s5_c_re[l], s5_c_im[l], s5_d[l])
        hy_p = (hy_w_short[l], hy_w1[l], hy_b1[l], hy_w2[l], hy_b2[l], hy_w3[l], hy_freq[l], hy_bias[l])
        rg_p = (rg_w_conv[l], rg_w_a[l], rg_b_a[l], rg_w_x[l], rg_b_x[l], rg_lam[l])
        router = None if dense else (moe_router[j], moe_router_b[j])

        def mixers(p2, n_tok, states, with_hyena):
            p = p2.reshape(n_b, n_tok, P_COLS)
            blk = lambda i, n=1: p[..., i * W_MIX:(i + n) * W_MIX]
            flat = lambda t: t.reshape(n_b * n_tok, W_MIX)
            gla_o, gla_s = gla_mixer(p, gla_w_up[l], gla_b_up[l], states[0])
            s5_y, s5_s = s5_mixer(blk(P_S5), s5_ops, states[1])
            rg_o, rg_s = rglru_mixer(blk(P_RGX), *rg_p, states[2])
            hy = flat(hyena_mixer(blk(P_HY, 3), *hy_p)) if with_hyena else None
            return ((flat(gla_o[0]), flat(gla_o[1]), flat(s5_y), hy, flat(rg_o[0]), flat(rg_o[1])),
                    (gla_s, s5_s, rg_s))

        def tail(stream, mods, p2, br, final):
            outs = merge(stream, tuple(mods[:5]), norm1_g[l], norm2_g[l], br[0], br[1], p2, br[2], br[3],
                         br[4], br[5], w_gate, wb, wo, wglu, s5_b_glu[l], router)
            if dense:
                return ffn_dense(outs[1], outs[0], mods[5], ffn_w_gu[j], ffn_w_down[j])
            return ffn_moe(outs[1], outs[2][:, :N_EXPERTS], outs[0], mods[5], moe_w_gu[j], moe_w_down[j],
                           final_g if final else None)

        zero_states = (jnp.zeros((2, n_b, GLA_HEADS, GLA_DV, GLA_DK), f32),
                       jnp.zeros((2, n_b, 2 * S5_NS), f32), jnp.zeros((2, n_b, W_MIX), f32))
        p_ctx = front(cs, m_ctx[0], m_ctx[1], norm1_g[l], w_mix)
        br_ctx, states = mixers(p_ctx, n_ctx, zero_states, not last)
        p_lat = front(xs, m_lat[0], m_lat[1], norm1_g[l], w_mix)
        br_lat, _ = mixers(p_lat, n_lat, states, True)
        xs = tail(xs, m_lat, p_lat, br_lat, last)
        if not last:
            cs = tail(cs, m_ctx, p_ctx, br_ctx, False)
    if (DEPTH - 1) % 2 == 0:
        xs = rmsnorm(xs, final_g)
    return xs.reshape(n_b, n_lat, D_MODEL)
```

```python
import functools
import math

import jax
import jax.numpy as jnp
import numpy as np
from jax import lax
from jax.experimental import pallas as pl
from jax.experimental.pallas import tpu as pltpu

D_MODEL = 1024
DEPTH = 2
GRID_W = 64
EPS = 1e-6
N_BRANCH = 4
W_MIX = D_MODEL // N_BRANCH
GLA_HEADS = 4
GLA_DK = W_MIX // GLA_HEADS
GLA_DV = W_MIX // GLA_HEADS
GLA_RANK = 16
GLA_TAU = 16.0
GLA_CHUNK = 64
S5_GROUP = 16
S5_GROUPS = W_MIX // S5_GROUP
S5_STATE = 64
S5_MAX_RE = -1e-4
HY_BANDS = 16
HY_DECAY_SHORT = 0.3
HY_DECAY_LONG = 1.5
HY_TARGET = 1e-2
RG_BLOCKS = 4
RG_BLOCK = W_MIX // RG_BLOCKS
RG_C = 8.0
N_EXPERTS = 8
TOP_K = 2
IN_SIZES = (GLA_HEADS * GLA_DK, GLA_HEADS * GLA_DK, GLA_HEADS * GLA_DV, GLA_HEADS * GLA_DV,
            2 * GLA_RANK, W_MIX, 3 * W_MIX, W_MIX, W_MIX, N_BRANCH * D_MODEL)

VMEM_LIMIT_BYTES = 48 * 1024 * 1024


def _mm_kernel(x_ref, w_ref, o_ref):
    o_ref[...] = jnp.dot(x_ref[...].astype(jnp.bfloat16), w_ref[...],
                         preferred_element_type=jnp.float32)


def _pick_tile(n, cap):
    best = None
    for t in range(128, cap + 1, 128):
        if n % t == 0:
            best = t
    return best if best is not None else n


def pmm(x, w):
    lead = x.shape[:-1]
    k = x.shape[-1]
    n = w.shape[-1]
    x2 = x.reshape(-1, k)
    m = x2.shape[0]
    tm = 512 if m % 512 == 0 else m
    if k > 2048 and m % 256 == 0:
        tm = 256
    tn = n if k * n * 2 <= 6 * 1024 * 1024 else _pick_tile(n, 1024)
    out = pl.pallas_call(
        _mm_kernel,
        grid=(m // tm, n // tn),
        in_specs=[pl.BlockSpec((tm, k), lambda i, j: (i, 0)),
                  pl.BlockSpec((k, tn), lambda i, j: (0, j))],
        out_specs=pl.BlockSpec((tm, tn), lambda i, j: (i, j)),
        out_shape=jax.ShapeDtypeStruct((m, n), jnp.float32),
        compiler_params=pltpu.CompilerParams(
            dimension_semantics=("arbitrary", "arbitrary"),
            vmem_limit_bytes=VMEM_LIMIT_BYTES),
    )(x2, w.astype(jnp.bfloat16))
    return out.reshape(lead + (n,))


def _mm_multi_kernel(*refs, transposed):
    o_ref = refs[-1]
    n = (len(refs) - 1) // 2
    acc = None
    for i in range(n):
        dims = (((1,), (1 if transposed[i] else 0,)), ((), ()))
        t = lax.dot_general(refs[i][...].astype(jnp.bfloat16), refs[n + i][...], dims,
                            preferred_element_type=jnp.float32)
        acc = t if acc is None else acc + t
    o_ref[...] = acc


def pmm_multi(xs, ws, transposed, tm=256, tn=512):
    m = xs[0].shape[0]
    n = ws[0].shape[0 if transposed[0] else 1]
    tm = tm if m % tm == 0 else m
    tn = tn if n % tn == 0 else n
    in_specs = ([pl.BlockSpec((tm, x.shape[1]), lambda i, j: (i, 0)) for x in xs]
                + [pl.BlockSpec((tn, w.shape[1]), lambda i, j: (j, 0)) if t else
                   pl.BlockSpec((w.shape[0], tn), lambda i, j: (0, j)) for w, t in zip(ws, transposed)])
    return pl.pallas_call(
        functools.partial(_mm_multi_kernel, transposed=tuple(transposed)),
        grid=(m // tm, n // tn),
        in_specs=in_specs,
        out_specs=pl.BlockSpec((tm, tn), lambda i, j: (i, j)),
        out_shape=jax.ShapeDtypeStruct((m, n), jnp.float32),
        compiler_params=pltpu.CompilerParams(
            dimension_semantics=("arbitrary", "arbitrary"),
            vmem_limit_bytes=VMEM_LIMIT_BYTES),
    )(*xs, *[w.astype(jnp.bfloat16) for w in ws])


RG_SCAN_ROWS = 256


def _rg_scan_kernel(a_ref, b_ref, s0_ref, h_ref, fin_ref, st_ref, *, reverse, tb, nb):
    @pl.when(pl.program_id(0) == 0)
    def _():
        st_ref[...] = s0_ref[...]

    def body(r, hs):
        rr = (tb - 1 - r) if reverse else r
        out = []
        for i in range(nb):
            h = a_ref[i, pl.ds(rr, 1), :] * hs[i] + b_ref[i, pl.ds(rr, 1), :]
            h_ref[i, pl.ds(rr, 1), :] = h
            out.append(h)
        return tuple(out)

    hs = lax.fori_loop(0, tb, body, tuple(st_ref[i:i + 1, :] for i in range(nb)), unroll=8)
    for i in range(nb):
        st_ref[i:i + 1, :] = hs[i]
        fin_ref[i:i + 1, :] = hs[i]


def rg_scan(a, b, s0, reverse):
    nb, n_tok, ch = a.shape
    tb = min(RG_SCAN_ROWS, n_tok)
    nblk = n_tok // tb
    imap = (lambda k: (0, nblk - 1 - k, 0)) if reverse else (lambda k: (0, k, 0))
    return pl.pallas_call(
        functools.partial(_rg_scan_kernel, reverse=reverse, tb=tb, nb=nb),
        grid=(nblk,),
        in_specs=[pl.BlockSpec((nb, tb, ch), imap), pl.BlockSpec((nb, tb, ch), imap),
                  pl.BlockSpec((nb, ch), lambda k: (0, 0))],
        out_specs=[pl.BlockSpec((nb, tb, ch), imap), pl.BlockSpec((nb, ch), lambda k: (0, 0))],
        out_shape=[jax.ShapeDtypeStruct((nb, n_tok, ch), jnp.float32),
                   jax.ShapeDtypeStruct((nb, ch), jnp.float32)],
        scratch_shapes=[pltpu.VMEM((nb, ch), jnp.float32)],
        compiler_params=pltpu.CompilerParams(dimension_semantics=("arbitrary",)),
    )(a, b, s0)


S5_T = 16
S5_NS = S5_GROUPS * S5_STATE
S5_SCAN_CHUNKS = 64


def _s5_scan_kernel(d_ref, s0_ref, a_ref, h_ref, fin_ref, st_ref, *, rc, nb):
    d = pl.program_id(0)

    @pl.when(pl.program_id(1) == 0)
    def _():
        st_ref[...] = s0_ref[0]

    ar = jnp.broadcast_to(a_ref[0, :, 0:S5_NS], (nb, S5_NS))
    ai = jnp.broadcast_to(a_ref[0, :, S5_NS:2 * S5_NS], (nb, S5_NS))

    def body(r, carry):
        hr, hi = carry
        rr = r + d * (rc - 1 - 2 * r)
        h_ref[rr, :, 0:S5_NS] = hr
        h_ref[rr, :, S5_NS:2 * S5_NS] = hi
        dr = d_ref[rr, :, 0:S5_NS]
        di = d_ref[rr, :, S5_NS:2 * S5_NS]
        return ar * hr - ai * hi + dr, ar * hi + ai * hr + di

    hr, hi = lax.fori_loop(0, rc, body, (st_ref[:, 0:S5_NS], st_ref[:, S5_NS:2 * S5_NS]))
    st_ref[:, 0:S5_NS] = hr
    st_ref[:, S5_NS:2 * S5_NS] = hi
    fin_ref[0, :, 0:S5_NS] = hr
    fin_ref[0, :, S5_NS:2 * S5_NS] = hi


def s5_scan(dmat, s0, a_t):
    n, nb, _ = dmat.shape
    rc = min(S5_SCAN_CHUNKS, n)
    nblk = n // rc
    w = 2 * S5_NS
    imap = lambda d, k: (k + d * (nblk - 1 - 2 * k), 0, d)
    return pl.pallas_call(
        functools.partial(_s5_scan_kernel, rc=rc, nb=nb),
        grid=(2, nblk),
        in_specs=[pl.BlockSpec((rc, nb, w), imap),
                  pl.BlockSpec((1, nb, w), lambda d, k: (d, 0, 0)),
                  pl.BlockSpec((1, 1, w), lambda d, k: (d, 0, 0))],
        out_specs=[pl.BlockSpec((rc, nb, w), imap),
                   pl.BlockSpec((1, nb, w), lambda d, k: (d, 0, 0))],
        out_shape=[jax.ShapeDtypeStruct((n, nb, 2 * w), jnp.float32),
                   jax.ShapeDtypeStruct((2, nb, w), jnp.float32)],
        scratch_shapes=[pltpu.VMEM((nb, w), jnp.float32)],
        compiler_params=pltpu.CompilerParams(dimension_semantics=("arbitrary", "arbitrary"),
                                             vmem_limit_bytes=VMEM_LIMIT_BYTES),
    )(dmat, s0, a_t)


def _cmul(ar, ai, br, bi):
    return ar * br - ai * bi, ar * bi + ai * br


def s5_prepare(lam_re, lam_im, log_dt, b_re, b_im, c_re, c_im, d_skip):
    f32 = jnp.float32
    hp = lax.Precision.HIGHEST
    t_len, g_n, p_n, h_n = S5_T, S5_GROUPS, S5_STATE, S5_GROUP
    bf16 = jnp.bfloat16
    eye_g = jnp.eye(g_n, dtype=f32)
    mask_gp = jnp.repeat(eye_g, p_n, axis=1).astype(bf16)[None, :, None, :]
    ar_t = jnp.arange(t_len)
    wd, wc, kk, a_t = [], [], [], []
    for d in range(2):
        lr = jnp.minimum(lam_re[d], S5_MAX_RE)
        li = lam_im[d]
        dt = jnp.exp(log_dt[d])[:, None]
        tt = jnp.arange(t_len + 1, dtype=f32)[:, None, None]
        mag = jnp.exp(lr * dt * tt)
        ang = li * dt * tt
        pr, pi = mag * jnp.cos(ang), mag * jnp.sin(ang)
        nr, ni = pr[1] - 1.0, pi[1]
        den = lr * lr + li * li
        qr, qi = (nr * lr + ni * li) / den, (ni * lr - nr * li) / den
        bbr, bbi = _cmul(qr[..., None], qi[..., None], b_re[d], b_im[d])
        cr, ci = c_re[d], c_im[d]

        idx = (t_len - 1 - ar_t) if d == 0 else ar_t
        wr, wi = _cmul(pr[idx][..., None], pi[idx][..., None], bbr[None], bbi[None])

        def place_d(w):
            wt = w.transpose(0, 3, 1, 2).reshape(t_len, 1, h_n, g_n * p_n).astype(bf16)
            return (wt * mask_gp).reshape(t_len * W_MIX, g_n * p_n)

        wd.append(jnp.concatenate([place_d(wr), place_d(wi)], axis=1))

        idx2 = (ar_t + 1) if d == 0 else (t_len - ar_t)
        cwr, cwi = _cmul(cr[None], ci[None], pr[idx2][:, :, None, :], pi[idx2][:, :, None, :])

        def place_c(w):
            wt = w.transpose(0, 2, 1, 3).reshape(t_len, 1, h_n, g_n * p_n).astype(bf16)
            return (wt * mask_gp).reshape(t_len * W_MIX, g_n * p_n)

        wc.append(jnp.concatenate([place_c(cwr), place_c(-cwi)], axis=1))

        er, ei = _cmul(pr[:t_len][:, :, None, :], pi[:t_len][:, :, None, :], cr[None], ci[None])
        kk.append(jnp.einsum('tghp,gpk->tghk', er, bbr, precision=hp)
                  - jnp.einsum('tghp,gpk->tghk', ei, bbi, precision=hp))
        a_t.append(jnp.concatenate([pr[t_len].reshape(1, -1), pi[t_len].reshape(1, -1)], axis=1))

    lag = ar_t[None, :] - ar_t[:, None]
    mf = jnp.where((lag >= 0)[..., None, None, None], kk[0][jnp.clip(lag, 0, t_len - 1)], 0.0)
    mb = jnp.where((lag <= 0)[..., None, None, None], kk[1][jnp.clip(-lag, 0, t_len - 1)], 0.0)
    skip = (jnp.eye(t_len, dtype=f32)[:, :, None, None, None] * d_skip[None, None, :, :, None]
            * jnp.eye(h_n, dtype=f32)[None, None, None])
    m = mf + mb + skip
    mt = m.transpose(0, 4, 1, 2, 3).reshape(t_len, 1, h_n, t_len * W_MIX).astype(bf16)
    mask_igh = jnp.tile(jnp.repeat(eye_g, h_n, axis=1), (1, t_len)).astype(bf16)[None, :, None, :]
    wk = (mt * mask_igh).reshape(t_len * W_MIX, t_len * W_MIX)
    return (jnp.concatenate(wd, axis=1), wk, jnp.concatenate(wc, axis=1), jnp.stack(a_t))


GLA_BLOCK = 512


def _gla_kernel(q_ref, k_ref, v_ref, g_ref, wup_ref, bup_ref, s0_ref, o_ref, fin_ref, st_ref,
                *, reverse, tb):
    f32, bf16 = jnp.float32, jnp.bfloat16
    hp = lax.Precision.HIGHEST
    cc = GLA_CHUNK

    @pl.when(pl.program_id(1) == 0)
    def _():
        st_ref[...] = s0_ref[0]

    r_i = lax.broadcasted_iota(jnp.int32, (cc, cc), 0)
    c_i = lax.broadcasted_iota(jnp.int32, (cc, cc), 1)
    keep = (c_i >= r_i) if reverse else (c_i <= r_i)
    tri = keep.astype(f32)
    nt = (((1,), (1,)), ((), ()))
    tn = (((0,), (0,)), ((), ()))
    n_ch = tb // cc
    for ci in (range(n_ch - 1, -1, -1) if reverse else range(n_ch)):
        rows = slice(ci * cc, (ci + 1) * cc)
        z = jnp.dot(g_ref[0, rows, :], wup_ref[...], precision=hp,
                    preferred_element_type=f32) + bup_ref[...]
        la = (jnp.minimum(z, 0.0) - jnp.log1p(jnp.exp(-jnp.abs(z)))) * (1.0 / GLA_TAU)
        cum = jnp.dot(tri, la, precision=hp, preferred_element_type=f32)
        last = cum[0:1, :] if reverse else cum[cc - 1:cc, :]
        k = k_ref[0, rows, :]
        q_in = (q_ref[0, rows, :] * (GLA_DK ** -0.5) * jnp.exp(cum)).astype(bf16)
        k_in = (k * jnp.exp(-cum)).astype(bf16)
        k_out = (k * jnp.exp(last - cum)).astype(bf16)
        dec = jnp.exp(last)
        vb = v_ref[0, rows, :].astype(bf16)
        for h in range(GLA_HEADS):
            sl = slice(h * GLA_DK, (h + 1) * GLA_DK)
            att = lax.dot_general(q_in[:, sl], k_in[:, sl], nt, preferred_element_type=f32)
            att = jnp.where(keep, att, 0.0).astype(bf16)
            st = st_ref[h]
            o_h = (jnp.dot(att, vb[:, sl], preferred_element_type=f32)
                   + lax.dot_general(q_in[:, sl], st.astype(bf16), nt, preferred_element_type=f32))
            st_ref[h] = st * dec[:, sl] + lax.dot_general(vb[:, sl], k_out[:, sl], tn,
                                                          preferred_element_type=f32)
            o_ref[0, rows, sl] = o_h
    fin_ref[0] = st_ref[...]


def gla_dir(p, gd, w_up, b_up, s0, reverse):
    nb, n_tok, _ = p.shape
    tb = min(GLA_BLOCK, n_tok)
    nblk = n_tok // tb
    blk = (lambda k: nblk - 1 - k) if reverse else (lambda k: k)
    col = lambda c: pl.BlockSpec((1, tb, W_MIX), lambda b, k: (b, blk(k), c))
    st_shape = (GLA_HEADS, GLA_DV, GLA_DK)
    return pl.pallas_call(
        functools.partial(_gla_kernel, reverse=reverse, tb=tb),
        grid=(nb, nblk),
        in_specs=[col(P_Q), col(P_Q + 1), col(P_Q + 2),
                  pl.BlockSpec((1, tb, GLA_RANK), lambda b, k: (b, blk(k), 0)),
                  pl.BlockSpec((GLA_RANK, W_MIX), lambda b, k: (0, 0)),
                  pl.BlockSpec((1, W_MIX), lambda b, k: (0, 0)),
                  pl.BlockSpec((1,) + st_shape, lambda b, k: (b, 0, 0, 0))],
        out_specs=[pl.BlockSpec((1, tb, W_MIX), lambda b, k: (b, blk(k), 0)),
                   pl.BlockSpec((1,) + st_shape, lambda b, k: (b, 0, 0, 0))],
        out_shape=[jax.ShapeDtypeStruct((nb, n_tok, W_MIX), jnp.float32),
                   jax.ShapeDtypeStruct((nb,) + st_shape, jnp.float32)],
        scratch_shapes=[pltpu.VMEM(st_shape, jnp.float32)],
        compiler_params=pltpu.CompilerParams(dimension_semantics=("arbitrary", "arbitrary")),
    )(p, p, p, gd, w_up, b_up.reshape(1, W_MIX), s0)


FFN_TM = 512


def _swiglu_kernel(te_ref, nu_ref, x_ref, wg_ref, wu_ref, wd_ref, *rest, nf, residual):
    del te_ref
    o_ref = rest[-1]
    j = pl.program_id(1)

    @pl.when(pl.program_id(0) < nu_ref[0])
    def _():
        x = x_ref[...]
        g = jnp.dot(x, wg_ref[0], preferred_element_type=jnp.float32)
        u = jnp.dot(x, wu_ref[0], preferred_element_type=jnp.float32)
        a = (g * jax.nn.sigmoid(g) * u).astype(jnp.bfloat16)
        part = jnp.dot(a, wd_ref[0], preferred_element_type=jnp.float32)

        @pl.when(j == 0)
        def _():
            o_ref[...] = part

        @pl.when(j > 0)
        def _():
            o_ref[...] += part

        if residual:
            xres_ref, gm_ref = rest[:2]

            @pl.when(j == nf - 1)
            def _():
                o_ref[...] = xres_ref[...] + gm_ref[0] * o_ref[...]


def grouped_swiglu(tile_expert, n_used, xs, w_gu, w_down, nf, residual=None):
    m, d = xs.shape
    f = w_down.shape[1]
    tf = f // nf
    n_tiles = m // FFN_TM
    in_specs = [pl.BlockSpec((FFN_TM, d), lambda t, j, te, nu: (t, 0)),
                pl.BlockSpec((1, d, tf), lambda t, j, te, nu: (te[t], 0, j)),
                pl.BlockSpec((1, d, tf), lambda t, j, te, nu: (te[t], 0, nf + j)),
                pl.BlockSpec((1, tf, d), lambda t, j, te, nu: (te[t], j, 0))]
    extra = ()
    if residual is not None:
        rows_per_mod = m // residual[1].shape[0]
        in_specs += [pl.BlockSpec((FFN_TM, d), lambda t, j, te, nu: (t, 0)),
                     pl.BlockSpec((1, 1, d), lambda t, j, te, nu: ((t * FFN_TM) // rows_per_mod, 0, 0))]
        extra = tuple(residual)
    grid_spec = pltpu.PrefetchScalarGridSpec(
        num_scalar_prefetch=2,
        grid=(n_tiles, nf),
        in_specs=in_specs,
        out_specs=pl.BlockSpec((FFN_TM, d), lambda t, j, te, nu: (t, 0)))
    return pl.pallas_call(
        functools.partial(_swiglu_kernel, nf=nf, residual=residual is not None),
        grid_spec=grid_spec,
        out_shape=jax.ShapeDtypeStruct((m, d), jnp.float32),
        compiler_params=pltpu.CompilerParams(
            dimension_semantics=("arbitrary", "arbitrary"),
            vmem_limit_bytes=VMEM_LIMIT_BYTES),
        name="grouped_swiglu",
    )(tile_expert, n_used, xs, w_gu, w_gu, w_down, *extra)


ROW_TILE = 512
P_HY, P_Q, P_OG, P_S5, P_RGX, P_RGG = 0, 3, 6, 7, 8, 9
P_GDN = 10 * W_MIX
P_COLS = P_GDN + 128


def _rms_mod(x, g, shift, scale):
    y = x * lax.rsqrt(jnp.mean(x * x, axis=-1, keepdims=True) + EPS) * g
    return y * (1.0 + scale) + shift


def _front_kernel(x_ref, sh_ref, sc_ref, g_ref, w_ref, o_ref):
    h = _rms_mod(x_ref[...], g_ref[...], sh_ref[0], sc_ref[0])
    o_ref[...] = jnp.dot(h.astype(jnp.bfloat16), w_ref[...], preferred_element_type=jnp.float32)


def _mod_spec(rows_per_mod):
    return pl.BlockSpec((1, 1, D_MODEL), lambda i: ((i * ROW_TILE) // rows_per_mod, 0, 0))


def front(x, shift, scale, g, w):
    m = x.shape[0]
    n_mod = shift.shape[0]
    n_out = w.shape[1]
    mod = _mod_spec(m // n_mod)
    return pl.pallas_call(
        _front_kernel,
        grid=(m // ROW_TILE,),
        in_specs=[pl.BlockSpec((ROW_TILE, D_MODEL), lambda i: (i, 0)), mod, mod,
                  pl.BlockSpec((1, D_MODEL), lambda i: (0, 0)),
                  pl.BlockSpec((D_MODEL, n_out), lambda i: (0, 0))],
        out_specs=pl.BlockSpec((ROW_TILE, n_out), lambda i: (i, 0)),
        out_shape=jax.ShapeDtypeStruct((m, n_out), jnp.float32),
        compiler_params=pltpu.CompilerParams(dimension_semantics=("arbitrary",),
                                             vmem_limit_bytes=VMEM_LIMIT_BYTES),
        name="front",
    )(x, shift, scale, g.reshape(1, D_MODEL), w)


MERGE_TILE = 256
ROUTER_PAD = 128


def _merge_kernel(*refs, with_router):
    (x_ref, sh1_ref, sc1_ref, gm_ref, sh2_ref, sc2_ref, g1_ref, g2_ref,
     of_ref, ob_ref, og_ref, s5_ref, hy_ref, rf_ref, rb_ref, rgg_ref,
     wg_ref, wb_ref, wo_ref, wglu_ref, bglu_ref, havg_ref) = refs[:22]
    f32, bf16 = jnp.float32, jnp.bfloat16
    hp = lax.Precision.HIGHEST
    x = x_ref[...]
    hb = _rms_mod(x, g1_ref[...], sh1_ref[0], sc1_ref[0]).astype(bf16)

    o = of_ref[...] + ob_ref[...]
    ms = jnp.dot(o * o, havg_ref[...], precision=hp, preferred_element_type=f32)
    og = og_ref[...]
    gla = o * lax.rsqrt(ms + EPS) * (og * jax.nn.sigmoid(og))
    g5 = jax.nn.gelu(s5_ref[...])
    s5o = g5 * jax.nn.sigmoid(jnp.dot(g5.astype(bf16), wglu_ref[...], preferred_element_type=f32)
                              + bglu_ref[...])
    rgo = (rf_ref[...] + rb_ref[...]) * jax.nn.gelu(rgg_ref[...])
    branches = (gla, s5o, hy_ref[...], rgo)

    y = None
    for k in range(N_BRANCH):
        gate = jax.nn.sigmoid(jnp.dot(hb, wg_ref[:, k * D_MODEL:(k + 1) * D_MODEL],
                                      preferred_element_type=f32))
        t = gate * jnp.dot(branches[k].astype(bf16), wb_ref[k], preferred_element_type=f32)
        y = t if y is None else y + t
    out = jnp.dot(y.astype(bf16), wo_ref[...], preferred_element_type=f32)
    xn = x + gm_ref[0] * out
    h2 = _rms_mod(xn, g2_ref[...], sh2_ref[0], sc2_ref[0])
    if with_router:
        rw_ref, rb2_ref, xo_ref, h2_ref, lg_ref = refs[22:]
        lg_ref[...] = jnp.dot(h2, rw_ref[...], precision=hp, preferred_element_type=f32) + rb2_ref[...]
    else:
        xo_ref, h2_ref = refs[22:]
    xo_ref[...] = xn
    h2_ref[...] = h2.astype(bf16)


def merge(x, mods, g1, g2, o_f, o_b, p, s5y, hy, r_f, r_b, wg, wb, wo, wglu, bglu, router=None):
    m = x.shape[0]
    tm = MERGE_TILE
    n_mod = mods[0].shape[0]
    rows_per_mod = m // n_mod
    mod = pl.BlockSpec((1, 1, D_MODEL), lambda i: ((i * tm) // rows_per_mod, 0, 0))
    row = pl.BlockSpec((tm, D_MODEL), lambda i: (i, 0))
    br = pl.BlockSpec((tm, W_MIX), lambda i: (i, 0))
    pcol = lambda c: pl.BlockSpec((tm, W_MIX), lambda i: (i, c))
    full = lambda a: pl.BlockSpec(a.shape, lambda i: (0,) * a.ndim)
    head = jnp.arange(W_MIX) // GLA_DV
    havg = (head[:, None] == head[None, :]).astype(jnp.float32) / GLA_DV
    vec = lambda v: v.reshape(1, -1)
    consts = [wg, wb, wo, wglu, vec(bglu), havg]
    out_specs = [row, pl.BlockSpec((tm, D_MODEL), lambda i: (i, 0))]
    out_shape = [jax.ShapeDtypeStruct((m, D_MODEL), jnp.float32),
                 jax.ShapeDtypeStruct((m, D_MODEL), jnp.bfloat16)]
    if router is not None:
        rw, rbias = router
        pad = ROUTER_PAD - rw.shape[1]
        consts += [jnp.pad(rw, ((0, 0), (0, pad))), jnp.pad(rbias, (0, pad)).reshape(1, -1)]
        out_specs.append(pl.BlockSpec((tm, ROUTER_PAD), lambda i: (i, 0)))
        out_shape.append(jax.ShapeDtypeStruct((m, ROUTER_PAD), jnp.float32))
    in_specs = ([row] + [mod] * 5 + [full(vec(g1)), full(vec(g2)), br, br, pcol(P_OG), br, br, br, br,
                                     pcol(P_RGG)] + [full(a) for a in consts])
    return pl.pallas_call(
        functools.partial(_merge_kernel, with_router=router is not None),
        grid=(m // tm,),
        in_specs=in_specs,
        out_specs=out_specs,
        out_shape=out_shape,
        compiler_params=pltpu.CompilerParams(dimension_semantics=("arbitrary",),
                                             vmem_limit_bytes=VMEM_LIMIT_BYTES),
        name="merge",
    )(x, *mods, vec(g1), vec(g2), o_f, o_b, p, s5y, hy, r_f, r_b, p, *consts)


def _combine_kernel(x_ref, y_ref, w_ref, gm_ref, g_ref, o_ref, *, final_norm):
    w = w_ref[...]
    y = w[:, 0:1] * y_ref[:, 0:D_MODEL] + w[:, 1:2] * y_ref[:, D_MODEL:2 * D_MODEL]
    xn = x_ref[...] + gm_ref[0] * y
    if final_norm:
        xn = xn * lax.rsqrt(jnp.mean(xn * xn, axis=-1, keepdims=True) + EPS) * g_ref[...]
    o_ref[...] = xn


def moe_combine(x, yk, w, gate_mod, final_g):
    m = x.shape[0]
    n_mod = gate_mod.shape[0]
    g = jnp.ones((1, D_MODEL), jnp.float32) if final_g is None else final_g.reshape(1, D_MODEL)
    return pl.pallas_call(
        functools.partial(_combine_kernel, final_norm=final_g is not None),
        grid=(m // ROW_TILE,),
        in_specs=[pl.BlockSpec((ROW_TILE, D_MODEL), lambda i: (i, 0)),
                  pl.BlockSpec((ROW_TILE, TOP_K * D_MODEL), lambda i: (i, 0)),
                  pl.BlockSpec((ROW_TILE, TOP_K), lambda i: (i, 0)),
                  _mod_spec(m // n_mod),
                  pl.BlockSpec((1, D_MODEL), lambda i: (0, 0))],
        out_specs=pl.BlockSpec((ROW_TILE, D_MODEL), lambda i: (i, 0)),
        out_shape=jax.ShapeDtypeStruct((m, D_MODEL), jnp.float32),
        compiler_params=pltpu.CompilerParams(dimension_semantics=("arbitrary",)),
        name="moe_combine",
    )(x, yk, w, gate_mod, g)


HY_LANES = 128
HY_SLABS = 8
HY_MIN_LEN = 1024


def _split_bf16(a):
    hi = a.astype(jnp.bfloat16)
    lo = (a - hi.astype(jnp.float32)).astype(jnp.bfloat16)
    return hi, lo


def _dot3(a_hi, a_lo, b_hi, b_lo):
    d = functools.partial(jnp.dot, preferred_element_type=jnp.float32)
    return d(a_hi, b_hi) + (d(a_lo, b_hi) + d(a_hi, b_lo))


def _hyena_dft_consts(n1):
    n = n1 * HY_LANES
    ka = np.arange(n1, dtype=np.float64)[:, None]
    f1_ang = 2.0 * np.pi * ka * np.arange(n1 // 2, dtype=np.float64)[None, :] / n1
    f1r, f1i = np.cos(f1_ang), -np.sin(f1_ang)
    tw_ang = 2.0 * np.pi * ka * np.arange(HY_LANES, dtype=np.float64)[None, :] / n
    lo = np.arange(HY_LANES, dtype=np.float64)
    f2_ang = 2.0 * np.pi * lo[:, None] * lo[None, :] / HY_LANES
    f2r, f2i = np.cos(f2_ang), -np.sin(f2_ang)
    fwd_rows = np.concatenate([f1r, f1i], axis=0)
    fwd_lanes = np.block([[f2r, f2i], [-f2i, f2r]])
    inv_lanes = np.block([[f2r, -f2i], [f2i, f2r]])
    inv_rows = np.concatenate([f1r.T, f1i.T], axis=1) / n
    out = []
    for m in (fwd_rows, fwd_lanes, inv_lanes, inv_rows):
        m32 = jnp.asarray(m, jnp.float32)
        out.extend(_split_bf16(m32))
    return out + [jnp.asarray(np.cos(tw_ang), jnp.float32), jnp.asarray(-np.sin(tw_ang), jnp.float32)]


def _hyena_fft_kernel(*refs, n1, ns, spectrum):
    if spectrum:
        z_ref, f1h, f1l, f2h, f2l, twr_ref, twi_ref, o_ref = refs
    else:
        z_ref, hf_ref, f1h, f1l, f2h, f2l, g2h, g2l, fih, fil, twr_ref, twi_ref, o_ref = refs
    w = HY_LANES
    twr, twi = twr_ref[...], twi_ref[...]
    z2 = jnp.concatenate([z_ref[0, s] for s in range(ns)], axis=1)
    a2 = _dot3(f1h[...], f1l[...], *_split_bf16(z2))
    rows = []
    for s in range(ns):
        r, i = a2[:n1, s * w:(s + 1) * w], a2[n1:, s * w:(s + 1) * w]
        rows.append(jnp.concatenate([r * twr - i * twi, r * twi + i * twr], axis=1))
    x = _dot3(*_split_bf16(jnp.concatenate(rows, axis=0)), f2h[...], f2l[...])
    if spectrum:
        o_ref[...] = x.reshape(ns, n1, 2 * w)
        return
    h = hf_ref[...].reshape(ns * n1, 2 * w)
    xr, xi, hr, hi = x[:, :w], x[:, w:], h[:, :w], h[:, w:]
    y = jnp.concatenate([xr * hr - xi * hi, xr * hi + xi * hr], axis=1)
    g = _dot3(*_split_bf16(y), g2h[...], g2l[...])
    cr, ci = [], []
    for s in range(ns):
        gr, gi = g[s * n1:(s + 1) * n1, :w], g[s * n1:(s + 1) * n1, w:]
        cr.append(gr * twr + gi * twi)
        ci.append(gi * twr - gr * twi)
    gc = jnp.concatenate([jnp.concatenate(cr, axis=1), jnp.concatenate(ci, axis=1)], axis=0)
    y2 = _dot3(fih[...], fil[...], *_split_bf16(gc))
    for s in range(ns):
        o_ref[0, s] = y2[:, s * w:(s + 1) * w]


def hyena_fft(zt, hf=None):
    nb, ch, half, w = zt.shape
    n1 = 2 * half
    ns = HY_SLABS
    f1h, f1l, f2h, f2l, g2h, g2l, fih, fil, twr, twi = _hyena_dft_consts(n1)
    full = lambda a: pl.BlockSpec(a.shape, lambda b, c: (0,) * a.ndim)
    zspec = pl.BlockSpec((1, ns, half, w), lambda b, c: (b, c, 0, 0))
    if hf is None:
        consts = [f1h, f1l, f2h, f2l, twr, twi]
        in_specs, args = [zspec], [zt]
        out_spec = pl.BlockSpec((ns, n1, 2 * w), lambda b, c: (b * (ch // ns) + c, 0, 0))
        out_shape = jax.ShapeDtypeStruct((nb * ch, n1, 2 * w), jnp.float32)
    else:
        consts = [f1h, f1l, f2h, f2l, g2h, g2l, fih, fil, twr, twi]
        in_specs = [zspec, pl.BlockSpec((ns, n1, 2 * w), lambda b, c: (c, 0, 0))]
        args = [zt, hf]
        out_spec = zspec
        out_shape = jax.ShapeDtypeStruct(zt.shape, jnp.float32)
    return pl.pallas_call(
        functools.partial(_hyena_fft_kernel, n1=n1, ns=ns, spectrum=hf is None),
        grid=(nb, ch // ns),
        in_specs=in_specs + [full(a) for a in consts],
        out_specs=out_spec,
        out_shape=out_shape,
        compiler_params=pltpu.CompilerParams(dimension_semantics=("arbitrary", "arbitrary"),
                                             vmem_limit_bytes=VMEM_LIMIT_BYTES),
        name="hyena_fft",
    )(*args, *consts)


def rmsnorm(x, g):
    y = x * lax.rsqrt(jnp.mean(x * x, axis=-1, keepdims=True) + EPS)
    return y * g


def adaln(cond, w, b):
    return jax.nn.silu(cond) @ w + b


def grid_pos_embed(n_tokens, dim):
    rows = n_tokens // GRID_W
    q = dim // 4
    omega = 1.0 / (10000.0 ** (jnp.arange(q, dtype=jnp.float32) / q))
    r = jnp.arange(rows, dtype=jnp.float32)[:, None] * omega
    cc = jnp.arange(GRID_W, dtype=jnp.float32)[:, None] * omega
    er = jnp.concatenate([jnp.sin(r), jnp.cos(r)], axis=-1)
    ec = jnp.concatenate([jnp.sin(cc), jnp.cos(cc)], axis=-1)
    emb = jnp.concatenate([jnp.broadcast_to(er[:, None], (rows, GRID_W, dim // 2)),
                           jnp.broadcast_to(ec[None], (rows, GRID_W, dim // 2))], axis=-1)
    return emb.reshape(rows * GRID_W, dim)


def dwconv(x, w, pad_l, pad_r):
    n = x.shape[1]
    xp = jnp.pad(x, ((0, 0), (pad_l, pad_r), (0, 0)))
    return sum(xp[:, k:k + n] * w[k] for k in range(w.shape[0]))


def gla_mixer(p, w_up, b_up, s0):
    gdn = p[..., P_GDN:P_GDN + 2 * GLA_RANK]
    outs, finals = [], []
    for d in range(2):
        od, sd = gla_dir(p, gdn[..., d * GLA_RANK:(d + 1) * GLA_RANK], w_up[d], b_up[d], s0[d], d == 1)
        outs.append(od)
        finals.append(sd)
    return outs, jnp.stack(finals)


def s5_mixer(u, prep, s0):
    wd, wk, wc, a_t = prep
    b_, n_tok, _ = u.shape
    n = n_tok // S5_T
    u2 = u.reshape(b_, n, S5_T * W_MIX).transpose(1, 0, 2).reshape(n * b_, S5_T * W_MIX)
    dmat = pmm(u2, wd)
    hmat, fin = s5_scan(dmat.reshape(n, b_, 4 * S5_NS), s0, a_t)
    y2 = pmm_multi([u2, hmat.reshape(n * b_, 4 * S5_NS)], [wk, wc], (False, True))
    y = y2.reshape(n, b_, S5_T, W_MIX).transpose(1, 0, 2, 3).reshape(b_, n_tok, W_MIX)
    return y, fin


def hyena_filters(n_tok, w1, b1, w2, b2, w3, freq):
    f32 = jnp.float32
    t = jnp.arange(n_tok, dtype=f32)[:, None]
    bands = jnp.linspace(1e-4, HY_BANDS - 1, HY_BANDS, dtype=f32)[None]
    ang = 2.0 * math.pi * bands * t / n_tok
    z = jnp.concatenate([t / n_tok, jnp.cos(ang), jnp.sin(ang)], axis=-1)
    hp = lax.Precision.HIGHEST
    h = jnp.sin(freq * (jnp.dot(z, w1, precision=hp) + b1))
    h = jnp.sin(freq * (jnp.dot(h, w2, precision=hp) + b2))
    h = jnp.dot(h, w3, precision=hp)
    t01 = t / max(n_tok - 1, 1)
    deltas = jnp.abs(jnp.linspace(math.log(HY_TARGET) / HY_DECAY_SHORT,
                                  math.log(HY_TARGET) / HY_DECAY_LONG, W_MIX, dtype=f32))
    h = h * jnp.exp(-t01 * jnp.tile(deltas, 2))
    return h / (jnp.sum(jnp.abs(h), axis=0, keepdims=True) + EPS)


def hyena_mixer(p, w_short, w1, b1, w2, b2, w3, freq, bias):
    nb, n_tok, _ = p.shape
    pc = dwconv(p, w_short, 1, 1)
    v, x0, x1 = jnp.split(pc, 3, axis=-1)
    z = x1 * v
    n_pad = max(n_tok, HY_MIN_LEN)
    half = n_pad // HY_LANES

    def frames(t):
        t = jnp.pad(t, [(0, 0)] * (t.ndim - 1) + [(0, n_pad - n_tok)])
        return t.reshape(t.shape[:-1] + (half, HY_LANES))

    filt = hyena_filters(n_tok, w1, b1, w2, b2, w3, freq)
    spec = hyena_fft(frames(filt.T)[None])
    sf, sb = spec[:W_MIX], spec[W_MIX:]
    hfreq = jnp.concatenate([sf[..., :HY_LANES] + sb[..., :HY_LANES],
                             sf[..., HY_LANES:] - sb[..., HY_LANES:]], axis=-1)
    conv = hyena_fft(frames(z.transpose(0, 2, 1)), hfreq)
    conv = conv.reshape(nb, W_MIX, n_pad)[:, :, :n_tok].transpose(0, 2, 1)
    return x0 * (conv + z * bias)


def rglru_mixer(xr, w_conv, w_a, b_a, w_x, b_x, lam, s0):
    b_, n_tok, _ = xr.shape
    xc = dwconv(xr, w_conv, 2, 1)
    xb = xc.reshape(b_, n_tok, RG_BLOCKS, RG_BLOCK)
    outs = []
    finals = []
    for d in range(2):
        r = jax.nn.sigmoid(jnp.einsum('blhi,hij->blhj', xb, w_a[d]) + b_a[d])
        i = jax.nn.sigmoid(jnp.einsum('blhi,hij->blhj', xb, w_x[d]) + b_x[d])
        log_a = -RG_C * r * jax.nn.softplus(-lam[d].reshape(RG_BLOCKS, RG_BLOCK))
        a = jnp.exp(log_a)
        bt = jnp.sqrt(-jnp.expm1(2.0 * log_a)) * (i * xb)
        h, fin = rg_scan(a.reshape(b_, n_tok, W_MIX), bt.reshape(b_, n_tok, W_MIX), s0[d], d == 1)
        finals.append(fin)
        outs.append(h)
    return outs, jnp.stack(finals)


def ffn_dense(h2, x, gate_mod, w_gu, w_down):
    n_tiles = h2.shape[0] // FFN_TM
    return grouped_swiglu(jnp.zeros((n_tiles,), jnp.int32), jnp.full((1,), n_tiles, jnp.int32), h2,
                          w_gu[None].astype(jnp.bfloat16), w_down[None].astype(jnp.bfloat16), nf=2,
                          residual=(x, gate_mod))


def ffn_moe(h2, logits, x, gate_mod, w_gu, w_down, final_g):
    n_tok = h2.shape[0]
    n_slot = TOP_K * n_tok
    top_v, top_i = lax.top_k(logits, TOP_K)
    w = jax.nn.softmax(top_v, axis=-1)
    e_flat = top_i.reshape(-1).astype(jnp.int32)
    onehot = (e_flat[:, None] == jnp.arange(N_EXPERTS, dtype=jnp.int32)[None]).astype(jnp.int32)
    csum = jnp.cumsum(onehot, axis=0)
    cnt = csum[-1]
    rank = jnp.sum(csum * onehot, axis=1) - 1
    padded = ((cnt + FFN_TM - 1) // FFN_TM) * FFN_TM
    ends = jnp.cumsum(padded)
    dest = (ends - padded)[e_flat] + rank
    n_rows = n_slot + N_EXPERTS * FFN_TM
    n_tiles = n_rows // FFN_TM
    src = jnp.zeros((n_rows,), jnp.int32).at[dest].set(jnp.arange(n_slot, dtype=jnp.int32) // TOP_K)
    tile_start = jnp.arange(n_tiles, dtype=jnp.int32) * FFN_TM
    tile_expert = jnp.minimum(jnp.sum((tile_start[:, None] >= ends[None, :]).astype(jnp.int32), axis=1),
                              N_EXPERTS - 1)
    n_used = (ends[-1:] // FFN_TM).astype(jnp.int32)
    xs = h2.at[src].get(mode="promise_in_bounds")
    ys = grouped_swiglu(tile_expert, n_used, xs, w_gu.astype(jnp.bfloat16),
                        w_down.astype(jnp.bfloat16), nf=2)
    yk = ys.at[dest].get(mode="promise_in_bounds").reshape(n_tok, TOP_K * D_MODEL)
    return moe_combine(x, yk, w, gate_mod, final_g)


def kernel(x, c, ctx, c_ctx, mod_w, mod_b, norm1_g, norm2_g, w_in, gla_w_up, gla_b_up,
           s5_lam_re, s5_lam_im, s5_log_dt, s5_b_re, s5_b_im, s5_c_re, s5_c_im, s5_d,
           s5_w_glu, s5_b_glu, hy_w_short, hy_w1, hy_b1, hy_w2, hy_b2, hy_w3, hy_freq,
           hy_bias, rg_w_conv, rg_w_a, rg_b_a, rg_w_x, rg_b_x, rg_lam, w_branch, w_out,
           ffn_w_gu, ffn_w_down, moe_router, moe_router_b, moe_w_gu, moe_w_down, final_g):
    f32, bf16 = jnp.float32, jnp.bfloat16
    n_b, n_lat, _ = x.shape
    n_ctx = ctx.shape[1]
    xs = (x + grid_pos_embed(n_lat, D_MODEL)[None]).reshape(n_b * n_lat, D_MODEL)
    cs = ctx.reshape(n_b * n_ctx, D_MODEL)
    offs = [0]
    for n in IN_SIZES:
        offs.append(offs[-1] + n)
    for l in range(DEPTH):
        last = l == DEPTH - 1
        dense = l % 2 == 0
        j = l // 2
        m_lat = [t[:, None, :] for t in jnp.split(adaln(c, mod_w[l], mod_b[l]), 6, axis=-1)]
        m_ctx = [t[None, None, :] for t in jnp.split(adaln(c_ctx, mod_w[l], mod_b[l]), 6, axis=-1)]
        wl = w_in[l]
        w_mix = jnp.concatenate([wl[:, offs[6]:offs[7]], wl[:, offs[0]:offs[4]], wl[:, offs[5]:offs[6]],
                                 wl[:, offs[7]:offs[9]], wl[:, offs[4]:offs[5]],
                                 jnp.zeros((D_MODEL, P_COLS - P_GDN - 2 * GLA_RANK), f32)], axis=1).astype(bf16)
        w_gate = wl[:, offs[9]:offs[10]].astype(bf16)
        wb, wo, wglu = w_branch[l].astype(bf16), w_out[l].astype(bf16), s5_w_glu[l].astype(bf16)
        s5_ops = s5_prepare(s5_lam_re[l], s5_lam_im[l], s5_log_dt[l], s5_b_re[l], s5_b_im[l],
                            s5_c_re[l], s5_c_im[l], s5_d[l])
        hy_p = (hy_w_short[l], hy_w1[l], hy_b1[l], hy_w2[l], hy_b2[l], hy_w3[l], hy_freq[l], hy_bias[l])
        rg_p = (rg_w_conv[l], rg_w_a[l], rg_b_a[l], rg_w_x[l], rg_b_x[l], rg_lam[l])
        router = None if dense else (moe_router[j], moe_router_b[j])

        def mixers(p2, n_tok, states, with_hyena):
            p = p2.reshape(n_b, n_tok, P_COLS)
            blk = lambda i, n=1: p[..., i * W_MIX:(i + n) * W_MIX]
            flat = lambda t: t.reshape(n_b * n_tok, W_MIX)
            gla_o, gla_s = gla_mixer(p, gla_w_up[l], gla_b_up[l], states[0])
            s5_y, s5_s = s5_mixer(blk(P_S5), s5_ops, states[1])
            rg_o, rg_s = rglru_mixer(blk(P_RGX), *rg_p, states[2])
            hy = flat(hyena_mixer(blk(P_HY, 3), *hy_p)) if with_hyena else None
            return ((flat(gla_o[0]), flat(gla_o[1]), flat(s5_y), hy, flat(rg_o[0]), flat(rg_o[1])),
                    (gla_s, s5_s, rg_s))

        def tail(stream, mods, p2, br, final):
            outs = merge(stream, tuple(mods[:5]), norm1_g[l], norm2_g[l], br[0], br[1], p2, br[2], br[3],
                         br[4], br[5], w_gate, wb, wo, wglu, s5_b_glu[l], router)
            if dense:
                return ffn_dense(outs[1], outs[0], mods[5], ffn_w_gu[j], ffn_w_down[j])
            return ffn_moe(outs[1], outs[2][:, :N_EXPERTS], outs[0], mods[5], moe_w_gu[j], moe_w_down[j],
                           final_g if final else None)

        zero_states = (jnp.zeros((2, n_b, GLA_HEADS, GLA_DV, GLA_DK), f32),
                       jnp.zeros((2, n_b, 2 * S5_NS), f32), jnp.zeros((2, n_b, W_MIX), f32))
        p_ctx = front(cs, m_ctx[0], m_ctx[1], norm1_g[l], w_mix)
        br_ctx, states = mixers(p_ctx, n_ctx, zero_states, not last)
        p_lat = front(xs, m_lat[0], m_lat[1], norm1_g[l], w_mix)
        br_lat, _ = mixers(p_lat, n_lat, states, True)
        xs = tail(xs, m_lat, p_lat, br_lat, last)
        if not last:
            cs = tail(cs, m_ctx, p_ctx, br_ctx, False)
    if (DEPTH - 1) % 2 == 0:
        xs = rmsnorm(xs, final_g)
    return xs.reshape(n_b, n_lat, D_MODEL)
```

```python
import functools
import math

import jax
import jax.numpy as jnp
import numpy as np
from jax import lax
from jax.experimental import pallas as pl
from jax.experimental.pallas import tpu as pltpu

D_MODEL = 1024
DEPTH = 2
GRID_W = 64
EPS = 1e-6
N_BRANCH = 4
W_MIX = D_MODEL // N_BRANCH
GLA_HEADS = 4
GLA_DK = W_MIX // GLA_HEADS
GLA_DV = W_MIX // GLA_HEADS
GLA_RANK = 16
GLA_TAU = 16.0
GLA_CHUNK = 64
S5_GROUP = 16
S5_GROUPS = W_MIX // S5_GROUP
S5_STATE = 64
S5_MAX_RE = -1e-4
HY_BANDS = 16
HY_DECAY_SHORT = 0.3
HY_DECAY_LONG = 1.5
HY_TARGET = 1e-2
RG_BLOCKS = 4
RG_BLOCK = W_MIX // RG_BLOCKS
RG_CONV = 4
RG_C = 8.0
N_EXPERTS = 8
TOP_K = 2
IN_SIZES = (GLA_HEADS * GLA_DK, GLA_HEADS * GLA_DK, GLA_HEADS * GLA_DV, GLA_HEADS * GLA_DV,
            2 * GLA_RANK, W_MIX, 3 * W_MIX, W_MIX, W_MIX, N_BRANCH * D_MODEL)

VMEM_LIMIT_BYTES = 48 * 1024 * 1024


def _mm_kernel(x_ref, w_ref, o_ref):
    o_ref[...] = jnp.dot(x_ref[...].astype(jnp.bfloat16), w_ref[...],
                         preferred_element_type=jnp.float32)


def _pick_tile(n, cap):
    best = None
    for t in range(128, cap + 1, 128):
        if n % t == 0:
            best = t
    return best if best is not None else n


def pmm(x, w):
    lead = x.shape[:-1]
    k = x.shape[-1]
    n = w.shape[-1]
    x2 = x.reshape(-1, k)
    m = x2.shape[0]
    tm = 512 if m % 512 == 0 else m
    if k > 2048 and m % 256 == 0:
        tm = 256
    tn = n if k * n * 2 <= 6 * 1024 * 1024 else _pick_tile(n, 1024)
    out = pl.pallas_call(
        _mm_kernel,
        grid=(m // tm, n // tn),
        in_specs=[pl.BlockSpec((tm, k), lambda i, j: (i, 0)),
                  pl.BlockSpec((k, tn), lambda i, j: (0, j))],
        out_specs=pl.BlockSpec((tm, tn), lambda i, j: (i, j)),
        out_shape=jax.ShapeDtypeStruct((m, n), jnp.float32),
        compiler_params=pltpu.CompilerParams(
            dimension_semantics=("arbitrary", "arbitrary"),
            vmem_limit_bytes=VMEM_LIMIT_BYTES),
    )(x2, w.astype(jnp.bfloat16))
    return out.reshape(lead + (n,))


def _mm_multi_kernel(*refs, transposed):
    o_ref = refs[-1]
    n = (len(refs) - 1) // 2
    acc = None
    for i in range(n):
        dims = (((1,), (1 if transposed[i] else 0,)), ((), ()))
        t = lax.dot_general(refs[i][...].astype(jnp.bfloat16), refs[n + i][...], dims,
                            preferred_element_type=jnp.float32)
        acc = t if acc is None else acc + t
    o_ref[...] = acc


def pmm_multi(xs, ws, transposed, tm=256, tn=512):
    m = xs[0].shape[0]
    n = ws[0].shape[0 if transposed[0] else 1]
    tm = tm if m % tm == 0 else m
    tn = tn if n % tn == 0 else n
    in_specs = ([pl.BlockSpec((tm, x.shape[1]), lambda i, j: (i, 0)) for x in xs]
                + [pl.BlockSpec((tn, w.shape[1]), lambda i, j: (j, 0)) if t else
                   pl.BlockSpec((w.shape[0], tn), lambda i, j: (0, j)) for w, t in zip(ws, transposed)])
    return pl.pallas_call(
        functools.partial(_mm_multi_kernel, transposed=tuple(transposed)),
        grid=(m // tm, n // tn),
        in_specs=in_specs,
        out_specs=pl.BlockSpec((tm, tn), lambda i, j: (i, j)),
        out_shape=jax.ShapeDtypeStruct((m, n), jnp.float32),
        compiler_params=pltpu.CompilerParams(
            dimension_semantics=("arbitrary", "arbitrary"),
            vmem_limit_bytes=VMEM_LIMIT_BYTES),
    )(*xs, *[w.astype(jnp.bfloat16) for w in ws])


RG_SCAN_ROWS = 256


def _rg_scan_kernel(a_ref, b_ref, s0_ref, h_ref, fin_ref, st_ref, *, reverse, tb, nb):
    @pl.when(pl.program_id(0) == 0)
    def _():
        st_ref[...] = s0_ref[...]

    def body(r, hs):
        rr = (tb - 1 - r) if reverse else r
        out = []
        for i in range(nb):
            h = a_ref[i, pl.ds(rr, 1), :] * hs[i] + b_ref[i, pl.ds(rr, 1), :]
            h_ref[i, pl.ds(rr, 1), :] = h
            out.append(h)
        return tuple(out)

    hs = lax.fori_loop(0, tb, body, tuple(st_ref[i:i + 1, :] for i in range(nb)), unroll=8)
    for i in range(nb):
        st_ref[i:i + 1, :] = hs[i]
        fin_ref[i:i + 1, :] = hs[i]


def rg_scan(a, b, s0, reverse):
    nb, n_tok, ch = a.shape
    tb = min(RG_SCAN_ROWS, n_tok)
    nblk = n_tok // tb
    imap = (lambda k: (0, nblk - 1 - k, 0)) if reverse else (lambda k: (0, k, 0))
    return pl.pallas_call(
        functools.partial(_rg_scan_kernel, reverse=reverse, tb=tb, nb=nb),
        grid=(nblk,),
        in_specs=[pl.BlockSpec((nb, tb, ch), imap), pl.BlockSpec((nb, tb, ch), imap),
                  pl.BlockSpec((nb, ch), lambda k: (0, 0))],
        out_specs=[pl.BlockSpec((nb, tb, ch), imap), pl.BlockSpec((nb, ch), lambda k: (0, 0))],
        out_shape=[jax.ShapeDtypeStruct((nb, n_tok, ch), jnp.float32),
                   jax.ShapeDtypeStruct((nb, ch), jnp.float32)],
        scratch_shapes=[pltpu.VMEM((nb, ch), jnp.float32)],
        compiler_params=pltpu.CompilerParams(dimension_semantics=("arbitrary",)),
        name="rg_scan",
    )(a, b, s0)


RG_HALO = 8


def _rg_pre_kernel(x_ref, prev_ref, next_ref, wc_ref, wg_ref, bg_ref, c_ref, af_ref, bf_ref, ab_ref, bb_ref,
                   *, tb):
    k = pl.program_id(1)
    prev = jnp.where(k > 0, prev_ref[0], 0.0)
    nxt = jnp.where(k < pl.num_programs(1) - 1, next_ref[0], 0.0)
    ext = jnp.concatenate([prev, x_ref[0], nxt], axis=0)
    xc = sum(ext[RG_HALO - 2 + j:RG_HALO - 2 + j + tb, :] * wc_ref[j:j + 1, :] for j in range(RG_CONV))
    m = jnp.dot(xc.astype(jnp.bfloat16), wg_ref[...], preferred_element_type=jnp.float32) + bg_ref[...]
    for d, (a_ref, b_ref) in enumerate(((af_ref, bf_ref), (ab_ref, bb_ref))):
        r = jax.nn.sigmoid(m[:, (2 * d) * W_MIX:(2 * d + 1) * W_MIX])
        i = jax.nn.sigmoid(m[:, (2 * d + 1) * W_MIX:(2 * d + 2) * W_MIX])
        log_a = -r * c_ref[d:d + 1, :]
        a_ref[0] = jnp.exp(log_a)
        u = jnp.tanh(log_a)
        b_ref[0] = jnp.sqrt(-2.0 * u / (1.0 - u)) * (i * xc)


def rg_pre(p, w_conv, w_a, b_a, w_x, b_x, lam):
    nb, n_tok, _ = p.shape
    tb = min(ROW_TILE, n_tok)
    nblk = n_tok // tb
    hb = tb // RG_HALO
    n_halo = n_tok // RG_HALO
    col = P_RGX
    blockdiag = lambda w: jax.scipy.linalg.block_diag(*[w[i] for i in range(RG_BLOCKS)])
    wg = jnp.concatenate([blockdiag(w_a[0]), blockdiag(w_x[0]), blockdiag(w_a[1]), blockdiag(w_x[1])],
                         axis=1).astype(jnp.bfloat16)
    bg = jnp.concatenate([b_a[0].reshape(-1), b_x[0].reshape(-1), b_a[1].reshape(-1), b_x[1].reshape(-1)]
                         ).reshape(1, 4 * W_MIX)
    c = RG_C * jax.nn.softplus(-lam)
    blk = pl.BlockSpec((1, tb, W_MIX), lambda b, k: (b, k, 0))
    full = lambda a: pl.BlockSpec(a.shape, lambda b, k: (0,) * a.ndim)
    out = jax.ShapeDtypeStruct((nb, n_tok, W_MIX), jnp.float32)
    return pl.pallas_call(
        functools.partial(_rg_pre_kernel, tb=tb),
        grid=(nb, nblk),
        in_specs=[pl.BlockSpec((1, tb, W_MIX), lambda b, k: (b, k, col)),
                  pl.BlockSpec((1, RG_HALO, W_MIX), lambda b, k: (b, jnp.maximum(k * hb - 1, 0), col)),
                  pl.BlockSpec((1, RG_HALO, W_MIX), lambda b, k: (b, jnp.minimum((k + 1) * hb, n_halo - 1), col)),
                  full(w_conv), full(wg), full(bg), full(c)],
        out_specs=[blk, blk, blk, blk],
        out_shape=[out, out, out, out],
        compiler_params=pltpu.CompilerParams(dimension_semantics=("arbitrary", "arbitrary")),
        name="rg_pre",
    )(p, p, p, w_conv, wg, bg, c)


S5_T = 16
S5_NS = S5_GROUPS * S5_STATE
S5_SCAN_CHUNKS = 64


def _s5_scan_kernel(d_ref, s0_ref, a_ref, h_ref, fin_ref, st_ref, *, rc, nb):
    d = pl.program_id(0)

    @pl.when(pl.program_id(1) == 0)
    def _():
        st_ref[...] = s0_ref[0]

    ar = jnp.broadcast_to(a_ref[0, :, 0:S5_NS], (nb, S5_NS))
    ai = jnp.broadcast_to(a_ref[0, :, S5_NS:2 * S5_NS], (nb, S5_NS))

    def body(r, carry):
        hr, hi = carry
        rr = r + d * (rc - 1 - 2 * r)
        h_ref[rr, :, 0:S5_NS] = hr
        h_ref[rr, :, S5_NS:2 * S5_NS] = hi
        dr = d_ref[rr, :, 0:S5_NS]
        di = d_ref[rr, :, S5_NS:2 * S5_NS]
        return ar * hr - ai * hi + dr, ar * hi + ai * hr + di

    hr, hi = lax.fori_loop(0, rc, body, (st_ref[:, 0:S5_NS], st_ref[:, S5_NS:2 * S5_NS]))
    st_ref[:, 0:S5_NS] = hr
    st_ref[:, S5_NS:2 * S5_NS] = hi
    fin_ref[0, :, 0:S5_NS] = hr
    fin_ref[0, :, S5_NS:2 * S5_NS] = hi


def s5_scan(dmat, s0, a_t):
    n, nb, _ = dmat.shape
    rc = min(S5_SCAN_CHUNKS, n)
    nblk = n // rc
    w = 2 * S5_NS
    imap = lambda d, k: (k + d * (nblk - 1 - 2 * k), 0, d)
    return pl.pallas_call(
        functools.partial(_s5_scan_kernel, rc=rc, nb=nb),
        grid=(2, nblk),
        in_specs=[pl.BlockSpec((rc, nb, w), imap),
                  pl.BlockSpec((1, nb, w), lambda d, k: (d, 0, 0)),
                  pl.BlockSpec((1, 1, w), lambda d, k: (d, 0, 0))],
        out_specs=[pl.BlockSpec((rc, nb, w), imap),
                   pl.BlockSpec((1, nb, w), lambda d, k: (d, 0, 0))],
        out_shape=[jax.ShapeDtypeStruct((n, nb, 2 * w), jnp.float32),
                   jax.ShapeDtypeStruct((2, nb, w), jnp.float32)],
        scratch_shapes=[pltpu.VMEM((nb, w), jnp.float32)],
        compiler_params=pltpu.CompilerParams(dimension_semantics=("arbitrary", "arbitrary"),
                                             vmem_limit_bytes=VMEM_LIMIT_BYTES),
    )(dmat, s0, a_t)


def _cmul(ar, ai, br, bi):
    return ar * br - ai * bi, ar * bi + ai * br


def s5_prepare(lam_re, lam_im, log_dt, b_re, b_im, c_re, c_im, d_skip):
    f32 = jnp.float32
    hp = lax.Precision.HIGHEST
    t_len, g_n, p_n, h_n = S5_T, S5_GROUPS, S5_STATE, S5_GROUP
    bf16 = jnp.bfloat16
    eye_g = jnp.eye(g_n, dtype=f32)
    mask_gp = jnp.repeat(eye_g, p_n, axis=1).astype(bf16)[None, :, None, :]
    ar_t = jnp.arange(t_len)
    wd, wc, kk, a_t = [], [], [], []
    for d in range(2):
        lr = jnp.minimum(lam_re[d], S5_MAX_RE)
        li = lam_im[d]
        dt = jnp.exp(log_dt[d])[:, None]
        tt = jnp.arange(t_len + 1, dtype=f32)[:, None, None]
        mag = jnp.exp(lr * dt * tt)
        ang = li * dt * tt
        pr, pi = mag * jnp.cos(ang), mag * jnp.sin(ang)
        nr, ni = pr[1] - 1.0, pi[1]
        den = lr * lr + li * li
        qr, qi = (nr * lr + ni * li) / den, (ni * lr - nr * li) / den
        bbr, bbi = _cmul(qr[..., None], qi[..., None], b_re[d], b_im[d])
        cr, ci = c_re[d], c_im[d]

        idx = (t_len - 1 - ar_t) if d == 0 else ar_t
        wr, wi = _cmul(pr[idx][..., None], pi[idx][..., None], bbr[None], bbi[None])

        def place_d(w):
            wt = w.transpose(0, 3, 1, 2).reshape(t_len, 1, h_n, g_n * p_n).astype(bf16)
            return (wt * mask_gp).reshape(t_len * W_MIX, g_n * p_n)

        wd.append(jnp.concatenate([place_d(wr), place_d(wi)], axis=1))

        idx2 = (ar_t + 1) if d == 0 else (t_len - ar_t)
        cwr, cwi = _cmul(cr[None], ci[None], pr[idx2][:, :, None, :], pi[idx2][:, :, None, :])

        def place_c(w):
            wt = w.transpose(0, 2, 1, 3).reshape(t_len, 1, h_n, g_n * p_n).astype(bf16)
            return (wt * mask_gp).reshape(t_len * W_MIX, g_n * p_n)

        wc.append(jnp.concatenate([place_c(cwr), place_c(-cwi)], axis=1))

        er, ei = _cmul(pr[:t_len][:, :, None, :], pi[:t_len][:, :, None, :], cr[None], ci[None])
        kk.append(jnp.einsum('tghp,gpk->tghk', er, bbr, precision=hp)
                  - jnp.einsum('tghp,gpk->tghk', ei, bbi, precision=hp))
        a_t.append(jnp.concatenate([pr[t_len].reshape(1, -1), pi[t_len].reshape(1, -1)], axis=1))

    lag = ar_t[None, :] - ar_t[:, None]
    mf = jnp.where((lag >= 0)[..., None, None, None], kk[0][jnp.clip(lag, 0, t_len - 1)], 0.0)
    mb = jnp.where((lag <= 0)[..., None, None, None], kk[1][jnp.clip(-lag, 0, t_len - 1)], 0.0)
    skip = (jnp.eye(t_len, dtype=f32)[:, :, None, None, None] * d_skip[None, None, :, :, None]
            * jnp.eye(h_n, dtype=f32)[None, None, None])
    m = mf + mb + skip
    mt = m.transpose(0, 4, 1, 2, 3).reshape(t_len, 1, h_n, t_len * W_MIX).astype(bf16)
    mask_igh = jnp.tile(jnp.repeat(eye_g, h_n, axis=1), (1, t_len)).astype(bf16)[None, :, None, :]
    wk = (mt * mask_igh).reshape(t_len * W_MIX, t_len * W_MIX)
    return (jnp.concatenate(wd, axis=1), wk, jnp.concatenate(wc, axis=1), jnp.stack(a_t))


GLA_BLOCK = 512


def _gla_kernel(q_ref, k_ref, v_ref, g_ref, wup_ref, bup_ref, s0_ref, o_ref, fin_ref, st_ref,
                *, reverse, tb):
    f32, bf16 = jnp.float32, jnp.bfloat16
    hp = lax.Precision.HIGHEST
    cc = GLA_CHUNK

    @pl.when(pl.program_id(1) == 0)
    def _():
        st_ref[...] = s0_ref[0]

    r_i = lax.broadcasted_iota(jnp.int32, (cc, cc), 0)
    c_i = lax.broadcasted_iota(jnp.int32, (cc, cc), 1)
    keep = (c_i >= r_i) if reverse else (c_i <= r_i)
    tri = keep.astype(f32)
    nt = (((1,), (1,)), ((), ()))
    tn = (((0,), (0,)), ((), ()))
    n_ch = tb // cc
    chunks = range(n_ch)
    heads = range(GLA_HEADS)
    hsl = [slice(h * GLA_DK, (h + 1) * GLA_DK) for h in heads]
    rows = [slice(c * cc, (c + 1) * cc) for c in chunks]

    z = jnp.dot(g_ref[0], wup_ref[...], precision=hp, preferred_element_type=f32) + bup_ref[...]
    la = (jnp.minimum(z, 0.0) - jnp.log1p(jnp.exp(-jnp.abs(z)))) * (1.0 / GLA_TAU)
    cum = [jnp.dot(tri, la[rows[c]], precision=hp, preferred_element_type=f32) for c in chunks]
    last = [cm[0:1, :] if reverse else cm[cc - 1:cc, :] for cm in cum]
    k = [k_ref[0, rows[c], :] for c in chunks]
    q_in = [(q_ref[0, rows[c], :] * (GLA_DK ** -0.5) * jnp.exp(cum[c])).astype(bf16) for c in chunks]
    k_in = [(k[c] * jnp.exp(-cum[c])).astype(bf16) for c in chunks]
    k_out = [(k[c] * jnp.exp(last[c] - cum[c])).astype(bf16) for c in chunks]
    dec = [jnp.exp(last[c]) for c in chunks]
    vb = [v_ref[0, rows[c], :].astype(bf16) for c in chunks]
    att = [[jnp.where(keep, lax.dot_general(q_in[c][:, s], k_in[c][:, s], nt, preferred_element_type=f32),
                      0.0).astype(bf16) for s in hsl] for c in chunks]
    o_intra = [[jnp.dot(att[c][h], vb[c][:, hsl[h]], preferred_element_type=f32) for h in heads]
               for c in chunks]
    d_state = [[lax.dot_general(vb[c][:, s], k_out[c][:, s], tn, preferred_element_type=f32) for s in hsl]
               for c in chunks]

    st = [st_ref[h] for h in heads]
    st_in = [None] * n_ch
    for c in (reversed(chunks) if reverse else chunks):
        st_in[c] = [s.astype(bf16) for s in st]
        st = [st[h] * dec[c][:, hsl[h]] + d_state[c][h] for h in heads]
    for h in heads:
        st_ref[h] = st[h]
        fin_ref[0, h] = st[h]

    for c in chunks:
        o_ref[0, rows[c], :] = jnp.concatenate(
            [o_intra[c][h] + lax.dot_general(q_in[c][:, hsl[h]], st_in[c][h], nt, preferred_element_type=f32)
             for h in heads], axis=1)


def gla_dir(p, gd, w_up, b_up, s0, reverse):
    nb, n_tok, _ = p.shape
    tb = min(GLA_BLOCK, n_tok)
    nblk = n_tok // tb
    blk = (lambda k: nblk - 1 - k) if reverse else (lambda k: k)
    col = lambda c: pl.BlockSpec((1, tb, W_MIX), lambda b, k: (b, blk(k), c))
    st_shape = (GLA_HEADS, GLA_DV, GLA_DK)
    return pl.pallas_call(
        functools.partial(_gla_kernel, reverse=reverse, tb=tb),
        grid=(nb, nblk),
        in_specs=[col(P_Q), col(P_Q + 1), col(P_Q + 2),
                  pl.BlockSpec((1, tb, GLA_RANK), lambda b, k: (b, blk(k), 0)),
                  pl.BlockSpec((GLA_RANK, W_MIX), lambda b, k: (0, 0)),
                  pl.BlockSpec((1, W_MIX), lambda b, k: (0, 0)),
                  pl.BlockSpec((1,) + st_shape, lambda b, k: (b, 0, 0, 0))],
        out_specs=[pl.BlockSpec((1, tb, W_MIX), lambda b, k: (b, blk(k), 0)),
                   pl.BlockSpec((1,) + st_shape, lambda b, k: (b, 0, 0, 0))],
        out_shape=[jax.ShapeDtypeStruct((nb, n_tok, W_MIX), jnp.float32),
                   jax.ShapeDtypeStruct((nb,) + st_shape, jnp.float32)],
        scratch_shapes=[pltpu.VMEM(st_shape, jnp.float32)],
        compiler_params=pltpu.CompilerParams(dimension_semantics=("arbitrary", "arbitrary"),
                                             vmem_limit_bytes=VMEM_LIMIT_BYTES),
        name="gla",
    )(p, p, p, gd, w_up, b_up.reshape(1, W_MIX), s0)


FFN_TM = 512


def _swiglu_kernel(te_ref, nu_ref, x_ref, wg_ref, wu_ref, wd_ref, *rest, nf, residual):
    del te_ref
    o_ref = rest[-1]
    j = pl.program_id(1)

    @pl.when(pl.program_id(0) < nu_ref[0])
    def _():
        x = x_ref[...]
        g = jnp.dot(x, wg_ref[0], preferred_element_type=jnp.float32)
        u = jnp.dot(x, wu_ref[0], preferred_element_type=jnp.float32)
        a = (g * jax.nn.sigmoid(g) * u).astype(jnp.bfloat16)
        part = jnp.dot(a, wd_ref[0], preferred_element_type=jnp.float32)

        @pl.when(j == 0)
        def _():
            o_ref[...] = part

        @pl.when(j > 0)
        def _():
            o_ref[...] += part

        if residual:
            xres_ref, gm_ref = rest[:2]

            @pl.when(j == nf - 1)
            def _():
                o_ref[...] = xres_ref[...] + gm_ref[0] * o_ref[...]


def grouped_swiglu(tile_expert, n_used, xs, w_gu, w_down, nf, residual=None):
    m, d = xs.shape
    f = w_down.shape[1]
    tf = f // nf
    n_tiles = m // FFN_TM
    in_specs = [pl.BlockSpec((FFN_TM, d), lambda t, j, te, nu: (t, 0)),
                pl.BlockSpec((1, d, tf), lambda t, j, te, nu: (te[t], 0, j)),
                pl.BlockSpec((1, d, tf), lambda t, j, te, nu: (te[t], 0, nf + j)),
                pl.BlockSpec((1, tf, d), lambda t, j, te, nu: (te[t], j, 0))]
    extra = ()
    if residual is not None:
        rows_per_mod = m // residual[1].shape[0]
        in_specs += [pl.BlockSpec((FFN_TM, d), lambda t, j, te, nu: (t, 0)),
                     pl.BlockSpec((1, 1, d), lambda t, j, te, nu: ((t * FFN_TM) // rows_per_mod, 0, 0))]
        extra = tuple(residual)
    grid_spec = pltpu.PrefetchScalarGridSpec(
        num_scalar_prefetch=2,
        grid=(n_tiles, nf),
        in_specs=in_specs,
        out_specs=pl.BlockSpec((FFN_TM, d), lambda t, j, te, nu: (t, 0)))
    return pl.pallas_call(
        functools.partial(_swiglu_kernel, nf=nf, residual=residual is not None),
        grid_spec=grid_spec,
        out_shape=jax.ShapeDtypeStruct((m, d), jnp.float32),
        compiler_params=pltpu.CompilerParams(
            dimension_semantics=("arbitrary", "arbitrary"),
            vmem_limit_bytes=VMEM_LIMIT_BYTES),
        name="grouped_swiglu",
    )(tile_expert, n_used, xs, w_gu, w_gu, w_down, *extra)


ROW_TILE = 512
P_HY, P_Q, P_OG, P_S5, P_RGX, P_RGG = 0, 3, 6, 7, 8, 9
P_GDN = 10 * W_MIX
P_COLS = P_GDN + 128


def _rms_mod(x, g, shift, scale):
    y = x * lax.rsqrt(jnp.mean(x * x, axis=-1, keepdims=True) + EPS) * g
    return y * (1.0 + scale) + shift


def _front_kernel(x_ref, sh_ref, sc_ref, g_ref, w_ref, o_ref):
    h = _rms_mod(x_ref[...], g_ref[...], sh_ref[0], sc_ref[0])
    o_ref[...] = jnp.dot(h.astype(jnp.bfloat16), w_ref[...], preferred_element_type=jnp.float32)


def _mod_spec(rows_per_mod):
    return pl.BlockSpec((1, 1, D_MODEL), lambda i: ((i * ROW_TILE) // rows_per_mod, 0, 0))


def front(x, shift, scale, g, w):
    m = x.shape[0]
    n_mod = shift.shape[0]
    n_out = w.shape[1]
    mod = _mod_spec(m // n_mod)
    return pl.pallas_call(
        _front_kernel,
        grid=(m // ROW_TILE,),
        in_specs=[pl.BlockSpec((ROW_TILE, D_MODEL), lambda i: (i, 0)), mod, mod,
                  pl.BlockSpec((1, D_MODEL), lambda i: (0, 0)),
                  pl.BlockSpec((D_MODEL, n_out), lambda i: (0, 0))],
        out_specs=pl.BlockSpec((ROW_TILE, n_out), lambda i: (i, 0)),
        out_shape=jax.ShapeDtypeStruct((m, n_out), jnp.float32),
        compiler_params=pltpu.CompilerParams(dimension_semantics=("arbitrary",),
                                             vmem_limit_bytes=VMEM_LIMIT_BYTES),
        name="front",
    )(x, shift, scale, g.reshape(1, D_MODEL), w)


MERGE_TILE = 256
ROUTER_PAD = 128


def _merge_kernel(*refs, with_router):
    (x_ref, sh1_ref, sc1_ref, gm_ref, sh2_ref, sc2_ref, g1_ref, g2_ref,
     of_ref, ob_ref, og_ref, s5_ref, hy_ref, rf_ref, rb_ref, rgg_ref,
     wg_ref, wb_ref, wo_ref, wglu_ref, bglu_ref, havg_ref) = refs[:22]
    f32, bf16 = jnp.float32, jnp.bfloat16
    hp = lax.Precision.HIGHEST
    x = x_ref[...]
    hb = _rms_mod(x, g1_ref[...], sh1_ref[0], sc1_ref[0]).astype(bf16)

    o = of_ref[...] + ob_ref[...]
    ms = jnp.dot(o * o, havg_ref[...], precision=hp, preferred_element_type=f32)
    og = og_ref[...]
    gla = o * lax.rsqrt(ms + EPS) * (og * jax.nn.sigmoid(og))
    g5 = jax.nn.gelu(s5_ref[...])
    s5o = g5 * jax.nn.sigmoid(jnp.dot(g5.astype(bf16), wglu_ref[...], preferred_element_type=f32)
                              + bglu_ref[...])
    rgo = (rf_ref[...] + rb_ref[...]) * jax.nn.gelu(rgg_ref[...])
    branches = (gla, s5o, hy_ref[...], rgo)

    y = None
    for k in range(N_BRANCH):
        gate = jax.nn.sigmoid(jnp.dot(hb, wg_ref[:, k * D_MODEL:(k + 1) * D_MODEL],
                                      preferred_element_type=f32))
        t = gate * jnp.dot(branches[k].astype(bf16), wb_ref[k], preferred_element_type=f32)
        y = t if y is None else y + t
    out = jnp.dot(y.astype(bf16), wo_ref[...], preferred_element_type=f32)
    xn = x + gm_ref[0] * out
    h2 = _rms_mod(xn, g2_ref[...], sh2_ref[0], sc2_ref[0])
    if with_router:
        rw_ref, rb2_ref, xo_ref, h2_ref, lg_ref = refs[22:]
        lg_ref[...] = jnp.dot(h2, rw_ref[...], precision=hp, preferred_element_type=f32) + rb2_ref[...]
    else:
        xo_ref, h2_ref = refs[22:]
    xo_ref[...] = xn
    h2_ref[...] = h2.astype(bf16)


def merge(x, mods, g1, g2, o_f, o_b, p, s5y, hy, r_f, r_b, wg, wb, wo, wglu, bglu, router=None):
    m = x.shape[0]
    tm = MERGE_TILE
    n_mod = mods[0].shape[0]
    rows_per_mod = m // n_mod
    mod = pl.BlockSpec((1, 1, D_MODEL), lambda i: ((i * tm) // rows_per_mod, 0, 0))
    row = pl.BlockSpec((tm, D_MODEL), lambda i: (i, 0))
    br = pl.BlockSpec((tm, W_MIX), lambda i: (i, 0))
    pcol = lambda c: pl.BlockSpec((tm, W_MIX), lambda i: (i, c))
    full = lambda a: pl.BlockSpec(a.shape, lambda i: (0,) * a.ndim)
    head = jnp.arange(W_MIX) // GLA_DV
    havg = (head[:, None] == head[None, :]).astype(jnp.float32) / GLA_DV
    vec = lambda v: v.reshape(1, -1)
    consts = [wg, wb, wo, wglu, vec(bglu), havg]
    out_specs = [row, pl.BlockSpec((tm, D_MODEL), lambda i: (i, 0))]
    out_shape = [jax.ShapeDtypeStruct((m, D_MODEL), jnp.float32),
                 jax.ShapeDtypeStruct((m, D_MODEL), jnp.bfloat16)]
    if router is not None:
        rw, rbias = router
        pad = ROUTER_PAD - rw.shape[1]
        consts += [jnp.pad(rw, ((0, 0), (0, pad))), jnp.pad(rbias, (0, pad)).reshape(1, -1)]
        out_specs.append(pl.BlockSpec((tm, ROUTER_PAD), lambda i: (i, 0)))
        out_shape.append(jax.ShapeDtypeStruct((m, ROUTER_PAD), jnp.float32))
    in_specs = ([row] + [mod] * 5 + [full(vec(g1)), full(vec(g2)), br, br, pcol(P_OG), br, br, br, br,
                                     pcol(P_RGG)] + [full(a) for a in consts])
    return pl.pallas_call(
        functools.partial(_merge_kernel, with_router=router is not None),
        grid=(m // tm,),
        in_specs=in_specs,
        out_specs=out_specs,
        out_shape=out_shape,
        compiler_params=pltpu.CompilerParams(dimension_semantics=("arbitrary",),
                                             vmem_limit_bytes=VMEM_LIMIT_BYTES),
        name="merge",
    )(x, *mods, vec(g1), vec(g2), o_f, o_b, p, s5y, hy, r_f, r_b, p, *consts)


def _combine_kernel(x_ref, y0_ref, y1_ref, w_ref, gm_ref, g_ref, o_ref, *, final_norm):
    w = w_ref[...]
    y = w[:, 0:1] * y0_ref[...] + w[:, 1:2] * y1_ref[...]
    xn = x_ref[...] + gm_ref[0] * y
    if final_norm:
        xn = xn * lax.rsqrt(jnp.mean(xn * xn, axis=-1, keepdims=True) + EPS) * g_ref[...]
    o_ref[...] = xn


def moe_combine(x, yk, w, gate_mod, final_g):
    m = x.shape[0]
    n_mod = gate_mod.shape[0]
    g = jnp.ones((1, D_MODEL), jnp.float32) if final_g is None else final_g.reshape(1, D_MODEL)
    return pl.pallas_call(
        functools.partial(_combine_kernel, final_norm=final_g is not None),
        grid=(m // ROW_TILE,),
        in_specs=[pl.BlockSpec((ROW_TILE, D_MODEL), lambda i: (i, 0)),
                  pl.BlockSpec((ROW_TILE, D_MODEL), lambda i: (i, 0)),
                  pl.BlockSpec((ROW_TILE, D_MODEL), lambda i: (i + m // ROW_TILE, 0)),
                  pl.BlockSpec((ROW_TILE, TOP_K), lambda i: (i, 0)),
                  _mod_spec(m // n_mod),
                  pl.BlockSpec((1, D_MODEL), lambda i: (0, 0))],
        out_specs=pl.BlockSpec((ROW_TILE, D_MODEL), lambda i: (i, 0)),
        out_shape=jax.ShapeDtypeStruct((m, D_MODEL), jnp.float32),
        compiler_params=pltpu.CompilerParams(dimension_semantics=("arbitrary",)),
        name="moe_combine",
    )(x, yk, yk, w, gate_mod, g)


HY_LANES = 128
HY_SLABS = 8
HY_MIN_LEN = 1024


def _split_bf16(a):
    hi = a.astype(jnp.bfloat16)
    lo = (a - hi.astype(jnp.float32)).astype(jnp.bfloat16)
    return hi, lo


def _dot3(a_hi, a_lo, b_hi, b_lo):
    d = functools.partial(jnp.dot, preferred_element_type=jnp.float32)
    return d(a_hi, b_hi) + (d(a_lo, b_hi) + d(a_hi, b_lo))


def _hyena_dft_consts(n1):
    n = n1 * HY_LANES
    ka = np.arange(n1, dtype=np.float64)[:, None]
    f1_ang = 2.0 * np.pi * ka * np.arange(n1 // 2, dtype=np.float64)[None, :] / n1
    f1r, f1i = np.cos(f1_ang), -np.sin(f1_ang)
    tw_ang = 2.0 * np.pi * ka * np.arange(HY_LANES, dtype=np.float64)[None, :] / n
    lo = np.arange(HY_LANES, dtype=np.float64)
    f2_ang = 2.0 * np.pi * lo[:, None] * lo[None, :] / HY_LANES
    f2r, f2i = np.cos(f2_ang), -np.sin(f2_ang)
    fwd_rows = np.concatenate([f1r, f1i], axis=0)
    fwd_lanes = np.block([[f2r, f2i], [-f2i, f2r]])
    inv_lanes = np.block([[f2r, -f2i], [f2i, f2r]])
    inv_rows = np.concatenate([f1r.T, f1i.T], axis=1) / n
    out = []
    for m in (fwd_rows, fwd_lanes, inv_lanes, inv_rows):
        m32 = jnp.asarray(m, jnp.float32)
        out.extend(_split_bf16(m32))
    return out + [jnp.asarray(np.cos(tw_ang), jnp.float32), jnp.asarray(-np.sin(tw_ang), jnp.float32)]


def _hyena_fft_kernel(*refs, n1, ns, spectrum):
    if spectrum:
        z_ref, f1h, f1l, f2h, f2l, twr_ref, twi_ref, o_ref = refs
    else:
        z_ref, hf_ref, f1h, f1l, f2h, f2l, g2h, g2l, fih, fil, twr_ref, twi_ref, o_ref = refs
    w = HY_LANES
    twr, twi = twr_ref[...], twi_ref[...]
    z2 = jnp.concatenate([z_ref[0, s] for s in range(ns)], axis=1)
    a2 = _dot3(f1h[...], f1l[...], *_split_bf16(z2))
    rows = []
    for s in range(ns):
        r, i = a2[:n1, s * w:(s + 1) * w], a2[n1:, s * w:(s + 1) * w]
        rows.append(jnp.concatenate([r * twr - i * twi, r * twi + i * twr], axis=1))
    x = _dot3(*_split_bf16(jnp.concatenate(rows, axis=0)), f2h[...], f2l[...])
    if spectrum:
        o_ref[...] = x.reshape(ns, n1, 2 * w)
        return
    h = hf_ref[...].reshape(ns * n1, 2 * w)
    xr, xi, hr, hi = x[:, :w], x[:, w:], h[:, :w], h[:, w:]
    y = jnp.concatenate([xr * hr - xi * hi, xr * hi + xi * hr], axis=1)
    g = _dot3(*_split_bf16(y), g2h[...], g2l[...])
    cr, ci = [], []
    for s in range(ns):
        gr, gi = g[s * n1:(s + 1) * n1, :w], g[s * n1:(s + 1) * n1, w:]
        cr.append(gr * twr + gi * twi)
        ci.append(gi * twr - gr * twi)
    gc = jnp.concatenate([jnp.concatenate(cr, axis=1), jnp.concatenate(ci, axis=1)], axis=0)
    y2 = _dot3(fih[...], fil[...], *_split_bf16(gc))
    for s in range(ns):
        o_ref[0, s] = y2[:, s * w:(s + 1) * w]


def hyena_fft(zt, hf=None):
    nb, ch, half, w = zt.shape
    n1 = 2 * half
    ns = HY_SLABS
    f1h, f1l, f2h, f2l, g2h, g2l, fih, fil, twr, twi = _hyena_dft_consts(n1)
    full = lambda a: pl.BlockSpec(a.shape, lambda b, c: (0,) * a.ndim)
    zspec = pl.BlockSpec((1, ns, half, w), lambda b, c: (b, c, 0, 0))
    if hf is None:
        consts = [f1h, f1l, f2h, f2l, twr, twi]
        in_specs, args = [zspec], [zt]
        out_spec = pl.BlockSpec((ns, n1, 2 * w), lambda b, c: (b * (ch // ns) + c, 0, 0))
        out_shape = jax.ShapeDtypeStruct((nb * ch, n1, 2 * w), jnp.float32)
    else:
        consts = [f1h, f1l, f2h, f2l, g2h, g2l, fih, fil, twr, twi]
        in_specs = [zspec, pl.BlockSpec((ns, n1, 2 * w), lambda b, c: (c, 0, 0))]
        args = [zt, hf]
        out_spec = zspec
        out_shape = jax.ShapeDtypeStruct(zt.shape, jnp.float32)
    return pl.pallas_call(
        functools.partial(_hyena_fft_kernel, n1=n1, ns=ns, spectrum=hf is None),
        grid=(nb, ch // ns),
        in_specs=in_specs + [full(a) for a in consts],
        out_specs=out_spec,
        out_shape=out_shape,
        compiler_params=pltpu.CompilerParams(dimension_semantics=("arbitrary", "arbitrary"),
                                             vmem_limit_bytes=VMEM_LIMIT_BYTES),
        name="hyena_fft",
    )(*args, *consts)


def rmsnorm(x, g):
    y = x * lax.rsqrt(jnp.mean(x * x, axis=-1, keepdims=True) + EPS)
    return y * g


def adaln(cond, w, b):
    return jax.nn.silu(cond) @ w + b


def grid_pos_embed(n_tokens, dim):
    rows = n_tokens // GRID_W
    q = dim // 4
    omega = 1.0 / (10000.0 ** (jnp.arange(q, dtype=jnp.float32) / q))
    r = jnp.arange(rows, dtype=jnp.float32)[:, None] * omega
    cc = jnp.arange(GRID_W, dtype=jnp.float32)[:, None] * omega
    er = jnp.concatenate([jnp.sin(r), jnp.cos(r)], axis=-1)
    ec = jnp.concatenate([jnp.sin(cc), jnp.cos(cc)], axis=-1)
    emb = jnp.concatenate([jnp.broadcast_to(er[:, None], (rows, GRID_W, dim // 2)),
                           jnp.broadcast_to(ec[None], (rows, GRID_W, dim // 2))], axis=-1)
    return emb.reshape(rows * GRID_W, dim)


def dwconv(x, w, pad_l, pad_r):
    n = x.shape[1]
    xp = jnp.pad(x, ((0, 0), (pad_l, pad_r), (0, 0)))
    return sum(xp[:, k:k + n] * w[k] for k in range(w.shape[0]))


def gla_mixer(p, w_up, b_up, s0):
    gdn = p[..., P_GDN:P_GDN + 2 * GLA_RANK]
    outs, finals = [], []
    for d in range(2):
        od, sd = gla_dir(p, gdn[..., d * GLA_RANK:(d + 1) * GLA_RANK], w_up[d], b_up[d], s0[d], d == 1)
        outs.append(od)
        finals.append(sd)
    return outs, jnp.stack(finals)


def s5_mixer(u, prep, s0):
    wd, wk, wc, a_t = prep
    b_, n_tok, _ = u.shape
    n = n_tok // S5_T
    u2 = u.reshape(b_, n, S5_T * W_MIX).transpose(1, 0, 2).reshape(n * b_, S5_T * W_MIX)
    dmat = pmm(u2, wd)
    hmat, fin = s5_scan(dmat.reshape(n, b_, 4 * S5_NS), s0, a_t)
    y2 = pmm_multi([u2, hmat.reshape(n * b_, 4 * S5_NS)], [wk, wc], (False, True))
    y = y2.reshape(n, b_, S5_T, W_MIX).transpose(1, 0, 2, 3).reshape(b_, n_tok, W_MIX)
    return y, fin


def hyena_filters(n_tok, w1, b1, w2, b2, w3, freq):
    f32 = jnp.float32
    t = jnp.arange(n_tok, dtype=f32)[:, None]
    bands = jnp.linspace(1e-4, HY_BANDS - 1, HY_BANDS, dtype=f32)[None]
    ang = 2.0 * math.pi * bands * t / n_tok
    z = jnp.concatenate([t / n_tok, jnp.cos(ang), jnp.sin(ang)], axis=-1)
    hp = lax.Precision.HIGHEST
    h = jnp.sin(freq * (jnp.dot(z, w1, precision=hp) + b1))
    h = jnp.sin(freq * (jnp.dot(h, w2, precision=hp) + b2))
    h = jnp.dot(h, w3, precision=hp)
    t01 = t / max(n_tok - 1, 1)
    deltas = jnp.abs(jnp.linspace(math.log(HY_TARGET) / HY_DECAY_SHORT,
                                  math.log(HY_TARGET) / HY_DECAY_LONG, W_MIX, dtype=f32))
    h = h * jnp.exp(-t01 * jnp.tile(deltas, 2))
    return h / (jnp.sum(jnp.abs(h), axis=0, keepdims=True) + EPS)


def hyena_mixer(p, w_short, w1, b1, w2, b2, w3, freq, bias):
    nb, n_tok, _ = p.shape
    pc = dwconv(p, w_short, 1, 1)
    v, x0, x1 = jnp.split(pc, 3, axis=-1)
    z = x1 * v
    n_pad = max(n_tok, HY_MIN_LEN)
    half = n_pad // HY_LANES

    def frames(t):
        t = jnp.pad(t, [(0, 0)] * (t.ndim - 1) + [(0, n_pad - n_tok)])
        return t.reshape(t.shape[:-1] + (half, HY_LANES))

    filt = hyena_filters(n_tok, w1, b1, w2, b2, w3, freq)
    spec = hyena_fft(frames(filt.T)[None])
    sf, sb = spec[:W_MIX], spec[W_MIX:]
    hfreq = jnp.concatenate([sf[..., :HY_LANES] + sb[..., :HY_LANES],
                             sf[..., HY_LANES:] - sb[..., HY_LANES:]], axis=-1)
    conv = hyena_fft(frames(z.transpose(0, 2, 1)), hfreq)
    conv = conv.reshape(nb, W_MIX, n_pad)[:, :, :n_tok].transpose(0, 2, 1)
    return x0 * (conv + z * bias)


def rglru_mixer(p, w_conv, w_a, b_a, w_x, b_x, lam, s0):
    a_f, b_f, a_b, b_b = rg_pre(p, w_conv, w_a, b_a, w_x, b_x, lam)
    h_f, fin_f = rg_scan(a_f, b_f, s0[0], False)
    h_b, fin_b = rg_scan(a_b, b_b, s0[1], True)
    return [h_f, h_b], jnp.stack([fin_f, fin_b])


def ffn_dense(h2, x, gate_mod, w_gu, w_down):
    n_tiles = h2.shape[0] // FFN_TM
    return grouped_swiglu(jnp.zeros((n_tiles,), jnp.int32), jnp.full((1,), n_tiles, jnp.int32), h2,
                          w_gu[None].astype(jnp.bfloat16), w_down[None].astype(jnp.bfloat16), nf=2,
                          residual=(x, gate_mod))


def ffn_moe(h2, logits, x, gate_mod, w_gu, w_down, final_g):
    n_tok = h2.shape[0]
    n_slot = TOP_K * n_tok
    top_v, top_i = lax.top_k(logits, TOP_K)
    w = jax.nn.softmax(top_v, axis=-1)
    e_flat = top_i.T.reshape(-1).astype(jnp.int32)
    onehot = (e_flat[:, None] == jnp.arange(N_EXPERTS, dtype=jnp.int32)[None]).astype(jnp.int32)
    csum = jnp.cumsum(onehot, axis=0)
    cnt = csum[-1]
    rank = jnp.sum(csum * onehot, axis=1) - 1
    padded = ((cnt + FFN_TM - 1) // FFN_TM) * FFN_TM
    ends = jnp.cumsum(padded)
    dest = (ends - padded)[e_flat] + rank
    n_rows = n_slot + N_EXPERTS * FFN_TM
    n_tiles = n_rows // FFN_TM
    src = jnp.zeros((n_rows,), jnp.int32).at[dest].set(jnp.arange(n_slot, dtype=jnp.int32) % n_tok)
    tile_start = jnp.arange(n_tiles, dtype=jnp.int32) * FFN_TM
    tile_expert = jnp.minimum(jnp.sum((tile_start[:, None] >= ends[None, :]).astype(jnp.int32), axis=1),
                              N_EXPERTS - 1)
    n_used = (ends[-1:] // FFN_TM).astype(jnp.int32)
    xs = h2.at[src].get(mode="promise_in_bounds")
    ys = grouped_swiglu(tile_expert, n_used, xs, w_gu.astype(jnp.bfloat16),
                        w_down.astype(jnp.bfloat16), nf=2)
    yk = ys.at[dest].get(mode="promise_in_bounds")
    return moe_combine(x, yk, w, gate_mod, final_g)


def kernel(x, c, ctx, c_ctx, mod_w, mod_b, norm1_g, norm2_g, w_in, gla_w_up, gla_b_up,
           s5_lam_re, s5_lam_im, s5_log_dt, s5_b_re, s5_b_im, s5_c_re, s5_c_im, s5_d,
           s5_w_glu, s5_b_glu, hy_w_short, hy_w1, hy_b1, hy_w2, hy_b2, hy_w3, hy_freq,
           hy_bias, rg_w_conv, rg_w_a, rg_b_a, rg_w_x, rg_b_x, rg_lam, w_branch, w_out,
           ffn_w_gu, ffn_w_down, moe_router, moe_router_b, moe_w_gu, moe_w_down, final_g):
    f32, bf16 = jnp.float32, jnp.bfloat16
    n_b, n_lat, _ = x.shape
    n_ctx = ctx.shape[1]
    xs = (x + grid_pos_embed(n_lat, D_MODEL)[None]).reshape(n_b * n_lat, D_MODEL)
    cs = ctx.reshape(n_b * n_ctx, D_MODEL)
    offs = [0]
    for n in IN_SIZES:
        offs.append(offs[-1] + n)
    for l in range(DEPTH):
        last = l == DEPTH - 1
        dense = l % 2 == 0
        j = l // 2
        m_lat = [t[:, None, :] for t in jnp.split(adaln(c, mod_w[l], mod_b[l]), 6, axis=-1)]
        m_ctx = [t[None, None, :] for t in jnp.split(adaln(c_ctx, mod_w[l], mod_b[l]), 6, axis=-1)]
        wl = w_in[l]
        w_mix = jnp.concatenate([wl[:, offs[6]:offs[7]], wl[:, offs[0]:offs[4]], wl[:, offs[5]:offs[6]],
                                 wl[:, offs[7]:offs[9]], wl[:, offs[4]:offs[5]],
                                 jnp.zeros((D_MODEL, P_COLS - P_GDN - 2 * GLA_RANK), f32)], axis=1).astype(bf16)
        w_gate = wl[:, offs[9]:offs[10]].astype(bf16)
        wb, wo, wglu = w_branch[l].astype(bf16), w_out[l].astype(bf16), s5_w_glu[l].astype(bf16)
        s5_ops = s5_prepare(s5_lam_re[l], s5_lam_im[l], s5_log_dt[l], s5_b_re[l], s5_b_im[l],
                            s5_c_re[l], s5_c_im[l], s5_d[l])
        hy_p = (hy_w_short[l], hy_w1[l], hy_b1[l], hy_w2[l], hy_b2[l], hy_w3[l], hy_freq[l], hy_bias[l])
        rg_p = (rg_w_conv[l], rg_w_a[l], rg_b_a[l], rg_w_x[l], rg_b_x[l], rg_lam[l])
        router = None if dense else (moe_router[j], moe_router_b[j])

        def mixers(p2, n_tok, states, with_hyena):
            p = p2.reshape(n_b, n_tok, P_COLS)
            blk = lambda i, n=1: p[..., i * W_MIX:(i + n) * W_MIX]
            flat = lambda t: t.reshape(n_b * n_tok, W_MIX)
            gla_o, gla_s = gla_mixer(p, gla_w_up[l], gla_b_up[l], states[0])
            s5_y, s5_s = s5_mixer(blk(P_S5), s5_ops, states[1])
            rg_o, rg_s = rglru_mixer(p, *rg_p, states[2])
            hy = flat(hyena_mixer(blk(P_HY, 3), *hy_p)) if with_hyena else None
            return ((flat(gla_o[0]), flat(gla_o[1]), flat(s5_y), hy, flat(rg_o[0]), flat(rg_o[1])),
                    (gla_s, s5_s, rg_s))

        def tail(stream, mods, p2, br, final):
            outs = merge(stream, tuple(mods[:5]), norm1_g[l], norm2_g[l], br[0], br[1], p2, br[2], br[3],
                         br[4], br[5], w_gate, wb, wo, wglu, s5_b_glu[l], router)
            if dense:
                return ffn_dense(outs[1], outs[0], mods[5], ffn_w_gu[j], ffn_w_down[j])
            return ffn_moe(outs[1], outs[2][:, :N_EXPERTS], outs[0], mods[5], moe_w_gu[j], moe_w_down[j],
                           final_g if final else None)

        zero_states = (jnp.zeros((2, n_b, GLA_HEADS, GLA_DV, GLA_DK), f32),
                       jnp.zeros((2, n_b, 2 * S5_NS), f32), jnp.zeros((2, n_b, W_MIX), f32))
        p_ctx = front(cs, m_ctx[0], m_ctx[1], norm1_g[l], w_mix)
        br_ctx, states = mixers(p_ctx, n_ctx, zero_states, not last)
        p_lat = front(xs, m_lat[0], m_lat[1], norm1_g[l], w_mix)
        br_lat, _ = mixers(p_lat, n_lat, states, True)
        xs = tail(xs, m_lat, p_lat, br_lat, last)
        if not last:
            cs = tail(cs, m_ctx, p_ctx, br_ctx, False)
    if (DEPTH - 1) % 2 == 0:
        xs = rmsnorm(xs, final_g)
    return xs.reshape(n_b, n_lat, D_MODEL)
```

```python
import functools
import math

import jax
import jax.numpy as jnp
import numpy as np
from jax import lax
from jax.experimental import pallas as pl
from jax.experimental.pallas import tpu as pltpu

D_MODEL = 1024
DEPTH = 2
GRID_W = 64
EPS = 1e-6
N_BRANCH = 4
W_MIX = D_MODEL // N_BRANCH
GLA_HEADS = 4
GLA_DK = W_MIX // GLA_HEADS
GLA_DV = W_MIX // GLA_HEADS
GLA_RANK = 16
GLA_TAU = 16.0
GLA_CHUNK = 64
S5_GROUP = 16
S5_GROUPS = W_MIX // S5_GROUP
S5_STATE = 64
S5_MAX_RE = -1e-4
HY_BANDS = 16
HY_DECAY_SHORT = 0.3
HY_DECAY_LONG = 1.5
HY_TARGET = 1e-2
RG_BLOCKS = 4
RG_BLOCK = W_MIX // RG_BLOCKS
RG_CONV = 4
RG_C = 8.0
N_EXPERTS = 8
TOP_K = 2
IN_SIZES = (GLA_HEADS * GLA_DK, GLA_HEADS * GLA_DK, GLA_HEADS * GLA_DV, GLA_HEADS * GLA_DV,
            2 * GLA_RANK, W_MIX, 3 * W_MIX, W_MIX, W_MIX, N_BRANCH * D_MODEL)

VMEM_LIMIT_BYTES = 48 * 1024 * 1024


def _mm_kernel(x_ref, w_ref, o_ref):
    o_ref[...] = jnp.dot(x_ref[...].astype(jnp.bfloat16), w_ref[...],
                         preferred_element_type=jnp.float32)


def _pick_tile(n, cap):
    best = None
    for t in range(128, cap + 1, 128):
        if n % t == 0:
            best = t
    return best if best is not None else n


def pmm(x, w):
    lead = x.shape[:-1]
    k = x.shape[-1]
    n = w.shape[-1]
    x2 = x.reshape(-1, k)
    m = x2.shape[0]
    tm = 512 if m % 512 == 0 else m
    if k > 2048 and m % 256 == 0:
        tm = 256
    tn = n if k * n * 2 <= 6 * 1024 * 1024 else _pick_tile(n, 1024)
    out = pl.pallas_call(
        _mm_kernel,
        grid=(m // tm, n // tn),
        in_specs=[pl.BlockSpec((tm, k), lambda i, j: (i, 0)),
                  pl.BlockSpec((k, tn), lambda i, j: (0, j))],
        out_specs=pl.BlockSpec((tm, tn), lambda i, j: (i, j)),
        out_shape=jax.ShapeDtypeStruct((m, n), jnp.float32),
        compiler_params=pltpu.CompilerParams(
            dimension_semantics=("arbitrary", "arbitrary"),
            vmem_limit_bytes=VMEM_LIMIT_BYTES),
    )(x2, w.astype(jnp.bfloat16))
    return out.reshape(lead + (n,))


def _mm_multi_kernel(*refs, transposed):
    o_ref = refs[-1]
    n = (len(refs) - 1) // 2
    acc = None
    for i in range(n):
        dims = (((1,), (1 if transposed[i] else 0,)), ((), ()))
        t = lax.dot_general(refs[i][...].astype(jnp.bfloat16), refs[n + i][...], dims,
                            preferred_element_type=jnp.float32)
        acc = t if acc is None else acc + t
    o_ref[...] = acc


def pmm_multi(xs, ws, transposed, tm=256, tn=512):
    m = xs[0].shape[0]
    n = ws[0].shape[0 if transposed[0] else 1]
    tm = tm if m % tm == 0 else m
    tn = tn if n % tn == 0 else n
    in_specs = ([pl.BlockSpec((tm, x.shape[1]), lambda i, j: (i, 0)) for x in xs]
                + [pl.BlockSpec((tn, w.shape[1]), lambda i, j: (j, 0)) if t else
                   pl.BlockSpec((w.shape[0], tn), lambda i, j: (0, j)) for w, t in zip(ws, transposed)])
    return pl.pallas_call(
        functools.partial(_mm_multi_kernel, transposed=tuple(transposed)),
        grid=(m // tm, n // tn),
        in_specs=in_specs,
        out_specs=pl.BlockSpec((tm, tn), lambda i, j: (i, j)),
        out_shape=jax.ShapeDtypeStruct((m, n), jnp.float32),
        compiler_params=pltpu.CompilerParams(
            dimension_semantics=("arbitrary", "arbitrary"),
            vmem_limit_bytes=VMEM_LIMIT_BYTES),
    )(*xs, *[w.astype(jnp.bfloat16) for w in ws])


RG_SCAN_ROWS = 256


def _rg_scan_kernel(a_ref, b_ref, s0_ref, h_ref, fin_ref, st_ref, *, reverse, tb, nb):
    @pl.when(pl.program_id(0) == 0)
    def _():
        st_ref[...] = s0_ref[...]

    def body(r, hs):
        rr = (tb - 1 - r) if reverse else r
        out = []
        for i in range(nb):
            h = a_ref[i, pl.ds(rr, 1), :] * hs[i] + b_ref[i, pl.ds(rr, 1), :]
            h_ref[i, pl.ds(rr, 1), :] = h
            out.append(h)
        return tuple(out)

    hs = lax.fori_loop(0, tb, body, tuple(st_ref[i:i + 1, :] for i in range(nb)), unroll=8)
    for i in range(nb):
        st_ref[i:i + 1, :] = hs[i]
        fin_ref[i:i + 1, :] = hs[i]


def rg_scan(a, b, s0, reverse):
    nb, n_tok, ch = a.shape
    tb = min(RG_SCAN_ROWS, n_tok)
    nblk = n_tok // tb
    imap = (lambda k: (0, nblk - 1 - k, 0)) if reverse else (lambda k: (0, k, 0))
    return pl.pallas_call(
        functools.partial(_rg_scan_kernel, reverse=reverse, tb=tb, nb=nb),
        grid=(nblk,),
        in_specs=[pl.BlockSpec((nb, tb, ch), imap), pl.BlockSpec((nb, tb, ch), imap),
                  pl.BlockSpec((nb, ch), lambda k: (0, 0))],
        out_specs=[pl.BlockSpec((nb, tb, ch), imap), pl.BlockSpec((nb, ch), lambda k: (0, 0))],
        out_shape=[jax.ShapeDtypeStruct((nb, n_tok, ch), jnp.float32),
                   jax.ShapeDtypeStruct((nb, ch), jnp.float32)],
        scratch_shapes=[pltpu.VMEM((nb, ch), jnp.float32)],
        compiler_params=pltpu.CompilerParams(dimension_semantics=("arbitrary",)),
        name="rg_scan",
    )(a, b, s0)


RG_HALO = 8


def _rg_pre_kernel(x_ref, prev_ref, next_ref, wc_ref, wg_ref, bg_ref, c_ref, af_ref, bf_ref, ab_ref, bb_ref,
                   *, tb):
    k = pl.program_id(1)
    prev = jnp.where(k > 0, prev_ref[0], 0.0)
    nxt = jnp.where(k < pl.num_programs(1) - 1, next_ref[0], 0.0)
    ext = jnp.concatenate([prev, x_ref[0], nxt], axis=0)
    xc = sum(ext[RG_HALO - 2 + j:RG_HALO - 2 + j + tb, :] * wc_ref[j:j + 1, :] for j in range(RG_CONV))
    m = jnp.dot(xc.astype(jnp.bfloat16), wg_ref[...], preferred_element_type=jnp.float32) + bg_ref[...]
    for d, (a_ref, b_ref) in enumerate(((af_ref, bf_ref), (ab_ref, bb_ref))):
        r = jax.nn.sigmoid(m[:, (2 * d) * W_MIX:(2 * d + 1) * W_MIX])
        i = jax.nn.sigmoid(m[:, (2 * d + 1) * W_MIX:(2 * d + 2) * W_MIX])
        log_a = -r * c_ref[d:d + 1, :]
        a_ref[0] = jnp.exp(log_a)
        u = jnp.tanh(log_a)
        b_ref[0] = jnp.sqrt(-2.0 * u / (1.0 - u)) * (i * xc)


def rg_pre(p, w_conv, w_a, b_a, w_x, b_x, lam):
    nb, n_tok, _ = p.shape
    tb = min(ROW_TILE, n_tok)
    nblk = n_tok // tb
    hb = tb // RG_HALO
    n_halo = n_tok // RG_HALO
    col = P_RGX
    blockdiag = lambda w: jax.scipy.linalg.block_diag(*[w[i] for i in range(RG_BLOCKS)])
    wg = jnp.concatenate([blockdiag(w_a[0]), blockdiag(w_x[0]), blockdiag(w_a[1]), blockdiag(w_x[1])],
                         axis=1).astype(jnp.bfloat16)
    bg = jnp.concatenate([b_a[0].reshape(-1), b_x[0].reshape(-1), b_a[1].reshape(-1), b_x[1].reshape(-1)]
                         ).reshape(1, 4 * W_MIX)
    c = RG_C * jax.nn.softplus(-lam)
    blk = pl.BlockSpec((1, tb, W_MIX), lambda b, k: (b, k, 0))
    full = lambda a: pl.BlockSpec(a.shape, lambda b, k: (0,) * a.ndim)
    out = jax.ShapeDtypeStruct((nb, n_tok, W_MIX), jnp.float32)
    return pl.pallas_call(
        functools.partial(_rg_pre_kernel, tb=tb),
        grid=(nb, nblk),
        in_specs=[pl.BlockSpec((1, tb, W_MIX), lambda b, k: (b, k, col)),
                  pl.BlockSpec((1, RG_HALO, W_MIX), lambda b, k: (b, jnp.maximum(k * hb - 1, 0), col)),
                  pl.BlockSpec((1, RG_HALO, W_MIX), lambda b, k: (b, jnp.minimum((k + 1) * hb, n_halo - 1), col)),
                  full(w_conv), full(wg), full(bg), full(c)],
        out_specs=[blk, blk, blk, blk],
        out_shape=[out, out, out, out],
        compiler_params=pltpu.CompilerParams(dimension_semantics=("arbitrary", "arbitrary")),
        name="rg_pre",
    )(p, p, p, w_conv, wg, bg, c)


S5_T = 16
S5_NS = S5_GROUPS * S5_STATE
S5_SCAN_CHUNKS = 64


def _s5_scan_kernel(d_ref, s0_ref, a_ref, h_ref, fin_ref, st_ref, *, rc, nb):
    d = pl.program_id(0)

    @pl.when(pl.program_id(1) == 0)
    def _():
        st_ref[...] = s0_ref[0]

    ar = jnp.broadcast_to(a_ref[0, :, 0:S5_NS], (nb, S5_NS))
    ai = jnp.broadcast_to(a_ref[0, :, S5_NS:2 * S5_NS], (nb, S5_NS))

    def body(r, carry):
        hr, hi = carry
        rr = r + d * (rc - 1 - 2 * r)
        h_ref[rr, :, 0:S5_NS] = hr
        h_ref[rr, :, S5_NS:2 * S5_NS] = hi
        dr = d_ref[rr, :, 0:S5_NS]
        di = d_ref[rr, :, S5_NS:2 * S5_NS]
        return ar * hr - ai * hi + dr, ar * hi + ai * hr + di

    hr, hi = lax.fori_loop(0, rc, body, (st_ref[:, 0:S5_NS], st_ref[:, S5_NS:2 * S5_NS]))
    st_ref[:, 0:S5_NS] = hr
    st_ref[:, S5_NS:2 * S5_NS] = hi
    fin_ref[0, :, 0:S5_NS] = hr
    fin_ref[0, :, S5_NS:2 * S5_NS] = hi


def s5_scan(dmat, s0, a_t):
    n, nb, _ = dmat.shape
    rc = min(S5_SCAN_CHUNKS, n)
    nblk = n // rc
    w = 2 * S5_NS
    imap = lambda d, k: (k + d * (nblk - 1 - 2 * k), 0, d)
    return pl.pallas_call(
        functools.partial(_s5_scan_kernel, rc=rc, nb=nb),
        grid=(2, nblk),
        in_specs=[pl.BlockSpec((rc, nb, w), imap),
                  pl.BlockSpec((1, nb, w), lambda d, k: (d, 0, 0)),
                  pl.BlockSpec((1, 1, w), lambda d, k: (d, 0, 0))],
        out_specs=[pl.BlockSpec((rc, nb, w), imap),
                   pl.BlockSpec((1, nb, w), lambda d, k: (d, 0, 0))],
        out_shape=[jax.ShapeDtypeStruct((n, nb, 2 * w), jnp.float32),
                   jax.ShapeDtypeStruct((2, nb, w), jnp.float32)],
        scratch_shapes=[pltpu.VMEM((nb, w), jnp.float32)],
        compiler_params=pltpu.CompilerParams(dimension_semantics=("arbitrary", "arbitrary"),
                                             vmem_limit_bytes=VMEM_LIMIT_BYTES),
    )(dmat, s0, a_t)


def _cmul(ar, ai, br, bi):
    return ar * br - ai * bi, ar * bi + ai * br


def s5_prepare(lam_re, lam_im, log_dt, b_re, b_im, c_re, c_im, d_skip):
    f32 = jnp.float32
    hp = lax.Precision.HIGHEST
    t_len, g_n, p_n, h_n = S5_T, S5_GROUPS, S5_STATE, S5_GROUP
    bf16 = jnp.bfloat16
    eye_g = jnp.eye(g_n, dtype=f32)
    mask_gp = jnp.repeat(eye_g, p_n, axis=1).astype(bf16)[None, :, None, :]
    ar_t = jnp.arange(t_len)
    wd, wc, kk, a_t = [], [], [], []
    for d in range(2):
        lr = jnp.minimum(lam_re[d], S5_MAX_RE)
        li = lam_im[d]
        dt = jnp.exp(log_dt[d])[:, None]
        tt = jnp.arange(t_len + 1, dtype=f32)[:, None, None]
        mag = jnp.exp(lr * dt * tt)
        ang = li * dt * tt
        pr, pi = mag * jnp.cos(ang), mag * jnp.sin(ang)
        nr, ni = pr[1] - 1.0, pi[1]
        den = lr * lr + li * li
        qr, qi = (nr * lr + ni * li) / den, (ni * lr - nr * li) / den
        bbr, bbi = _cmul(qr[..., None], qi[..., None], b_re[d], b_im[d])
        cr, ci = c_re[d], c_im[d]

        idx = (t_len - 1 - ar_t) if d == 0 else ar_t
        wr, wi = _cmul(pr[idx][..., None], pi[idx][..., None], bbr[None], bbi[None])

        def place_d(w):
            wt = w.transpose(0, 3, 1, 2).reshape(t_len, 1, h_n, g_n * p_n).astype(bf16)
            return (wt * mask_gp).reshape(t_len * W_MIX, g_n * p_n)

        wd.append(jnp.concatenate([place_d(wr), place_d(wi)], axis=1))

        idx2 = (ar_t + 1) if d == 0 else (t_len - ar_t)
        cwr, cwi = _cmul(cr[None], ci[None], pr[idx2][:, :, None, :], pi[idx2][:, :, None, :])

        def place_c(w):
            wt = w.transpose(0, 2, 1, 3).reshape(t_len, 1, h_n, g_n * p_n).astype(bf16)
            return (wt * mask_gp).reshape(t_len * W_MIX, g_n * p_n)

        wc.append(jnp.concatenate([place_c(cwr), place_c(-cwi)], axis=1))

        er, ei = _cmul(pr[:t_len][:, :, None, :], pi[:t_len][:, :, None, :], cr[None], ci[None])
        kk.append(jnp.einsum('tghp,gpk->tghk', er, bbr, precision=hp)
                  - jnp.einsum('tghp,gpk->tghk', ei, bbi, precision=hp))
        a_t.append(jnp.concatenate([pr[t_len].reshape(1, -1), pi[t_len].reshape(1, -1)], axis=1))

    lag = ar_t[None, :] - ar_t[:, None]
    mf = jnp.where((lag >= 0)[..., None, None, None], kk[0][jnp.clip(lag, 0, t_len - 1)], 0.0)
    mb = jnp.where((lag <= 0)[..., None, None, None], kk[1][jnp.clip(-lag, 0, t_len - 1)], 0.0)
    skip = (jnp.eye(t_len, dtype=f32)[:, :, None, None, None] * d_skip[None, None, :, :, None]
            * jnp.eye(h_n, dtype=f32)[None, None, None])
    m = mf + mb + skip
    mt = m.transpose(0, 4, 1, 2, 3).reshape(t_len, 1, h_n, t_len * W_MIX).astype(bf16)
    mask_igh = jnp.tile(jnp.repeat(eye_g, h_n, axis=1), (1, t_len)).astype(bf16)[None, :, None, :]
    wk = (mt * mask_igh).reshape(t_len * W_MIX, t_len * W_MIX)
    return (jnp.concatenate(wd, axis=1), wk, jnp.concatenate(wc, axis=1), jnp.stack(a_t))


GLA_BLOCK = 512


def _gla_kernel(q_ref, k_ref, v_ref, g_ref, wup_ref, bup_ref, s0_ref, o_ref, fin_ref, st_ref,
                *, reverse, tb):
    f32, bf16 = jnp.float32, jnp.bfloat16
    hp = lax.Precision.HIGHEST
    cc = GLA_CHUNK

    @pl.when(pl.program_id(1) == 0)
    def _():
        st_ref[...] = s0_ref[0]

    r_i = lax.broadcasted_iota(jnp.int32, (cc, cc), 0)
    c_i = lax.broadcasted_iota(jnp.int32, (cc, cc), 1)
    keep = (c_i >= r_i) if reverse else (c_i <= r_i)
    tri = keep.astype(f32)
    nt = (((1,), (1,)), ((), ()))
    tn = (((0,), (0,)), ((), ()))
    n_ch = tb // cc
    chunks = range(n_ch)
    heads = range(GLA_HEADS)
    hsl = [slice(h * GLA_DK, (h + 1) * GLA_DK) for h in heads]
    rows = [slice(c * cc, (c + 1) * cc) for c in chunks]

    z = jnp.dot(g_ref[0], wup_ref[...], precision=hp, preferred_element_type=f32) + bup_ref[...]
    la = (jnp.minimum(z, 0.0) - jnp.log1p(jnp.exp(-jnp.abs(z)))) * (1.0 / GLA_TAU)
    cum = [jnp.dot(tri, la[rows[c]], precision=hp, preferred_element_type=f32) for c in chunks]
    last = [cm[0:1, :] if reverse else cm[cc - 1:cc, :] for cm in cum]
    k = [k_ref[0, rows[c], :] for c in chunks]
    q_in = [(q_ref[0, rows[c], :] * (GLA_DK ** -0.5) * jnp.exp(cum[c])).astype(bf16) for c in chunks]
    k_in = [(k[c] * jnp.exp(-cum[c])).astype(bf16) for c in chunks]
    k_out = [(k[c] * jnp.exp(last[c] - cum[c])).astype(bf16) for c in chunks]
    dec = [jnp.exp(last[c]) for c in chunks]
    vb = [v_ref[0, rows[c], :].astype(bf16) for c in chunks]
    att = [[jnp.where(keep, lax.dot_general(q_in[c][:, s], k_in[c][:, s], nt, preferred_element_type=f32),
                      0.0).astype(bf16) for s in hsl] for c in chunks]
    o_intra = [[jnp.dot(att[c][h], vb[c][:, hsl[h]], preferred_element_type=f32) for h in heads]
               for c in chunks]
    d_state = [[lax.dot_general(vb[c][:, s], k_out[c][:, s], tn, preferred_element_type=f32) for s in hsl]
               for c in chunks]

    st = [st_ref[h] for h in heads]
    st_in = [None] * n_ch
    for c in (reversed(chunks) if reverse else chunks):
        st_in[c] = [s.astype(bf16) for s in st]
        st = [st[h] * dec[c][:, hsl[h]] + d_state[c][h] for h in heads]
    for h in heads:
        st_ref[h] = st[h]
        fin_ref[0, h] = st[h]

    for c in chunks:
        o_ref[0, rows[c], :] = jnp.concatenate(
            [o_intra[c][h] + lax.dot_general(q_in[c][:, hsl[h]], st_in[c][h], nt, preferred_element_type=f32)
             for h in heads], axis=1)


def gla_dir(p, gd, w_up, b_up, s0, reverse):
    nb, n_tok, _ = p.shape
    tb = min(GLA_BLOCK, n_tok)
    nblk = n_tok // tb
    blk = (lambda k: nblk - 1 - k) if reverse else (lambda k: k)
    col = lambda c: pl.BlockSpec((1, tb, W_MIX), lambda b, k: (b, blk(k), c))
    st_shape = (GLA_HEADS, GLA_DV, GLA_DK)
    return pl.pallas_call(
        functools.partial(_gla_kernel, reverse=reverse, tb=tb),
        grid=(nb, nblk),
        in_specs=[col(P_Q), col(P_Q + 1), col(P_Q + 2),
                  pl.BlockSpec((1, tb, GLA_RANK), lambda b, k: (b, blk(k), 0)),
                  pl.BlockSpec((GLA_RANK, W_MIX), lambda b, k: (0, 0)),
                  pl.BlockSpec((1, W_MIX), lambda b, k: (0, 0)),
                  pl.BlockSpec((1,) + st_shape, lambda b, k: (b, 0, 0, 0))],
        out_specs=[pl.BlockSpec((1, tb, W_MIX), lambda b, k: (b, blk(k), 0)),
                   pl.BlockSpec((1,) + st_shape, lambda b, k: (b, 0, 0, 0))],
        out_shape=[jax.ShapeDtypeStruct((nb, n_tok, W_MIX), jnp.float32),
                   jax.ShapeDtypeStruct((nb,) + st_shape, jnp.float32)],
        scratch_shapes=[pltpu.VMEM(st_shape, jnp.float32)],
        compiler_params=pltpu.CompilerParams(dimension_semantics=("arbitrary", "arbitrary"),
                                             vmem_limit_bytes=VMEM_LIMIT_BYTES),
        name="gla",
    )(p, p, p, gd, w_up, b_up.reshape(1, W_MIX), s0)


CAST_ROWS = 256


def _cast_kernel(x_ref, o_ref):
    o_ref[...] = x_ref[...].astype(o_ref.dtype)


def cast_bf16(w):
    e, r, c = w.shape
    spec = pl.BlockSpec((1, CAST_ROWS, c), lambda i, j: (i, j, 0))
    return pl.pallas_call(
        _cast_kernel,
        grid=(e, r // CAST_ROWS),
        in_specs=[spec],
        out_specs=spec,
        out_shape=jax.ShapeDtypeStruct(w.shape, jnp.bfloat16),
        compiler_params=pltpu.CompilerParams(dimension_semantics=("arbitrary", "arbitrary"),
                                             vmem_limit_bytes=VMEM_LIMIT_BYTES),
        name="cast_bf16",
    )(w)


FFN_TM = 512


def _swiglu_kernel(te_ref, nu_ref, x_ref, wg_ref, wu_ref, wd_ref, *rest, nf, residual):
    del te_ref
    o_ref = rest[-1]
    j = pl.program_id(1)

    @pl.when(pl.program_id(0) < nu_ref[0])
    def _():
        x = x_ref[...]
        g = jnp.dot(x, wg_ref[0], preferred_element_type=jnp.float32)
        u = jnp.dot(x, wu_ref[0], preferred_element_type=jnp.float32)
        a = (g * jax.nn.sigmoid(g) * u).astype(jnp.bfloat16)
        part = jnp.dot(a, wd_ref[0], preferred_element_type=jnp.float32)

        @pl.when(j == 0)
        def _():
            o_ref[...] = part

        @pl.when(j > 0)
        def _():
            o_ref[...] += part

        if residual:
            xres_ref, gm_ref = rest[:2]

            @pl.when(j == nf - 1)
            def _():
                o_ref[...] = xres_ref[...] + gm_ref[0] * o_ref[...]


def grouped_swiglu(tile_expert, n_used, xs, w_gu, w_down, nf, residual=None):
    m, d = xs.shape
    f = w_down.shape[1]
    tf = f // nf
    n_tiles = m // FFN_TM
    in_specs = [pl.BlockSpec((FFN_TM, d), lambda t, j, te, nu: (t, 0)),
                pl.BlockSpec((1, d, tf), lambda t, j, te, nu: (te[t], 0, j)),
                pl.BlockSpec((1, d, tf), lambda t, j, te, nu: (te[t], 0, nf + j)),
                pl.BlockSpec((1, tf, d), lambda t, j, te, nu: (te[t], j, 0))]
    extra = ()
    if residual is not None:
        rows_per_mod = m // residual[1].shape[0]
        in_specs += [pl.BlockSpec((FFN_TM, d), lambda t, j, te, nu: (t, 0)),
                     pl.BlockSpec((1, 1, d), lambda t, j, te, nu: ((t * FFN_TM) // rows_per_mod, 0, 0))]
        extra = tuple(residual)
    grid_spec = pltpu.PrefetchScalarGridSpec(
        num_scalar_prefetch=2,
        grid=(n_tiles, nf),
        in_specs=in_specs,
        out_specs=pl.BlockSpec((FFN_TM, d), lambda t, j, te, nu: (t, 0)))
    return pl.pallas_call(
        functools.partial(_swiglu_kernel, nf=nf, residual=residual is not None),
        grid_spec=grid_spec,
        out_shape=jax.ShapeDtypeStruct((m, d), jnp.float32),
        compiler_params=pltpu.CompilerParams(
            dimension_semantics=("arbitrary", "arbitrary"),
            vmem_limit_bytes=VMEM_LIMIT_BYTES),
        name="grouped_swiglu",
    )(tile_expert, n_used, xs, w_gu, w_gu, w_down, *extra)


ROW_TILE = 512
P_HY, P_Q, P_OG, P_S5, P_RGX, P_RGG = 0, 3, 6, 7, 8, 9
P_GDN = 10 * W_MIX
P_COLS = P_GDN + 128


def _rms_mod(x, g, shift, scale):
    y = x * lax.rsqrt(jnp.mean(x * x, axis=-1, keepdims=True) + EPS) * g
    return y * (1.0 + scale) + shift


def _front_kernel(x_ref, sh_ref, sc_ref, g_ref, w_ref, o_ref):
    h = _rms_mod(x_ref[...], g_ref[...], sh_ref[0], sc_ref[0])
    o_ref[...] = jnp.dot(h.astype(jnp.bfloat16), w_ref[...], preferred_element_type=jnp.float32)


def _mod_spec(rows_per_mod):
    return pl.BlockSpec((1, 1, D_MODEL), lambda i: ((i * ROW_TILE) // rows_per_mod, 0, 0))


def front(x, shift, scale, g, w):
    m = x.shape[0]
    n_mod = shift.shape[0]
    n_out = w.shape[1]
    mod = _mod_spec(m // n_mod)
    return pl.pallas_call(
        _front_kernel,
        grid=(m // ROW_TILE,),
        in_specs=[pl.BlockSpec((ROW_TILE, D_MODEL), lambda i: (i, 0)), mod, mod,
                  pl.BlockSpec((1, D_MODEL), lambda i: (0, 0)),
                  pl.BlockSpec((D_MODEL, n_out), lambda i: (0, 0))],
        out_specs=pl.BlockSpec((ROW_TILE, n_out), lambda i: (i, 0)),
        out_shape=jax.ShapeDtypeStruct((m, n_out), jnp.float32),
        compiler_params=pltpu.CompilerParams(dimension_semantics=("arbitrary",),
                                             vmem_limit_bytes=VMEM_LIMIT_BYTES),
        name="front",
    )(x, shift, scale, g.reshape(1, D_MODEL), w)


MERGE_TILE = 512
MERGE_VMEM_BYTES = 56 * 1024 * 1024
ROUTER_PAD = 128


def _merge_kernel(*refs, with_router):
    (x_ref, sh1_ref, sc1_ref, gm_ref, sh2_ref, sc2_ref, g1_ref, g2_ref,
     of_ref, ob_ref, og_ref, s5_ref, hy_ref, rf_ref, rb_ref, rgg_ref,
     wg_ref, wb_ref, wo_ref, wglu_ref, bglu_ref, havg_ref) = refs[:22]
    f32, bf16 = jnp.float32, jnp.bfloat16
    hp = lax.Precision.HIGHEST
    x = x_ref[...]
    hb = _rms_mod(x, g1_ref[...], sh1_ref[0], sc1_ref[0]).astype(bf16)

    o = of_ref[...] + ob_ref[...]
    ms = jnp.dot(o * o, havg_ref[...], precision=hp, preferred_element_type=f32)
    og = og_ref[...]
    gla = o * lax.rsqrt(ms + EPS) * (og * jax.nn.sigmoid(og))
    g5 = jax.nn.gelu(s5_ref[...])
    s5o = g5 * jax.nn.sigmoid(jnp.dot(g5.astype(bf16), wglu_ref[...], preferred_element_type=f32)
                              + bglu_ref[...])
    rgo = (rf_ref[...] + rb_ref[...]) * jax.nn.gelu(rgg_ref[...])
    branches = (gla, s5o, hy_ref[...], rgo)

    y = None
    for k in range(N_BRANCH):
        gate = jax.nn.sigmoid(jnp.dot(hb, wg_ref[:, k * D_MODEL:(k + 1) * D_MODEL],
                                      preferred_element_type=f32))
        t = gate * jnp.dot(branches[k].astype(bf16), wb_ref[k], preferred_element_type=f32)
        y = t if y is None else y + t
    out = jnp.dot(y.astype(bf16), wo_ref[...], preferred_element_type=f32)
    xn = x + gm_ref[0] * out
    h2 = _rms_mod(xn, g2_ref[...], sh2_ref[0], sc2_ref[0])
    if with_router:
        rw_ref, rb2_ref, xo_ref, h2_ref, lg_ref = refs[22:]
        lg_ref[...] = _dot3(*_split_bf16(h2), *_split_bf16(rw_ref[...])) + rb2_ref[...]
    else:
        xo_ref, h2_ref = refs[22:]
    xo_ref[...] = xn
    h2_ref[...] = h2.astype(bf16)


def merge(x, mods, g1, g2, o_f, o_b, p, s5y, hy, r_f, r_b, wg, wb, wo, wglu, bglu, router=None):
    m = x.shape[0]
    tm = MERGE_TILE
    n_mod = mods[0].shape[0]
    rows_per_mod = m // n_mod
    mod = pl.BlockSpec((1, 1, D_MODEL), lambda i: ((i * tm) // rows_per_mod, 0, 0))
    row = pl.BlockSpec((tm, D_MODEL), lambda i: (i, 0))
    br = pl.BlockSpec((tm, W_MIX), lambda i: (i, 0))
    pcol = lambda c: pl.BlockSpec((tm, W_MIX), lambda i: (i, c))
    full = lambda a: pl.BlockSpec(a.shape, lambda i: (0,) * a.ndim)
    head = jnp.arange(W_MIX) // GLA_DV
    havg = (head[:, None] == head[None, :]).astype(jnp.float32) / GLA_DV
    vec = lambda v: v.reshape(1, -1)
    consts = [wg, wb, wo, wglu, vec(bglu), havg]
    out_specs = [row, pl.BlockSpec((tm, D_MODEL), lambda i: (i, 0))]
    out_shape = [jax.ShapeDtypeStruct((m, D_MODEL), jnp.float32),
                 jax.ShapeDtypeStruct((m, D_MODEL), jnp.bfloat16)]
    if router is not None:
        rw, rbias = router
        pad = ROUTER_PAD - rw.shape[1]
        consts += [jnp.pad(rw, ((0, 0), (0, pad))), jnp.pad(rbias, (0, pad)).reshape(1, -1)]
        out_specs.append(pl.BlockSpec((tm, ROUTER_PAD), lambda i: (i, 0)))
        out_shape.append(jax.ShapeDtypeStruct((m, ROUTER_PAD), jnp.float32))
    in_specs = ([row] + [mod] * 5 + [full(vec(g1)), full(vec(g2)), br, br, pcol(P_OG), br, br, br, br,
                                     pcol(P_RGG)] + [full(a) for a in consts])
    return pl.pallas_call(
        functools.partial(_merge_kernel, with_router=router is not None),
        grid=(m // tm,),
        in_specs=in_specs,
        out_specs=out_specs,
        out_shape=out_shape,
        compiler_params=pltpu.CompilerParams(dimension_semantics=("arbitrary",),
                                             vmem_limit_bytes=MERGE_VMEM_BYTES),
        name="merge",
    )(x, *mods, vec(g1), vec(g2), o_f, o_b, p, s5y, hy, r_f, r_b, p, *consts)


def _combine_kernel(x_ref, y0_ref, y1_ref, w_ref, gm_ref, g_ref, o_ref, *, final_norm):
    w = w_ref[...]
    y = w[:, 0:1] * y0_ref[...] + w[:, 1:2] * y1_ref[...]
    xn = x_ref[...] + gm_ref[0] * y
    if final_norm:
        xn = xn * lax.rsqrt(jnp.mean(xn * xn, axis=-1, keepdims=True) + EPS) * g_ref[...]
    o_ref[...] = xn


def moe_combine(x, yk, w, gate_mod, final_g):
    m = x.shape[0]
    n_mod = gate_mod.shape[0]
    g = jnp.ones((1, D_MODEL), jnp.float32) if final_g is None else final_g.reshape(1, D_MODEL)
    return pl.pallas_call(
        functools.partial(_combine_kernel, final_norm=final_g is not None),
        grid=(m // ROW_TILE,),
        in_specs=[pl.BlockSpec((ROW_TILE, D_MODEL), lambda i: (i, 0)),
                  pl.BlockSpec((ROW_TILE, D_MODEL), lambda i: (i, 0)),
                  pl.BlockSpec((ROW_TILE, D_MODEL), lambda i: (i + m // ROW_TILE, 0)),
                  pl.BlockSpec((ROW_TILE, TOP_K), lambda i: (i, 0)),
                  _mod_spec(m // n_mod),
                  pl.BlockSpec((1, D_MODEL), lambda i: (0, 0))],
        out_specs=pl.BlockSpec((ROW_TILE, D_MODEL), lambda i: (i, 0)),
        out_shape=jax.ShapeDtypeStruct((m, D_MODEL), jnp.float32),
        compiler_params=pltpu.CompilerParams(dimension_semantics=("arbitrary",)),
        name="moe_combine",
    )(x, yk, yk, w, gate_mod, g)


HY_LANES = 128
HY_SLABS = 8
HY_MIN_LEN = 1024


def _split_bf16(a):
    hi = a.astype(jnp.bfloat16)
    lo = (a - hi.astype(jnp.float32)).astype(jnp.bfloat16)
    return hi, lo


def _dot3(a_hi, a_lo, b_hi, b_lo):
    d = functools.partial(jnp.dot, preferred_element_type=jnp.float32)
    return d(a_hi, b_hi) + (d(a_lo, b_hi) + d(a_hi, b_lo))


def _hyena_dft_consts(n1):
    n = n1 * HY_LANES
    ka = np.arange(n1, dtype=np.float64)[:, None]
    f1_ang = 2.0 * np.pi * ka * np.arange(n1 // 2, dtype=np.float64)[None, :] / n1
    f1r, f1i = np.cos(f1_ang), -np.sin(f1_ang)
    tw_ang = 2.0 * np.pi * ka * np.arange(HY_LANES, dtype=np.float64)[None, :] / n
    lo = np.arange(HY_LANES, dtype=np.float64)
    f2_ang = 2.0 * np.pi * lo[:, None] * lo[None, :] / HY_LANES
    f2r, f2i = np.cos(f2_ang), -np.sin(f2_ang)
    fwd_rows = np.concatenate([f1r, f1i], axis=0)
    fwd_lanes = np.block([[f2r, f2i], [-f2i, f2r]])
    inv_lanes = np.block([[f2r, -f2i], [f2i, f2r]])
    inv_rows = np.concatenate([f1r.T, f1i.T], axis=1) / n
    out = []
    for m in (fwd_rows, fwd_lanes, inv_lanes, inv_rows):
        m32 = jnp.asarray(m, jnp.float32)
        out.extend(_split_bf16(m32))
    return out + [jnp.asarray(np.cos(tw_ang), jnp.float32), jnp.asarray(-np.sin(tw_ang), jnp.float32)]


def _hyena_fft_kernel(*refs, n1, ns, spectrum):
    if spectrum:
        z_ref, f1h, f1l, f2h, f2l, twr_ref, twi_ref, o_ref = refs
    else:
        z_ref, hf_ref, f1h, f1l, f2h, f2l, g2h, g2l, fih, fil, twr_ref, twi_ref, o_ref = refs
    w = HY_LANES
    twr, twi = twr_ref[...], twi_ref[...]
    z2 = jnp.concatenate([z_ref[0, s] for s in range(ns)], axis=1)
    a2 = _dot3(f1h[...], f1l[...], *_split_bf16(z2))
    rows = []
    for s in range(ns):
        r, i = a2[:n1, s * w:(s + 1) * w], a2[n1:, s * w:(s + 1) * w]
        rows.append(jnp.concatenate([r * twr - i * twi, r * twi + i * twr], axis=1))
    x = _dot3(*_split_bf16(jnp.concatenate(rows, axis=0)), f2h[...], f2l[...])
    if spectrum:
        o_ref[...] = x.reshape(ns, n1, 2 * w)
        return
    h = hf_ref[...].reshape(ns * n1, 2 * w)
    xr, xi, hr, hi = x[:, :w], x[:, w:], h[:, :w], h[:, w:]
    y = jnp.concatenate([xr * hr - xi * hi, xr * hi + xi * hr], axis=1)
    g = _dot3(*_split_bf16(y), g2h[...], g2l[...])
    cr, ci = [], []
    for s in range(ns):
        gr, gi = g[s * n1:(s + 1) * n1, :w], g[s * n1:(s + 1) * n1, w:]
        cr.append(gr * twr + gi * twi)
        ci.append(gi * twr - gr * twi)
    gc = jnp.concatenate([jnp.concatenate(cr, axis=1), jnp.concatenate(ci, axis=1)], axis=0)
    y2 = _dot3(fih[...], fil[...], *_split_bf16(gc))
    for s in range(ns):
        o_ref[0, s] = y2[:, s * w:(s + 1) * w]


def hyena_fft(zt, hf=None):
    nb, ch, half, w = zt.shape
    n1 = 2 * half
    ns = HY_SLABS
    f1h, f1l, f2h, f2l, g2h, g2l, fih, fil, twr, twi = _hyena_dft_consts(n1)
    full = lambda a: pl.BlockSpec(a.shape, lambda b, c: (0,) * a.ndim)
    zspec = pl.BlockSpec((1, ns, half, w), lambda b, c: (b, c, 0, 0))
    if hf is None:
        consts = [f1h, f1l, f2h, f2l, twr, twi]
        in_specs, args = [zspec], [zt]
        out_spec = pl.BlockSpec((ns, n1, 2 * w), lambda b, c: (b * (ch // ns) + c, 0, 0))
        out_shape = jax.ShapeDtypeStruct((nb * ch, n1, 2 * w), jnp.float32)
    else:
        consts = [f1h, f1l, f2h, f2l, g2h, g2l, fih, fil, twr, twi]
        in_specs = [zspec, pl.BlockSpec((ns, n1, 2 * w), lambda b, c: (c, 0, 0))]
        args = [zt, hf]
        out_spec = zspec
        out_shape = jax.ShapeDtypeStruct(zt.shape, jnp.float32)
    return pl.pallas_call(
        functools.partial(_hyena_fft_kernel, n1=n1, ns=ns, spectrum=hf is None),
        grid=(nb, ch // ns),
        in_specs=in_specs + [full(a) for a in consts],
        out_specs=out_spec,
        out_shape=out_shape,
        compiler_params=pltpu.CompilerParams(dimension_semantics=("arbitrary", "arbitrary"),
                                             vmem_limit_bytes=VMEM_LIMIT_BYTES),
        name="hyena_fft",
    )(*args, *consts)


def rmsnorm(x, g):
    y = x * lax.rsqrt(jnp.mean(x * x, axis=-1, keepdims=True) + EPS)
    return y * g


def adaln(cond, w, b):
    return jax.nn.silu(cond) @ w + b


def grid_pos_embed(n_tokens, dim):
    rows = n_tokens // GRID_W
    q = dim // 4
    omega = 1.0 / (10000.0 ** (jnp.arange(q, dtype=jnp.float32) / q))
    r = jnp.arange(rows, dtype=jnp.float32)[:, None] * omega
    cc = jnp.arange(GRID_W, dtype=jnp.float32)[:, None] * omega
    er = jnp.concatenate([jnp.sin(r), jnp.cos(r)], axis=-1)
    ec = jnp.concatenate([jnp.sin(cc), jnp.cos(cc)], axis=-1)
    emb = jnp.concatenate([jnp.broadcast_to(er[:, None], (rows, GRID_W, dim // 2)),
                           jnp.broadcast_to(ec[None], (rows, GRID_W, dim // 2))], axis=-1)
    return emb.reshape(rows * GRID_W, dim)


def dwconv(x, w, pad_l, pad_r):
    n = x.shape[1]
    xp = jnp.pad(x, ((0, 0), (pad_l, pad_r), (0, 0)))
    return sum(xp[:, k:k + n] * w[k] for k in range(w.shape[0]))


def gla_mixer(p, w_up, b_up, s0):
    gdn = p[..., P_GDN:P_GDN + 2 * GLA_RANK]
    outs, finals = [], []
    for d in range(2):
        od, sd = gla_dir(p, gdn[..., d * GLA_RANK:(d + 1) * GLA_RANK], w_up[d], b_up[d], s0[d], d == 1)
        outs.append(od)
        finals.append(sd)
    return outs, jnp.stack(finals)


def s5_mixer(u, prep, s0):
    wd, wk, wc, a_t = prep
    b_, n_tok, _ = u.shape
    n = n_tok // S5_T
    u2 = u.reshape(b_, n, S5_T * W_MIX).transpose(1, 0, 2).reshape(n * b_, S5_T * W_MIX)
    dmat = pmm(u2, wd)
    hmat, fin = s5_scan(dmat.reshape(n, b_, 4 * S5_NS), s0, a_t)
    y2 = pmm_multi([u2, hmat.reshape(n * b_, 4 * S5_NS)], [wk, wc], (False, True))
    y = y2.reshape(n, b_, S5_T, W_MIX).transpose(1, 0, 2, 3).reshape(b_, n_tok, W_MIX)
    return y, fin


def hyena_filters(n_tok, w1, b1, w2, b2, w3, freq):
    f32 = jnp.float32
    t = jnp.arange(n_tok, dtype=f32)[:, None]
    bands = jnp.linspace(1e-4, HY_BANDS - 1, HY_BANDS, dtype=f32)[None]
    ang = 2.0 * math.pi * bands * t / n_tok
    z = jnp.concatenate([t / n_tok, jnp.cos(ang), jnp.sin(ang)], axis=-1)
    hp = lax.Precision.HIGHEST
    h = jnp.sin(freq * (jnp.dot(z, w1, precision=hp) + b1))
    h = jnp.sin(freq * (jnp.dot(h, w2, precision=hp) + b2))
    h = jnp.dot(h, w3, precision=hp)
    t01 = t / max(n_tok - 1, 1)
    deltas = jnp.abs(jnp.linspace(math.log(HY_TARGET) / HY_DECAY_SHORT,
                                  math.log(HY_TARGET) / HY_DECAY_LONG, W_MIX, dtype=f32))
    h = h * jnp.exp(-t01 * jnp.tile(deltas, 2))
    return h / (jnp.sum(jnp.abs(h), axis=0, keepdims=True) + EPS)


def hyena_mixer(p, w_short, w1, b1, w2, b2, w3, freq, bias):
    nb, n_tok, _ = p.shape
    pc = dwconv(p, w_short, 1, 1)
    v, x0, x1 = jnp.split(pc, 3, axis=-1)
    z = x1 * v
    n_pad = max(n_tok, HY_MIN_LEN)
    half = n_pad // HY_LANES

    def frames(t):
        t = jnp.pad(t, [(0, 0)] * (t.ndim - 1) + [(0, n_pad - n_tok)])
        return t.reshape(t.shape[:-1] + (half, HY_LANES))

    filt = hyena_filters(n_tok, w1, b1, w2, b2, w3, freq)
    spec = hyena_fft(frames(filt.T)[None])
    sf, sb = spec[:W_MIX], spec[W_MIX:]
    hfreq = jnp.concatenate([sf[..., :HY_LANES] + sb[..., :HY_LANES],
                             sf[..., HY_LANES:] - sb[..., HY_LANES:]], axis=-1)
    conv = hyena_fft(frames(z.transpose(0, 2, 1)), hfreq)
    conv = conv.reshape(nb, W_MIX, n_pad)[:, :, :n_tok].transpose(0, 2, 1)
    return x0 * (conv + z * bias)


def rglru_mixer(p, w_conv, w_a, b_a, w_x, b_x, lam, s0):
    a_f, b_f, a_b, b_b = rg_pre(p, w_conv, w_a, b_a, w_x, b_x, lam)
    h_f, fin_f = rg_scan(a_f, b_f, s0[0], False)
    h_b, fin_b = rg_scan(a_b, b_b, s0[1], True)
    return [h_f, h_b], jnp.stack([fin_f, fin_b])


def ffn_dense(h2, x, gate_mod, w_gu, w_down):
    n_tiles = h2.shape[0] // FFN_TM
    return grouped_swiglu(jnp.zeros((n_tiles,), jnp.int32), jnp.full((1,), n_tiles, jnp.int32), h2,
                          w_gu[None].astype(jnp.bfloat16), w_down[None].astype(jnp.bfloat16), nf=2,
                          residual=(x, gate_mod))


def ffn_moe(h2, logits, x, gate_mod, w_gu, w_down, final_g):
    n_tok = h2.shape[0]
    n_slot = TOP_K * n_tok
    top_v, top_i = lax.top_k(logits, TOP_K)
    w = jax.nn.softmax(top_v, axis=-1)
    e_flat = top_i.T.reshape(-1).astype(jnp.int32)
    onehot = (e_flat[:, None] == jnp.arange(N_EXPERTS, dtype=jnp.int32)[None]).astype(jnp.int32)
    csum = jnp.cumsum(onehot, axis=0)
    cnt = csum[-1]
    rank = jnp.sum(csum * onehot, axis=1) - 1
    padded = ((cnt + FFN_TM - 1) // FFN_TM) * FFN_TM
    ends = jnp.cumsum(padded)
    dest = (ends - padded)[e_flat] + rank
    n_rows = n_slot + N_EXPERTS * FFN_TM
    n_tiles = n_rows // FFN_TM
    src = jnp.zeros((n_rows,), jnp.int32).at[dest].set(jnp.arange(n_slot, dtype=jnp.int32) % n_tok)
    tile_start = jnp.arange(n_tiles, dtype=jnp.int32) * FFN_TM
    tile_expert = jnp.minimum(jnp.sum((tile_start[:, None] >= ends[None, :]).astype(jnp.int32), axis=1),
                              N_EXPERTS - 1)
    n_used = (ends[-1:] // FFN_TM).astype(jnp.int32)
    xs = h2.at[src].get(mode="promise_in_bounds")
    ys = grouped_swiglu(tile_expert, n_used, xs, cast_bf16(w_gu), cast_bf16(w_down), nf=2)
    yk = ys.at[dest].get(mode="promise_in_bounds")
    return moe_combine(x, yk, w, gate_mod, final_g)


def kernel(x, c, ctx, c_ctx, mod_w, mod_b, norm1_g, norm2_g, w_in, gla_w_up, gla_b_up,
           s5_lam_re, s5_lam_im, s5_log_dt, s5_b_re, s5_b_im, s5_c_re, s5_c_im, s5_d,
           s5_w_glu, s5_b_glu, hy_w_short, hy_w1, hy_b1, hy_w2, hy_b2, hy_w3, hy_freq,
           hy_bias, rg_w_conv, rg_w_a, rg_b_a, rg_w_x, rg_b_x, rg_lam, w_branch, w_out,
           ffn_w_gu, ffn_w_down, moe_router, moe_router_b, moe_w_gu, moe_w_down, final_g):
    f32, bf16 = jnp.float32, jnp.bfloat16
    n_b, n_lat, _ = x.shape
    n_ctx = ctx.shape[1]
    xs = (x + grid_pos_embed(n_lat, D_MODEL)[None]).reshape(n_b * n_lat, D_MODEL)
    cs = ctx.reshape(n_b * n_ctx, D_MODEL)
    offs = [0]
    for n in IN_SIZES:
        offs.append(offs[-1] + n)
    for l in range(DEPTH):
        last = l == DEPTH - 1
        dense = l % 2 == 0
        j = l // 2
        m_lat = [t[:, None, :] for t in jnp.split(adaln(c, mod_w[l], mod_b[l]), 6, axis=-1)]
        m_ctx = [t[None, None, :] for t in jnp.split(adaln(c_ctx, mod_w[l], mod_b[l]), 6, axis=-1)]
        wl = w_in[l]
        w_mix = jnp.concatenate([wl[:, offs[6]:offs[7]], wl[:, offs[0]:offs[4]], wl[:, offs[5]:offs[6]],
                                 wl[:, offs[7]:offs[9]], wl[:, offs[4]:offs[5]],
                                 jnp.zeros((D_MODEL, P_COLS - P_GDN - 2 * GLA_RANK), f32)], axis=1).astype(bf16)
        w_gate = wl[:, offs[9]:offs[10]].astype(bf16)
        wb, wo, wglu = w_branch[l].astype(bf16), w_out[l].astype(bf16), s5_w_glu[l].astype(bf16)
        s5_ops = s5_prepare(s5_lam_re[l], s5_lam_im[l], s5_log_dt[l], s5_b_re[l], s5_b_im[l],
                            s5_c_re[l], s5_c_im[l], s5_d[l])
        hy_p = (hy_w_short[l], hy_w1[l], hy_b1[l], hy_w2[l], hy_b2[l], hy_w3[l], hy_freq[l], hy_bias[l])
        rg_p = (rg_w_conv[l], rg_w_a[l], rg_b_a[l], rg_w_x[l], rg_b_x[l], rg_lam[l])
        router = None if dense else (moe_router[j], moe_router_b[j])

        def mixers(p2, n_tok, states, with_hyena):
            p = p2.reshape(n_b, n_tok, P_COLS)
            blk = lambda i, n=1: p[..., i * W_MIX:(i + n) * W_MIX]
            flat = lambda t: t.reshape(n_b * n_tok, W_MIX)
            gla_o, gla_s = gla_mixer(p, gla_w_up[l], gla_b_up[l], states[0])
            s5_y, s5_s = s5_mixer(blk(P_S5), s5_ops, states[1])
            rg_o, rg_s = rglru_mixer(p, *rg_p, states[2])
            hy = flat(hyena_mixer(blk(P_HY, 3), *hy_p)) if with_hyena else None
            return ((flat(gla_o[0]), flat(gla_o[1]), flat(s5_y), hy, flat(rg_o[0]), flat(rg_o[1])),
                    (gla_s, s5_s, rg_s))

        def tail(stream, mods, p2, br, final):
            outs = merge(stream, tuple(mods[:5]), norm1_g[l], norm2_g[l], br[0], br[1], p2, br[2], br[3],
                         br[4], br[5], w_gate, wb, wo, wglu, s5_b_glu[l], router)
            if dense:
                return ffn_dense(outs[1], outs[0], mods[5], ffn_w_gu[j], ffn_w_down[j])
            return ffn_moe(outs[1], outs[2][:, :N_EXPERTS], outs[0], mods[5], moe_w_gu[j], moe_w_down[j],
                           final_g if final else None)

        zero_states = (jnp.zeros((2, n_b, GLA_HEADS, GLA_DV, GLA_DK), f32),
                       jnp.zeros((2, n_b, 2 * S5_NS), f32), jnp.zeros((2, n_b, W_MIX), f32))
        p_ctx = front(cs, m_ctx[0], m_ctx[1], norm1_g[l], w_mix)
        br_ctx, states = mixers(p_ctx, n_ctx, zero_states, not last)
        p_lat = front(xs, m_lat[0], m_lat[1], norm1_g[l], w_mix)
        br_lat, _ = mixers(p_lat, n_lat, states, True)
        xs = tail(xs, m_lat, p_lat, br_lat, last)
        if not last:
            cs = tail(cs, m_ctx, p_ctx, br_ctx, False)
    if (DEPTH - 1) % 2 == 0:
        xs = rmsnorm(xs, final_g)
    return xs.reshape(n_b, n_lat, D_MODEL)
```

```python
import functools
import math

import jax
import jax.numpy as jnp
import numpy as np
from jax import lax
from jax.experimental import pallas as pl
from jax.experimental.pallas import tpu as pltpu

D_MODEL = 1024
DEPTH = 2
GRID_W = 64
EPS = 1e-6
N_BRANCH = 4
W_MIX = D_MODEL // N_BRANCH
GLA_HEADS = 4
GLA_DK = W_MIX // GLA_HEADS
GLA_DV = W_MIX // GLA_HEADS
GLA_RANK = 16
GLA_TAU = 16.0
GLA_CHUNK = 64
S5_GROUP = 16
S5_GROUPS = W_MIX // S5_GROUP
S5_STATE = 64
S5_MAX_RE = -1e-4
HY_BANDS = 16
HY_DECAY_SHORT = 0.3
HY_DECAY_LONG = 1.5
HY_TARGET = 1e-2
RG_BLOCKS = 4
RG_BLOCK = W_MIX // RG_BLOCKS
RG_CONV = 4
RG_C = 8.0
N_EXPERTS = 8
TOP_K = 2
IN_SIZES = (GLA_HEADS * GLA_DK, GLA_HEADS * GLA_DK, GLA_HEADS * GLA_DV, GLA_HEADS * GLA_DV,
            2 * GLA_RANK, W_MIX, 3 * W_MIX, W_MIX, W_MIX, N_BRANCH * D_MODEL)

VMEM_LIMIT_BYTES = 48 * 1024 * 1024


def _mm_kernel(x_ref, w_ref, o_ref):
    o_ref[...] = jnp.dot(x_ref[...].astype(jnp.bfloat16), w_ref[...],
                         preferred_element_type=jnp.float32)


def _pick_tile(n, cap):
    best = None
    for t in range(128, cap + 1, 128):
        if n % t == 0:
            best = t
    return best if best is not None else n


def pmm(x, w):
    lead = x.shape[:-1]
    k = x.shape[-1]
    n = w.shape[-1]
    x2 = x.reshape(-1, k)
    m = x2.shape[0]
    tm = 512 if m % 512 == 0 else m
    if k > 2048 and m % 256 == 0:
        tm = 256
    tn = n if k * n * 2 <= 6 * 1024 * 1024 else _pick_tile(n, 1024)
    out = pl.pallas_call(
        _mm_kernel,
        grid=(m // tm, n // tn),
        in_specs=[pl.BlockSpec((tm, k), lambda i, j: (i, 0)),
                  pl.BlockSpec((k, tn), lambda i, j: (0, j))],
        out_specs=pl.BlockSpec((tm, tn), lambda i, j: (i, j)),
        out_shape=jax.ShapeDtypeStruct((m, n), jnp.float32),
        compiler_params=pltpu.CompilerParams(
            dimension_semantics=("arbitrary", "arbitrary"),
            vmem_limit_bytes=VMEM_LIMIT_BYTES),
    )(x2, w.astype(jnp.bfloat16))
    return out.reshape(lead + (n,))


def _mm_multi_kernel(*refs, transposed):
    o_ref = refs[-1]
    n = (len(refs) - 1) // 2
    acc = None
    for i in range(n):
        dims = (((1,), (1 if transposed[i] else 0,)), ((), ()))
        t = lax.dot_general(refs[i][...].astype(jnp.bfloat16), refs[n + i][...], dims,
                            preferred_element_type=jnp.float32)
        acc = t if acc is None else acc + t
    o_ref[...] = acc


def pmm_multi(xs, ws, transposed, tm=256, tn=512):
    m = xs[0].shape[0]
    n = ws[0].shape[0 if transposed[0] else 1]
    tm = tm if m % tm == 0 else m
    tn = tn if n % tn == 0 else n
    in_specs = ([pl.BlockSpec((tm, x.shape[1]), lambda i, j: (i, 0)) for x in xs]
                + [pl.BlockSpec((tn, w.shape[1]), lambda i, j: (j, 0)) if t else
                   pl.BlockSpec((w.shape[0], tn), lambda i, j: (0, j)) for w, t in zip(ws, transposed)])
    return pl.pallas_call(
        functools.partial(_mm_multi_kernel, transposed=tuple(transposed)),
        grid=(m // tm, n // tn),
        in_specs=in_specs,
        out_specs=pl.BlockSpec((tm, tn), lambda i, j: (i, j)),
        out_shape=jax.ShapeDtypeStruct((m, n), jnp.float32),
        compiler_params=pltpu.CompilerParams(
            dimension_semantics=("arbitrary", "arbitrary"),
            vmem_limit_bytes=VMEM_LIMIT_BYTES),
    )(*xs, *[w.astype(jnp.bfloat16) for w in ws])


RG_SCAN_ROWS = 256


def _rg_scan_kernel(a_ref, b_ref, s0_ref, h_ref, fin_ref, st_ref, *, reverse, tb, nb):
    @pl.when(pl.program_id(0) == 0)
    def _():
        st_ref[...] = s0_ref[...]

    def body(r, hs):
        rr = (tb - 1 - r) if reverse else r
        out = []
        for i in range(nb):
            h = a_ref[i, pl.ds(rr, 1), :] * hs[i] + b_ref[i, pl.ds(rr, 1), :]
            h_ref[i, pl.ds(rr, 1), :] = h
            out.append(h)
        return tuple(out)

    hs = lax.fori_loop(0, tb, body, tuple(st_ref[i:i + 1, :] for i in range(nb)), unroll=8)
    for i in range(nb):
        st_ref[i:i + 1, :] = hs[i]
        fin_ref[i:i + 1, :] = hs[i]


def rg_scan(a, b, s0, reverse):
    nb, n_tok, ch = a.shape
    tb = min(RG_SCAN_ROWS, n_tok)
    nblk = n_tok // tb
    imap = (lambda k: (0, nblk - 1 - k, 0)) if reverse else (lambda k: (0, k, 0))
    return pl.pallas_call(
        functools.partial(_rg_scan_kernel, reverse=reverse, tb=tb, nb=nb),
        grid=(nblk,),
        in_specs=[pl.BlockSpec((nb, tb, ch), imap), pl.BlockSpec((nb, tb, ch), imap),
                  pl.BlockSpec((nb, ch), lambda k: (0, 0))],
        out_specs=[pl.BlockSpec((nb, tb, ch), imap), pl.BlockSpec((nb, ch), lambda k: (0, 0))],
        out_shape=[jax.ShapeDtypeStruct((nb, n_tok, ch), jnp.float32),
                   jax.ShapeDtypeStruct((nb, ch), jnp.float32)],
        scratch_shapes=[pltpu.VMEM((nb, ch), jnp.float32)],
        compiler_params=pltpu.CompilerParams(dimension_semantics=("arbitrary",)),
        name="rg_scan",
    )(a, b, s0)


RG_HALO = 8


def _rg_pre_kernel(x_ref, prev_ref, next_ref, wc_ref, wg_ref, bg_ref, c_ref, af_ref, bf_ref, ab_ref, bb_ref,
                   *, tb):
    k = pl.program_id(1)
    prev = jnp.where(k > 0, prev_ref[0], 0.0)
    nxt = jnp.where(k < pl.num_programs(1) - 1, next_ref[0], 0.0)
    ext = jnp.concatenate([prev, x_ref[0], nxt], axis=0)
    xc = sum(ext[RG_HALO - 2 + j:RG_HALO - 2 + j + tb, :] * wc_ref[j:j + 1, :] for j in range(RG_CONV))
    m = jnp.dot(xc.astype(jnp.bfloat16), wg_ref[...], preferred_element_type=jnp.float32) + bg_ref[...]
    for d, (a_ref, b_ref) in enumerate(((af_ref, bf_ref), (ab_ref, bb_ref))):
        r = jax.nn.sigmoid(m[:, (2 * d) * W_MIX:(2 * d + 1) * W_MIX])
        i = jax.nn.sigmoid(m[:, (2 * d + 1) * W_MIX:(2 * d + 2) * W_MIX])
        log_a = -r * c_ref[d:d + 1, :]
        a_ref[0] = jnp.exp(log_a)
        u = jnp.tanh(log_a)
        b_ref[0] = jnp.sqrt(-2.0 * u / (1.0 - u)) * (i * xc)


def rg_pre(p, w_conv, w_a, b_a, w_x, b_x, lam):
    nb, n_tok, _ = p.shape
    tb = min(ROW_TILE, n_tok)
    nblk = n_tok // tb
    hb = tb // RG_HALO
    n_halo = n_tok // RG_HALO
    col = P_RGX
    blockdiag = lambda w: jax.scipy.linalg.block_diag(*[w[i] for i in range(RG_BLOCKS)])
    wg = jnp.concatenate([blockdiag(w_a[0]), blockdiag(w_x[0]), blockdiag(w_a[1]), blockdiag(w_x[1])],
                         axis=1).astype(jnp.bfloat16)
    bg = jnp.concatenate([b_a[0].reshape(-1), b_x[0].reshape(-1), b_a[1].reshape(-1), b_x[1].reshape(-1)]
                         ).reshape(1, 4 * W_MIX)
    c = RG_C * jax.nn.softplus(-lam)
    blk = pl.BlockSpec((1, tb, W_MIX), lambda b, k: (b, k, 0))
    full = lambda a: pl.BlockSpec(a.shape, lambda b, k: (0,) * a.ndim)
    out = jax.ShapeDtypeStruct((nb, n_tok, W_MIX), jnp.float32)
    return pl.pallas_call(
        functools.partial(_rg_pre_kernel, tb=tb),
        grid=(nb, nblk),
        in_specs=[pl.BlockSpec((1, tb, W_MIX), lambda b, k: (b, k, col)),
                  pl.BlockSpec((1, RG_HALO, W_MIX), lambda b, k: (b, jnp.maximum(k * hb - 1, 0), col)),
                  pl.BlockSpec((1, RG_HALO, W_MIX), lambda b, k: (b, jnp.minimum((k + 1) * hb, n_halo - 1), col)),
                  full(w_conv), full(wg), full(bg), full(c)],
        out_specs=[blk, blk, blk, blk],
        out_shape=[out, out, out, out],
        compiler_params=pltpu.CompilerParams(dimension_semantics=("arbitrary", "arbitrary")),
        name="rg_pre",
    )(p, p, p, w_conv, wg, bg, c)


S5_T = 16
S5_NS = S5_GROUPS * S5_STATE
S5_SCAN_CHUNKS = 64


def _s5_scan_kernel(d_ref, s0_ref, a_ref, h_ref, fin_ref, st_ref, *, rc, nb):
    d = pl.program_id(0)

    @pl.when(pl.program_id(1) == 0)
    def _():
        st_ref[...] = s0_ref[0]

    ar = jnp.broadcast_to(a_ref[0, :, 0:S5_NS], (nb, S5_NS))
    ai = jnp.broadcast_to(a_ref[0, :, S5_NS:2 * S5_NS], (nb, S5_NS))

    def body(r, carry):
        hr, hi = carry
        rr = r + d * (rc - 1 - 2 * r)
        h_ref[rr, :, 0:S5_NS] = hr
        h_ref[rr, :, S5_NS:2 * S5_NS] = hi
        dr = d_ref[rr, :, 0:S5_NS]
        di = d_ref[rr, :, S5_NS:2 * S5_NS]
        return ar * hr - ai * hi + dr, ar * hi + ai * hr + di

    hr, hi = lax.fori_loop(0, rc, body, (st_ref[:, 0:S5_NS], st_ref[:, S5_NS:2 * S5_NS]))
    st_ref[:, 0:S5_NS] = hr
    st_ref[:, S5_NS:2 * S5_NS] = hi
    fin_ref[0, :, 0:S5_NS] = hr
    fin_ref[0, :, S5_NS:2 * S5_NS] = hi


def s5_scan(dmat, s0, a_t):
    n, nb, _ = dmat.shape
    rc = min(S5_SCAN_CHUNKS, n)
    nblk = n // rc
    w = 2 * S5_NS
    imap = lambda d, k: (k + d * (nblk - 1 - 2 * k), 0, d)
    return pl.pallas_call(
        functools.partial(_s5_scan_kernel, rc=rc, nb=nb),
        grid=(2, nblk),
        in_specs=[pl.BlockSpec((rc, nb, w), imap),
                  pl.BlockSpec((1, nb, w), lambda d, k: (d, 0, 0)),
                  pl.BlockSpec((1, 1, w), lambda d, k: (d, 0, 0))],
        out_specs=[pl.BlockSpec((rc, nb, w), imap),
                   pl.BlockSpec((1, nb, w), lambda d, k: (d, 0, 0))],
        out_shape=[jax.ShapeDtypeStruct((n, nb, 2 * w), jnp.float32),
                   jax.ShapeDtypeStruct((2, nb, w), jnp.float32)],
        scratch_shapes=[pltpu.VMEM((nb, w), jnp.float32)],
        compiler_params=pltpu.CompilerParams(dimension_semantics=("arbitrary", "arbitrary"),
                                             vmem_limit_bytes=VMEM_LIMIT_BYTES),
    )(dmat, s0, a_t)


def _cmul(ar, ai, br, bi):
    return ar * br - ai * bi, ar * bi + ai * br


def s5_prepare(lam_re, lam_im, log_dt, b_re, b_im, c_re, c_im, d_skip):
    f32 = jnp.float32
    hp = lax.Precision.HIGHEST
    t_len, g_n, p_n, h_n = S5_T, S5_GROUPS, S5_STATE, S5_GROUP
    bf16 = jnp.bfloat16
    eye_g = jnp.eye(g_n, dtype=f32)
    mask_gp = jnp.repeat(eye_g, p_n, axis=1).astype(bf16)[None, :, None, :]
    ar_t = jnp.arange(t_len)
    wd, wc, kk, a_t = [], [], [], []
    for d in range(2):
        lr = jnp.minimum(lam_re[d], S5_MAX_RE)
        li = lam_im[d]
        dt = jnp.exp(log_dt[d])[:, None]
        tt = jnp.arange(t_len + 1, dtype=f32)[:, None, None]
        mag = jnp.exp(lr * dt * tt)
        ang = li * dt * tt
        pr, pi = mag * jnp.cos(ang), mag * jnp.sin(ang)
        nr, ni = pr[1] - 1.0, pi[1]
        den = lr * lr + li * li
        qr, qi = (nr * lr + ni * li) / den, (ni * lr - nr * li) / den
        bbr, bbi = _cmul(qr[..., None], qi[..., None], b_re[d], b_im[d])
        cr, ci = c_re[d], c_im[d]

        idx = (t_len - 1 - ar_t) if d == 0 else ar_t
        wr, wi = _cmul(pr[idx][..., None], pi[idx][..., None], bbr[None], bbi[None])

        def place_d(w):
            wt = w.transpose(0, 3, 1, 2).reshape(t_len, 1, h_n, g_n * p_n).astype(bf16)
            return (wt * mask_gp).reshape(t_len * W_MIX, g_n * p_n)

        wd.append(jnp.concatenate([place_d(wr), place_d(wi)], axis=1))

        idx2 = (ar_t + 1) if d == 0 else (t_len - ar_t)
        cwr, cwi = _cmul(cr[None], ci[None], pr[idx2][:, :, None, :], pi[idx2][:, :, None, :])

        def place_c(w):
            wt = w.transpose(0, 2, 1, 3).reshape(t_len, 1, h_n, g_n * p_n).astype(bf16)
            return (wt * mask_gp).reshape(t_len * W_MIX, g_n * p_n)

        wc.append(jnp.concatenate([place_c(cwr), place_c(-cwi)], axis=1))

        er, ei = _cmul(pr[:t_len][:, :, None, :], pi[:t_len][:, :, None, :], cr[None], ci[None])
        kk.append(jnp.einsum('tghp,gpk->tghk', er, bbr, precision=hp)
                  - jnp.einsum('tghp,gpk->tghk', ei, bbi, precision=hp))
        a_t.append(jnp.concatenate([pr[t_len].reshape(1, -1), pi[t_len].reshape(1, -1)], axis=1))

    lag = ar_t[None, :] - ar_t[:, None]
    mf = jnp.where((lag >= 0)[..., None, None, None], kk[0][jnp.clip(lag, 0, t_len - 1)], 0.0)
    mb = jnp.where((lag <= 0)[..., None, None, None], kk[1][jnp.clip(-lag, 0, t_len - 1)], 0.0)
    skip = (jnp.eye(t_len, dtype=f32)[:, :, None, None, None] * d_skip[None, None, :, :, None]
            * jnp.eye(h_n, dtype=f32)[None, None, None])
    m = mf + mb + skip
    mt = m.transpose(0, 4, 1, 2, 3).reshape(t_len, 1, h_n, t_len * W_MIX).astype(bf16)
    mask_igh = jnp.tile(jnp.repeat(eye_g, h_n, axis=1), (1, t_len)).astype(bf16)[None, :, None, :]
    wk = (mt * mask_igh).reshape(t_len * W_MIX, t_len * W_MIX)
    return (jnp.concatenate(wd, axis=1), wk, jnp.concatenate(wc, axis=1), jnp.stack(a_t))


GLA_BLOCK = 512


def _gla_kernel(q_ref, k_ref, v_ref, g_ref, wup_ref, bup_ref, s0_ref, o_ref, fin_ref, st_ref,
                *, reverse, tb):
    f32, bf16 = jnp.float32, jnp.bfloat16
    hp = lax.Precision.HIGHEST
    cc = GLA_CHUNK

    @pl.when(pl.program_id(1) == 0)
    def _():
        st_ref[...] = s0_ref[0]

    r_i = lax.broadcasted_iota(jnp.int32, (cc, cc), 0)
    c_i = lax.broadcasted_iota(jnp.int32, (cc, cc), 1)
    keep = (c_i >= r_i) if reverse else (c_i <= r_i)
    tri = keep.astype(f32)
    nt = (((1,), (1,)), ((), ()))
    tn = (((0,), (0,)), ((), ()))
    n_ch = tb // cc
    chunks = range(n_ch)
    heads = range(GLA_HEADS)
    hsl = [slice(h * GLA_DK, (h + 1) * GLA_DK) for h in heads]
    rows = [slice(c * cc, (c + 1) * cc) for c in chunks]

    z = jnp.dot(g_ref[0], wup_ref[...], precision=hp, preferred_element_type=f32) + bup_ref[...]
    la = (jnp.minimum(z, 0.0) - jnp.log1p(jnp.exp(-jnp.abs(z)))) * (1.0 / GLA_TAU)
    cum = [jnp.dot(tri, la[rows[c]], precision=hp, preferred_element_type=f32) for c in chunks]
    last = [cm[0:1, :] if reverse else cm[cc - 1:cc, :] for cm in cum]
    k = [k_ref[0, rows[c], :] for c in chunks]
    q_in = [(q_ref[0, rows[c], :] * (GLA_DK ** -0.5) * jnp.exp(cum[c])).astype(bf16) for c in chunks]
    k_in = [(k[c] * jnp.exp(-cum[c])).astype(bf16) for c in chunks]
    k_out = [(k[c] * jnp.exp(last[c] - cum[c])).astype(bf16) for c in chunks]
    dec = [jnp.exp(last[c]) for c in chunks]
    vb = [v_ref[0, rows[c], :].astype(bf16) for c in chunks]
    att = [[jnp.where(keep, lax.dot_general(q_in[c][:, s], k_in[c][:, s], nt, preferred_element_type=f32),
                      0.0).astype(bf16) for s in hsl] for c in chunks]
    o_intra = [[jnp.dot(att[c][h], vb[c][:, hsl[h]], preferred_element_type=f32) for h in heads]
               for c in chunks]
    d_state = [[lax.dot_general(vb[c][:, s], k_out[c][:, s], tn, preferred_element_type=f32) for s in hsl]
               for c in chunks]

    st = [st_ref[h] for h in heads]
    st_in = [None] * n_ch
    for c in (reversed(chunks) if reverse else chunks):
        st_in[c] = [s.astype(bf16) for s in st]
        st = [st[h] * dec[c][:, hsl[h]] + d_state[c][h] for h in heads]
    for h in heads:
        st_ref[h] = st[h]
        fin_ref[0, h] = st[h]

    for c in chunks:
        o_ref[0, rows[c], :] = jnp.concatenate(
            [o_intra[c][h] + lax.dot_general(q_in[c][:, hsl[h]], st_in[c][h], nt, preferred_element_type=f32)
             for h in heads], axis=1)


def gla_dir(p, gd, w_up, b_up, s0, reverse):
    nb, n_tok, _ = p.shape
    tb = min(GLA_BLOCK, n_tok)
    nblk = n_tok // tb
    blk = (lambda k: nblk - 1 - k) if reverse else (lambda k: k)
    col = lambda c: pl.BlockSpec((1, tb, W_MIX), lambda b, k: (b, blk(k), c))
    st_shape = (GLA_HEADS, GLA_DV, GLA_DK)
    return pl.pallas_call(
        functools.partial(_gla_kernel, reverse=reverse, tb=tb),
        grid=(nb, nblk),
        in_specs=[col(P_Q), col(P_Q + 1), col(P_Q + 2),
                  pl.BlockSpec((1, tb, GLA_RANK), lambda b, k: (b, blk(k), 0)),
                  pl.BlockSpec((GLA_RANK, W_MIX), lambda b, k: (0, 0)),
                  pl.BlockSpec((1, W_MIX), lambda b, k: (0, 0)),
                  pl.BlockSpec((1,) + st_shape, lambda b, k: (b, 0, 0, 0))],
        out_specs=[pl.BlockSpec((1, tb, W_MIX), lambda b, k: (b, blk(k), 0)),
                   pl.BlockSpec((1,) + st_shape, lambda b, k: (b, 0, 0, 0))],
        out_shape=[jax.ShapeDtypeStruct((nb, n_tok, W_MIX), jnp.float32),
                   jax.ShapeDtypeStruct((nb,) + st_shape, jnp.float32)],
        scratch_shapes=[pltpu.VMEM(st_shape, jnp.float32)],
        compiler_params=pltpu.CompilerParams(dimension_semantics=("arbitrary", "arbitrary"),
                                             vmem_limit_bytes=VMEM_LIMIT_BYTES),
        name="gla",
    )(p, p, p, gd, w_up, b_up.reshape(1, W_MIX), s0)


FFN_TM = 512


def _swiglu_kernel(te_ref, nu_ref, x_ref, wg_ref, wu_ref, wd_ref, *rest, nf, residual):
    del te_ref
    o_ref = rest[-1]
    j = pl.program_id(1)

    @pl.when(pl.program_id(0) < nu_ref[0])
    def _():
        x = x_ref[...]
        g = jnp.dot(x, wg_ref[0], preferred_element_type=jnp.float32)
        u = jnp.dot(x, wu_ref[0], preferred_element_type=jnp.float32)
        a = (g * jax.nn.sigmoid(g) * u).astype(jnp.bfloat16)
        part = jnp.dot(a, wd_ref[0], preferred_element_type=jnp.float32)

        @pl.when(j == 0)
        def _():
            o_ref[...] = part

        @pl.when(j > 0)
        def _():
            o_ref[...] += part

        if residual:
            xres_ref, gm_ref = rest[:2]

            @pl.when(j == nf - 1)
            def _():
                o_ref[...] = xres_ref[...] + gm_ref[0] * o_ref[...]


def grouped_swiglu(tile_expert, n_used, xs, w_gu, w_down, nf, residual=None):
    m, d = xs.shape
    f = w_down.shape[1]
    tf = f // nf
    n_tiles = m // FFN_TM
    in_specs = [pl.BlockSpec((FFN_TM, d), lambda t, j, te, nu: (t, 0)),
                pl.BlockSpec((1, d, tf), lambda t, j, te, nu: (te[t], 0, j)),
                pl.BlockSpec((1, d, tf), lambda t, j, te, nu: (te[t], 0, nf + j)),
                pl.BlockSpec((1, tf, d), lambda t, j, te, nu: (te[t], j, 0))]
    extra = ()
    if residual is not None:
        rows_per_mod = m // residual[1].shape[0]
        in_specs += [pl.BlockSpec((FFN_TM, d), lambda t, j, te, nu: (t, 0)),
                     pl.BlockSpec((1, 1, d), lambda t, j, te, nu: ((t * FFN_TM) // rows_per_mod, 0, 0))]
        extra = tuple(residual)
    grid_spec = pltpu.PrefetchScalarGridSpec(
        num_scalar_prefetch=2,
        grid=(n_tiles, nf),
        in_specs=in_specs,
        out_specs=pl.BlockSpec((FFN_TM, d), lambda t, j, te, nu: (t, 0)))
    return pl.pallas_call(
        functools.partial(_swiglu_kernel, nf=nf, residual=residual is not None),
        grid_spec=grid_spec,
        out_shape=jax.ShapeDtypeStruct((m, d), jnp.float32),
        compiler_params=pltpu.CompilerParams(
            dimension_semantics=("arbitrary", "arbitrary"),
            vmem_limit_bytes=VMEM_LIMIT_BYTES),
        name="grouped_swiglu",
    )(tile_expert, n_used, xs, w_gu, w_gu, w_down, *extra)


ROW_TILE = 512
P_HY, P_Q, P_OG, P_S5, P_RGX, P_RGG = 0, 3, 6, 7, 8, 9
P_GDN = 10 * W_MIX
P_COLS = P_GDN + 128


def _rms_mod(x, g, shift, scale):
    y = x * lax.rsqrt(jnp.mean(x * x, axis=-1, keepdims=True) + EPS) * g
    return y * (1.0 + scale) + shift


def _front_kernel(x_ref, sh_ref, sc_ref, g_ref, w_ref, o_ref):
    h = _rms_mod(x_ref[...], g_ref[...], sh_ref[0], sc_ref[0])
    o_ref[...] = jnp.dot(h.astype(jnp.bfloat16), w_ref[...], preferred_element_type=jnp.float32)


def _mod_spec(rows_per_mod):
    return pl.BlockSpec((1, 1, D_MODEL), lambda i: ((i * ROW_TILE) // rows_per_mod, 0, 0))


def front(x, shift, scale, g, w):
    m = x.shape[0]
    n_mod = shift.shape[0]
    n_out = w.shape[1]
    mod = _mod_spec(m // n_mod)
    return pl.pallas_call(
        _front_kernel,
        grid=(m // ROW_TILE,),
        in_specs=[pl.BlockSpec((ROW_TILE, D_MODEL), lambda i: (i, 0)), mod, mod,
                  pl.BlockSpec((1, D_MODEL), lambda i: (0, 0)),
                  pl.BlockSpec((D_MODEL, n_out), lambda i: (0, 0))],
        out_specs=pl.BlockSpec((ROW_TILE, n_out), lambda i: (i, 0)),
        out_shape=jax.ShapeDtypeStruct((m, n_out), jnp.float32),
        compiler_params=pltpu.CompilerParams(dimension_semantics=("arbitrary",),
                                             vmem_limit_bytes=VMEM_LIMIT_BYTES),
        name="front",
    )(x, shift, scale, g.reshape(1, D_MODEL), w)


MERGE_TILE = 256
ROUTER_PAD = 128


def _merge_kernel(*refs, with_router):
    (x_ref, sh1_ref, sc1_ref, gm_ref, sh2_ref, sc2_ref, g1_ref, g2_ref,
     of_ref, ob_ref, og_ref, s5_ref, hy_ref, rf_ref, rb_ref, rgg_ref,
     wg_ref, wb_ref, wo_ref, wglu_ref, bglu_ref, havg_ref) = refs[:22]
    f32, bf16 = jnp.float32, jnp.bfloat16
    hp = lax.Precision.HIGHEST
    x = x_ref[...]
    hb = _rms_mod(x, g1_ref[...], sh1_ref[0], sc1_ref[0]).astype(bf16)

    o = of_ref[...] + ob_ref[...]
    ms = jnp.dot(o * o, havg_ref[...], precision=hp, preferred_element_type=f32)
    og = og_ref[...]
    gla = o * lax.rsqrt(ms + EPS) * (og * jax.nn.sigmoid(og))
    g5 = jax.nn.gelu(s5_ref[...])
    s5o = g5 * jax.nn.sigmoid(jnp.dot(g5.astype(bf16), wglu_ref[...], preferred_element_type=f32)
                              + bglu_ref[...])
    rgo = (rf_ref[...] + rb_ref[...]) * jax.nn.gelu(rgg_ref[...])
    branches = (gla, s5o, hy_ref[...], rgo)

    y = None
    for k in range(N_BRANCH):
        gate = jax.nn.sigmoid(jnp.dot(hb, wg_ref[:, k * D_MODEL:(k + 1) * D_MODEL],
                                      preferred_element_type=f32))
        t = gate * jnp.dot(branches[k].astype(bf16), wb_ref[k], preferred_element_type=f32)
        y = t if y is None else y + t
    out = jnp.dot(y.astype(bf16), wo_ref[...], preferred_element_type=f32)
    xn = x + gm_ref[0] * out
    h2 = _rms_mod(xn, g2_ref[...], sh2_ref[0], sc2_ref[0])
    if with_router:
        rw_ref, rb2_ref, xo_ref, h2_ref, lg_ref = refs[22:]
        lg_ref[...] = _dot3(*_split_bf16(h2), *_split_bf16(rw_ref[...])) + rb2_ref[...]
    else:
        xo_ref, h2_ref = refs[22:]
    xo_ref[...] = xn
    h2_ref[...] = h2.astype(bf16)


def merge(x, mods, g1, g2, o_f, o_b, p, s5y, hy, r_f, r_b, wg, wb, wo, wglu, bglu, router=None):
    m = x.shape[0]
    tm = MERGE_TILE
    n_mod = mods[0].shape[0]
    rows_per_mod = m // n_mod
    mod = pl.BlockSpec((1, 1, D_MODEL), lambda i: ((i * tm) // rows_per_mod, 0, 0))
    row = pl.BlockSpec((tm, D_MODEL), lambda i: (i, 0))
    br = pl.BlockSpec((tm, W_MIX), lambda i: (i, 0))
    pcol = lambda c: pl.BlockSpec((tm, W_MIX), lambda i: (i, c))
    full = lambda a: pl.BlockSpec(a.shape, lambda i: (0,) * a.ndim)
    head = jnp.arange(W_MIX) // GLA_DV
    havg = (head[:, None] == head[None, :]).astype(jnp.float32) / GLA_DV
    vec = lambda v: v.reshape(1, -1)
    consts = [wg, wb, wo, wglu, vec(bglu), havg]
    out_specs = [row, pl.BlockSpec((tm, D_MODEL), lambda i: (i, 0))]
    out_shape = [jax.ShapeDtypeStruct((m, D_MODEL), jnp.float32),
                 jax.ShapeDtypeStruct((m, D_MODEL), jnp.bfloat16)]
    if router is not None:
        rw, rbias = router
        pad = ROUTER_PAD - rw.shape[1]
        consts += [jnp.pad(rw, ((0, 0), (0, pad))), jnp.pad(rbias, (0, pad)).reshape(1, -1)]
        out_specs.append(pl.BlockSpec((tm, ROUTER_PAD), lambda i: (i, 0)))
        out_shape.append(jax.ShapeDtypeStruct((m, ROUTER_PAD), jnp.float32))
    in_specs = ([row] + [mod] * 5 + [full(vec(g1)), full(vec(g2)), br, br, pcol(P_OG), br, br, br, br,
                                     pcol(P_RGG)] + [full(a) for a in consts])
    return pl.pallas_call(
        functools.partial(_merge_kernel, with_router=router is not None),
        grid=(m // tm,),
        in_specs=in_specs,
        out_specs=out_specs,
        out_shape=out_shape,
        compiler_params=pltpu.CompilerParams(dimension_semantics=("arbitrary",),
                                             vmem_limit_bytes=VMEM_LIMIT_BYTES),
        name="merge",
    )(x, *mods, vec(g1), vec(g2), o_f, o_b, p, s5y, hy, r_f, r_b, p, *consts)


def _combine_kernel(x_ref, y0_ref, y1_ref, w_ref, gm_ref, g_ref, o_ref, *, final_norm):
    w = w_ref[...]
    y = w[:, 0:1] * y0_ref[...] + w[:, 1:2] * y1_ref[...]
    xn = x_ref[...] + gm_ref[0] * y
    if final_norm:
        xn = xn * lax.rsqrt(jnp.mean(xn * xn, axis=-1, keepdims=True) + EPS) * g_ref[...]
    o_ref[...] = xn


def moe_combine(x, yk, w, gate_mod, final_g):
    m = x.shape[0]
    n_mod = gate_mod.shape[0]
    g = jnp.ones((1, D_MODEL), jnp.float32) if final_g is None else final_g.reshape(1, D_MODEL)
    return pl.pallas_call(
        functools.partial(_combine_kernel, final_norm=final_g is not None),
        grid=(m // ROW_TILE,),
        in_specs=[pl.BlockSpec((ROW_TILE, D_MODEL), lambda i: (i, 0)),
                  pl.BlockSpec((ROW_TILE, D_MODEL), lambda i: (i, 0)),
                  pl.BlockSpec((ROW_TILE, D_MODEL), lambda i: (i + m // ROW_TILE, 0)),
                  pl.BlockSpec((ROW_TILE, TOP_K), lambda i: (i, 0)),
                  _mod_spec(m // n_mod),
                  pl.BlockSpec((1, D_MODEL), lambda i: (0, 0))],
        out_specs=pl.BlockSpec((ROW_TILE, D_MODEL), lambda i: (i, 0)),
        out_shape=jax.ShapeDtypeStruct((m, D_MODEL), jnp.float32),
        compiler_params=pltpu.CompilerParams(dimension_semantics=("arbitrary",)),
        name="moe_combine",
    )(x, yk, yk, w, gate_mod, g)


HY_LANES = 128
HY_SLABS = 8
HY_MIN_LEN = 1024


def _split_bf16(a):
    hi = a.astype(jnp.bfloat16)
    lo = (a - hi.astype(jnp.float32)).astype(jnp.bfloat16)
    return hi, lo


def _dot3(a_hi, a_lo, b_hi, b_lo):
    d = functools.partial(jnp.dot, preferred_element_type=jnp.float32)
    return d(a_hi, b_hi) + (d(a_lo, b_hi) + d(a_hi, b_lo))


def _hyena_dft_consts(n1):
    n = n1 * HY_LANES
    ka = np.arange(n1, dtype=np.float64)[:, None]
    f1_ang = 2.0 * np.pi * ka * np.arange(n1 // 2, dtype=np.float64)[None, :] / n1
    f1r, f1i = np.cos(f1_ang), -np.sin(f1_ang)
    tw_ang = 2.0 * np.pi * ka * np.arange(HY_LANES, dtype=np.float64)[None, :] / n
    lo = np.arange(HY_LANES, dtype=np.float64)
    f2_ang = 2.0 * np.pi * lo[:, None] * lo[None, :] / HY_LANES
    f2r, f2i = np.cos(f2_ang), -np.sin(f2_ang)
    fwd_rows = np.concatenate([f1r, f1i], axis=0)
    fwd_lanes = np.block([[f2r, f2i], [-f2i, f2r]])
    inv_lanes = np.block([[f2r, -f2i], [f2i, f2r]])
    inv_rows = np.concatenate([f1r.T, f1i.T], axis=1) / n
    out = []
    for m in (fwd_rows, fwd_lanes, inv_lanes, inv_rows):
        m32 = jnp.asarray(m, jnp.float32)
        out.extend(_split_bf16(m32))
    return out + [jnp.asarray(np.cos(tw_ang), jnp.float32), jnp.asarray(-np.sin(tw_ang), jnp.float32)]


def _hyena_fft_kernel(*refs, n1, ns, spectrum):
    bf16 = jnp.bfloat16
    if spectrum:
        z_ref, f1h, f1l, f2h, f2l, twr_ref, twi_ref, o_ref = refs
        rows_dft = lambda t: _dot3(f1h[...], f1l[...], *_split_bf16(t))
        lanes_dft = lambda t: _dot3(*_split_bf16(t), f2h[...], f2l[...])
    else:
        z_ref, hf_ref, f1h, f2h, g2h, fih, twr_ref, twi_ref, o_ref = refs
        mm = lambda a, b: jnp.dot(a.astype(bf16), b.astype(bf16), preferred_element_type=jnp.float32)
        rows_dft = lambda t: mm(f1h[...], t)
        lanes_dft = lambda t: mm(t, f2h[...])
    w = HY_LANES
    twr, twi = twr_ref[...], twi_ref[...]
    z2 = jnp.concatenate([z_ref[0, s] for s in range(ns)], axis=1)
    a2 = rows_dft(z2)
    rows = []
    for s in range(ns):
        r, i = a2[:n1, s * w:(s + 1) * w], a2[n1:, s * w:(s + 1) * w]
        rows.append(jnp.concatenate([r * twr - i * twi, r * twi + i * twr], axis=1))
    x = lanes_dft(jnp.concatenate(rows, axis=0))
    if spectrum:
        o_ref[...] = x.reshape(ns, n1, 2 * w)
        return
    h = hf_ref[...].reshape(ns * n1, 2 * w)
    xr, xi, hr, hi = x[:, :w], x[:, w:], h[:, :w], h[:, w:]
    y = jnp.concatenate([xr * hr - xi * hi, xr * hi + xi * hr], axis=1)
    g = mm(y, g2h[...])
    cr, ci = [], []
    for s in range(ns):
        gr, gi = g[s * n1:(s + 1) * n1, :w], g[s * n1:(s + 1) * n1, w:]
        cr.append(gr * twr + gi * twi)
        ci.append(gi * twr - gr * twi)
    gc = jnp.concatenate([jnp.concatenate(cr, axis=1), jnp.concatenate(ci, axis=1)], axis=0)
    y2 = mm(fih[...], gc)
    for s in range(ns):
        o_ref[0, s] = y2[:, s * w:(s + 1) * w]


def hyena_fft(zt, hf=None):
    nb, ch, half, w = zt.shape
    n1 = 2 * half
    ns = HY_SLABS
    f1h, f1l, f2h, f2l, g2h, _, fih, _, twr, twi = _hyena_dft_consts(n1)
    full = lambda a: pl.BlockSpec(a.shape, lambda b, c: (0,) * a.ndim)
    zspec = pl.BlockSpec((1, ns, half, w), lambda b, c: (b, c, 0, 0))
    if hf is None:
        consts = [f1h, f1l, f2h, f2l, twr, twi]
        in_specs, args = [zspec], [zt]
        out_spec = pl.BlockSpec((ns, n1, 2 * w), lambda b, c: (b * (ch // ns) + c, 0, 0))
        out_shape = jax.ShapeDtypeStruct((nb * ch, n1, 2 * w), jnp.float32)
    else:
        consts = [f1h, f2h, g2h, fih, twr, twi]
        in_specs = [zspec, pl.BlockSpec((ns, n1, 2 * w), lambda b, c: (c, 0, 0))]
        args = [zt, hf]
        out_spec = zspec
        out_shape = jax.ShapeDtypeStruct(zt.shape, jnp.float32)
    return pl.pallas_call(
        functools.partial(_hyena_fft_kernel, n1=n1, ns=ns, spectrum=hf is None),
        grid=(nb, ch // ns),
        in_specs=in_specs + [full(a) for a in consts],
        out_specs=out_spec,
        out_shape=out_shape,
        compiler_params=pltpu.CompilerParams(dimension_semantics=("arbitrary", "arbitrary"),
                                             vmem_limit_bytes=VMEM_LIMIT_BYTES),
        name="hyena_fft",
    )(*args, *consts)


def rmsnorm(x, g):
    y = x * lax.rsqrt(jnp.mean(x * x, axis=-1, keepdims=True) + EPS)
    return y * g


def adaln(cond, w, b):
    return jax.nn.silu(cond) @ w + b


def grid_pos_embed(n_tokens, dim):
    rows = n_tokens // GRID_W
    q = dim // 4
    omega = 1.0 / (10000.0 ** (jnp.arange(q, dtype=jnp.float32) / q))
    r = jnp.arange(rows, dtype=jnp.float32)[:, None] * omega
    cc = jnp.arange(GRID_W, dtype=jnp.float32)[:, None] * omega
    er = jnp.concatenate([jnp.sin(r), jnp.cos(r)], axis=-1)
    ec = jnp.concatenate([jnp.sin(cc), jnp.cos(cc)], axis=-1)
    emb = jnp.concatenate([jnp.broadcast_to(er[:, None], (rows, GRID_W, dim // 2)),
                           jnp.broadcast_to(ec[None], (rows, GRID_W, dim // 2))], axis=-1)
    return emb.reshape(rows * GRID_W, dim)


def dwconv(x, w, pad_l, pad_r):
    n = x.shape[1]
    xp = jnp.pad(x, ((0, 0), (pad_l, pad_r), (0, 0)))
    return sum(xp[:, k:k + n] * w[k] for k in range(w.shape[0]))


def gla_mixer(p, w_up, b_up, s0):
    gdn = p[..., P_GDN:P_GDN + 2 * GLA_RANK]
    outs, finals = [], []
    for d in range(2):
        od, sd = gla_dir(p, gdn[..., d * GLA_RANK:(d + 1) * GLA_RANK], w_up[d], b_up[d], s0[d], d == 1)
        outs.append(od)
        finals.append(sd)
    return outs, jnp.stack(finals)


def s5_mixer(u, prep, s0):
    wd, wk, wc, a_t = prep
    b_, n_tok, _ = u.shape
    n = n_tok // S5_T
    u2 = u.reshape(b_, n, S5_T * W_MIX).transpose(1, 0, 2).reshape(n * b_, S5_T * W_MIX)
    dmat = pmm(u2, wd)
    hmat, fin = s5_scan(dmat.reshape(n, b_, 4 * S5_NS), s0, a_t)
    y2 = pmm_multi([u2, hmat.reshape(n * b_, 4 * S5_NS)], [wk, wc], (False, True))
    y = y2.reshape(n, b_, S5_T, W_MIX).transpose(1, 0, 2, 3).reshape(b_, n_tok, W_MIX)
    return y, fin


def hyena_filters(n_tok, w1, b1, w2, b2, w3, freq):
    f32 = jnp.float32
    t = jnp.arange(n_tok, dtype=f32)[:, None]
    bands = jnp.linspace(1e-4, HY_BANDS - 1, HY_BANDS, dtype=f32)[None]
    ang = 2.0 * math.pi * bands * t / n_tok
    z = jnp.concatenate([t / n_tok, jnp.cos(ang), jnp.sin(ang)], axis=-1)
    hp = lax.Precision.HIGHEST
    h = jnp.sin(freq * (jnp.dot(z, w1, precision=hp) + b1))
    h = jnp.sin(freq * (jnp.dot(h, w2, precision=hp) + b2))
    h = jnp.dot(h, w3, precision=hp)
    t01 = t / max(n_tok - 1, 1)
    deltas = jnp.abs(jnp.linspace(math.log(HY_TARGET) / HY_DECAY_SHORT,
                                  math.log(HY_TARGET) / HY_DECAY_LONG, W_MIX, dtype=f32))
    h = h * jnp.exp(-t01 * jnp.tile(deltas, 2))
    return h / (jnp.sum(jnp.abs(h), axis=0, keepdims=True) + EPS)


def hyena_mixer(p, w_short, w1, b1, w2, b2, w3, freq, bias):
    nb, n_tok, _ = p.shape
    pc = dwconv(p, w_short, 1, 1)
    v, x0, x1 = jnp.split(pc, 3, axis=-1)
    z = x1 * v
    n_pad = max(n_tok, HY_MIN_LEN)
    half = n_pad // HY_LANES

    def frames(t):
        t = jnp.pad(t, [(0, 0)] * (t.ndim - 1) + [(0, n_pad - n_tok)])
        return t.reshape(t.shape[:-1] + (half, HY_LANES))

    filt = hyena_filters(n_tok, w1, b1, w2, b2, w3, freq)
    spec = hyena_fft(frames(filt.T)[None])
    sf, sb = spec[:W_MIX], spec[W_MIX:]
    hfreq = jnp.concatenate([sf[..., :HY_LANES] + sb[..., :HY_LANES],
                             sf[..., HY_LANES:] - sb[..., HY_LANES:]], axis=-1)
    conv = hyena_fft(frames(z.transpose(0, 2, 1)), hfreq)
    conv = conv.reshape(nb, W_MIX, n_pad)[:, :, :n_tok].transpose(0, 2, 1)
    return x0 * (conv + z * bias)


def rglru_mixer(p, w_conv, w_a, b_a, w_x, b_x, lam, s0):
    a_f, b_f, a_b, b_b = rg_pre(p, w_conv, w_a, b_a, w_x, b_x, lam)
    h_f, fin_f = rg_scan(a_f, b_f, s0[0], False)
    h_b, fin_b = rg_scan(a_b, b_b, s0[1], True)
    return [h_f, h_b], jnp.stack([fin_f, fin_b])


def ffn_dense(h2, x, gate_mod, w_gu, w_down):
    n_tiles = h2.shape[0] // FFN_TM
    return grouped_swiglu(jnp.zeros((n_tiles,), jnp.int32), jnp.full((1,), n_tiles, jnp.int32), h2,
                          w_gu[None].astype(jnp.bfloat16), w_down[None].astype(jnp.bfloat16), nf=2,
                          residual=(x, gate_mod))


def ffn_moe(h2, logits, x, gate_mod, w_gu, w_down, final_g):
    n_tok = h2.shape[0]
    n_slot = TOP_K * n_tok
    top_v, top_i = lax.top_k(logits, TOP_K)
    w = jax.nn.softmax(top_v, axis=-1)
    e_flat = top_i.T.reshape(-1).astype(jnp.int32)
    onehot = (e_flat[:, None] == jnp.arange(N_EXPERTS, dtype=jnp.int32)[None]).astype(jnp.int32)
    csum = jnp.cumsum(onehot, axis=0)
    cnt = csum[-1]
    rank = jnp.sum(csum * onehot, axis=1) - 1
    padded = ((cnt + FFN_TM - 1) // FFN_TM) * FFN_TM
    ends = jnp.cumsum(padded)
    dest = (ends - padded)[e_flat] + rank
    n_rows = n_slot + N_EXPERTS * FFN_TM
    n_tiles = n_rows // FFN_TM
    src = jnp.zeros((n_rows,), jnp.int32).at[dest].set(jnp.arange(n_slot, dtype=jnp.int32) % n_tok)
    tile_start = jnp.arange(n_tiles, dtype=jnp.int32) * FFN_TM
    tile_expert = jnp.minimum(jnp.sum((tile_start[:, None] >= ends[None, :]).astype(jnp.int32), axis=1),
                              N_EXPERTS - 1)
    n_used = (ends[-1:] // FFN_TM).astype(jnp.int32)
    xs = h2.at[src].get(mode="promise_in_bounds")
    ys = grouped_swiglu(tile_expert, n_used, xs, w_gu.astype(jnp.bfloat16),
                        w_down.astype(jnp.bfloat16), nf=2)
    yk = ys.at[dest].get(mode="promise_in_bounds")
    return moe_combine(x, yk, w, gate_mod, final_g)


def kernel(x, c, ctx, c_ctx, mod_w, mod_b, norm1_g, norm2_g, w_in, gla_w_up, gla_b_up,
           s5_lam_re, s5_lam_im, s5_log_dt, s5_b_re, s5_b_im, s5_c_re, s5_c_im, s5_d,
           s5_w_glu, s5_b_glu, hy_w_short, hy_w1, hy_b1, hy_w2, hy_b2, hy_w3, hy_freq,
           hy_bias, rg_w_conv, rg_w_a, rg_b_a, rg_w_x, rg_b_x, rg_lam, w_branch, w_out,
           ffn_w_gu, ffn_w_down, moe_router, moe_router_b, moe_w_gu, moe_w_down, final_g):
    f32, bf16 = jnp.float32, jnp.bfloat16
    n_b, n_lat, _ = x.shape
    n_ctx = ctx.shape[1]
    xs = (x + grid_pos_embed(n_lat, D_MODEL)[None]).reshape(n_b * n_lat, D_MODEL)
    cs = ctx.reshape(n_b * n_ctx, D_MODEL)
    offs = [0]
    for n in IN_SIZES:
        offs.append(offs[-1] + n)
    for l in range(DEPTH):
        last = l == DEPTH - 1
        dense = l % 2 == 0
        j = l // 2
        m_lat = [t[:, None, :] for t in jnp.split(adaln(c, mod_w[l], mod_b[l]), 6, axis=-1)]
        m_ctx = [t[None, None, :] for t in jnp.split(adaln(c_ctx, mod_w[l], mod_b[l]), 6, axis=-1)]
        wl = w_in[l]
        w_mix = jnp.concatenate([wl[:, offs[6]:offs[7]], wl[:, offs[0]:offs[4]], wl[:, offs[5]:offs[6]],
                                 wl[:, offs[7]:offs[9]], wl[:, offs[4]:offs[5]],
                                 jnp.zeros((D_MODEL, P_COLS - P_GDN - 2 * GLA_RANK), f32)], axis=1).astype(bf16)
        w_gate = wl[:, offs[9]:offs[10]].astype(bf16)
        wb, wo, wglu = w_branch[l].astype(bf16), w_out[l].astype(bf16), s5_w_glu[l].astype(bf16)
        s5_ops = s5_prepare(s5_lam_re[l], s5_lam_im[l], s5_log_dt[l], s5_b_re[l], s5_b_im[l],
                            s5_c_re[l], s5_c_im[l], s5_d[l])
        hy_p = (hy_w_short[l], hy_w1[l], hy_b1[l], hy_w2[l], hy_b2[l], hy_w3[l], hy_freq[l], hy_bias[l])
        rg_p = (rg_w_conv[l], rg_w_a[l], rg_b_a[l], rg_w_x[l], rg_b_x[l], rg_lam[l])
        router = None if dense else (moe_router[j], moe_router_b[j])

        def mixers(p2, n_tok, states, with_hyena):
            p = p2.reshape(n_b, n_tok, P_COLS)
            blk = lambda i, n=1: p[..., i * W_MIX:(i + n) * W_MIX]
            flat = lambda t: t.reshape(n_b * n_tok, W_MIX)
            gla_o, gla_s = gla_mixer(p, gla_w_up[l], gla_b_up[l], states[0])
            s5_y, s5_s = s5_mixer(blk(P_S5), s5_ops, states[1])
            rg_o, rg_s = rglru_mixer(p, *rg_p, states[2])
            hy = flat(hyena_mixer(blk(P_HY, 3), *hy_p)) if with_hyena else None
            return ((flat(gla_o[0]), flat(gla_o[1]), flat(s5_y), hy, flat(rg_o[0]), flat(rg_o[1])),
                    (gla_s, s5_s, rg_s))

        def tail(stream, mods, p2, br, final):
            outs = merge(stream, tuple(mods[:5]), norm1_g[l], norm2_g[l], br[0], br[1], p2, br[2], br[3],
                         br[4], br[5], w_gate, wb, wo, wglu, s5_b_glu[l], router)
            if dense:
                return ffn_dense(outs[1], outs[0], mods[5], ffn_w_gu[j], ffn_w_down[j])
            return ffn_moe(outs[1], outs[2][:, :N_EXPERTS], outs[0], mods[5], moe_w_gu[j], moe_w_down[j],
                           final_g if final else None)

        zero_states = (jnp.zeros((2, n_b, GLA_HEADS, GLA_DV, GLA_DK), f32),
                       jnp.zeros((2, n_b, 2 * S5_NS), f32), jnp.zeros((2, n_b, W_MIX), f32))
        p_ctx = front(cs, m_ctx[0], m_ctx[1], norm1_g[l], w_mix)
        br_ctx, states = mixers(p_ctx, n_ctx, zero_states, not last)
        p_lat = front(xs, m_lat[0], m_lat[1], norm1_g[l], w_mix)
        br_lat, _ = mixers(p_lat, n_lat, states, True)
        xs = tail(xs, m_lat, p_lat, br_lat, last)
        if not last:
            cs = tail(cs, m_ctx, p_ctx, br_ctx, False)
    if (DEPTH - 1) % 2 == 0:
        xs = rmsnorm(xs, final_g)
    return xs.reshape(n_b, n_lat, D_MODEL)
```

```python
import functools
import math

import jax
import jax.numpy as jnp
import numpy as np
from jax import lax
from jax.experimental import pallas as pl
from jax.experimental.pallas import tpu as pltpu

D_MODEL = 1024
DEPTH = 2
GRID_W = 64
EPS = 1e-6
N_BRANCH = 4
W_MIX = D_MODEL // N_BRANCH
GLA_HEADS = 4
GLA_DK = W_MIX // GLA_HEADS
GLA_DV = W_MIX // GLA_HEADS
GLA_RANK = 16
GLA_TAU = 16.0
GLA_CHUNK = 64
S5_GROUP = 16
S5_GROUPS = W_MIX // S5_GROUP
S5_STATE = 64
S5_MAX_RE = -1e-4
HY_BANDS = 16
HY_SHORT = 3
HY_DECAY_SHORT = 0.3
HY_DECAY_LONG = 1.5
HY_TARGET = 1e-2
RG_BLOCKS = 4
RG_BLOCK = W_MIX // RG_BLOCKS
RG_CONV = 4
RG_C = 8.0
N_EXPERTS = 8
TOP_K = 2
IN_SIZES = (GLA_HEADS * GLA_DK, GLA_HEADS * GLA_DK, GLA_HEADS * GLA_DV, GLA_HEADS * GLA_DV,
            2 * GLA_RANK, W_MIX, 3 * W_MIX, W_MIX, W_MIX, N_BRANCH * D_MODEL)

VMEM_LIMIT_BYTES = 48 * 1024 * 1024


def _mm_kernel(x_ref, w_ref, o_ref):
    o_ref[...] = jnp.dot(x_ref[...].astype(jnp.bfloat16), w_ref[...],
                         preferred_element_type=jnp.float32)


def _pick_tile(n, cap):
    best = None
    for t in range(128, cap + 1, 128):
        if n % t == 0:
            best = t
    return best if best is not None else n


def pmm(x, w):
    lead = x.shape[:-1]
    k = x.shape[-1]
    n = w.shape[-1]
    x2 = x.reshape(-1, k)
    m = x2.shape[0]
    tm = 512 if m % 512 == 0 else m
    if k > 2048 and m % 256 == 0:
        tm = 256
    tn = n if k * n * 2 <= 6 * 1024 * 1024 else _pick_tile(n, 1024)
    out = pl.pallas_call(
        _mm_kernel,
        grid=(m // tm, n // tn),
        in_specs=[pl.BlockSpec((tm, k), lambda i, j: (i, 0)),
                  pl.BlockSpec((k, tn), lambda i, j: (0, j))],
        out_specs=pl.BlockSpec((tm, tn), lambda i, j: (i, j)),
        out_shape=jax.ShapeDtypeStruct((m, n), jnp.float32),
        compiler_params=pltpu.CompilerParams(
            dimension_semantics=("arbitrary", "arbitrary"),
            vmem_limit_bytes=VMEM_LIMIT_BYTES),
    )(x2, w.astype(jnp.bfloat16))
    return out.reshape(lead + (n,))


def _mm_multi_kernel(*refs, transposed):
    o_ref = refs[-1]
    n = (len(refs) - 1) // 2
    acc = None
    for i in range(n):
        dims = (((1,), (1 if transposed[i] else 0,)), ((), ()))
        t = lax.dot_general(refs[i][...].astype(jnp.bfloat16), refs[n + i][...], dims,
                            preferred_element_type=jnp.float32)
        acc = t if acc is None else acc + t
    o_ref[...] = acc


def pmm_multi(xs, ws, transposed, tm=256, tn=512):
    m = xs[0].shape[0]
    n = ws[0].shape[0 if transposed[0] else 1]
    tm = tm if m % tm == 0 else m
    tn = tn if n % tn == 0 else n
    in_specs = ([pl.BlockSpec((tm, x.shape[1]), lambda i, j: (i, 0)) for x in xs]
                + [pl.BlockSpec((tn, w.shape[1]), lambda i, j: (j, 0)) if t else
                   pl.BlockSpec((w.shape[0], tn), lambda i, j: (0, j)) for w, t in zip(ws, transposed)])
    return pl.pallas_call(
        functools.partial(_mm_multi_kernel, transposed=tuple(transposed)),
        grid=(m // tm, n // tn),
        in_specs=in_specs,
        out_specs=pl.BlockSpec((tm, tn), lambda i, j: (i, j)),
        out_shape=jax.ShapeDtypeStruct((m, n), jnp.float32),
        compiler_params=pltpu.CompilerParams(
            dimension_semantics=("arbitrary", "arbitrary"),
            vmem_limit_bytes=VMEM_LIMIT_BYTES),
    )(*xs, *[w.astype(jnp.bfloat16) for w in ws])


RG_SCAN_ROWS = 256


def _rg_scan_kernel(a_ref, b_ref, s0_ref, h_ref, fin_ref, st_ref, *, reverse, tb, nb):
    @pl.when(pl.program_id(0) == 0)
    def _():
        st_ref[...] = s0_ref[...]

    def body(r, hs):
        rr = (tb - 1 - r) if reverse else r
        out = []
        for i in range(nb):
            h = a_ref[i, pl.ds(rr, 1), :] * hs[i] + b_ref[i, pl.ds(rr, 1), :]
            h_ref[i, pl.ds(rr, 1), :] = h
            out.append(h)
        return tuple(out)

    hs = lax.fori_loop(0, tb, body, tuple(st_ref[i:i + 1, :] for i in range(nb)), unroll=8)
    for i in range(nb):
        st_ref[i:i + 1, :] = hs[i]
        fin_ref[i:i + 1, :] = hs[i]


def rg_scan(a, b, s0, reverse):
    nb, n_tok, ch = a.shape
    tb = min(RG_SCAN_ROWS, n_tok)
    nblk = n_tok // tb
    imap = (lambda k: (0, nblk - 1 - k, 0)) if reverse else (lambda k: (0, k, 0))
    return pl.pallas_call(
        functools.partial(_rg_scan_kernel, reverse=reverse, tb=tb, nb=nb),
        grid=(nblk,),
        in_specs=[pl.BlockSpec((nb, tb, ch), imap), pl.BlockSpec((nb, tb, ch), imap),
                  pl.BlockSpec((nb, ch), lambda k: (0, 0))],
        out_specs=[pl.BlockSpec((nb, tb, ch), imap), pl.BlockSpec((nb, ch), lambda k: (0, 0))],
        out_shape=[jax.ShapeDtypeStruct((nb, n_tok, ch), jnp.float32),
                   jax.ShapeDtypeStruct((nb, ch), jnp.float32)],
        scratch_shapes=[pltpu.VMEM((nb, ch), jnp.float32)],
        compiler_params=pltpu.CompilerParams(dimension_semantics=("arbitrary",)),
        name="rg_scan",
    )(a, b, s0)


RG_HALO = 8


def _rg_pre_kernel(x_ref, prev_ref, next_ref, wc_ref, wg_ref, bg_ref, c_ref, af_ref, bf_ref, ab_ref, bb_ref,
                   *, tb):
    k = pl.program_id(1)
    prev = jnp.where(k > 0, prev_ref[0], 0.0)
    nxt = jnp.where(k < pl.num_programs(1) - 1, next_ref[0], 0.0)
    ext = jnp.concatenate([prev, x_ref[0], nxt], axis=0)
    xc = sum(ext[RG_HALO - 2 + j:RG_HALO - 2 + j + tb, :] * wc_ref[j:j + 1, :] for j in range(RG_CONV))
    m = jnp.dot(xc.astype(jnp.bfloat16), wg_ref[...], preferred_element_type=jnp.float32) + bg_ref[...]
    for d, (a_ref, b_ref) in enumerate(((af_ref, bf_ref), (ab_ref, bb_ref))):
        r = jax.nn.sigmoid(m[:, (2 * d) * W_MIX:(2 * d + 1) * W_MIX])
        i = jax.nn.sigmoid(m[:, (2 * d + 1) * W_MIX:(2 * d + 2) * W_MIX])
        log_a = -r * c_ref[d:d + 1, :]
        a_ref[0] = jnp.exp(log_a)
        u = jnp.tanh(log_a)
        b_ref[0] = jnp.sqrt(-2.0 * u / (1.0 - u)) * (i * xc)


def rg_pre(p, w_conv, w_a, b_a, w_x, b_x, lam):
    nb, n_tok, _ = p.shape
    tb = min(ROW_TILE, n_tok)
    nblk = n_tok // tb
    hb = tb // RG_HALO
    n_halo = n_tok // RG_HALO
    col = P_RGX
    blockdiag = lambda w: jax.scipy.linalg.block_diag(*[w[i] for i in range(RG_BLOCKS)])
    wg = jnp.concatenate([blockdiag(w_a[0]), blockdiag(w_x[0]), blockdiag(w_a[1]), blockdiag(w_x[1])],
                         axis=1).astype(jnp.bfloat16)
    bg = jnp.concatenate([b_a[0].reshape(-1), b_x[0].reshape(-1), b_a[1].reshape(-1), b_x[1].reshape(-1)]
                         ).reshape(1, 4 * W_MIX)
    c = RG_C * jax.nn.softplus(-lam)
    blk = pl.BlockSpec((1, tb, W_MIX), lambda b, k: (b, k, 0))
    full = lambda a: pl.BlockSpec(a.shape, lambda b, k: (0,) * a.ndim)
    out = jax.ShapeDtypeStruct((nb, n_tok, W_MIX), jnp.float32)
    return pl.pallas_call(
        functools.partial(_rg_pre_kernel, tb=tb),
        grid=(nb, nblk),
        in_specs=[pl.BlockSpec((1, tb, W_MIX), lambda b, k: (b, k, col)),
                  pl.BlockSpec((1, RG_HALO, W_MIX), lambda b, k: (b, jnp.maximum(k * hb - 1, 0), col)),
                  pl.BlockSpec((1, RG_HALO, W_MIX), lambda b, k: (b, jnp.minimum((k + 1) * hb, n_halo - 1), col)),
                  full(w_conv), full(wg), full(bg), full(c)],
        out_specs=[blk, blk, blk, blk],
        out_shape=[out, out, out, out],
        compiler_params=pltpu.CompilerParams(dimension_semantics=("arbitrary", "arbitrary")),
        name="rg_pre",
    )(p, p, p, w_conv, wg, bg, c)


S5_T = 16
S5_NS = S5_GROUPS * S5_STATE
S5_SCAN_CHUNKS = 64


def _s5_scan_kernel(d_ref, s0_ref, a_ref, h_ref, fin_ref, st_ref, *, rc, nb):
    d = pl.program_id(0)

    @pl.when(pl.program_id(1) == 0)
    def _():
        st_ref[...] = s0_ref[0]

    ar = jnp.broadcast_to(a_ref[0, :, 0:S5_NS], (nb, S5_NS))
    ai = jnp.broadcast_to(a_ref[0, :, S5_NS:2 * S5_NS], (nb, S5_NS))

    def body(r, carry):
        hr, hi = carry
        rr = r + d * (rc - 1 - 2 * r)
        h_ref[rr, :, 0:S5_NS] = hr
        h_ref[rr, :, S5_NS:2 * S5_NS] = hi
        dr = d_ref[rr, :, 0:S5_NS]
        di = d_ref[rr, :, S5_NS:2 * S5_NS]
        return ar * hr - ai * hi + dr, ar * hi + ai * hr + di

    hr, hi = lax.fori_loop(0, rc, body, (st_ref[:, 0:S5_NS], st_ref[:, S5_NS:2 * S5_NS]))
    st_ref[:, 0:S5_NS] = hr
    st_ref[:, S5_NS:2 * S5_NS] = hi
    fin_ref[0, :, 0:S5_NS] = hr
    fin_ref[0, :, S5_NS:2 * S5_NS] = hi


def s5_scan(dmat, s0, a_t):
    n, nb, _ = dmat.shape
    rc = min(S5_SCAN_CHUNKS, n)
    nblk = n // rc
    w = 2 * S5_NS
    imap = lambda d, k: (k + d * (nblk - 1 - 2 * k), 0, d)
    return pl.pallas_call(
        functools.partial(_s5_scan_kernel, rc=rc, nb=nb),
        grid=(2, nblk),
        in_specs=[pl.BlockSpec((rc, nb, w), imap),
                  pl.BlockSpec((1, nb, w), lambda d, k: (d, 0, 0)),
                  pl.BlockSpec((1, 1, w), lambda d, k: (d, 0, 0))],
        out_specs=[pl.BlockSpec((rc, nb, w), imap),
                   pl.BlockSpec((1, nb, w), lambda d, k: (d, 0, 0))],
        out_shape=[jax.ShapeDtypeStruct((n, nb, 2 * w), jnp.float32),
                   jax.ShapeDtypeStruct((2, nb, w), jnp.float32)],
        scratch_shapes=[pltpu.VMEM((nb, w), jnp.float32)],
        compiler_params=pltpu.CompilerParams(dimension_semantics=("arbitrary", "arbitrary"),
                                             vmem_limit_bytes=VMEM_LIMIT_BYTES),
    )(dmat, s0, a_t)


def _cmul(ar, ai, br, bi):
    return ar * br - ai * bi, ar * bi + ai * br


def s5_prepare(lam_re, lam_im, log_dt, b_re, b_im, c_re, c_im, d_skip):
    f32 = jnp.float32
    hp = lax.Precision.HIGHEST
    t_len, g_n, p_n, h_n = S5_T, S5_GROUPS, S5_STATE, S5_GROUP
    bf16 = jnp.bfloat16
    eye_g = jnp.eye(g_n, dtype=f32)
    mask_gp = jnp.repeat(eye_g, p_n, axis=1).astype(bf16)[None, :, None, :]
    ar_t = jnp.arange(t_len)
    wd, wc, kk, a_t = [], [], [], []
    for d in range(2):
        lr = jnp.minimum(lam_re[d], S5_MAX_RE)
        li = lam_im[d]
        dt = jnp.exp(log_dt[d])[:, None]
        tt = jnp.arange(t_len + 1, dtype=f32)[:, None, None]
        mag = jnp.exp(lr * dt * tt)
        ang = li * dt * tt
        pr, pi = mag * jnp.cos(ang), mag * jnp.sin(ang)
        nr, ni = pr[1] - 1.0, pi[1]
        den = lr * lr + li * li
        qr, qi = (nr * lr + ni * li) / den, (ni * lr - nr * li) / den
        bbr, bbi = _cmul(qr[..., None], qi[..., None], b_re[d], b_im[d])
        cr, ci = c_re[d], c_im[d]

        idx = (t_len - 1 - ar_t) if d == 0 else ar_t
        wr, wi = _cmul(pr[idx][..., None], pi[idx][..., None], bbr[None], bbi[None])

        def place_d(w):
            wt = w.transpose(0, 3, 1, 2).reshape(t_len, 1, h_n, g_n * p_n).astype(bf16)
            return (wt * mask_gp).reshape(t_len * W_MIX, g_n * p_n)

        wd.append(jnp.concatenate([place_d(wr), place_d(wi)], axis=1))

        idx2 = (ar_t + 1) if d == 0 else (t_len - ar_t)
        cwr, cwi = _cmul(cr[None], ci[None], pr[idx2][:, :, None, :], pi[idx2][:, :, None, :])

        def place_c(w):
            wt = w.transpose(0, 2, 1, 3).reshape(t_len, 1, h_n, g_n * p_n).astype(bf16)
            return (wt * mask_gp).reshape(t_len * W_MIX, g_n * p_n)

        wc.append(jnp.concatenate([place_c(cwr), place_c(-cwi)], axis=1))

        er, ei = _cmul(pr[:t_len][:, :, None, :], pi[:t_len][:, :, None, :], cr[None], ci[None])
        kk.append(jnp.einsum('tghp,gpk->tghk', er, bbr, precision=hp)
                  - jnp.einsum('tghp,gpk->tghk', ei, bbi, precision=hp))
        a_t.append(jnp.concatenate([pr[t_len].reshape(1, -1), pi[t_len].reshape(1, -1)], axis=1))

    lag = ar_t[None, :] - ar_t[:, None]
    mf = jnp.where((lag >= 0)[..., None, None, None], kk[0][jnp.clip(lag, 0, t_len - 1)], 0.0)
    mb = jnp.where((lag <= 0)[..., None, None, None], kk[1][jnp.clip(-lag, 0, t_len - 1)], 0.0)
    skip = (jnp.eye(t_len, dtype=f32)[:, :, None, None, None] * d_skip[None, None, :, :, None]
            * jnp.eye(h_n, dtype=f32)[None, None, None])
    m = mf + mb + skip
    mt = m.transpose(0, 4, 1, 2, 3).reshape(t_len, 1, h_n, t_len * W_MIX).astype(bf16)
    mask_igh = jnp.tile(jnp.repeat(eye_g, h_n, axis=1), (1, t_len)).astype(bf16)[None, :, None, :]
    wk = (mt * mask_igh).reshape(t_len * W_MIX, t_len * W_MIX)
    return (jnp.concatenate(wd, axis=1), wk, jnp.concatenate(wc, axis=1), jnp.stack(a_t))


GLA_BLOCK = 512


def _gla_kernel(q_ref, k_ref, v_ref, g_ref, wup_ref, bup_ref, s0_ref, o_ref, fin_ref, st_ref,
                *, reverse, tb):
    f32, bf16 = jnp.float32, jnp.bfloat16
    hp = lax.Precision.HIGHEST
    cc = GLA_CHUNK

    @pl.when(pl.program_id(1) == 0)
    def _():
        st_ref[...] = s0_ref[0]

    r_i = lax.broadcasted_iota(jnp.int32, (cc, cc), 0)
    c_i = lax.broadcasted_iota(jnp.int32, (cc, cc), 1)
    keep = (c_i >= r_i) if reverse else (c_i <= r_i)
    tri = keep.astype(f32)
    nt = (((1,), (1,)), ((), ()))
    tn = (((0,), (0,)), ((), ()))
    n_ch = tb // cc
    chunks = range(n_ch)
    heads = range(GLA_HEADS)
    hsl = [slice(h * GLA_DK, (h + 1) * GLA_DK) for h in heads]
    rows = [slice(c * cc, (c + 1) * cc) for c in chunks]

    z = jnp.dot(g_ref[0], wup_ref[...], precision=hp, preferred_element_type=f32) + bup_ref[...]
    la = (jnp.minimum(z, 0.0) - jnp.log1p(jnp.exp(-jnp.abs(z)))) * (1.0 / GLA_TAU)
    cum = [jnp.dot(tri, la[rows[c]], precision=hp, preferred_element_type=f32) for c in chunks]
    last = [cm[0:1, :] if reverse else cm[cc - 1:cc, :] for cm in cum]
    k = [k_ref[0, rows[c], :] for c in chunks]
    q_in = [(q_ref[0, rows[c], :] * (GLA_DK ** -0.5) * jnp.exp(cum[c])).astype(bf16) for c in chunks]
    k_in = [(k[c] * jnp.exp(-cum[c])).astype(bf16) for c in chunks]
    k_out = [(k[c] * jnp.exp(last[c] - cum[c])).astype(bf16) for c in chunks]
    dec = [jnp.exp(last[c]) for c in chunks]
    vb = [v_ref[0, rows[c], :].astype(bf16) for c in chunks]
    att = [[jnp.where(keep, lax.dot_general(q_in[c][:, s], k_in[c][:, s], nt, preferred_element_type=f32),
                      0.0).astype(bf16) for s in hsl] for c in chunks]
    o_intra = [[jnp.dot(att[c][h], vb[c][:, hsl[h]], preferred_element_type=f32) for h in heads]
               for c in chunks]
    d_state = [[lax.dot_general(vb[c][:, s], k_out[c][:, s], tn, preferred_element_type=f32) for s in hsl]
               for c in chunks]

    st = [st_ref[h] for h in heads]
    st_in = [None] * n_ch
    for c in (reversed(chunks) if reverse else chunks):
        st_in[c] = [s.astype(bf16) for s in st]
        st = [st[h] * dec[c][:, hsl[h]] + d_state[c][h] for h in heads]
    for h in heads:
        st_ref[h] = st[h]
        fin_ref[0, h] = st[h]

    for c in chunks:
        o_ref[0, rows[c], :] = jnp.concatenate(
            [o_intra[c][h] + lax.dot_general(q_in[c][:, hsl[h]], st_in[c][h], nt, preferred_element_type=f32)
             for h in heads], axis=1)


def gla_dir(p, gd, w_up, b_up, s0, reverse):
    nb, n_tok, _ = p.shape
    tb = min(GLA_BLOCK, n_tok)
    nblk = n_tok // tb
    blk = (lambda k: nblk - 1 - k) if reverse else (lambda k: k)
    col = lambda c: pl.BlockSpec((1, tb, W_MIX), lambda b, k: (b, blk(k), c))
    st_shape = (GLA_HEADS, GLA_DV, GLA_DK)
    return pl.pallas_call(
        functools.partial(_gla_kernel, reverse=reverse, tb=tb),
        grid=(nb, nblk),
        in_specs=[col(P_Q), col(P_Q + 1), col(P_Q + 2),
                  pl.BlockSpec((1, tb, GLA_RANK), lambda b, k: (b, blk(k), 0)),
                  pl.BlockSpec((GLA_RANK, W_MIX), lambda b, k: (0, 0)),
                  pl.BlockSpec((1, W_MIX), lambda b, k: (0, 0)),
                  pl.BlockSpec((1,) + st_shape, lambda b, k: (b, 0, 0, 0))],
        out_specs=[pl.BlockSpec((1, tb, W_MIX), lambda b, k: (b, blk(k), 0)),
                   pl.BlockSpec((1,) + st_shape, lambda b, k: (b, 0, 0, 0))],
        out_shape=[jax.ShapeDtypeStruct((nb, n_tok, W_MIX), jnp.float32),
                   jax.ShapeDtypeStruct((nb,) + st_shape, jnp.float32)],
        scratch_shapes=[pltpu.VMEM(st_shape, jnp.float32)],
        compiler_params=pltpu.CompilerParams(dimension_semantics=("arbitrary", "arbitrary"),
                                             vmem_limit_bytes=VMEM_LIMIT_BYTES),
        name="gla",
    )(p, p, p, gd, w_up, b_up.reshape(1, W_MIX), s0)


FFN_TM = 512


def _swiglu_kernel(te_ref, nu_ref, x_ref, wg_ref, wu_ref, wd_ref, *rest, nf, residual):
    del te_ref
    o_ref = rest[-1]
    j = pl.program_id(1)

    @pl.when(pl.program_id(0) < nu_ref[0])
    def _():
        x = x_ref[...]
        g = jnp.dot(x, wg_ref[0], preferred_element_type=jnp.float32)
        u = jnp.dot(x, wu_ref[0], preferred_element_type=jnp.float32)
        a = (g * jax.nn.sigmoid(g) * u).astype(jnp.bfloat16)
        part = jnp.dot(a, wd_ref[0], preferred_element_type=jnp.float32)

        @pl.when(j == 0)
        def _():
            o_ref[...] = part

        @pl.when(j > 0)
        def _():
            o_ref[...] += part

        if residual:
            xres_ref, gm_ref = rest[:2]

            @pl.when(j == nf - 1)
            def _():
                o_ref[...] = xres_ref[...] + gm_ref[0] * o_ref[...]


def grouped_swiglu(tile_expert, n_used, xs, w_gu, w_down, nf, residual=None):
    m, d = xs.shape
    f = w_down.shape[1]
    tf = f // nf
    n_tiles = m // FFN_TM
    in_specs = [pl.BlockSpec((FFN_TM, d), lambda t, j, te, nu: (t, 0)),
                pl.BlockSpec((1, d, tf), lambda t, j, te, nu: (te[t], 0, j)),
                pl.BlockSpec((1, d, tf), lambda t, j, te, nu: (te[t], 0, nf + j)),
                pl.BlockSpec((1, tf, d), lambda t, j, te, nu: (te[t], j, 0))]
    extra = ()
    if residual is not None:
        rows_per_mod = m // residual[1].shape[0]
        in_specs += [pl.BlockSpec((FFN_TM, d), lambda t, j, te, nu: (t, 0)),
                     pl.BlockSpec((1, 1, d), lambda t, j, te, nu: ((t * FFN_TM) // rows_per_mod, 0, 0))]
        extra = tuple(residual)
    grid_spec = pltpu.PrefetchScalarGridSpec(
        num_scalar_prefetch=2,
        grid=(n_tiles, nf),
        in_specs=in_specs,
        out_specs=pl.BlockSpec((FFN_TM, d), lambda t, j, te, nu: (t, 0)))
    return pl.pallas_call(
        functools.partial(_swiglu_kernel, nf=nf, residual=residual is not None),
        grid_spec=grid_spec,
        out_shape=jax.ShapeDtypeStruct((m, d), jnp.float32),
        compiler_params=pltpu.CompilerParams(
            dimension_semantics=("arbitrary", "arbitrary"),
            vmem_limit_bytes=VMEM_LIMIT_BYTES),
        name="grouped_swiglu",
    )(tile_expert, n_used, xs, w_gu, w_gu, w_down, *extra)


ROW_TILE = 512
P_HY, P_Q, P_OG, P_S5, P_RGX, P_RGG = 0, 3, 6, 7, 8, 9
P_GDN = 10 * W_MIX
P_COLS = P_GDN + 128


def _rms_mod(x, g, shift, scale):
    y = x * lax.rsqrt(jnp.mean(x * x, axis=-1, keepdims=True) + EPS) * g
    return y * (1.0 + scale) + shift


def _front_kernel(x_ref, sh_ref, sc_ref, g_ref, w_ref, o_ref):
    h = _rms_mod(x_ref[...], g_ref[...], sh_ref[0], sc_ref[0])
    o_ref[...] = jnp.dot(h.astype(jnp.bfloat16), w_ref[...], preferred_element_type=jnp.float32)


def _mod_spec(rows_per_mod):
    return pl.BlockSpec((1, 1, D_MODEL), lambda i: ((i * ROW_TILE) // rows_per_mod, 0, 0))


def front(x, shift, scale, g, w):
    m = x.shape[0]
    n_mod = shift.shape[0]
    n_out = w.shape[1]
    mod = _mod_spec(m // n_mod)
    return pl.pallas_call(
        _front_kernel,
        grid=(m // ROW_TILE,),
        in_specs=[pl.BlockSpec((ROW_TILE, D_MODEL), lambda i: (i, 0)), mod, mod,
                  pl.BlockSpec((1, D_MODEL), lambda i: (0, 0)),
                  pl.BlockSpec((D_MODEL, n_out), lambda i: (0, 0))],
        out_specs=pl.BlockSpec((ROW_TILE, n_out), lambda i: (i, 0)),
        out_shape=jax.ShapeDtypeStruct((m, n_out), jnp.float32),
        compiler_params=pltpu.CompilerParams(dimension_semantics=("arbitrary",),
                                             vmem_limit_bytes=VMEM_LIMIT_BYTES),
        name="front",
    )(x, shift, scale, g.reshape(1, D_MODEL), w)


MERGE_TILE = 256
ROUTER_PAD = 128
N_MERGE_IN = 25


def _merge_kernel(*refs, with_router):
    (x_ref, sh1_ref, sc1_ref, gm_ref, sh2_ref, sc2_ref, g1_ref, g2_ref,
     of_ref, ob_ref, og_ref, s5_ref, hc_ref, hz_ref, hx0_ref, rf_ref, rb_ref, rgg_ref,
     wg_ref, wb_ref, wo_ref, wglu_ref, bglu_ref, havg_ref, hbias_ref) = refs[:N_MERGE_IN]
    f32, bf16 = jnp.float32, jnp.bfloat16
    hp = lax.Precision.HIGHEST
    x = x_ref[...]
    hb = _rms_mod(x, g1_ref[...], sh1_ref[0], sc1_ref[0]).astype(bf16)

    o = of_ref[...] + ob_ref[...]
    ms = jnp.dot(o * o, havg_ref[...], precision=hp, preferred_element_type=f32)
    og = og_ref[...]
    gla = o * lax.rsqrt(ms + EPS) * (og * jax.nn.sigmoid(og))
    g5 = jax.nn.gelu(s5_ref[...])
    s5o = g5 * jax.nn.sigmoid(jnp.dot(g5.astype(bf16), wglu_ref[...], preferred_element_type=f32)
                              + bglu_ref[...])
    rgo = (rf_ref[...] + rb_ref[...]) * jax.nn.gelu(rgg_ref[...])
    hyo = hx0_ref[...] * (hc_ref[...] + hz_ref[...] * hbias_ref[...])
    branches = (gla, s5o, hyo, rgo)

    y = None
    for k in range(N_BRANCH):
        gate = jax.nn.sigmoid(jnp.dot(hb, wg_ref[:, k * D_MODEL:(k + 1) * D_MODEL],
                                      preferred_element_type=f32))
        t = gate * jnp.dot(branches[k].astype(bf16), wb_ref[k], preferred_element_type=f32)
        y = t if y is None else y + t
    out = jnp.dot(y.astype(bf16), wo_ref[...], preferred_element_type=f32)
    xn = x + gm_ref[0] * out
    h2 = _rms_mod(xn, g2_ref[...], sh2_ref[0], sc2_ref[0])
    if with_router:
        rw_ref, rb2_ref, xo_ref, h2_ref, lg_ref = refs[N_MERGE_IN:]
        lg_ref[...] = _dot3(*_split_bf16(h2), *_split_bf16(rw_ref[...])) + rb2_ref[...]
    else:
        xo_ref, h2_ref = refs[N_MERGE_IN:]
    xo_ref[...] = xn
    h2_ref[...] = h2.astype(bf16)


def merge(x, mods, g1, g2, o_f, o_b, p, s5y, hy, r_f, r_b, wg, wb, wo, wglu, bglu, hbias, router=None):
    m = x.shape[0]
    tm = MERGE_TILE
    n_mod = mods[0].shape[0]
    rows_per_mod = m // n_mod
    mod = pl.BlockSpec((1, 1, D_MODEL), lambda i: ((i * tm) // rows_per_mod, 0, 0))
    row = pl.BlockSpec((tm, D_MODEL), lambda i: (i, 0))
    br = pl.BlockSpec((tm, W_MIX), lambda i: (i, 0))
    pcol = lambda c: pl.BlockSpec((tm, W_MIX), lambda i: (i, c))
    full = lambda a: pl.BlockSpec(a.shape, lambda i: (0,) * a.ndim)
    head = jnp.arange(W_MIX) // GLA_DV
    havg = (head[:, None] == head[None, :]).astype(jnp.float32) / GLA_DV
    vec = lambda v: v.reshape(1, -1)
    consts = [wg, wb, wo, wglu, vec(bglu), havg, vec(hbias)]
    out_specs = [row, pl.BlockSpec((tm, D_MODEL), lambda i: (i, 0))]
    out_shape = [jax.ShapeDtypeStruct((m, D_MODEL), jnp.float32),
                 jax.ShapeDtypeStruct((m, D_MODEL), jnp.bfloat16)]
    if router is not None:
        rw, rbias = router
        pad = ROUTER_PAD - rw.shape[1]
        consts += [jnp.pad(rw, ((0, 0), (0, pad))), jnp.pad(rbias, (0, pad)).reshape(1, -1)]
        out_specs.append(pl.BlockSpec((tm, ROUTER_PAD), lambda i: (i, 0)))
        out_shape.append(jax.ShapeDtypeStruct((m, ROUTER_PAD), jnp.float32))
    in_specs = ([row] + [mod] * 5 + [full(vec(g1)), full(vec(g2)), br, br, pcol(P_OG), br, br, br, br, br, br,
                                     pcol(P_RGG)] + [full(a) for a in consts])
    return pl.pallas_call(
        functools.partial(_merge_kernel, with_router=router is not None),
        grid=(m // tm,),
        in_specs=in_specs,
        out_specs=out_specs,
        out_shape=out_shape,
        compiler_params=pltpu.CompilerParams(dimension_semantics=("arbitrary",),
                                             vmem_limit_bytes=VMEM_LIMIT_BYTES),
        name="merge",
    )(x, *mods, vec(g1), vec(g2), o_f, o_b, p, s5y, *hy, r_f, r_b, p, *consts)


def _combine_kernel(x_ref, y0_ref, y1_ref, w_ref, gm_ref, g_ref, o_ref, *, final_norm):
    w = w_ref[...]
    y = w[:, 0:1] * y0_ref[...] + w[:, 1:2] * y1_ref[...]
    xn = x_ref[...] + gm_ref[0] * y
    if final_norm:
        xn = xn * lax.rsqrt(jnp.mean(xn * xn, axis=-1, keepdims=True) + EPS) * g_ref[...]
    o_ref[...] = xn


def moe_combine(x, yk, w, gate_mod, final_g):
    m = x.shape[0]
    n_mod = gate_mod.shape[0]
    g = jnp.ones((1, D_MODEL), jnp.float32) if final_g is None else final_g.reshape(1, D_MODEL)
    return pl.pallas_call(
        functools.partial(_combine_kernel, final_norm=final_g is not None),
        grid=(m // ROW_TILE,),
        in_specs=[pl.BlockSpec((ROW_TILE, D_MODEL), lambda i: (i, 0)),
                  pl.BlockSpec((ROW_TILE, D_MODEL), lambda i: (i, 0)),
                  pl.BlockSpec((ROW_TILE, D_MODEL), lambda i: (i + m // ROW_TILE, 0)),
                  pl.BlockSpec((ROW_TILE, TOP_K), lambda i: (i, 0)),
                  _mod_spec(m // n_mod),
                  pl.BlockSpec((1, D_MODEL), lambda i: (0, 0))],
        out_specs=pl.BlockSpec((ROW_TILE, D_MODEL), lambda i: (i, 0)),
        out_shape=jax.ShapeDtypeStruct((m, D_MODEL), jnp.float32),
        compiler_params=pltpu.CompilerParams(dimension_semantics=("arbitrary",)),
        name="moe_combine",
    )(x, yk, yk, w, gate_mod, g)


HY_LANES = 128
HY_SLABS = 8
HY_MIN_LEN = 1024


def _hyena_pre_kernel(x_ref, prev_ref, next_ref, w_ref, zt_ref, z_ref, x0_ref, *, tb):
    k = pl.program_id(1)
    prev = jnp.where(k > 0, prev_ref[0], 0.0)
    nxt = jnp.where(k < pl.num_programs(1) - 1, next_ref[0], 0.0)
    ext = jnp.concatenate([prev, x_ref[0], nxt], axis=0)
    pc = sum(ext[RG_HALO - 1 + j:RG_HALO - 1 + j + tb, :] * w_ref[j:j + 1, :] for j in range(HY_SHORT))
    v, x0, x1 = pc[:, :W_MIX], pc[:, W_MIX:2 * W_MIX], pc[:, 2 * W_MIX:]
    z = x1 * v
    z_ref[0] = z
    x0_ref[0] = x0
    zt_ref[0] = z.T


def hyena_pre(p, w_short):
    nb, n_tok, _ = p.shape
    tb = min(ROW_TILE, n_tok)
    hb = tb // RG_HALO
    n_halo = n_tok // RG_HALO
    wide = 3 * W_MIX
    tok = pl.BlockSpec((1, tb, W_MIX), lambda b, k: (b, k, 0))
    tok_out = jax.ShapeDtypeStruct((nb, n_tok, W_MIX), jnp.float32)
    return pl.pallas_call(
        functools.partial(_hyena_pre_kernel, tb=tb),
        grid=(nb, n_tok // tb),
        in_specs=[pl.BlockSpec((1, tb, wide), lambda b, k: (b, k, P_HY)),
                  pl.BlockSpec((1, RG_HALO, wide), lambda b, k: (b, jnp.maximum(k * hb - 1, 0), P_HY)),
                  pl.BlockSpec((1, RG_HALO, wide), lambda b, k: (b, jnp.minimum((k + 1) * hb, n_halo - 1), P_HY)),
                  pl.BlockSpec(w_short.shape, lambda b, k: (0, 0))],
        out_specs=[pl.BlockSpec((1, W_MIX, tb), lambda b, k: (b, 0, k)), tok, tok],
        out_shape=[jax.ShapeDtypeStruct((nb, W_MIX, n_tok), jnp.float32), tok_out, tok_out],
        compiler_params=pltpu.CompilerParams(dimension_semantics=("arbitrary", "arbitrary"),
                                             vmem_limit_bytes=VMEM_LIMIT_BYTES),
        name="hyena_pre",
    )(p, p, p, w_short)


def _split_bf16(a):
    hi = a.astype(jnp.bfloat16)
    lo = (a - hi.astype(jnp.float32)).astype(jnp.bfloat16)
    return hi, lo


def _dot3(a_hi, a_lo, b_hi, b_lo):
    d = functools.partial(jnp.dot, preferred_element_type=jnp.float32)
    return d(a_hi, b_hi) + (d(a_lo, b_hi) + d(a_hi, b_lo))


def _hyena_dft_consts(n1):
    n = n1 * HY_LANES
    ka = np.arange(n1, dtype=np.float64)[:, None]
    f1_ang = 2.0 * np.pi * ka * np.arange(n1 // 2, dtype=np.float64)[None, :] / n1
    f1r, f1i = np.cos(f1_ang), -np.sin(f1_ang)
    tw_ang = 2.0 * np.pi * ka * np.arange(HY_LANES, dtype=np.float64)[None, :] / n
    lo = np.arange(HY_LANES, dtype=np.float64)
    f2_ang = 2.0 * np.pi * lo[:, None] * lo[None, :] / HY_LANES
    f2r, f2i = np.cos(f2_ang), -np.sin(f2_ang)
    fwd_rows = np.concatenate([f1r, f1i], axis=0)
    fwd_lanes = np.block([[f2r, f2i], [-f2i, f2r]])
    inv_lanes = np.block([[f2r, -f2i], [f2i, f2r]])
    inv_rows = np.concatenate([f1r.T, f1i.T], axis=1) / n
    out = []
    for m in (fwd_rows, fwd_lanes, inv_lanes, inv_rows):
        m32 = jnp.asarray(m, jnp.float32)
        out.extend(_split_bf16(m32))
    return out + [jnp.asarray(np.cos(tw_ang), jnp.float32), jnp.asarray(-np.sin(tw_ang), jnp.float32)]


def _hyena_fft_kernel(*refs, n1, ns, spectrum):
    bf16 = jnp.bfloat16
    if spectrum:
        z_ref, f1h, f1l, f2h, f2l, twr_ref, twi_ref, o_ref = refs
        rows_dft = lambda t: _dot3(f1h[...], f1l[...], *_split_bf16(t))
        lanes_dft = lambda t: _dot3(*_split_bf16(t), f2h[...], f2l[...])
    else:
        z_ref, hf_ref, f1h, f2h, g2h, fih, twr_ref, twi_ref, o_ref = refs
        mm = lambda a, b: jnp.dot(a.astype(bf16), b.astype(bf16), preferred_element_type=jnp.float32)
        rows_dft = lambda t: mm(f1h[...], t)
        lanes_dft = lambda t: mm(t, f2h[...])
    w = HY_LANES
    twr, twi = twr_ref[...], twi_ref[...]
    z2 = jnp.concatenate([z_ref[0, s] for s in range(ns)], axis=1)
    a2 = rows_dft(z2)
    rows = []
    for s in range(ns):
        r, i = a2[:n1, s * w:(s + 1) * w], a2[n1:, s * w:(s + 1) * w]
        rows.append(jnp.concatenate([r * twr - i * twi, r * twi + i * twr], axis=1))
    x = lanes_dft(jnp.concatenate(rows, axis=0))
    if spectrum:
        o_ref[...] = x.reshape(ns, n1, 2 * w)
        return
    h = hf_ref[...].reshape(ns * n1, 2 * w)
    xr, xi, hr, hi = x[:, :w], x[:, w:], h[:, :w], h[:, w:]
    y = jnp.concatenate([xr * hr - xi * hi, xr * hi + xi * hr], axis=1)
    g = mm(y, g2h[...])
    cr, ci = [], []
    for s in range(ns):
        gr, gi = g[s * n1:(s + 1) * n1, :w], g[s * n1:(s + 1) * n1, w:]
        cr.append(gr * twr + gi * twi)
        ci.append(gi * twr - gr * twi)
    gc = jnp.concatenate([jnp.concatenate(cr, axis=1), jnp.concatenate(ci, axis=1)], axis=0)
    y2 = mm(fih[...], gc)
    for s in range(ns):
        o_ref[0, s] = y2[:, s * w:(s + 1) * w]


def hyena_fft(zt, hf=None):
    nb, ch, half, w = zt.shape
    n1 = 2 * half
    ns = HY_SLABS
    f1h, f1l, f2h, f2l, g2h, _, fih, _, twr, twi = _hyena_dft_consts(n1)
    full = lambda a: pl.BlockSpec(a.shape, lambda b, c: (0,) * a.ndim)
    zspec = pl.BlockSpec((1, ns, half, w), lambda b, c: (b, c, 0, 0))
    if hf is None:
        consts = [f1h, f1l, f2h, f2l, twr, twi]
        in_specs, args = [zspec], [zt]
        out_spec = pl.BlockSpec((ns, n1, 2 * w), lambda b, c: (b * (ch // ns) + c, 0, 0))
        out_shape = jax.ShapeDtypeStruct((nb * ch, n1, 2 * w), jnp.float32)
    else:
        consts = [f1h, f2h, g2h, fih, twr, twi]
        in_specs = [zspec, pl.BlockSpec((ns, n1, 2 * w), lambda b, c: (c, 0, 0))]
        args = [zt, hf]
        out_spec = zspec
        out_shape = jax.ShapeDtypeStruct(zt.shape, jnp.float32)
    return pl.pallas_call(
        functools.partial(_hyena_fft_kernel, n1=n1, ns=ns, spectrum=hf is None),
        grid=(nb, ch // ns),
        in_specs=in_specs + [full(a) for a in consts],
        out_specs=out_spec,
        out_shape=out_shape,
        compiler_params=pltpu.CompilerParams(dimension_semantics=("arbitrary", "arbitrary"),
                                             vmem_limit_bytes=VMEM_LIMIT_BYTES),
        name="hyena_fft",
    )(*args, *consts)


def rmsnorm(x, g):
    y = x * lax.rsqrt(jnp.mean(x * x, axis=-1, keepdims=True) + EPS)
    return y * g


def adaln(cond, w, b):
    return jax.nn.silu(cond) @ w + b


def grid_pos_embed(n_tokens, dim):
    rows = n_tokens // GRID_W
    q = dim // 4
    omega = 1.0 / (10000.0 ** (jnp.arange(q, dtype=jnp.float32) / q))
    r = jnp.arange(rows, dtype=jnp.float32)[:, None] * omega
    cc = jnp.arange(GRID_W, dtype=jnp.float32)[:, None] * omega
    er = jnp.concatenate([jnp.sin(r), jnp.cos(r)], axis=-1)
    ec = jnp.concatenate([jnp.sin(cc), jnp.cos(cc)], axis=-1)
    emb = jnp.concatenate([jnp.broadcast_to(er[:, None], (rows, GRID_W, dim // 2)),
                           jnp.broadcast_to(ec[None], (rows, GRID_W, dim // 2))], axis=-1)
    return emb.reshape(rows * GRID_W, dim)


def gla_mixer(p, w_up, b_up, s0):
    gdn = p[..., P_GDN:P_GDN + 2 * GLA_RANK]
    outs, finals = [], []
    for d in range(2):
        od, sd = gla_dir(p, gdn[..., d * GLA_RANK:(d + 1) * GLA_RANK], w_up[d], b_up[d], s0[d], d == 1)
        outs.append(od)
        finals.append(sd)
    return outs, jnp.stack(finals)


def s5_mixer(u, prep, s0):
    wd, wk, wc, a_t = prep
    b_, n_tok, _ = u.shape
    n = n_tok // S5_T
    u2 = u.reshape(b_, n, S5_T * W_MIX).transpose(1, 0, 2).reshape(n * b_, S5_T * W_MIX)
    dmat = pmm(u2, wd)
    hmat, fin = s5_scan(dmat.reshape(n, b_, 4 * S5_NS), s0, a_t)
    y2 = pmm_multi([u2, hmat.reshape(n * b_, 4 * S5_NS)], [wk, wc], (False, True))
    y = y2.reshape(n, b_, S5_T, W_MIX).transpose(1, 0, 2, 3).reshape(b_, n_tok, W_MIX)
    return y, fin


def hyena_filters(n_tok, w1, b1, w2, b2, w3, freq):
    f32 = jnp.float32
    t = jnp.arange(n_tok, dtype=f32)[:, None]
    bands = jnp.linspace(1e-4, HY_BANDS - 1, HY_BANDS, dtype=f32)[None]
    ang = 2.0 * math.pi * bands * t / n_tok
    z = jnp.concatenate([t / n_tok, jnp.cos(ang), jnp.sin(ang)], axis=-1)
    hp = lax.Precision.HIGHEST
    h = jnp.sin(freq * (jnp.dot(z, w1, precision=hp) + b1))
    h = jnp.sin(freq * (jnp.dot(h, w2, precision=hp) + b2))
    h = jnp.dot(h, w3, precision=hp)
    t01 = t / max(n_tok - 1, 1)
    deltas = jnp.abs(jnp.linspace(math.log(HY_TARGET) / HY_DECAY_SHORT,
                                  math.log(HY_TARGET) / HY_DECAY_LONG, W_MIX, dtype=f32))
    h = h * jnp.exp(-t01 * jnp.tile(deltas, 2))
    return h / (jnp.sum(jnp.abs(h), axis=0, keepdims=True) + EPS)


def hyena_mixer(p, w_short, w1, b1, w2, b2, w3, freq):
    nb, n_tok, _ = p.shape
    zt, z, x0 = hyena_pre(p, w_short)
    n_pad = max(n_tok, HY_MIN_LEN)
    half = n_pad // HY_LANES

    def frames(t):
        t = jnp.pad(t, [(0, 0)] * (t.ndim - 1) + [(0, n_pad - n_tok)])
        return t.reshape(t.shape[:-1] + (half, HY_LANES))

    filt = hyena_filters(n_tok, w1, b1, w2, b2, w3, freq)
    spec = hyena_fft(frames(filt.T)[None])
    sf, sb = spec[:W_MIX], spec[W_MIX:]
    hfreq = jnp.concatenate([sf[..., :HY_LANES] + sb[..., :HY_LANES],
                             sf[..., HY_LANES:] - sb[..., HY_LANES:]], axis=-1)
    conv = hyena_fft(frames(zt), hfreq)
    conv = conv.reshape(nb, W_MIX, n_pad)[:, :, :n_tok].transpose(0, 2, 1)
    return conv, z, x0


def rglru_mixer(p, w_conv, w_a, b_a, w_x, b_x, lam, s0):
    a_f, b_f, a_b, b_b = rg_pre(p, w_conv, w_a, b_a, w_x, b_x, lam)
    h_f, fin_f = rg_scan(a_f, b_f, s0[0], False)
    h_b, fin_b = rg_scan(a_b, b_b, s0[1], True)
    return [h_f, h_b], jnp.stack([fin_f, fin_b])


def ffn_dense(h2, x, gate_mod, w_gu, w_down):
    n_tiles = h2.shape[0] // FFN_TM
    return grouped_swiglu(jnp.zeros((n_tiles,), jnp.int32), jnp.full((1,), n_tiles, jnp.int32), h2,
                          w_gu[None].astype(jnp.bfloat16), w_down[None].astype(jnp.bfloat16), nf=2,
                          residual=(x, gate_mod))


def ffn_moe(h2, logits, x, gate_mod, w_gu, w_down, final_g):
    n_tok = h2.shape[0]
    n_slot = TOP_K * n_tok
    top_v, top_i = lax.top_k(logits, TOP_K)
    w = jax.nn.softmax(top_v, axis=-1)
    e_flat = top_i.T.reshape(-1).astype(jnp.int32)
    onehot = (e_flat[:, None] == jnp.arange(N_EXPERTS, dtype=jnp.int32)[None]).astype(jnp.int32)
    csum = jnp.cumsum(onehot, axis=0)
    cnt = csum[-1]
    rank = jnp.sum(csum * onehot, axis=1) - 1
    padded = ((cnt + FFN_TM - 1) // FFN_TM) * FFN_TM
    ends = jnp.cumsum(padded)
    dest = (ends - padded)[e_flat] + rank
    n_rows = n_slot + N_EXPERTS * FFN_TM
    n_tiles = n_rows // FFN_TM
    src = jnp.zeros((n_rows,), jnp.int32).at[dest].set(jnp.arange(n_slot, dtype=jnp.int32) % n_tok)
    tile_start = jnp.arange(n_tiles, dtype=jnp.int32) * FFN_TM
    tile_expert = jnp.minimum(jnp.sum((tile_start[:, None] >= ends[None, :]).astype(jnp.int32), axis=1),
                              N_EXPERTS - 1)
    n_used = (ends[-1:] // FFN_TM).astype(jnp.int32)
    xs = h2.at[src].get(mode="promise_in_bounds")
    ys = grouped_swiglu(tile_expert, n_used, xs, w_gu.astype(jnp.bfloat16),
                        w_down.astype(jnp.bfloat16), nf=2)
    yk = ys.at[dest].get(mode="promise_in_bounds")
    return moe_combine(x, yk, w, gate_mod, final_g)


def kernel(x, c, ctx, c_ctx, mod_w, mod_b, norm1_g, norm2_g, w_in, gla_w_up, gla_b_up,
           s5_lam_re, s5_lam_im, s5_log_dt, s5_b_re, s5_b_im, s5_c_re, s5_c_im, s5_d,
           s5_w_glu, s5_b_glu, hy_w_short, hy_w1, hy_b1, hy_w2, hy_b2, hy_w3, hy_freq,
           hy_bias, rg_w_conv, rg_w_a, rg_b_a, rg_w_x, rg_b_x, rg_lam, w_branch, w_out,
           ffn_w_gu, ffn_w_down, moe_router, moe_router_b, moe_w_gu, moe_w_down, final_g):
    f32, bf16 = jnp.float32, jnp.bfloat16
    n_b, n_lat, _ = x.shape
    n_ctx = ctx.shape[1]
    xs = (x + grid_pos_embed(n_lat, D_MODEL)[None]).reshape(n_b * n_lat, D_MODEL)
    cs = ctx.reshape(n_b * n_ctx, D_MODEL)
    offs = [0]
    for n in IN_SIZES:
        offs.append(offs[-1] + n)
    for l in range(DEPTH):
        last = l == DEPTH - 1
        dense = l % 2 == 0
        j = l // 2
        m_lat = [t[:, None, :] for t in jnp.split(adaln(c, mod_w[l], mod_b[l]), 6, axis=-1)]
        m_ctx = [t[None, None, :] for t in jnp.split(adaln(c_ctx, mod_w[l], mod_b[l]), 6, axis=-1)]
        wl = w_in[l]
        w_mix = jnp.concatenate([wl[:, offs[6]:offs[7]], wl[:, offs[0]:offs[4]], wl[:, offs[5]:offs[6]],
                                 wl[:, offs[7]:offs[9]], wl[:, offs[4]:offs[5]],
                                 jnp.zeros((D_MODEL, P_COLS - P_GDN - 2 * GLA_RANK), f32)], axis=1).astype(bf16)
        w_gate = wl[:, offs[9]:offs[10]].astype(bf16)
        wb, wo, wglu = w_branch[l].astype(bf16), w_out[l].astype(bf16), s5_w_glu[l].astype(bf16)
        s5_ops = s5_prepare(s5_lam_re[l], s5_lam_im[l], s5_log_dt[l], s5_b_re[l], s5_b_im[l],
                            s5_c_re[l], s5_c_im[l], s5_d[l])
        hy_p = (hy_w_short[l], hy_w1[l], hy_b1[l], hy_w2[l], hy_b2[l], hy_w3[l], hy_freq[l])
        rg_p = (rg_w_conv[l], rg_w_a[l], rg_b_a[l], rg_w_x[l], rg_b_x[l], rg_lam[l])
        router = None if dense else (moe_router[j], moe_router_b[j])

        def mixers(p2, n_tok, states, with_hyena):
            p = p2.reshape(n_b, n_tok, P_COLS)
            blk = lambda i, n=1: p[..., i * W_MIX:(i + n) * W_MIX]
            flat = lambda t: t.reshape(n_b * n_tok, W_MIX)
            gla_o, gla_s = gla_mixer(p, gla_w_up[l], gla_b_up[l], states[0])
            s5_y, s5_s = s5_mixer(blk(P_S5), s5_ops, states[1])
            rg_o, rg_s = rglru_mixer(p, *rg_p, states[2])
            hy = tuple(flat(t) for t in hyena_mixer(p, *hy_p)) if with_hyena else None
            return ((flat(gla_o[0]), flat(gla_o[1]), flat(s5_y), hy, flat(rg_o[0]), flat(rg_o[1])),
                    (gla_s, s5_s, rg_s))

        def tail(stream, mods, p2, br, final):
            outs = merge(stream, tuple(mods[:5]), norm1_g[l], norm2_g[l], br[0], br[1], p2, br[2], br[3],
                         br[4], br[5], w_gate, wb, wo, wglu, s5_b_glu[l], hy_bias[l], router)
            if dense:
                return ffn_dense(outs[1], outs[0], mods[5], ffn_w_gu[j], ffn_w_down[j])
            return ffn_moe(outs[1], outs[2][:, :N_EXPERTS], outs[0], mods[5], moe_w_gu[j], moe_w_down[j],
                           final_g if final else None)

        zero_states = (jnp.zeros((2, n_b, GLA_HEADS, GLA_DV, GLA_DK), f32),
                       jnp.zeros((2, n_b, 2 * S5_NS), f32), jnp.zeros((2, n_b, W_MIX), f32))
        p_ctx = front(cs, m_ctx[0], m_ctx[1], norm1_g[l], w_mix)
        br_ctx, states = mixers(p_ctx, n_ctx, zero_states, not last)
        p_lat = front(xs, m_lat[0], m_lat[1], norm1_g[l], w_mix)
        br_lat, _ = mixers(p_lat, n_lat, states, True)
        xs = tail(xs, m_lat, p_lat, br_lat, last)
        if not last:
            cs = tail(cs, m_ctx, p_ctx, br_ctx, False)
    if (DEPTH - 1) % 2 == 0:
        xs = rmsnorm(xs, final_g)
    return xs.reshape(n_b, n_lat, D_MODEL)
```

```python
import functools
import math

import jax
import jax.numpy as jnp
import numpy as np
from jax import lax
from jax.experimental import pallas as pl
from jax.experimental.pallas import tpu as pltpu

D_MODEL = 1024
DEPTH = 2
GRID_W = 64
EPS = 1e-6
N_BRANCH = 4
W_MIX = D_MODEL // N_BRANCH
GLA_HEADS = 4
GLA_DK = W_MIX // GLA_HEADS
GLA_DV = W_MIX // GLA_HEADS
GLA_RANK = 16
GLA_TAU = 16.0
GLA_CHUNK = 64
S5_GROUP = 16
S5_GROUPS = W_MIX // S5_GROUP
S5_STATE = 64
S5_MAX_RE = -1e-4
HY_BANDS = 16
HY_SHORT = 3
HY_DECAY_SHORT = 0.3
HY_DECAY_LONG = 1.5
HY_TARGET = 1e-2
RG_BLOCKS = 4
RG_BLOCK = W_MIX // RG_BLOCKS
RG_CONV = 4
RG_C = 8.0
N_EXPERTS = 8
TOP_K = 2
IN_SIZES = (GLA_HEADS * GLA_DK, GLA_HEADS * GLA_DK, GLA_HEADS * GLA_DV, GLA_HEADS * GLA_DV,
            2 * GLA_RANK, W_MIX, 3 * W_MIX, W_MIX, W_MIX, N_BRANCH * D_MODEL)

VMEM_LIMIT_BYTES = 48 * 1024 * 1024


def _mm_kernel(x_ref, w_ref, o_ref):
    o_ref[...] = jnp.dot(x_ref[...].astype(jnp.bfloat16), w_ref[...],
                         preferred_element_type=jnp.float32)


def _pick_tile(n, cap):
    best = None
    for t in range(128, cap + 1, 128):
        if n % t == 0:
            best = t
    return best if best is not None else n


def pmm(x, w):
    lead = x.shape[:-1]
    k = x.shape[-1]
    n = w.shape[-1]
    x2 = x.reshape(-1, k)
    m = x2.shape[0]
    tm = 512 if m % 512 == 0 else m
    if k > 2048 and m % 256 == 0:
        tm = 256
    tn = n if k * n * 2 <= 6 * 1024 * 1024 else _pick_tile(n, 1024)
    out = pl.pallas_call(
        _mm_kernel,
        grid=(m // tm, n // tn),
        in_specs=[pl.BlockSpec((tm, k), lambda i, j: (i, 0)),
                  pl.BlockSpec((k, tn), lambda i, j: (0, j))],
        out_specs=pl.BlockSpec((tm, tn), lambda i, j: (i, j)),
        out_shape=jax.ShapeDtypeStruct((m, n), jnp.float32),
        compiler_params=pltpu.CompilerParams(
            dimension_semantics=("arbitrary", "arbitrary"),
            vmem_limit_bytes=VMEM_LIMIT_BYTES),
    )(x2, w.astype(jnp.bfloat16))
    return out.reshape(lead + (n,))


def _mm_multi_kernel(*refs, transposed):
    o_ref = refs[-1]
    n = (len(refs) - 1) // 2
    acc = None
    for i in range(n):
        dims = (((1,), (1 if transposed[i] else 0,)), ((), ()))
        t = lax.dot_general(refs[i][...].astype(jnp.bfloat16), refs[n + i][...], dims,
                            preferred_element_type=jnp.float32)
        acc = t if acc is None else acc + t
    o_ref[...] = acc


def pmm_multi(xs, ws, transposed, tm=256, tn=512):
    m = xs[0].shape[0]
    n = ws[0].shape[0 if transposed[0] else 1]
    tm = tm if m % tm == 0 else m
    tn = tn if n % tn == 0 else n
    in_specs = ([pl.BlockSpec((tm, x.shape[1]), lambda i, j: (i, 0)) for x in xs]
                + [pl.BlockSpec((tn, w.shape[1]), lambda i, j: (j, 0)) if t else
                   pl.BlockSpec((w.shape[0], tn), lambda i, j: (0, j)) for w, t in zip(ws, transposed)])
    return pl.pallas_call(
        functools.partial(_mm_multi_kernel, transposed=tuple(transposed)),
        grid=(m // tm, n // tn),
        in_specs=in_specs,
        out_specs=pl.BlockSpec((tm, tn), lambda i, j: (i, j)),
        out_shape=jax.ShapeDtypeStruct((m, n), jnp.float32),
        compiler_params=pltpu.CompilerParams(
            dimension_semantics=("arbitrary", "arbitrary"),
            vmem_limit_bytes=VMEM_LIMIT_BYTES),
    )(*xs, *[w.astype(jnp.bfloat16) for w in ws])


RG_SCAN_ROWS = 256


def _rg_scan_kernel(a_ref, b_ref, s0_ref, h_ref, fin_ref, st_ref, *, reverse, tb, nb):
    @pl.when(pl.program_id(0) == 0)
    def _():
        st_ref[...] = s0_ref[...]

    def body(r, hs):
        rr = (tb - 1 - r) if reverse else r
        out = []
        for i in range(nb):
            h = a_ref[i, pl.ds(rr, 1), :] * hs[i] + b_ref[i, pl.ds(rr, 1), :]
            h_ref[i, pl.ds(rr, 1), :] = h
            out.append(h)
        return tuple(out)

    hs = lax.fori_loop(0, tb, body, tuple(st_ref[i:i + 1, :] for i in range(nb)), unroll=8)
    for i in range(nb):
        st_ref[i:i + 1, :] = hs[i]
        fin_ref[i:i + 1, :] = hs[i]


def rg_scan(a, b, s0, reverse):
    nb, n_tok, ch = a.shape
    tb = min(RG_SCAN_ROWS, n_tok)
    nblk = n_tok // tb
    imap = (lambda k: (0, nblk - 1 - k, 0)) if reverse else (lambda k: (0, k, 0))
    return pl.pallas_call(
        functools.partial(_rg_scan_kernel, reverse=reverse, tb=tb, nb=nb),
        grid=(nblk,),
        in_specs=[pl.BlockSpec((nb, tb, ch), imap), pl.BlockSpec((nb, tb, ch), imap),
                  pl.BlockSpec((nb, ch), lambda k: (0, 0))],
        out_specs=[pl.BlockSpec((nb, tb, ch), imap), pl.BlockSpec((nb, ch), lambda k: (0, 0))],
        out_shape=[jax.ShapeDtypeStruct((nb, n_tok, ch), jnp.float32),
                   jax.ShapeDtypeStruct((nb, ch), jnp.float32)],
        scratch_shapes=[pltpu.VMEM((nb, ch), jnp.float32)],
        compiler_params=pltpu.CompilerParams(dimension_semantics=("arbitrary",)),
        name="rg_scan",
    )(a, b, s0)


RG_HALO = 8


def _rg_pre_kernel(x_ref, prev_ref, next_ref, wc_ref, wg_ref, bg_ref, c_ref, af_ref, bf_ref, ab_ref, bb_ref,
                   *, tb):
    k = pl.program_id(1)
    prev = jnp.where(k > 0, prev_ref[0], 0.0)
    nxt = jnp.where(k < pl.num_programs(1) - 1, next_ref[0], 0.0)
    ext = jnp.concatenate([prev, x_ref[0], nxt], axis=0)
    xc = sum(ext[RG_HALO - 2 + j:RG_HALO - 2 + j + tb, :] * wc_ref[j:j + 1, :] for j in range(RG_CONV))
    m = jnp.dot(xc.astype(jnp.bfloat16), wg_ref[...], preferred_element_type=jnp.float32) + bg_ref[...]
    for d, (a_ref, b_ref) in enumerate(((af_ref, bf_ref), (ab_ref, bb_ref))):
        r = jax.nn.sigmoid(m[:, (2 * d) * W_MIX:(2 * d + 1) * W_MIX])
        i = jax.nn.sigmoid(m[:, (2 * d + 1) * W_MIX:(2 * d + 2) * W_MIX])
        log_a = -r * c_ref[d:d + 1, :]
        a_ref[0] = jnp.exp(log_a)
        u = jnp.tanh(log_a)
        b_ref[0] = jnp.sqrt(-2.0 * u / (1.0 - u)) * (i * xc)


def rg_pre(p, w_conv, w_a, b_a, w_x, b_x, lam):
    nb, n_tok, _ = p.shape
    tb = min(ROW_TILE, n_tok)
    nblk = n_tok // tb
    hb = tb // RG_HALO
    n_halo = n_tok // RG_HALO
    col = P_RGX
    blockdiag = lambda w: jax.scipy.linalg.block_diag(*[w[i] for i in range(RG_BLOCKS)])
    wg = jnp.concatenate([blockdiag(w_a[0]), blockdiag(w_x[0]), blockdiag(w_a[1]), blockdiag(w_x[1])],
                         axis=1).astype(jnp.bfloat16)
    bg = jnp.concatenate([b_a[0].reshape(-1), b_x[0].reshape(-1), b_a[1].reshape(-1), b_x[1].reshape(-1)]
                         ).reshape(1, 4 * W_MIX)
    c = RG_C * jax.nn.softplus(-lam)
    blk = pl.BlockSpec((1, tb, W_MIX), lambda b, k: (b, k, 0))
    full = lambda a: pl.BlockSpec(a.shape, lambda b, k: (0,) * a.ndim)
    out = jax.ShapeDtypeStruct((nb, n_tok, W_MIX), jnp.float32)
    return pl.pallas_call(
        functools.partial(_rg_pre_kernel, tb=tb),
        grid=(nb, nblk),
        in_specs=[pl.BlockSpec((1, tb, W_MIX), lambda b, k: (b, k, col)),
                  pl.BlockSpec((1, RG_HALO, W_MIX), lambda b, k: (b, jnp.maximum(k * hb - 1, 0), col)),
                  pl.BlockSpec((1, RG_HALO, W_MIX), lambda b, k: (b, jnp.minimum((k + 1) * hb, n_halo - 1), col)),
                  full(w_conv), full(wg), full(bg), full(c)],
        out_specs=[blk, blk, blk, blk],
        out_shape=[out, out, out, out],
        compiler_params=pltpu.CompilerParams(dimension_semantics=("arbitrary", "arbitrary")),
        name="rg_pre",
    )(p, p, p, w_conv, wg, bg, c)


S5_T = 16
S5_NS = S5_GROUPS * S5_STATE
S5_SCAN_CHUNKS = 64


def _s5_scan_kernel(d_ref, s0_ref, a_ref, h_ref, fin_ref, st_ref, *, rc, nb):
    d = pl.program_id(0)

    @pl.when(pl.program_id(1) == 0)
    def _():
        st_ref[...] = s0_ref[0]

    ar = jnp.broadcast_to(a_ref[0, :, 0:S5_NS], (nb, S5_NS))
    ai = jnp.broadcast_to(a_ref[0, :, S5_NS:2 * S5_NS], (nb, S5_NS))

    def body(r, carry):
        hr, hi = carry
        rr = r + d * (rc - 1 - 2 * r)
        h_ref[rr, :, 0:S5_NS] = hr
        h_ref[rr, :, S5_NS:2 * S5_NS] = hi
        dr = d_ref[rr, :, 0:S5_NS]
        di = d_ref[rr, :, S5_NS:2 * S5_NS]
        return ar * hr - ai * hi + dr, ar * hi + ai * hr + di

    hr, hi = lax.fori_loop(0, rc, body, (st_ref[:, 0:S5_NS], st_ref[:, S5_NS:2 * S5_NS]))
    st_ref[:, 0:S5_NS] = hr
    st_ref[:, S5_NS:2 * S5_NS] = hi
    fin_ref[0, :, 0:S5_NS] = hr
    fin_ref[0, :, S5_NS:2 * S5_NS] = hi


def s5_scan(dmat, s0, a_t):
    n, nb, _ = dmat.shape
    rc = min(S5_SCAN_CHUNKS, n)
    nblk = n // rc
    w = 2 * S5_NS
    imap = lambda d, k: (k + d * (nblk - 1 - 2 * k), 0, d)
    return pl.pallas_call(
        functools.partial(_s5_scan_kernel, rc=rc, nb=nb),
        grid=(2, nblk),
        in_specs=[pl.BlockSpec((rc, nb, w), imap),
                  pl.BlockSpec((1, nb, w), lambda d, k: (d, 0, 0)),
                  pl.BlockSpec((1, 1, w), lambda d, k: (d, 0, 0))],
        out_specs=[pl.BlockSpec((rc, nb, w), imap),
                   pl.BlockSpec((1, nb, w), lambda d, k: (d, 0, 0))],
        out_shape=[jax.ShapeDtypeStruct((n, nb, 2 * w), jnp.float32),
                   jax.ShapeDtypeStruct((2, nb, w), jnp.float32)],
        scratch_shapes=[pltpu.VMEM((nb, w), jnp.float32)],
        compiler_params=pltpu.CompilerParams(dimension_semantics=("arbitrary", "arbitrary"),
                                             vmem_limit_bytes=VMEM_LIMIT_BYTES),
    )(dmat, s0, a_t)


def _cmul(ar, ai, br, bi):
    return ar * br - ai * bi, ar * bi + ai * br


def s5_prepare(lam_re, lam_im, log_dt, b_re, b_im, c_re, c_im, d_skip):
    f32 = jnp.float32
    hp = lax.Precision.HIGHEST
    t_len, g_n, p_n, h_n = S5_T, S5_GROUPS, S5_STATE, S5_GROUP
    bf16 = jnp.bfloat16
    eye_g = jnp.eye(g_n, dtype=f32)
    mask_gp = jnp.repeat(eye_g, p_n, axis=1).astype(bf16)[None, :, None, :]
    ar_t = jnp.arange(t_len)
    wd, wc, kk, a_t = [], [], [], []
    for d in range(2):
        lr = jnp.minimum(lam_re[d], S5_MAX_RE)
        li = lam_im[d]
        dt = jnp.exp(log_dt[d])[:, None]
        tt = jnp.arange(t_len + 1, dtype=f32)[:, None, None]
        mag = jnp.exp(lr * dt * tt)
        ang = li * dt * tt
        pr, pi = mag * jnp.cos(ang), mag * jnp.sin(ang)
        nr, ni = pr[1] - 1.0, pi[1]
        den = lr * lr + li * li
        qr, qi = (nr * lr + ni * li) / den, (ni * lr - nr * li) / den
        bbr, bbi = _cmul(qr[..., None], qi[..., None], b_re[d], b_im[d])
        cr, ci = c_re[d], c_im[d]

        idx = (t_len - 1 - ar_t) if d == 0 else ar_t
        wr, wi = _cmul(pr[idx][..., None], pi[idx][..., None], bbr[None], bbi[None])

        def place_d(w):
            wt = w.transpose(0, 3, 1, 2).reshape(t_len, 1, h_n, g_n * p_n).astype(bf16)
            return (wt * mask_gp).reshape(t_len * W_MIX, g_n * p_n)

        wd.append(jnp.concatenate([place_d(wr), place_d(wi)], axis=1))

        idx2 = (ar_t + 1) if d == 0 else (t_len - ar_t)
        cwr, cwi = _cmul(cr[None], ci[None], pr[idx2][:, :, None, :], pi[idx2][:, :, None, :])

        def place_c(w):
            wt = w.transpose(0, 2, 1, 3).reshape(t_len, 1, h_n, g_n * p_n).astype(bf16)
            return (wt * mask_gp).reshape(t_len * W_MIX, g_n * p_n)

        wc.append(jnp.concatenate([place_c(cwr), place_c(-cwi)], axis=1))

        er, ei = _cmul(pr[:t_len][:, :, None, :], pi[:t_len][:, :, None, :], cr[None], ci[None])
        kk.append(jnp.einsum('tghp,gpk->tghk', er, bbr, precision=hp)
                  - jnp.einsum('tghp,gpk->tghk', ei, bbi, precision=hp))
        a_t.append(jnp.concatenate([pr[t_len].reshape(1, -1), pi[t_len].reshape(1, -1)], axis=1))

    lag = ar_t[None, :] - ar_t[:, None]
    mf = jnp.where((lag >= 0)[..., None, None, None], kk[0][jnp.clip(lag, 0, t_len - 1)], 0.0)
    mb = jnp.where((lag <= 0)[..., None, None, None], kk[1][jnp.clip(-lag, 0, t_len - 1)], 0.0)
    skip = (jnp.eye(t_len, dtype=f32)[:, :, None, None, None] * d_skip[None, None, :, :, None]
            * jnp.eye(h_n, dtype=f32)[None, None, None])
    m = mf + mb + skip
    mt = m.transpose(0, 4, 1, 2, 3).reshape(t_len, 1, h_n, t_len * W_MIX).astype(bf16)
    mask_igh = jnp.tile(jnp.repeat(eye_g, h_n, axis=1), (1, t_len)).astype(bf16)[None, :, None, :]
    wk = (mt * mask_igh).reshape(t_len * W_MIX, t_len * W_MIX)
    return (jnp.concatenate(wd, axis=1), wk, jnp.concatenate(wc, axis=1), jnp.stack(a_t))


GLA_BLOCK = 1024


def _gla_kernel(q_ref, k_ref, v_ref, g_ref, wup_ref, bup_ref, s0_ref, o_ref, fin_ref, st_ref,
                *, reverse, tb):
    f32, bf16 = jnp.float32, jnp.bfloat16
    hp = lax.Precision.HIGHEST
    cc = GLA_CHUNK

    @pl.when(pl.program_id(1) == 0)
    def _():
        st_ref[...] = s0_ref[0]

    r_i = lax.broadcasted_iota(jnp.int32, (cc, cc), 0)
    c_i = lax.broadcasted_iota(jnp.int32, (cc, cc), 1)
    keep = (c_i >= r_i) if reverse else (c_i <= r_i)
    tri = keep.astype(f32)
    nt = (((1,), (1,)), ((), ()))
    tn = (((0,), (0,)), ((), ()))
    n_ch = tb // cc
    chunks = range(n_ch)
    heads = range(GLA_HEADS)
    hsl = [slice(h * GLA_DK, (h + 1) * GLA_DK) for h in heads]
    rows = [slice(c * cc, (c + 1) * cc) for c in chunks]

    z = jnp.dot(g_ref[0], wup_ref[...], precision=hp, preferred_element_type=f32) + bup_ref[...]
    la = (jnp.minimum(z, 0.0) - jnp.log1p(jnp.exp(-jnp.abs(z)))) * (1.0 / GLA_TAU)
    cum = [jnp.dot(tri, la[rows[c]], precision=hp, preferred_element_type=f32) for c in chunks]
    last = [cm[0:1, :] if reverse else cm[cc - 1:cc, :] for cm in cum]
    k = [k_ref[0, rows[c], :] for c in chunks]
    q_in = [(q_ref[0, rows[c], :] * (GLA_DK ** -0.5) * jnp.exp(cum[c])).astype(bf16) for c in chunks]
    k_in = [(k[c] * jnp.exp(-cum[c])).astype(bf16) for c in chunks]
    k_out = [(k[c] * jnp.exp(last[c] - cum[c])).astype(bf16) for c in chunks]
    dec = [jnp.exp(last[c]) for c in chunks]
    vb = [v_ref[0, rows[c], :].astype(bf16) for c in chunks]
    att = [[jnp.where(keep, lax.dot_general(q_in[c][:, s], k_in[c][:, s], nt, preferred_element_type=f32),
                      0.0).astype(bf16) for s in hsl] for c in chunks]
    o_intra = [[jnp.dot(att[c][h], vb[c][:, hsl[h]], preferred_element_type=f32) for h in heads]
               for c in chunks]
    d_state = [[lax.dot_general(vb[c][:, s], k_out[c][:, s], tn, preferred_element_type=f32) for s in hsl]
               for c in chunks]

    st = [st_ref[h] for h in heads]
    st_in = [None] * n_ch
    for c in (reversed(chunks) if reverse else chunks):
        st_in[c] = [s.astype(bf16) for s in st]
        st = [st[h] * dec[c][:, hsl[h]] + d_state[c][h] for h in heads]
    for h in heads:
        st_ref[h] = st[h]
        fin_ref[0, h] = st[h]

    for c in chunks:
        o_ref[0, rows[c], :] = jnp.concatenate(
            [o_intra[c][h] + lax.dot_general(q_in[c][:, hsl[h]], st_in[c][h], nt, preferred_element_type=f32)
             for h in heads], axis=1)


def gla_dir(p, gd, w_up, b_up, s0, reverse):
    nb, n_tok, _ = p.shape
    tb = min(GLA_BLOCK, n_tok)
    nblk = n_tok // tb
    blk = (lambda k: nblk - 1 - k) if reverse else (lambda k: k)
    col = lambda c: pl.BlockSpec((1, tb, W_MIX), lambda b, k: (b, blk(k), c))
    st_shape = (GLA_HEADS, GLA_DV, GLA_DK)
    return pl.pallas_call(
        functools.partial(_gla_kernel, reverse=reverse, tb=tb),
        grid=(nb, nblk),
        in_specs=[col(P_Q), col(P_Q + 1), col(P_Q + 2),
                  pl.BlockSpec((1, tb, GLA_RANK), lambda b, k: (b, blk(k), 0)),
                  pl.BlockSpec((GLA_RANK, W_MIX), lambda b, k: (0, 0)),
                  pl.BlockSpec((1, W_MIX), lambda b, k: (0, 0)),
                  pl.BlockSpec((1,) + st_shape, lambda b, k: (b, 0, 0, 0))],
        out_specs=[pl.BlockSpec((1, tb, W_MIX), lambda b, k: (b, blk(k), 0)),
                   pl.BlockSpec((1,) + st_shape, lambda b, k: (b, 0, 0, 0))],
        out_shape=[jax.ShapeDtypeStruct((nb, n_tok, W_MIX), jnp.float32),
                   jax.ShapeDtypeStruct((nb,) + st_shape, jnp.float32)],
        scratch_shapes=[pltpu.VMEM(st_shape, jnp.float32)],
        compiler_params=pltpu.CompilerParams(dimension_semantics=("arbitrary", "arbitrary"),
                                             vmem_limit_bytes=VMEM_LIMIT_BYTES),
        name="gla",
    )(p, p, p, gd, w_up, b_up.reshape(1, W_MIX), s0)


FFN_TM = 512
CAST_ROWS = 256


def _swiglu_kernel(te_ref, nu_ref, x_ref, wg_ref, wu_ref, wd_ref, *rest, nf, residual):
    del te_ref
    o_ref = rest[-1]
    j = pl.program_id(1)

    @pl.when(pl.program_id(0) < nu_ref[0])
    def _():
        x = x_ref[...]
        g = jnp.dot(x, wg_ref[0], preferred_element_type=jnp.float32)
        u = jnp.dot(x, wu_ref[0], preferred_element_type=jnp.float32)
        a = (g * jax.nn.sigmoid(g) * u).astype(jnp.bfloat16)
        part = jnp.dot(a, wd_ref[0], preferred_element_type=jnp.float32)

        @pl.when(j == 0)
        def _():
            o_ref[...] = part

        @pl.when(j > 0)
        def _():
            o_ref[...] += part

        if residual:
            xres_ref, gm_ref = rest[:2]

            @pl.when(j == nf - 1)
            def _():
                o_ref[...] = xres_ref[...] + gm_ref[0] * o_ref[...]


def _cast_split_kernel(a_ref, b_ref, oa_ref, ob_ref):
    oa_ref[...] = a_ref[...].astype(oa_ref.dtype)
    ob_ref[...] = b_ref[...].astype(ob_ref.dtype)


def cast_split_bf16(w):
    e, r, c2 = w.shape
    c = c2 // 2
    rows = min(CAST_ROWS, r)
    half = lambda h: pl.BlockSpec((1, rows, c), lambda i, j: (i, j, h))
    out = jax.ShapeDtypeStruct((e, r, c), jnp.bfloat16)
    return pl.pallas_call(
        _cast_split_kernel,
        grid=(e, r // rows),
        in_specs=[half(0), half(1)],
        out_specs=[half(0), half(0)],
        out_shape=[out, out],
        compiler_params=pltpu.CompilerParams(dimension_semantics=("arbitrary", "arbitrary"),
                                             vmem_limit_bytes=VMEM_LIMIT_BYTES),
        name="cast_split_bf16",
    )(w, w)


def grouped_swiglu(tile_expert, n_used, xs, w_g, w_u, w_down, nf, residual=None):
    m, d = xs.shape
    f = w_down.shape[1]
    tf = f // nf
    n_tiles = m // FFN_TM
    in_specs = [pl.BlockSpec((FFN_TM, d), lambda t, j, te, nu: (t, 0)),
                pl.BlockSpec((1, d, tf), lambda t, j, te, nu: (te[t], 0, j)),
                pl.BlockSpec((1, d, tf), lambda t, j, te, nu: (te[t], 0, j)),
                pl.BlockSpec((1, tf, d), lambda t, j, te, nu: (te[t], j, 0))]
    extra = ()
    if residual is not None:
        rows_per_mod = m // residual[1].shape[0]
        in_specs += [pl.BlockSpec((FFN_TM, d), lambda t, j, te, nu: (t, 0)),
                     pl.BlockSpec((1, 1, d), lambda t, j, te, nu: ((t * FFN_TM) // rows_per_mod, 0, 0))]
        extra = tuple(residual)
    grid_spec = pltpu.PrefetchScalarGridSpec(
        num_scalar_prefetch=2,
        grid=(n_tiles, nf),
        in_specs=in_specs,
        out_specs=pl.BlockSpec((FFN_TM, d), lambda t, j, te, nu: (t, 0)))
    return pl.pallas_call(
        functools.partial(_swiglu_kernel, nf=nf, residual=residual is not None),
        grid_spec=grid_spec,
        out_shape=jax.ShapeDtypeStruct((m, d), jnp.float32),
        compiler_params=pltpu.CompilerParams(
            dimension_semantics=("arbitrary", "arbitrary"),
            vmem_limit_bytes=VMEM_LIMIT_BYTES),
        name="grouped_swiglu",
    )(tile_expert, n_used, xs, w_g, w_u, w_down, *extra)


ROW_TILE = 512
P_HY, P_Q, P_OG, P_S5, P_RGX, P_RGG = 0, 3, 6, 7, 8, 9
P_GDN = 10 * W_MIX
P_COLS = P_GDN + 128


def _rms_mod(x, g, shift, scale):
    y = x * lax.rsqrt(jnp.mean(x * x, axis=-1, keepdims=True) + EPS) * g
    return y * (1.0 + scale) + shift


def _front_kernel(x_ref, sh_ref, sc_ref, g_ref, w_ref, o_ref):
    h = _rms_mod(x_ref[...], g_ref[...], sh_ref[0], sc_ref[0])
    o_ref[...] = jnp.dot(h.astype(jnp.bfloat16), w_ref[...], preferred_element_type=jnp.float32)


def _mod_spec(rows_per_mod):
    return pl.BlockSpec((1, 1, D_MODEL), lambda i: ((i * ROW_TILE) // rows_per_mod, 0, 0))


def front(x, shift, scale, g, w):
    m = x.shape[0]
    n_mod = shift.shape[0]
    n_out = w.shape[1]
    mod = _mod_spec(m // n_mod)
    return pl.pallas_call(
        _front_kernel,
        grid=(m // ROW_TILE,),
        in_specs=[pl.BlockSpec((ROW_TILE, D_MODEL), lambda i: (i, 0)), mod, mod,
                  pl.BlockSpec((1, D_MODEL), lambda i: (0, 0)),
                  pl.BlockSpec((D_MODEL, n_out), lambda i: (0, 0))],
        out_specs=pl.BlockSpec((ROW_TILE, n_out), lambda i: (i, 0)),
        out_shape=jax.ShapeDtypeStruct((m, n_out), jnp.float32),
        compiler_params=pltpu.CompilerParams(dimension_semantics=("arbitrary",),
                                             vmem_limit_bytes=VMEM_LIMIT_BYTES),
        name="front",
    )(x, shift, scale, g.reshape(1, D_MODEL), w)


MERGE_TILE = 256
ROUTER_PAD = 128
N_MERGE_IN = 25


def _merge_kernel(*refs, with_router):
    (x_ref, sh1_ref, sc1_ref, gm_ref, sh2_ref, sc2_ref, g1_ref, g2_ref,
     of_ref, ob_ref, og_ref, s5_ref, hc_ref, hz_ref, hx0_ref, rf_ref, rb_ref, rgg_ref,
     wg_ref, wb_ref, wo_ref, wglu_ref, bglu_ref, havg_ref, hbias_ref) = refs[:N_MERGE_IN]
    f32, bf16 = jnp.float32, jnp.bfloat16
    hp = lax.Precision.HIGHEST
    x = x_ref[...]
    hb = _rms_mod(x, g1_ref[...], sh1_ref[0], sc1_ref[0]).astype(bf16)

    o = of_ref[...] + ob_ref[...]
    ms = jnp.dot(o * o, havg_ref[...], precision=hp, preferred_element_type=f32)
    og = og_ref[...]
    gla = o * lax.rsqrt(ms + EPS) * (og * jax.nn.sigmoid(og))
    g5 = jax.nn.gelu(s5_ref[...])
    s5o = g5 * jax.nn.sigmoid(jnp.dot(g5.astype(bf16), wglu_ref[...], preferred_element_type=f32)
                              + bglu_ref[...])
    rgo = (rf_ref[...] + rb_ref[...]) * jax.nn.gelu(rgg_ref[...])
    hyo = hx0_ref[...] * (hc_ref[...] + hz_ref[...] * hbias_ref[...])
    branches = (gla, s5o, hyo, rgo)

    y = None
    for k in range(N_BRANCH):
        gate = jax.nn.sigmoid(jnp.dot(hb, wg_ref[:, k * D_MODEL:(k + 1) * D_MODEL],
                                      preferred_element_type=f32))
        t = gate * jnp.dot(branches[k].astype(bf16), wb_ref[k], preferred_element_type=f32)
        y = t if y is None else y + t
    out = jnp.dot(y.astype(bf16), wo_ref[...], preferred_element_type=f32)
    xn = x + gm_ref[0] * out
    h2 = _rms_mod(xn, g2_ref[...], sh2_ref[0], sc2_ref[0])
    if with_router:
        rw_ref, rb2_ref, xo_ref, h2_ref, lg_ref = refs[N_MERGE_IN:]
        lg_ref[...] = _dot3(*_split_bf16(h2), *_split_bf16(rw_ref[...])) + rb2_ref[...]
    else:
        xo_ref, h2_ref = refs[N_MERGE_IN:]
    xo_ref[...] = xn
    h2_ref[...] = h2.astype(bf16)


def merge(x, mods, g1, g2, o_f, o_b, p, s5y, hy, r_f, r_b, wg, wb, wo, wglu, bglu, hbias, router=None):
    m = x.shape[0]
    tm = MERGE_TILE
    n_mod = mods[0].shape[0]
    rows_per_mod = m // n_mod
    mod = pl.BlockSpec((1, 1, D_MODEL), lambda i: ((i * tm) // rows_per_mod, 0, 0))
    row = pl.BlockSpec((tm, D_MODEL), lambda i: (i, 0))
    br = pl.BlockSpec((tm, W_MIX), lambda i: (i, 0))
    pcol = lambda c: pl.BlockSpec((tm, W_MIX), lambda i: (i, c))
    full = lambda a: pl.BlockSpec(a.shape, lambda i: (0,) * a.ndim)
    head = jnp.arange(W_MIX) // GLA_DV
    havg = (head[:, None] == head[None, :]).astype(jnp.float32) / GLA_DV
    vec = lambda v: v.reshape(1, -1)
    consts = [wg, wb, wo, wglu, vec(bglu), havg, vec(hbias)]
    out_specs = [row, pl.BlockSpec((tm, D_MODEL), lambda i: (i, 0))]
    out_shape = [jax.ShapeDtypeStruct((m, D_MODEL), jnp.float32),
                 jax.ShapeDtypeStruct((m, D_MODEL), jnp.bfloat16)]
    if router is not None:
        rw, rbias = router
        pad = ROUTER_PAD - rw.shape[1]
        consts += [jnp.pad(rw, ((0, 0), (0, pad))), jnp.pad(rbias, (0, pad)).reshape(1, -1)]
        out_specs.append(pl.BlockSpec((tm, ROUTER_PAD), lambda i: (i, 0)))
        out_shape.append(jax.ShapeDtypeStruct((m, ROUTER_PAD), jnp.float32))
    in_specs = ([row] + [mod] * 5 + [full(vec(g1)), full(vec(g2)), br, br, pcol(P_OG), br, br, br, br, br, br,
                                     pcol(P_RGG)] + [full(a) for a in consts])
    return pl.pallas_call(
        functools.partial(_merge_kernel, with_router=router is not None),
        grid=(m // tm,),
        in_specs=in_specs,
        out_specs=out_specs,
        out_shape=out_shape,
        compiler_params=pltpu.CompilerParams(dimension_semantics=("arbitrary",),
                                             vmem_limit_bytes=VMEM_LIMIT_BYTES),
        name="merge",
    )(x, *mods, vec(g1), vec(g2), o_f, o_b, p, s5y, *hy, r_f, r_b, p, *consts)


def _combine_kernel(x_ref, y0_ref, y1_ref, w_ref, gm_ref, g_ref, o_ref, *, final_norm):
    w = w_ref[...]
    y = w[:, 0:1] * y0_ref[...] + w[:, 1:2] * y1_ref[...]
    xn = x_ref[...] + gm_ref[0] * y
    if final_norm:
        xn = xn * lax.rsqrt(jnp.mean(xn * xn, axis=-1, keepdims=True) + EPS) * g_ref[...]
    o_ref[...] = xn


def moe_combine(x, yk, w, gate_mod, final_g):
    m = x.shape[0]
    n_mod = gate_mod.shape[0]
    g = jnp.ones((1, D_MODEL), jnp.float32) if final_g is None else final_g.reshape(1, D_MODEL)
    return pl.pallas_call(
        functools.partial(_combine_kernel, final_norm=final_g is not None),
        grid=(m // ROW_TILE,),
        in_specs=[pl.BlockSpec((ROW_TILE, D_MODEL), lambda i: (i, 0)),
                  pl.BlockSpec((ROW_TILE, D_MODEL), lambda i: (i, 0)),
                  pl.BlockSpec((ROW_TILE, D_MODEL), lambda i: (i + m // ROW_TILE, 0)),
                  pl.BlockSpec((ROW_TILE, TOP_K), lambda i: (i, 0)),
                  _mod_spec(m // n_mod),
                  pl.BlockSpec((1, D_MODEL), lambda i: (0, 0))],
        out_specs=pl.BlockSpec((ROW_TILE, D_MODEL), lambda i: (i, 0)),
        out_shape=jax.ShapeDtypeStruct((m, D_MODEL), jnp.float32),
        compiler_params=pltpu.CompilerParams(dimension_semantics=("arbitrary",)),
        name="moe_combine",
    )(x, yk, yk, w, gate_mod, g)


HY_LANES = 128
HY_SLABS = 8
HY_MIN_LEN = 1024


def _hyena_pre_kernel(x_ref, prev_ref, next_ref, w_ref, zt_ref, z_ref, x0_ref, *, tb):
    k = pl.program_id(1)
    prev = jnp.where(k > 0, prev_ref[0], 0.0)
    nxt = jnp.where(k < pl.num_programs(1) - 1, next_ref[0], 0.0)
    ext = jnp.concatenate([prev, x_ref[0], nxt], axis=0)
    pc = sum(ext[RG_HALO - 1 + j:RG_HALO - 1 + j + tb, :] * w_ref[j:j + 1, :] for j in range(HY_SHORT))
    v, x0, x1 = pc[:, :W_MIX], pc[:, W_MIX:2 * W_MIX], pc[:, 2 * W_MIX:]
    z = x1 * v
    z_ref[0] = z
    x0_ref[0] = x0
    zt_ref[0] = z.T


def hyena_pre(p, w_short):
    nb, n_tok, _ = p.shape
    tb = min(ROW_TILE, n_tok)
    hb = tb // RG_HALO
    n_halo = n_tok // RG_HALO
    wide = 3 * W_MIX
    tok = pl.BlockSpec((1, tb, W_MIX), lambda b, k: (b, k, 0))
    tok_out = jax.ShapeDtypeStruct((nb, n_tok, W_MIX), jnp.float32)
    return pl.pallas_call(
        functools.partial(_hyena_pre_kernel, tb=tb),
        grid=(nb, n_tok // tb),
        in_specs=[pl.BlockSpec((1, tb, wide), lambda b, k: (b, k, P_HY)),
                  pl.BlockSpec((1, RG_HALO, wide), lambda b, k: (b, jnp.maximum(k * hb - 1, 0), P_HY)),
                  pl.BlockSpec((1, RG_HALO, wide), lambda b, k: (b, jnp.minimum((k + 1) * hb, n_halo - 1), P_HY)),
                  pl.BlockSpec(w_short.shape, lambda b, k: (0, 0))],
        out_specs=[pl.BlockSpec((1, W_MIX, tb), lambda b, k: (b, 0, k)), tok, tok],
        out_shape=[jax.ShapeDtypeStruct((nb, W_MIX, n_tok), jnp.float32), tok_out, tok_out],
        compiler_params=pltpu.CompilerParams(dimension_semantics=("arbitrary", "arbitrary"),
                                             vmem_limit_bytes=VMEM_LIMIT_BYTES),
        name="hyena_pre",
    )(p, p, p, w_short)


def _split_bf16(a):
    hi = a.astype(jnp.bfloat16)
    lo = (a - hi.astype(jnp.float32)).astype(jnp.bfloat16)
    return hi, lo


def _dot3(a_hi, a_lo, b_hi, b_lo):
    d = functools.partial(jnp.dot, preferred_element_type=jnp.float32)
    return d(a_hi, b_hi) + (d(a_lo, b_hi) + d(a_hi, b_lo))


def _hyena_dft_consts(n1):
    n = n1 * HY_LANES
    ka = np.arange(n1, dtype=np.float64)[:, None]
    f1_ang = 2.0 * np.pi * ka * np.arange(n1 // 2, dtype=np.float64)[None, :] / n1
    f1r, f1i = np.cos(f1_ang), -np.sin(f1_ang)
    tw_ang = 2.0 * np.pi * ka * np.arange(HY_LANES, dtype=np.float64)[None, :] / n
    lo = np.arange(HY_LANES, dtype=np.float64)
    f2_ang = 2.0 * np.pi * lo[:, None] * lo[None, :] / HY_LANES
    f2r, f2i = np.cos(f2_ang), -np.sin(f2_ang)
    fwd_rows = np.concatenate([f1r, f1i], axis=0)
    fwd_lanes = np.block([[f2r, f2i], [-f2i, f2r]])
    inv_lanes = np.block([[f2r, -f2i], [f2i, f2r]])
    inv_rows = np.concatenate([f1r.T, f1i.T], axis=1) / n
    out = []
    for m in (fwd_rows, fwd_lanes, inv_lanes, inv_rows):
        m32 = jnp.asarray(m, jnp.float32)
        out.extend(_split_bf16(m32))
    return out + [jnp.asarray(np.cos(tw_ang), jnp.float32), jnp.asarray(-np.sin(tw_ang), jnp.float32)]


def _hyena_fft_kernel(*refs, n1, ns, spectrum):
    bf16 = jnp.bfloat16
    if spectrum:
        z_ref, f1h, f1l, f2h, f2l, twr_ref, twi_ref, o_ref = refs
        rows_dft = lambda t: _dot3(f1h[...], f1l[...], *_split_bf16(t))
        lanes_dft = lambda t: _dot3(*_split_bf16(t), f2h[...], f2l[...])
    else:
        z_ref, hf_ref, f1h, f2h, g2h, fih, twr_ref, twi_ref, o_ref = refs
        mm = lambda a, b: jnp.dot(a.astype(bf16), b.astype(bf16), preferred_element_type=jnp.float32)
        rows_dft = lambda t: mm(f1h[...], t)
        lanes_dft = lambda t: mm(t, f2h[...])
    w = HY_LANES
    twr, twi = twr_ref[...], twi_ref[...]
    z2 = jnp.concatenate([z_ref[0, s] for s in range(ns)], axis=1)
    a2 = rows_dft(z2)
    rows = []
    for s in range(ns):
        r, i = a2[:n1, s * w:(s + 1) * w], a2[n1:, s * w:(s + 1) * w]
        rows.append(jnp.concatenate([r * twr - i * twi, r * twi + i * twr], axis=1))
    x = lanes_dft(jnp.concatenate(rows, axis=0))
    if spectrum:
        o_ref[...] = x.reshape(ns, n1, 2 * w)
        return
    h = hf_ref[...].reshape(ns * n1, 2 * w)
    xr, xi, hr, hi = x[:, :w], x[:, w:], h[:, :w], h[:, w:]
    y = jnp.concatenate([xr * hr - xi * hi, xr * hi + xi * hr], axis=1)
    g = mm(y, g2h[...])
    cr, ci = [], []
    for s in range(ns):
        gr, gi = g[s * n1:(s + 1) * n1, :w], g[s * n1:(s + 1) * n1, w:]
        cr.append(gr * twr + gi * twi)
        ci.append(gi * twr - gr * twi)
    gc = jnp.concatenate([jnp.concatenate(cr, axis=1), jnp.concatenate(ci, axis=1)], axis=0)
    y2 = mm(fih[...], gc)
    for s in range(ns):
        o_ref[0, s] = y2[:, s * w:(s + 1) * w]


def hyena_fft(zt, hf=None):
    nb, ch, half, w = zt.shape
    n1 = 2 * half
    ns = HY_SLABS
    f1h, f1l, f2h, f2l, g2h, _, fih, _, twr, twi = _hyena_dft_consts(n1)
    full = lambda a: pl.BlockSpec(a.shape, lambda b, c: (0,) * a.ndim)
    zspec = pl.BlockSpec((1, ns, half, w), lambda b, c: (b, c, 0, 0))
    if hf is None:
        consts = [f1h, f1l, f2h, f2l, twr, twi]
        in_specs, args = [zspec], [zt]
        out_spec = pl.BlockSpec((ns, n1, 2 * w), lambda b, c: (b * (ch // ns) + c, 0, 0))
        out_shape = jax.ShapeDtypeStruct((nb * ch, n1, 2 * w), jnp.float32)
    else:
        consts = [f1h, f2h, g2h, fih, twr, twi]
        in_specs = [zspec, pl.BlockSpec((ns, n1, 2 * w), lambda b, c: (c, 0, 0))]
        args = [zt, hf]
        out_spec = zspec
        out_shape = jax.ShapeDtypeStruct(zt.shape, jnp.float32)
    return pl.pallas_call(
        functools.partial(_hyena_fft_kernel, n1=n1, ns=ns, spectrum=hf is None),
        grid=(nb, ch // ns),
        in_specs=in_specs + [full(a) for a in consts],
        out_specs=out_spec,
        out_shape=out_shape,
        compiler_params=pltpu.CompilerParams(dimension_semantics=("arbitrary", "arbitrary"),
                                             vmem_limit_bytes=VMEM_LIMIT_BYTES),
        name="hyena_fft",
    )(*args, *consts)


def rmsnorm(x, g):
    y = x * lax.rsqrt(jnp.mean(x * x, axis=-1, keepdims=True) + EPS)
    return y * g


def adaln(cond, w, b):
    return jax.nn.silu(cond) @ w + b


def grid_pos_embed(n_tokens, dim):
    rows = n_tokens // GRID_W
    q = dim // 4
    omega = 1.0 / (10000.0 ** (jnp.arange(q, dtype=jnp.float32) / q))
    r = jnp.arange(rows, dtype=jnp.float32)[:, None] * omega
    cc = jnp.arange(GRID_W, dtype=jnp.float32)[:, None] * omega
    er = jnp.concatenate([jnp.sin(r), jnp.cos(r)], axis=-1)
    ec = jnp.concatenate([jnp.sin(cc), jnp.cos(cc)], axis=-1)
    emb = jnp.concatenate([jnp.broadcast_to(er[:, None], (rows, GRID_W, dim // 2)),
                           jnp.broadcast_to(ec[None], (rows, GRID_W, dim // 2))], axis=-1)
    return emb.reshape(rows * GRID_W, dim)


def gla_mixer(p, w_up, b_up, s0):
    gdn = p[..., P_GDN:P_GDN + 2 * GLA_RANK]
    outs, finals = [], []
    for d in range(2):
        od, sd = gla_dir(p, gdn[..., d * GLA_RANK:(d + 1) * GLA_RANK], w_up[d], b_up[d], s0[d], d == 1)
        outs.append(od)
        finals.append(sd)
    return outs, jnp.stack(finals)


def s5_mixer(u, prep, s0):
    wd, wk, wc, a_t = prep
    b_, n_tok, _ = u.shape
    n = n_tok // S5_T
    u2 = u.reshape(b_, n, S5_T * W_MIX).transpose(1, 0, 2).reshape(n * b_, S5_T * W_MIX)
    dmat = pmm(u2, wd)
    hmat, fin = s5_scan(dmat.reshape(n, b_, 4 * S5_NS), s0, a_t)
    y2 = pmm_multi([u2, hmat.reshape(n * b_, 4 * S5_NS)], [wk, wc], (False, True))
    y = y2.reshape(n, b_, S5_T, W_MIX).transpose(1, 0, 2, 3).reshape(b_, n_tok, W_MIX)
    return y, fin


def hyena_filters(n_tok, w1, b1, w2, b2, w3, freq):
    f32 = jnp.float32
    t = jnp.arange(n_tok, dtype=f32)[:, None]
    bands = jnp.linspace(1e-4, HY_BANDS - 1, HY_BANDS, dtype=f32)[None]
    ang = 2.0 * math.pi * bands * t / n_tok
    z = jnp.concatenate([t / n_tok, jnp.cos(ang), jnp.sin(ang)], axis=-1)
    hp = lax.Precision.HIGHEST
    h = jnp.sin(freq * (jnp.dot(z, w1, precision=hp) + b1))
    h = jnp.sin(freq * (jnp.dot(h, w2, precision=hp) + b2))
    h = jnp.dot(h, w3, precision=hp)
    t01 = t / max(n_tok - 1, 1)
    deltas = jnp.abs(jnp.linspace(math.log(HY_TARGET) / HY_DECAY_SHORT,
                                  math.log(HY_TARGET) / HY_DECAY_LONG, W_MIX, dtype=f32))
    h = h * jnp.exp(-t01 * jnp.tile(deltas, 2))
    return h / (jnp.sum(jnp.abs(h), axis=0, keepdims=True) + EPS)


def hyena_mixer(p, w_short, w1, b1, w2, b2, w3, freq):
    nb, n_tok, _ = p.shape
    zt, z, x0 = hyena_pre(p, w_short)
    n_pad = max(n_tok, HY_MIN_LEN)
    half = n_pad // HY_LANES

    def frames(t):
        t = jnp.pad(t, [(0, 0)] * (t.ndim - 1) + [(0, n_pad - n_tok)])
        return t.reshape(t.shape[:-1] + (half, HY_LANES))

    filt = hyena_filters(n_tok, w1, b1, w2, b2, w3, freq)
    spec = hyena_fft(frames(filt.T)[None])
    sf, sb = spec[:W_MIX], spec[W_MIX:]
    hfreq = jnp.concatenate([sf[..., :HY_LANES] + sb[..., :HY_LANES],
                             sf[..., HY_LANES:] - sb[..., HY_LANES:]], axis=-1)
    conv = hyena_fft(frames(zt), hfreq)
    conv = conv.reshape(nb, W_MIX, n_pad)[:, :, :n_tok].transpose(0, 2, 1)
    return conv, z, x0


def rglru_mixer(p, w_conv, w_a, b_a, w_x, b_x, lam, s0):
    a_f, b_f, a_b, b_b = rg_pre(p, w_conv, w_a, b_a, w_x, b_x, lam)
    h_f, fin_f = rg_scan(a_f, b_f, s0[0], False)
    h_b, fin_b = rg_scan(a_b, b_b, s0[1], True)
    return [h_f, h_b], jnp.stack([fin_f, fin_b])


def ffn_dense(h2, x, gate_mod, w_gu, w_down):
    n_tiles = h2.shape[0] // FFN_TM
    d_ff = w_down.shape[0]
    return grouped_swiglu(jnp.zeros((n_tiles,), jnp.int32), jnp.full((1,), n_tiles, jnp.int32), h2,
                          w_gu[None, :, :d_ff].astype(jnp.bfloat16), w_gu[None, :, d_ff:].astype(jnp.bfloat16),
                          w_down[None].astype(jnp.bfloat16), nf=2,
                          residual=(x, gate_mod))


def ffn_moe(h2, logits, x, gate_mod, w_gu, w_down, final_g):
    n_tok = h2.shape[0]
    n_slot = TOP_K * n_tok
    top_v, top_i = lax.top_k(logits, TOP_K)
    w = jax.nn.softmax(top_v, axis=-1)
    e_flat = top_i.T.reshape(-1).astype(jnp.int32)
    onehot = (e_flat[:, None] == jnp.arange(N_EXPERTS, dtype=jnp.int32)[None]).astype(jnp.int32)
    csum = jnp.cumsum(onehot, axis=0)
    cnt = csum[-1]
    rank = jnp.sum(csum * onehot, axis=1) - 1
    padded = ((cnt + FFN_TM - 1) // FFN_TM) * FFN_TM
    ends = jnp.cumsum(padded)
    dest = (ends - padded)[e_flat] + rank
    n_rows = n_slot + N_EXPERTS * FFN_TM
    n_tiles = n_rows // FFN_TM
    tile_start = jnp.arange(n_tiles, dtype=jnp.int32) * FFN_TM
    tile_expert = jnp.minimum(jnp.sum((tile_start[:, None] >= ends[None, :]).astype(jnp.int32), axis=1),
                              N_EXPERTS - 1)
    n_used = (ends[-1:] // FFN_TM).astype(jnp.int32)
    order = jnp.argsort(e_flat, stable=True).astype(jnp.int32)
    row_expert = jnp.repeat(tile_expert, FFN_TM)
    shift = (ends - padded) - (jnp.cumsum(cnt) - cnt)
    q = jnp.arange(n_rows, dtype=jnp.int32) - shift[row_expert]
    src = order[jnp.clip(q, 0, n_slot - 1)] % n_tok
    xs = h2.at[src].get(mode="promise_in_bounds")
    wg, wu = cast_split_bf16(w_gu)
    ys = grouped_swiglu(tile_expert, n_used, xs, wg, wu, w_down.astype(jnp.bfloat16), nf=2)
    yk = ys.at[dest].get(mode="promise_in_bounds")
    return moe_combine(x, yk, w, gate_mod, final_g)


def kernel(x, c, ctx, c_ctx, mod_w, mod_b, norm1_g, norm2_g, w_in, gla_w_up, gla_b_up,
           s5_lam_re, s5_lam_im, s5_log_dt, s5_b_re, s5_b_im, s5_c_re, s5_c_im, s5_d,
           s5_w_glu, s5_b_glu, hy_w_short, hy_w1, hy_b1, hy_w2, hy_b2, hy_w3, hy_freq,
           hy_bias, rg_w_conv, rg_w_a, rg_b_a, rg_w_x, rg_b_x, rg_lam, w_branch, w_out,
           ffn_w_gu, ffn_w_down, moe_router, moe_router_b, moe_w_gu, moe_w_down, final_g):
    f32, bf16 = jnp.float32, jnp.bfloat16
    n_b, n_lat, _ = x.shape
    n_ctx = ctx.shape[1]
    xs = (x + grid_pos_embed(n_lat, D_MODEL)[None]).reshape(n_b * n_lat, D_MODEL)
    cs = ctx.reshape(n_b * n_ctx, D_MODEL)
    offs = [0]
    for n in IN_SIZES:
        offs.append(offs[-1] + n)
    for l in range(DEPTH):
        last = l == DEPTH - 1
        dense = l % 2 == 0
        j = l // 2
        m_lat = [t[:, None, :] for t in jnp.split(adaln(c, mod_w[l], mod_b[l]), 6, axis=-1)]
        m_ctx = [t[None, None, :] for t in jnp.split(adaln(c_ctx, mod_w[l], mod_b[l]), 6, axis=-1)]
        wl = w_in[l]
        w_mix = jnp.concatenate([wl[:, offs[6]:offs[7]], wl[:, offs[0]:offs[4]], wl[:, offs[5]:offs[6]],
                                 wl[:, offs[7]:offs[9]], wl[:, offs[4]:offs[5]],
                                 jnp.zeros((D_MODEL, P_COLS - P_GDN - 2 * GLA_RANK), f32)], axis=1).astype(bf16)
        w_gate = wl[:, offs[9]:offs[10]].astype(bf16)
        wb, wo, wglu = w_branch[l].astype(bf16), w_out[l].astype(bf16), s5_w_glu[l].astype(bf16)
        s5_ops = s5_prepare(s5_lam_re[l], s5_lam_im[l], s5_log_dt[l], s5_b_re[l], s5_b_im[l],
                            s5_c_re[l], s5_c_im[l], s5_d[l])
        hy_p = (hy_w_short[l], hy_w1[l], hy_b1[l], hy_w2[l], hy_b2[l], hy_w3[l], hy_freq[l])
        rg_p = (rg_w_conv[l], rg_w_a[l], rg_b_a[l], rg_w_x[l], rg_b_x[l], rg_lam[l])
        router = None if dense else (moe_router[j], moe_router_b[j])

        def mixers(p2, n_tok, states, with_hyena):
            p = p2.reshape(n_b, n_tok, P_COLS)
            blk = lambda i, n=1: p[..., i * W_MIX:(i + n) * W_MIX]
            flat = lambda t: t.reshape(n_b * n_tok, W_MIX)
            gla_o, gla_s = gla_mixer(p, gla_w_up[l], gla_b_up[l], states[0])
            s5_y, s5_s = s5_mixer(blk(P_S5), s5_ops, states[1])
            rg_o, rg_s = rglru_mixer(p, *rg_p, states[2])
            hy = tuple(flat(t) for t in hyena_mixer(p, *hy_p)) if with_hyena else None
            return ((flat(gla_o[0]), flat(gla_o[1]), flat(s5_y), hy, flat(rg_o[0]), flat(rg_o[1])),
                    (gla_s, s5_s, rg_s))

        def tail(stream, mods, p2, br, final):
            outs = merge(stream, tuple(mods[:5]), norm1_g[l], norm2_g[l], br[0], br[1], p2, br[2], br[3],
                         br[4], br[5], w_gate, wb, wo, wglu, s5_b_glu[l], hy_bias[l], router)
            if dense:
                return ffn_dense(outs[1], outs[0], mods[5], ffn_w_gu[j], ffn_w_down[j])
            return ffn_moe(outs[1], outs[2][:, :N_EXPERTS], outs[0], mods[5], moe_w_gu[j], moe_w_down[j],
                           final_g if final else None)

        zero_states = (jnp.zeros((2, n_b, GLA_HEADS, GLA_DV, GLA_DK), f32),
                       jnp.zeros((2, n_b, 2 * S5_NS), f32), jnp.zeros((2, n_b, W_MIX), f32))
        p_ctx = front(cs, m_ctx[0], m_ctx[1], norm1_g[l], w_mix)
        br_ctx, states = mixers(p_ctx, n_ctx, zero_states, not last)
        p_lat = front(xs, m_lat[0], m_lat[1], norm1_g[l], w_mix)
        br_lat, _ = mixers(p_lat, n_lat, states, True)
        xs = tail(xs, m_lat, p_lat, br_lat, last)
        if not last:
            cs = tail(cs, m_ctx, p_ctx, br_ctx, False)
    if (DEPTH - 1) % 2 == 0:
        xs = rmsnorm(xs, final_g)
    return xs.reshape(n_b, n_lat, D_MODEL)
```

```python
import functools
import math

import jax
import jax.numpy as jnp
import numpy as np
from jax import lax
from jax.experimental import pallas as pl
from jax.experimental.pallas import tpu as pltpu

D_MODEL = 1024
DEPTH = 2
GRID_W = 64
EPS = 1e-6
N_BRANCH = 4
W_MIX = D_MODEL // N_BRANCH
GLA_HEADS = 4
GLA_DK = W_MIX // GLA_HEADS
GLA_DV = W_MIX // GLA_HEADS
GLA_RANK = 16
GLA_TAU = 16.0
GLA_CHUNK = 64
S5_GROUP = 16
S5_GROUPS = W_MIX // S5_GROUP
S5_STATE = 64
S5_MAX_RE = -1e-4
HY_BANDS = 16
HY_SHORT = 3
HY_DECAY_SHORT = 0.3
HY_DECAY_LONG = 1.5
HY_TARGET = 1e-2
RG_BLOCKS = 4
RG_BLOCK = W_MIX // RG_BLOCKS
RG_CONV = 4
RG_C = 8.0
N_EXPERTS = 8
TOP_K = 2
IN_SIZES = (GLA_HEADS * GLA_DK, GLA_HEADS * GLA_DK, GLA_HEADS * GLA_DV, GLA_HEADS * GLA_DV,
            2 * GLA_RANK, W_MIX, 3 * W_MIX, W_MIX, W_MIX, N_BRANCH * D_MODEL)

VMEM_LIMIT_BYTES = 48 * 1024 * 1024


def _mm_kernel(x_ref, w_ref, o_ref):
    o_ref[...] = jnp.dot(x_ref[...].astype(jnp.bfloat16), w_ref[...],
                         preferred_element_type=jnp.float32)


def _pick_tile(n, cap):
    best = None
    for t in range(128, cap + 1, 128):
        if n % t == 0:
            best = t
    return best if best is not None else n


def pmm(x, w):
    lead = x.shape[:-1]
    k = x.shape[-1]
    n = w.shape[-1]
    x2 = x.reshape(-1, k)
    m = x2.shape[0]
    tm = 512 if m % 512 == 0 else m
    if k > 2048 and m % 256 == 0:
        tm = 256
    tn = n if k * n * 2 <= 6 * 1024 * 1024 else _pick_tile(n, 1024)
    out = pl.pallas_call(
        _mm_kernel,
        grid=(m // tm, n // tn),
        in_specs=[pl.BlockSpec((tm, k), lambda i, j: (i, 0)),
                  pl.BlockSpec((k, tn), lambda i, j: (0, j))],
        out_specs=pl.BlockSpec((tm, tn), lambda i, j: (i, j)),
        out_shape=jax.ShapeDtypeStruct((m, n), jnp.float32),
        compiler_params=pltpu.CompilerParams(
            dimension_semantics=("arbitrary", "arbitrary"),
            vmem_limit_bytes=VMEM_LIMIT_BYTES),
    )(x2, w.astype(jnp.bfloat16))
    return out.reshape(lead + (n,))


def _mm_multi_kernel(*refs, transposed):
    o_ref = refs[-1]
    n = (len(refs) - 1) // 2
    acc = None
    for i in range(n):
        dims = (((1,), (1 if transposed[i] else 0,)), ((), ()))
        t = lax.dot_general(refs[i][...].astype(jnp.bfloat16), refs[n + i][...], dims,
                            preferred_element_type=jnp.float32)
        acc = t if acc is None else acc + t
    o_ref[...] = acc


def pmm_multi(xs, ws, transposed, tm=256, tn=512):
    m = xs[0].shape[0]
    n = ws[0].shape[0 if transposed[0] else 1]
    tm = tm if m % tm == 0 else m
    tn = tn if n % tn == 0 else n
    in_specs = ([pl.BlockSpec((tm, x.shape[1]), lambda i, j: (i, 0)) for x in xs]
                + [pl.BlockSpec((tn, w.shape[1]), lambda i, j: (j, 0)) if t else
                   pl.BlockSpec((w.shape[0], tn), lambda i, j: (0, j)) for w, t in zip(ws, transposed)])
    return pl.pallas_call(
        functools.partial(_mm_multi_kernel, transposed=tuple(transposed)),
        grid=(m // tm, n // tn),
        in_specs=in_specs,
        out_specs=pl.BlockSpec((tm, tn), lambda i, j: (i, j)),
        out_shape=jax.ShapeDtypeStruct((m, n), jnp.float32),
        compiler_params=pltpu.CompilerParams(
            dimension_semantics=("arbitrary", "arbitrary"),
            vmem_limit_bytes=VMEM_LIMIT_BYTES),
    )(*xs, *[w.astype(jnp.bfloat16) for w in ws])


RG_SCAN_ROWS = 256


def _rg_scan_kernel(a_ref, b_ref, s0_ref, h_ref, fin_ref, st_ref, *, reverse, tb, nb):
    @pl.when(pl.program_id(0) == 0)
    def _():
        st_ref[...] = s0_ref[...]

    def body(r, hs):
        rr = (tb - 1 - r) if reverse else r
        out = []
        for i in range(nb):
            h = a_ref[i, pl.ds(rr, 1), :] * hs[i] + b_ref[i, pl.ds(rr, 1), :]
            h_ref[i, pl.ds(rr, 1), :] = h
            out.append(h)
        return tuple(out)

    hs = lax.fori_loop(0, tb, body, tuple(st_ref[i:i + 1, :] for i in range(nb)), unroll=8)
    for i in range(nb):
        st_ref[i:i + 1, :] = hs[i]
        fin_ref[i:i + 1, :] = hs[i]


def rg_scan(a, b, s0, reverse):
    nb, n_tok, ch = a.shape
    tb = min(RG_SCAN_ROWS, n_tok)
    nblk = n_tok // tb
    imap = (lambda k: (0, nblk - 1 - k, 0)) if reverse else (lambda k: (0, k, 0))
    return pl.pallas_call(
        functools.partial(_rg_scan_kernel, reverse=reverse, tb=tb, nb=nb),
        grid=(nblk,),
        in_specs=[pl.BlockSpec((nb, tb, ch), imap), pl.BlockSpec((nb, tb, ch), imap),
                  pl.BlockSpec((nb, ch), lambda k: (0, 0))],
        out_specs=[pl.BlockSpec((nb, tb, ch), imap), pl.BlockSpec((nb, ch), lambda k: (0, 0))],
        out_shape=[jax.ShapeDtypeStruct((nb, n_tok, ch), jnp.float32),
                   jax.ShapeDtypeStruct((nb, ch), jnp.float32)],
        scratch_shapes=[pltpu.VMEM((nb, ch), jnp.float32)],
        compiler_params=pltpu.CompilerParams(dimension_semantics=("arbitrary",)),
        name="rg_scan",
    )(a, b, s0)


RG_HALO = 8


def _rg_pre_kernel(x_ref, prev_ref, next_ref, wc_ref, wg_ref, bg_ref, c_ref, af_ref, bf_ref, ab_ref, bb_ref,
                   *, tb):
    k = pl.program_id(1)
    prev = jnp.where(k > 0, prev_ref[0], 0.0)
    nxt = jnp.where(k < pl.num_programs(1) - 1, next_ref[0], 0.0)
    ext = jnp.concatenate([prev, x_ref[0], nxt], axis=0)
    xc = sum(ext[RG_HALO - 2 + j:RG_HALO - 2 + j + tb, :] * wc_ref[j:j + 1, :] for j in range(RG_CONV))
    m = jnp.dot(xc.astype(jnp.bfloat16), wg_ref[...], preferred_element_type=jnp.float32) + bg_ref[...]
    for d, (a_ref, b_ref) in enumerate(((af_ref, bf_ref), (ab_ref, bb_ref))):
        r = jax.nn.sigmoid(m[:, (2 * d) * W_MIX:(2 * d + 1) * W_MIX])
        i = jax.nn.sigmoid(m[:, (2 * d + 1) * W_MIX:(2 * d + 2) * W_MIX])
        log_a = -r * c_ref[d:d + 1, :]
        a_ref[0] = jnp.exp(log_a)
        u = jnp.tanh(log_a)
        b_ref[0] = jnp.sqrt(-2.0 * u / (1.0 - u)) * (i * xc)


def rg_pre(p, w_conv, w_a, b_a, w_x, b_x, lam):
    nb, n_tok, _ = p.shape
    tb = min(ROW_TILE, n_tok)
    nblk = n_tok // tb
    hb = tb // RG_HALO
    n_halo = n_tok // RG_HALO
    col = P_RGX
    blockdiag = lambda w: jax.scipy.linalg.block_diag(*[w[i] for i in range(RG_BLOCKS)])
    wg = jnp.concatenate([blockdiag(w_a[0]), blockdiag(w_x[0]), blockdiag(w_a[1]), blockdiag(w_x[1])],
                         axis=1).astype(jnp.bfloat16)
    bg = jnp.concatenate([b_a[0].reshape(-1), b_x[0].reshape(-1), b_a[1].reshape(-1), b_x[1].reshape(-1)]
                         ).reshape(1, 4 * W_MIX)
    c = RG_C * jax.nn.softplus(-lam)
    blk = pl.BlockSpec((1, tb, W_MIX), lambda b, k: (b, k, 0))
    full = lambda a: pl.BlockSpec(a.shape, lambda b, k: (0,) * a.ndim)
    out = jax.ShapeDtypeStruct((nb, n_tok, W_MIX), jnp.float32)
    return pl.pallas_call(
        functools.partial(_rg_pre_kernel, tb=tb),
        grid=(nb, nblk),
        in_specs=[pl.BlockSpec((1, tb, W_MIX), lambda b, k: (b, k, col)),
                  pl.BlockSpec((1, RG_HALO, W_MIX), lambda b, k: (b, jnp.maximum(k * hb - 1, 0), col)),
                  pl.BlockSpec((1, RG_HALO, W_MIX), lambda b, k: (b, jnp.minimum((k + 1) * hb, n_halo - 1), col)),
                  full(w_conv), full(wg), full(bg), full(c)],
        out_specs=[blk, blk, blk, blk],
        out_shape=[out, out, out, out],
        compiler_params=pltpu.CompilerParams(dimension_semantics=("arbitrary", "arbitrary")),
        name="rg_pre",
    )(p, p, p, w_conv, wg, bg, c)


S5_T = 16
S5_NS = S5_GROUPS * S5_STATE
S5_SCAN_CHUNKS = 64


def _s5_scan_kernel(d_ref, s0_ref, a_ref, h_ref, fin_ref, st_ref, *, rc, nb):
    d = pl.program_id(0)

    @pl.when(pl.program_id(1) == 0)
    def _():
        st_ref[...] = s0_ref[0]

    ar = jnp.broadcast_to(a_ref[0, :, 0:S5_NS], (nb, S5_NS))
    ai = jnp.broadcast_to(a_ref[0, :, S5_NS:2 * S5_NS], (nb, S5_NS))

    def body(r, carry):
        hr, hi = carry
        rr = r + d * (rc - 1 - 2 * r)
        h_ref[rr, :, 0:S5_NS] = hr
        h_ref[rr, :, S5_NS:2 * S5_NS] = hi
        dr = d_ref[rr, :, 0:S5_NS]
        di = d_ref[rr, :, S5_NS:2 * S5_NS]
        return ar * hr - ai * hi + dr, ar * hi + ai * hr + di

    hr, hi = lax.fori_loop(0, rc, body, (st_ref[:, 0:S5_NS], st_ref[:, S5_NS:2 * S5_NS]))
    st_ref[:, 0:S5_NS] = hr
    st_ref[:, S5_NS:2 * S5_NS] = hi
    fin_ref[0, :, 0:S5_NS] = hr
    fin_ref[0, :, S5_NS:2 * S5_NS] = hi


def s5_scan(dmat, s0, a_t):
    n, nb, _ = dmat.shape
    rc = min(S5_SCAN_CHUNKS, n)
    nblk = n // rc
    w = 2 * S5_NS
    imap = lambda d, k: (k + d * (nblk - 1 - 2 * k), 0, d)
    return pl.pallas_call(
        functools.partial(_s5_scan_kernel, rc=rc, nb=nb),
        grid=(2, nblk),
        in_specs=[pl.BlockSpec((rc, nb, w), imap),
                  pl.BlockSpec((1, nb, w), lambda d, k: (d, 0, 0)),
                  pl.BlockSpec((1, 1, w), lambda d, k: (d, 0, 0))],
        out_specs=[pl.BlockSpec((rc, nb, w), imap),
                   pl.BlockSpec((1, nb, w), lambda d, k: (d, 0, 0))],
        out_shape=[jax.ShapeDtypeStruct((n, nb, 2 * w), jnp.float32),
                   jax.ShapeDtypeStruct((2, nb, w), jnp.float32)],
        scratch_shapes=[pltpu.VMEM((nb, w), jnp.float32)],
        compiler_params=pltpu.CompilerParams(dimension_semantics=("arbitrary", "arbitrary"),
                                             vmem_limit_bytes=VMEM_LIMIT_BYTES),
    )(dmat, s0, a_t)


def _cmul(ar, ai, br, bi):
    return ar * br - ai * bi, ar * bi + ai * br


def s5_prepare(lam_re, lam_im, log_dt, b_re, b_im, c_re, c_im, d_skip):
    f32 = jnp.float32
    hp = lax.Precision.HIGHEST
    t_len, g_n, p_n, h_n = S5_T, S5_GROUPS, S5_STATE, S5_GROUP
    bf16 = jnp.bfloat16
    eye_g = jnp.eye(g_n, dtype=f32)
    mask_gp = jnp.repeat(eye_g, p_n, axis=1).astype(bf16)[None, :, None, :]
    ar_t = jnp.arange(t_len)
    wd, wc, kk, a_t = [], [], [], []
    for d in range(2):
        lr = jnp.minimum(lam_re[d], S5_MAX_RE)
        li = lam_im[d]
        dt = jnp.exp(log_dt[d])[:, None]
        tt = jnp.arange(t_len + 1, dtype=f32)[:, None, None]
        mag = jnp.exp(lr * dt * tt)
        ang = li * dt * tt
        pr, pi = mag * jnp.cos(ang), mag * jnp.sin(ang)
        nr, ni = pr[1] - 1.0, pi[1]
        den = lr * lr + li * li
        qr, qi = (nr * lr + ni * li) / den, (ni * lr - nr * li) / den
        bbr, bbi = _cmul(qr[..., None], qi[..., None], b_re[d], b_im[d])
        cr, ci = c_re[d], c_im[d]

        idx = (t_len - 1 - ar_t) if d == 0 else ar_t
        wr, wi = _cmul(pr[idx][..., None], pi[idx][..., None], bbr[None], bbi[None])

        def place_d(w):
            wt = w.transpose(0, 3, 1, 2).reshape(t_len, 1, h_n, g_n * p_n).astype(bf16)
            return (wt * mask_gp).reshape(t_len * W_MIX, g_n * p_n)

        wd.append(jnp.concatenate([place_d(wr), place_d(wi)], axis=1))

        idx2 = (ar_t + 1) if d == 0 else (t_len - ar_t)
        cwr, cwi = _cmul(cr[None], ci[None], pr[idx2][:, :, None, :], pi[idx2][:, :, None, :])

        def place_c(w):
            wt = w.transpose(0, 2, 1, 3).reshape(t_len, 1, h_n, g_n * p_n).astype(bf16)
            return (wt * mask_gp).reshape(t_len * W_MIX, g_n * p_n)

        wc.append(jnp.concatenate([place_c(cwr), place_c(-cwi)], axis=1))

        er, ei = _cmul(pr[:t_len][:, :, None, :], pi[:t_len][:, :, None, :], cr[None], ci[None])
        kk.append(jnp.einsum('tghp,gpk->tghk', er, bbr, precision=hp)
                  - jnp.einsum('tghp,gpk->tghk', ei, bbi, precision=hp))
        a_t.append(jnp.concatenate([pr[t_len].reshape(1, -1), pi[t_len].reshape(1, -1)], axis=1))

    lag = ar_t[None, :] - ar_t[:, None]
    mf = jnp.where((lag >= 0)[..., None, None, None], kk[0][jnp.clip(lag, 0, t_len - 1)], 0.0)
    mb = jnp.where((lag <= 0)[..., None, None, None], kk[1][jnp.clip(-lag, 0, t_len - 1)], 0.0)
    skip = (jnp.eye(t_len, dtype=f32)[:, :, None, None, None] * d_skip[None, None, :, :, None]
            * jnp.eye(h_n, dtype=f32)[None, None, None])
    m = mf + mb + skip
    mt = m.transpose(0, 4, 1, 2, 3).reshape(t_len, 1, h_n, t_len * W_MIX).astype(bf16)
    mask_igh = jnp.tile(jnp.repeat(eye_g, h_n, axis=1), (1, t_len)).astype(bf16)[None, :, None, :]
    wk = (mt * mask_igh).reshape(t_len * W_MIX, t_len * W_MIX)
    return (jnp.concatenate(wd, axis=1), wk, jnp.concatenate(wc, axis=1), jnp.stack(a_t))


GLA_BLOCK = 1024


def _gla_kernel(q_ref, k_ref, v_ref, g_ref, wup_ref, bup_ref, s0_ref, o_ref, fin_ref, st_ref,
                *, reverse, tb):
    f32, bf16 = jnp.float32, jnp.bfloat16
    hp = lax.Precision.HIGHEST
    cc = GLA_CHUNK

    @pl.when(pl.program_id(1) == 0)
    def _():
        st_ref[...] = s0_ref[0]

    r_i = lax.broadcasted_iota(jnp.int32, (cc, cc), 0)
    c_i = lax.broadcasted_iota(jnp.int32, (cc, cc), 1)
    keep = (c_i >= r_i) if reverse else (c_i <= r_i)
    tri = keep.astype(f32)
    nt = (((1,), (1,)), ((), ()))
    tn = (((0,), (0,)), ((), ()))
    n_ch = tb // cc
    chunks = range(n_ch)
    heads = range(GLA_HEADS)
    hsl = [slice(h * GLA_DK, (h + 1) * GLA_DK) for h in heads]
    rows = [slice(c * cc, (c + 1) * cc) for c in chunks]

    z = jnp.dot(g_ref[0], wup_ref[...], precision=hp, preferred_element_type=f32) + bup_ref[...]
    la = (jnp.minimum(z, 0.0) - jnp.log1p(jnp.exp(-jnp.abs(z)))) * (1.0 / GLA_TAU)
    cum = [jnp.dot(tri, la[rows[c]], precision=hp, preferred_element_type=f32) for c in chunks]
    last = [cm[0:1, :] if reverse else cm[cc - 1:cc, :] for cm in cum]
    k = [k_ref[0, rows[c], :] for c in chunks]
    q_in = [(q_ref[0, rows[c], :] * (GLA_DK ** -0.5) * jnp.exp(cum[c])).astype(bf16) for c in chunks]
    k_in = [(k[c] * jnp.exp(-cum[c])).astype(bf16) for c in chunks]
    k_out = [(k[c] * jnp.exp(last[c] - cum[c])).astype(bf16) for c in chunks]
    dec = [jnp.exp(last[c]) for c in chunks]
    vb = [v_ref[0, rows[c], :].astype(bf16) for c in chunks]
    att = [[jnp.where(keep, lax.dot_general(q_in[c][:, s], k_in[c][:, s], nt, preferred_element_type=f32),
                      0.0).astype(bf16) for s in hsl] for c in chunks]
    o_intra = [[jnp.dot(att[c][h], vb[c][:, hsl[h]], preferred_element_type=f32) for h in heads]
               for c in chunks]
    d_state = [[lax.dot_general(vb[c][:, s], k_out[c][:, s], tn, preferred_element_type=f32) for s in hsl]
               for c in chunks]

    st = [st_ref[h] for h in heads]
    st_in = [None] * n_ch
    for c in (reversed(chunks) if reverse else chunks):
        st_in[c] = [s.astype(bf16) for s in st]
        st = [st[h] * dec[c][:, hsl[h]] + d_state[c][h] for h in heads]
    for h in heads:
        st_ref[h] = st[h]
        fin_ref[0, h] = st[h]

    for c in chunks:
        o_ref[0, rows[c], :] = jnp.concatenate(
            [o_intra[c][h] + lax.dot_general(q_in[c][:, hsl[h]], st_in[c][h], nt, preferred_element_type=f32)
             for h in heads], axis=1)


def gla_dir(p, gd, w_up, b_up, s0, reverse):
    nb, n_tok, _ = p.shape
    tb = min(GLA_BLOCK, n_tok)
    nblk = n_tok // tb
    blk = (lambda k: nblk - 1 - k) if reverse else (lambda k: k)
    col = lambda c: pl.BlockSpec((1, tb, W_MIX), lambda b, k: (b, blk(k), c))
    st_shape = (GLA_HEADS, GLA_DV, GLA_DK)
    return pl.pallas_call(
        functools.partial(_gla_kernel, reverse=reverse, tb=tb),
        grid=(nb, nblk),
        in_specs=[col(P_Q), col(P_Q + 1), col(P_Q + 2),
                  pl.BlockSpec((1, tb, GLA_RANK), lambda b, k: (b, blk(k), 0)),
                  pl.BlockSpec((GLA_RANK, W_MIX), lambda b, k: (0, 0)),
                  pl.BlockSpec((1, W_MIX), lambda b, k: (0, 0)),
                  pl.BlockSpec((1,) + st_shape, lambda b, k: (b, 0, 0, 0))],
        out_specs=[pl.BlockSpec((1, tb, W_MIX), lambda b, k: (b, blk(k), 0)),
                   pl.BlockSpec((1,) + st_shape, lambda b, k: (b, 0, 0, 0))],
        out_shape=[jax.ShapeDtypeStruct((nb, n_tok, W_MIX), jnp.float32),
                   jax.ShapeDtypeStruct((nb,) + st_shape, jnp.float32)],
        scratch_shapes=[pltpu.VMEM(st_shape, jnp.float32)],
        compiler_params=pltpu.CompilerParams(dimension_semantics=("arbitrary", "arbitrary"),
                                             vmem_limit_bytes=VMEM_LIMIT_BYTES),
        name="gla",
    )(p, p, p, gd, w_up, b_up.reshape(1, W_MIX), s0)


FFN_TM = 512
CAST_ROWS = 256


def _swiglu_kernel(te_ref, nu_ref, x_ref, wg_ref, wu_ref, wd_ref, *rest, nf, residual):
    del te_ref
    o_ref, acc_ref = rest[-2:]
    j = pl.program_id(1)

    @pl.when(pl.program_id(0) < nu_ref[0])
    def _():
        x = x_ref[...]
        g = jnp.dot(x, wg_ref[0], preferred_element_type=jnp.float32)
        u = jnp.dot(x, wu_ref[0], preferred_element_type=jnp.float32)
        a = (g * jax.nn.sigmoid(g) * u).astype(jnp.bfloat16)
        part = jnp.dot(a, wd_ref[0], preferred_element_type=jnp.float32)

        @pl.when(j == 0)
        def _():
            acc_ref[...] = part

        @pl.when(j > 0)
        def _():
            acc_ref[...] += part

        @pl.when(j == nf - 1)
        def _():
            if residual:
                xres_ref, gm_ref = rest[:2]
                o_ref[...] = (xres_ref[...] + gm_ref[0] * acc_ref[...]).astype(o_ref.dtype)
            else:
                o_ref[...] = acc_ref[...].astype(o_ref.dtype)


def _cast_split_kernel(a_ref, b_ref, oa_ref, ob_ref):
    oa_ref[...] = a_ref[...].astype(oa_ref.dtype)
    ob_ref[...] = b_ref[...].astype(ob_ref.dtype)


def cast_split_bf16(w):
    e, r, c2 = w.shape
    c = c2 // 2
    rows = min(CAST_ROWS, r)
    half = lambda h: pl.BlockSpec((1, rows, c), lambda i, j: (i, j, h))
    out = jax.ShapeDtypeStruct((e, r, c), jnp.bfloat16)
    return pl.pallas_call(
        _cast_split_kernel,
        grid=(e, r // rows),
        in_specs=[half(0), half(1)],
        out_specs=[half(0), half(0)],
        out_shape=[out, out],
        compiler_params=pltpu.CompilerParams(dimension_semantics=("arbitrary", "arbitrary"),
                                             vmem_limit_bytes=VMEM_LIMIT_BYTES),
        name="cast_split_bf16",
    )(w, w)


def _cast_rows_kernel(a_ref, b_ref, o_ref):
    o_ref[0, 0] = a_ref[0, 0].astype(o_ref.dtype)
    o_ref[0, 1] = b_ref[0, 0].astype(o_ref.dtype)


def cast_rows_bf16(w):
    e, r, c = w.shape
    rows = min(CAST_ROWS, r // 2)
    w4 = w.reshape(e, 2, r // 2, c)
    half = lambda h: pl.BlockSpec((1, 1, rows, c), lambda i, j: (i, h, j, 0))
    out = pl.pallas_call(
        _cast_rows_kernel,
        grid=(e, r // 2 // rows),
        in_specs=[half(0), half(1)],
        out_specs=pl.BlockSpec((1, 2, rows, c), lambda i, j: (i, 0, j, 0)),
        out_shape=jax.ShapeDtypeStruct(w4.shape, jnp.bfloat16),
        compiler_params=pltpu.CompilerParams(dimension_semantics=("arbitrary", "arbitrary"),
                                             vmem_limit_bytes=VMEM_LIMIT_BYTES),
        name="cast_rows_bf16",
    )(w4, w4)
    return out.reshape(e, r, c)


def grouped_swiglu(tile_expert, n_used, xs, w_g, w_u, w_down, nf, residual=None, out_dtype=jnp.float32):
    m, d = xs.shape
    f = w_down.shape[1]
    tf = f // nf
    n_tiles = m // FFN_TM
    in_specs = [pl.BlockSpec((FFN_TM, d), lambda t, j, te, nu: (t, 0)),
                pl.BlockSpec((1, d, tf), lambda t, j, te, nu: (te[t], 0, j)),
                pl.BlockSpec((1, d, tf), lambda t, j, te, nu: (te[t], 0, j)),
                pl.BlockSpec((1, tf, d), lambda t, j, te, nu: (te[t], j, 0))]
    extra = ()
    if residual is not None:
        rows_per_mod = m // residual[1].shape[0]
        in_specs += [pl.BlockSpec((FFN_TM, d), lambda t, j, te, nu: (t, 0)),
                     pl.BlockSpec((1, 1, d), lambda t, j, te, nu: ((t * FFN_TM) // rows_per_mod, 0, 0))]
        extra = tuple(residual)
    grid_spec = pltpu.PrefetchScalarGridSpec(
        num_scalar_prefetch=2,
        grid=(n_tiles, nf),
        in_specs=in_specs,
        out_specs=pl.BlockSpec((FFN_TM, d), lambda t, j, te, nu: (t, 0)),
        scratch_shapes=[pltpu.VMEM((FFN_TM, d), jnp.float32)])
    return pl.pallas_call(
        functools.partial(_swiglu_kernel, nf=nf, residual=residual is not None),
        grid_spec=grid_spec,
        out_shape=jax.ShapeDtypeStruct((m, d), out_dtype),
        compiler_params=pltpu.CompilerParams(
            dimension_semantics=("arbitrary", "arbitrary"),
            vmem_limit_bytes=VMEM_LIMIT_BYTES),
        name="grouped_swiglu",
    )(tile_expert, n_used, xs, w_g, w_u, w_down, *extra)


ROW_TILE = 512
P_HY, P_Q, P_OG, P_S5, P_RGX, P_RGG = 0, 3, 6, 7, 8, 9
P_GDN = 10 * W_MIX
P_COLS = P_GDN + 128


def _rms_mod(x, g, shift, scale):
    y = x * lax.rsqrt(jnp.mean(x * x, axis=-1, keepdims=True) + EPS) * g
    return y * (1.0 + scale) + shift


def _front_kernel(x_ref, sh_ref, sc_ref, g_ref, w_ref, o_ref):
    h = _rms_mod(x_ref[...], g_ref[...], sh_ref[0], sc_ref[0])
    o_ref[...] = jnp.dot(h.astype(jnp.bfloat16), w_ref[...], preferred_element_type=jnp.float32)


def _mod_spec(rows_per_mod):
    return pl.BlockSpec((1, 1, D_MODEL), lambda i: ((i * ROW_TILE) // rows_per_mod, 0, 0))


def front(x, shift, scale, g, w):
    m = x.shape[0]
    n_mod = shift.shape[0]
    n_out = w.shape[1]
    mod = _mod_spec(m // n_mod)
    return pl.pallas_call(
        _front_kernel,
        grid=(m // ROW_TILE,),
        in_specs=[pl.BlockSpec((ROW_TILE, D_MODEL), lambda i: (i, 0)), mod, mod,
                  pl.BlockSpec((1, D_MODEL), lambda i: (0, 0)),
                  pl.BlockSpec((D_MODEL, n_out), lambda i: (0, 0))],
        out_specs=pl.BlockSpec((ROW_TILE, n_out), lambda i: (i, 0)),
        out_shape=jax.ShapeDtypeStruct((m, n_out), jnp.float32),
        compiler_params=pltpu.CompilerParams(dimension_semantics=("arbitrary",),
                                             vmem_limit_bytes=VMEM_LIMIT_BYTES),
        name="front",
    )(x, shift, scale, g.reshape(1, D_MODEL), w)


MERGE_TILE = 256
ROUTER_PAD = 128
N_MERGE_IN = 25


def _merge_kernel(*refs, with_router):
    (x_ref, sh1_ref, sc1_ref, gm_ref, sh2_ref, sc2_ref, g1_ref, g2_ref,
     of_ref, ob_ref, og_ref, s5_ref, hc_ref, hz_ref, hx0_ref, rf_ref, rb_ref, rgg_ref,
     wg_ref, wb_ref, wo_ref, wglu_ref, bglu_ref, havg_ref, hbias_ref) = refs[:N_MERGE_IN]
    f32, bf16 = jnp.float32, jnp.bfloat16
    hp = lax.Precision.HIGHEST
    x = x_ref[...]
    hb = _rms_mod(x, g1_ref[...], sh1_ref[0], sc1_ref[0]).astype(bf16)

    o = of_ref[...] + ob_ref[...]
    ms = jnp.dot(o * o, havg_ref[...], precision=hp, preferred_element_type=f32)
    og = og_ref[...]
    gla = o * lax.rsqrt(ms + EPS) * (og * jax.nn.sigmoid(og))
    g5 = jax.nn.gelu(s5_ref[...])
    s5o = g5 * jax.nn.sigmoid(jnp.dot(g5.astype(bf16), wglu_ref[...], preferred_element_type=f32)
                              + bglu_ref[...])
    rgo = (rf_ref[...] + rb_ref[...]) * jax.nn.gelu(rgg_ref[...])
    hyo = hx0_ref[...] * (hc_ref[...] + hz_ref[...] * hbias_ref[...])
    branches = (gla, s5o, hyo, rgo)

    y = None
    for k in range(N_BRANCH):
        gate = jax.nn.sigmoid(jnp.dot(hb, wg_ref[:, k * D_MODEL:(k + 1) * D_MODEL],
                                      preferred_element_type=f32))
        t = gate * jnp.dot(branches[k].astype(bf16), wb_ref[k], preferred_element_type=f32)
        y = t if y is None else y + t
    out = jnp.dot(y.astype(bf16), wo_ref[...], preferred_element_type=f32)
    xn = x + gm_ref[0] * out
    h2 = _rms_mod(xn, g2_ref[...], sh2_ref[0], sc2_ref[0])
    if with_router:
        rw_ref, rb2_ref, xo_ref, h2_ref, lg_ref = refs[N_MERGE_IN:]
        lg_ref[...] = _dot3(*_split_bf16(h2), *_split_bf16(rw_ref[...])) + rb2_ref[...]
    else:
        xo_ref, h2_ref = refs[N_MERGE_IN:]
    xo_ref[...] = xn
    h2_ref[...] = h2.astype(bf16)


def merge(x, mods, g1, g2, o_f, o_b, p, s5y, hy, r_f, r_b, wg, wb, wo, wglu, bglu, hbias, router=None):
    m = x.shape[0]
    tm = MERGE_TILE
    n_mod = mods[0].shape[0]
    rows_per_mod = m // n_mod
    mod = pl.BlockSpec((1, 1, D_MODEL), lambda i: ((i * tm) // rows_per_mod, 0, 0))
    row = pl.BlockSpec((tm, D_MODEL), lambda i: (i, 0))
    br = pl.BlockSpec((tm, W_MIX), lambda i: (i, 0))
    pcol = lambda c: pl.BlockSpec((tm, W_MIX), lambda i: (i, c))
    full = lambda a: pl.BlockSpec(a.shape, lambda i: (0,) * a.ndim)
    head = jnp.arange(W_MIX) // GLA_DV
    havg = (head[:, None] == head[None, :]).astype(jnp.float32) / GLA_DV
    vec = lambda v: v.reshape(1, -1)
    consts = [wg, wb, wo, wglu, vec(bglu), havg, vec(hbias)]
    out_specs = [row, pl.BlockSpec((tm, D_MODEL), lambda i: (i, 0))]
    out_shape = [jax.ShapeDtypeStruct((m, D_MODEL), jnp.float32),
                 jax.ShapeDtypeStruct((m, D_MODEL), jnp.bfloat16)]
    if router is not None:
        rw, rbias = router
        pad = ROUTER_PAD - rw.shape[1]
        consts += [jnp.pad(rw, ((0, 0), (0, pad))), jnp.pad(rbias, (0, pad)).reshape(1, -1)]
        out_specs.append(pl.BlockSpec((tm, ROUTER_PAD), lambda i: (i, 0)))
        out_shape.append(jax.ShapeDtypeStruct((m, ROUTER_PAD), jnp.float32))
    in_specs = ([row] + [mod] * 5 + [full(vec(g1)), full(vec(g2)), br, br, pcol(P_OG), br, br, br, br, br, br,
                                     pcol(P_RGG)] + [full(a) for a in consts])
    return pl.pallas_call(
        functools.partial(_merge_kernel, with_router=router is not None),
        grid=(m // tm,),
        in_specs=in_specs,
        out_specs=out_specs,
        out_shape=out_shape,
        compiler_params=pltpu.CompilerParams(dimension_semantics=("arbitrary",),
                                             vmem_limit_bytes=VMEM_LIMIT_BYTES),
        name="merge",
    )(x, *mods, vec(g1), vec(g2), o_f, o_b, p, s5y, *hy, r_f, r_b, p, *consts)


def _combine_kernel(x_ref, y0_ref, y1_ref, w_ref, gm_ref, g_ref, o_ref, *, final_norm):
    w = w_ref[...]
    y = w[:, 0:1] * y0_ref[...].astype(jnp.float32) + w[:, 1:2] * y1_ref[...].astype(jnp.float32)
    xn = x_ref[...] + gm_ref[0] * y
    if final_norm:
        xn = xn * lax.rsqrt(jnp.mean(xn * xn, axis=-1, keepdims=True) + EPS) * g_ref[...]
    o_ref[...] = xn


def moe_combine(x, yk, w, gate_mod, final_g):
    m = x.shape[0]
    n_mod = gate_mod.shape[0]
    g = jnp.ones((1, D_MODEL), jnp.float32) if final_g is None else final_g.reshape(1, D_MODEL)
    return pl.pallas_call(
        functools.partial(_combine_kernel, final_norm=final_g is not None),
        grid=(m // ROW_TILE,),
        in_specs=[pl.BlockSpec((ROW_TILE, D_MODEL), lambda i: (i, 0)),
                  pl.BlockSpec((ROW_TILE, D_MODEL), lambda i: (i, 0)),
                  pl.BlockSpec((ROW_TILE, D_MODEL), lambda i: (i + m // ROW_TILE, 0)),
                  pl.BlockSpec((ROW_TILE, TOP_K), lambda i: (i, 0)),
                  _mod_spec(m // n_mod),
                  pl.BlockSpec((1, D_MODEL), lambda i: (0, 0))],
        out_specs=pl.BlockSpec((ROW_TILE, D_MODEL), lambda i: (i, 0)),
        out_shape=jax.ShapeDtypeStruct((m, D_MODEL), jnp.float32),
        compiler_params=pltpu.CompilerParams(dimension_semantics=("arbitrary",)),
        name="moe_combine",
    )(x, yk, yk, w, gate_mod, g)


HY_LANES = 128
HY_STEP_ROWS = 1024
HY_MIN_LEN = 1024


def _hyena_pre_kernel(x_ref, prev_ref, next_ref, w_ref, zt_ref, z_ref, x0_ref, *, tb):
    k = pl.program_id(1)
    prev = jnp.where(k > 0, prev_ref[0], 0.0)
    nxt = jnp.where(k < pl.num_programs(1) - 1, next_ref[0], 0.0)
    ext = jnp.concatenate([prev, x_ref[0], nxt], axis=0)
    pc = sum(ext[RG_HALO - 1 + j:RG_HALO - 1 + j + tb, :] * w_ref[j:j + 1, :] for j in range(HY_SHORT))
    v, x0, x1 = pc[:, :W_MIX], pc[:, W_MIX:2 * W_MIX], pc[:, 2 * W_MIX:]
    z = x1 * v
    z_ref[0] = z
    x0_ref[0] = x0
    zt_ref[0] = z.T


def hyena_pre(p, w_short):
    nb, n_tok, _ = p.shape
    tb = min(ROW_TILE, n_tok)
    hb = tb // RG_HALO
    n_halo = n_tok // RG_HALO
    wide = 3 * W_MIX
    tok = pl.BlockSpec((1, tb, W_MIX), lambda b, k: (b, k, 0))
    tok_out = jax.ShapeDtypeStruct((nb, n_tok, W_MIX), jnp.float32)
    return pl.pallas_call(
        functools.partial(_hyena_pre_kernel, tb=tb),
        grid=(nb, n_tok // tb),
        in_specs=[pl.BlockSpec((1, tb, wide), lambda b, k: (b, k, P_HY)),
                  pl.BlockSpec((1, RG_HALO, wide), lambda b, k: (b, jnp.maximum(k * hb - 1, 0), P_HY)),
                  pl.BlockSpec((1, RG_HALO, wide), lambda b, k: (b, jnp.minimum((k + 1) * hb, n_halo - 1), P_HY)),
                  pl.BlockSpec(w_short.shape, lambda b, k: (0, 0))],
        out_specs=[pl.BlockSpec((1, W_MIX, tb), lambda b, k: (b, 0, k)), tok, tok],
        out_shape=[jax.ShapeDtypeStruct((nb, W_MIX, n_tok), jnp.float32), tok_out, tok_out],
        compiler_params=pltpu.CompilerParams(dimension_semantics=("arbitrary", "arbitrary"),
                                             vmem_limit_bytes=VMEM_LIMIT_BYTES),
        name="hyena_pre",
    )(p, p, p, w_short)


def _split_bf16(a):
    hi = a.astype(jnp.bfloat16)
    lo = (a - hi.astype(jnp.float32)).astype(jnp.bfloat16)
    return hi, lo


def _dot3(a_hi, a_lo, b_hi, b_lo):
    d = functools.partial(jnp.dot, preferred_element_type=jnp.float32)
    return d(a_hi, b_hi) + (d(a_lo, b_hi) + d(a_hi, b_lo))


def _hyena_dft_consts(n1):
    n = n1 * HY_LANES
    ka = np.arange(n1, dtype=np.float64)[:, None]
    f1_ang = 2.0 * np.pi * ka * np.arange(n1 // 2, dtype=np.float64)[None, :] / n1
    f1r, f1i = np.cos(f1_ang), -np.sin(f1_ang)
    tw_ang = 2.0 * np.pi * ka * np.arange(HY_LANES, dtype=np.float64)[None, :] / n
    lo = np.arange(HY_LANES, dtype=np.float64)
    f2_ang = 2.0 * np.pi * lo[:, None] * lo[None, :] / HY_LANES
    f2r, f2i = np.cos(f2_ang), -np.sin(f2_ang)
    fwd_rows = np.concatenate([f1r, f1i], axis=0)
    fwd_lanes = np.block([[f2r, f2i], [-f2i, f2r]])
    inv_lanes = np.block([[f2r, -f2i], [f2i, f2r]])
    inv_rows = np.concatenate([f1r.T, f1i.T], axis=1) / n
    out = []
    for m in (fwd_rows, fwd_lanes, inv_lanes, inv_rows):
        m32 = jnp.asarray(m, jnp.float32)
        out.extend(_split_bf16(m32))
    return out + [jnp.asarray(np.cos(tw_ang), jnp.float32), jnp.asarray(-np.sin(tw_ang), jnp.float32)]


def _hyena_fft_kernel(*refs, n1, ns, spectrum):
    bf16 = jnp.bfloat16
    if spectrum:
        z_ref, f1h, f1l, f2h, f2l, twr_ref, twi_ref, o_ref = refs
        rows_dft = lambda t: _dot3(f1h[...], f1l[...], *_split_bf16(t))
        lanes_dft = lambda t: _dot3(*_split_bf16(t), f2h[...], f2l[...])
    else:
        z_ref, hf_ref, f1h, f2h, g2h, fih, twr_ref, twi_ref, o_ref = refs
        mm = lambda a, b: jnp.dot(a.astype(bf16), b.astype(bf16), preferred_element_type=jnp.float32)
        rows_dft = lambda t: mm(f1h[...], t)
        lanes_dft = lambda t: mm(t, f2h[...])
    w = HY_LANES
    twr, twi = twr_ref[...], twi_ref[...]
    z2 = jnp.concatenate([z_ref[0, s] for s in range(ns)], axis=1)
    a2 = rows_dft(z2)
    rows = []
    for s in range(ns):
        r, i = a2[:n1, s * w:(s + 1) * w], a2[n1:, s * w:(s + 1) * w]
        rows.append(jnp.concatenate([r * twr - i * twi, r * twi + i * twr], axis=1))
    x = lanes_dft(jnp.concatenate(rows, axis=0))
    if spectrum:
        o_ref[...] = x.reshape(ns, n1, 2 * w)
        return
    h = hf_ref[...].reshape(ns * n1, 2 * w)
    xr, xi, hr, hi = x[:, :w], x[:, w:], h[:, :w], h[:, w:]
    y = jnp.concatenate([xr * hr - xi * hi, xr * hi + xi * hr], axis=1)
    g = mm(y, g2h[...])
    cr, ci = [], []
    for s in range(ns):
        gr, gi = g[s * n1:(s + 1) * n1, :w], g[s * n1:(s + 1) * n1, w:]
        cr.append(gr * twr + gi * twi)
        ci.append(gi * twr - gr * twi)
    gc = jnp.concatenate([jnp.concatenate(cr, axis=1), jnp.concatenate(ci, axis=1)], axis=0)
    y2 = mm(fih[...], gc)
    for s in range(ns):
        o_ref[0, s] = y2[:, s * w:(s + 1) * w]


def hyena_fft(zt, hf=None):
    nb, ch, half, w = zt.shape
    n1 = 2 * half
    ns = HY_STEP_ROWS // n1
    f1h, f1l, f2h, f2l, g2h, _, fih, _, twr, twi = _hyena_dft_consts(n1)
    full = lambda a: pl.BlockSpec(a.shape, lambda b, c: (0,) * a.ndim)
    zspec = pl.BlockSpec((1, ns, half, w), lambda b, c: (b, c, 0, 0))
    if hf is None:
        consts = [f1h, f1l, f2h, f2l, twr, twi]
        in_specs, args = [zspec], [zt]
        out_spec = pl.BlockSpec((ns, n1, 2 * w), lambda b, c: (b * (ch // ns) + c, 0, 0))
        out_shape = jax.ShapeDtypeStruct((nb * ch, n1, 2 * w), jnp.float32)
    else:
        consts = [f1h, f2h, g2h, fih, twr, twi]
        in_specs = [zspec, pl.BlockSpec((ns, n1, 2 * w), lambda b, c: (c, 0, 0))]
        args = [zt, hf]
        out_spec = zspec
        out_shape = jax.ShapeDtypeStruct(zt.shape, jnp.float32)
    return pl.pallas_call(
        functools.partial(_hyena_fft_kernel, n1=n1, ns=ns, spectrum=hf is None),
        grid=(nb, ch // ns),
        in_specs=in_specs + [full(a) for a in consts],
        out_specs=out_spec,
        out_shape=out_shape,
        compiler_params=pltpu.CompilerParams(dimension_semantics=("arbitrary", "arbitrary"),
                                             vmem_limit_bytes=VMEM_LIMIT_BYTES),
        name="hyena_fft",
    )(*args, *consts)


def rmsnorm(x, g):
    y = x * lax.rsqrt(jnp.mean(x * x, axis=-1, keepdims=True) + EPS)
    return y * g


def adaln(cond, w, b):
    return jax.nn.silu(cond) @ w + b


def grid_pos_embed(n_tokens, dim):
    rows = n_tokens // GRID_W
    q = dim // 4
    omega = 1.0 / (10000.0 ** (jnp.arange(q, dtype=jnp.float32) / q))
    r = jnp.arange(rows, dtype=jnp.float32)[:, None] * omega
    cc = jnp.arange(GRID_W, dtype=jnp.float32)[:, None] * omega
    er = jnp.concatenate([jnp.sin(r), jnp.cos(r)], axis=-1)
    ec = jnp.concatenate([jnp.sin(cc), jnp.cos(cc)], axis=-1)
    emb = jnp.concatenate([jnp.broadcast_to(er[:, None], (rows, GRID_W, dim // 2)),
                           jnp.broadcast_to(ec[None], (rows, GRID_W, dim // 2))], axis=-1)
    return emb.reshape(rows * GRID_W, dim)


def gla_mixer(p, w_up, b_up, s0):
    gdn = p[..., P_GDN:P_GDN + 2 * GLA_RANK]
    outs, finals = [], []
    for d in range(2):
        od, sd = gla_dir(p, gdn[..., d * GLA_RANK:(d + 1) * GLA_RANK], w_up[d], b_up[d], s0[d], d == 1)
        outs.append(od)
        finals.append(sd)
    return outs, jnp.stack(finals)


def s5_mixer(u, prep, s0):
    wd, wk, wc, a_t = prep
    b_, n_tok, _ = u.shape
    n = n_tok // S5_T
    u2 = u.reshape(b_, n, S5_T * W_MIX).transpose(1, 0, 2).reshape(n * b_, S5_T * W_MIX)
    dmat = pmm(u2, wd)
    hmat, fin = s5_scan(dmat.reshape(n, b_, 4 * S5_NS), s0, a_t)
    y2 = pmm_multi([u2, hmat.reshape(n * b_, 4 * S5_NS)], [wk, wc], (False, True))
    y = y2.reshape(n, b_, S5_T, W_MIX).transpose(1, 0, 2, 3).reshape(b_, n_tok, W_MIX)
    return y, fin


def hyena_filters(n_tok, w1, b1, w2, b2, w3, freq):
    f32 = jnp.float32
    t = jnp.arange(n_tok, dtype=f32)[:, None]
    bands = jnp.linspace(1e-4, HY_BANDS - 1, HY_BANDS, dtype=f32)[None]
    ang = 2.0 * math.pi * bands * t / n_tok
    z = jnp.concatenate([t / n_tok, jnp.cos(ang), jnp.sin(ang)], axis=-1)
    hp = lax.Precision.HIGHEST
    h = jnp.sin(freq * (jnp.dot(z, w1, precision=hp) + b1))
    h = jnp.sin(freq * (jnp.dot(h, w2, precision=hp) + b2))
    h = jnp.dot(h, w3, precision=hp)
    t01 = t / max(n_tok - 1, 1)
    deltas = jnp.abs(jnp.linspace(math.log(HY_TARGET) / HY_DECAY_SHORT,
                                  math.log(HY_TARGET) / HY_DECAY_LONG, W_MIX, dtype=f32))
    h = h * jnp.exp(-t01 * jnp.tile(deltas, 2))
    return h / (jnp.sum(jnp.abs(h), axis=0, keepdims=True) + EPS)


def hyena_mixer(p, w_short, w1, b1, w2, b2, w3, freq):
    nb, n_tok, _ = p.shape
    zt, z, x0 = hyena_pre(p, w_short)
    n_pad = max(n_tok, HY_MIN_LEN)
    half = n_pad // HY_LANES

    def frames(t):
        t = jnp.pad(t, [(0, 0)] * (t.ndim - 1) + [(0, n_pad - n_tok)])
        return t.reshape(t.shape[:-1] + (half, HY_LANES))

    filt = hyena_filters(n_tok, w1, b1, w2, b2, w3, freq)
    spec = hyena_fft(frames(filt.T)[None])
    sf, sb = spec[:W_MIX], spec[W_MIX:]
    hfreq = jnp.concatenate([sf[..., :HY_LANES] + sb[..., :HY_LANES],
                             sf[..., HY_LANES:] - sb[..., HY_LANES:]], axis=-1)
    conv = hyena_fft(frames(zt), hfreq)
    conv = conv.reshape(nb, W_MIX, n_pad)[:, :, :n_tok].transpose(0, 2, 1)
    return conv, z, x0


def rglru_mixer(p, w_conv, w_a, b_a, w_x, b_x, lam, s0):
    a_f, b_f, a_b, b_b = rg_pre(p, w_conv, w_a, b_a, w_x, b_x, lam)
    h_f, fin_f = rg_scan(a_f, b_f, s0[0], False)
    h_b, fin_b = rg_scan(a_b, b_b, s0[1], True)
    return [h_f, h_b], jnp.stack([fin_f, fin_b])


def ffn_dense(h2, x, gate_mod, w_gu, w_down):
    n_tiles = h2.shape[0] // FFN_TM
    d_ff = w_down.shape[0]
    return grouped_swiglu(jnp.zeros((n_tiles,), jnp.int32), jnp.full((1,), n_tiles, jnp.int32), h2,
                          w_gu[None, :, :d_ff].astype(jnp.bfloat16), w_gu[None, :, d_ff:].astype(jnp.bfloat16),
                          w_down[None].astype(jnp.bfloat16), nf=2,
                          residual=(x, gate_mod))


def ffn_moe(h2, logits, x, gate_mod, w_gu, w_down, final_g):
    n_tok = h2.shape[0]
    n_slot = TOP_K * n_tok
    top_v, top_i = lax.top_k(logits, TOP_K)
    w = jax.nn.softmax(top_v, axis=-1)
    e_flat = top_i.T.reshape(-1).astype(jnp.int32)
    onehot = (e_flat[:, None] == jnp.arange(N_EXPERTS, dtype=jnp.int32)[None]).astype(jnp.int32)
    csum = jnp.cumsum(onehot, axis=0)
    cnt = csum[-1]
    rank = jnp.sum(csum * onehot, axis=1) - 1
    padded = ((cnt + FFN_TM - 1) // FFN_TM) * FFN_TM
    ends = jnp.cumsum(padded)
    dest = (ends - padded)[e_flat] + rank
    n_rows = n_slot + N_EXPERTS * FFN_TM
    n_tiles = n_rows // FFN_TM
    tile_start = jnp.arange(n_tiles, dtype=jnp.int32) * FFN_TM
    tile_expert = jnp.minimum(jnp.sum((tile_start[:, None] >= ends[None, :]).astype(jnp.int32), axis=1),
                              N_EXPERTS - 1)
    n_used = (ends[-1:] // FFN_TM).astype(jnp.int32)
    order = jnp.argsort(e_flat, stable=True).astype(jnp.int32)
    row_expert = jnp.repeat(tile_expert, FFN_TM)
    shift = (ends - padded) - (jnp.cumsum(cnt) - cnt)
    q = jnp.arange(n_rows, dtype=jnp.int32) - shift[row_expert]
    src = order[jnp.clip(q, 0, n_slot - 1)] % n_tok
    xs = h2.at[src].get(mode="promise_in_bounds")
    wg, wu = cast_split_bf16(w_gu)
    ys = grouped_swiglu(tile_expert, n_used, xs, wg, wu, cast_rows_bf16(w_down), nf=2,
                        out_dtype=jnp.bfloat16)
    yk = ys.at[dest].get(mode="promise_in_bounds")
    return moe_combine(x, yk, w, gate_mod, final_g)


def kernel(x, c, ctx, c_ctx, mod_w, mod_b, norm1_g, norm2_g, w_in, gla_w_up, gla_b_up,
           s5_lam_re, s5_lam_im, s5_log_dt, s5_b_re, s5_b_im, s5_c_re, s5_c_im, s5_d,
           s5_w_glu, s5_b_glu, hy_w_short, hy_w1, hy_b1, hy_w2, hy_b2, hy_w3, hy_freq,
           hy_bias, rg_w_conv, rg_w_a, rg_b_a, rg_w_x, rg_b_x, rg_lam, w_branch, w_out,
           ffn_w_gu, ffn_w_down, moe_router, moe_router_b, moe_w_gu, moe_w_down, final_g):
    f32, bf16 = jnp.float32, jnp.bfloat16
    n_b, n_lat, _ = x.shape
    n_ctx = ctx.shape[1]
    xs = (x + grid_pos_embed(n_lat, D_MODEL)[None]).reshape(n_b * n_lat, D_MODEL)
    cs = ctx.reshape(n_b * n_ctx, D_MODEL)
    offs = [0]
    for n in IN_SIZES:
        offs.append(offs[-1] + n)
    for l in range(DEPTH):
        last = l == DEPTH - 1
        dense = l % 2 == 0
        j = l // 2
        m_lat = [t[:, None, :] for t in jnp.split(adaln(c, mod_w[l], mod_b[l]), 6, axis=-1)]
        m_ctx = [t[None, None, :] for t in jnp.split(adaln(c_ctx, mod_w[l], mod_b[l]), 6, axis=-1)]
        wl = w_in[l]
        w_mix = jnp.concatenate([wl[:, offs[6]:offs[7]], wl[:, offs[0]:offs[4]], wl[:, offs[5]:offs[6]],
                                 wl[:, offs[7]:offs[9]], wl[:, offs[4]:offs[5]],
                                 jnp.zeros((D_MODEL, P_COLS - P_GDN - 2 * GLA_RANK), f32)], axis=1).astype(bf16)
        w_gate = wl[:, offs[9]:offs[10]].astype(bf16)
        wb, wo, wglu = w_branch[l].astype(bf16), w_out[l].astype(bf16), s5_w_glu[l].astype(bf16)
        s5_ops = s5_prepare(s5_lam_re[l], s5_lam_im[l], s5_log_dt[l], s5_b_re[l], s5_b_im[l],
                            s5_c_re[l], s5_c_im[l], s5_d[l])
        hy_p = (hy_w_short[l], hy_w1[l], hy_b1[l], hy_w2[l], hy_b2[l], hy_w3[l], hy_freq[l])
        rg_p = (rg_w_conv[l], rg_w_a[l], rg_b_a[l], rg_w_x[l], rg_b_x[l], rg_lam[l])
        router = None if dense else (moe_router[j], moe_router_b[j])

        def mixers(p2, n_tok, states, with_hyena):
            p = p2.reshape(n_b, n_tok, P_COLS)
            blk = lambda i, n=1: p[..., i * W_MIX:(i + n) * W_MIX]
            flat = lambda t: t.reshape(n_b * n_tok, W_MIX)
            gla_o, gla_s = gla_mixer(p, gla_w_up[l], gla_b_up[l], states[0])
            s5_y, s5_s = s5_mixer(blk(P_S5), s5_ops, states[1])
            rg_o, rg_s = rglru_mixer(p, *rg_p, states[2])
            hy = tuple(flat(t) for t in hyena_mixer(p, *hy_p)) if with_hyena else None
            return ((flat(gla_o[0]), flat(gla_o[1]), flat(s5_y), hy, flat(rg_o[0]), flat(rg_o[1])),
                    (gla_s, s5_s, rg_s))

        def tail(stream, mods, p2, br, final):
            outs = merge(stream, tuple(mods[:5]), norm1_g[l], norm2_g[l], br[0], br[1], p2, br[2], br[3],
                         br[4], br[5], w_gate, wb, wo, wglu, s5_b_glu[l], hy_bias[l], router)
            if dense:
                return ffn_dense(outs[1], outs[0], mods[5], ffn_w_gu[j], ffn_w_down[j])
            return ffn_moe(outs[1], outs[2][:, :N_EXPERTS], outs[0], mods[5], moe_w_gu[j], moe_w_down[j],
                           final_g if final else None)

        zero_states = (jnp.zeros((2, n_b, GLA_HEADS, GLA_DV, GLA_DK), f32),
                       jnp.zeros((2, n_b, 2 * S5_NS), f32), jnp.zeros((2, n_b, W_MIX), f32))
        p_ctx = front(cs, m_ctx[0], m_ctx[1], norm1_g[l], w_mix)
        br_ctx, states = mixers(p_ctx, n_ctx, zero_states, not last)
        p_lat = front(xs, m_lat[0], m_lat[1], norm1_g[l], w_mix)
        br_lat, _ = mixers(p_lat, n_lat, states, True)
        xs = tail(xs, m_lat, p_lat, br_lat, last)
        if not last:
            cs = tail(cs, m_ctx, p_ctx, br_ctx, False)
    if (DEPTH - 1) % 2 == 0:
        xs = rmsnorm(xs, final_g)
    return xs.reshape(n_b, n_lat, D_MODEL)
```

```python
import functools
import math

import jax
import jax.numpy as jnp
import numpy as np
from jax import lax
from jax.experimental import pallas as pl
from jax.experimental.pallas import tpu as pltpu

D_MODEL = 1024
DEPTH = 2
GRID_W = 64
EPS = 1e-6
N_BRANCH = 4
W_MIX = D_MODEL // N_BRANCH
GLA_HEADS = 4
GLA_DK = W_MIX // GLA_HEADS
GLA_DV = W_MIX // GLA_HEADS
GLA_RANK = 16
GLA_TAU = 16.0
GLA_CHUNK = 64
S5_GROUP = 16
S5_GROUPS = W_MIX // S5_GROUP
S5_STATE = 64
S5_MAX_RE = -1e-4
HY_BANDS = 16
HY_SHORT = 3
HY_DECAY_SHORT = 0.3
HY_DECAY_LONG = 1.5
HY_TARGET = 1e-2
RG_BLOCKS = 4
RG_BLOCK = W_MIX // RG_BLOCKS
RG_CONV = 4
RG_C = 8.0
N_EXPERTS = 8
TOP_K = 2
IN_SIZES = (GLA_HEADS * GLA_DK, GLA_HEADS * GLA_DK, GLA_HEADS * GLA_DV, GLA_HEADS * GLA_DV,
            2 * GLA_RANK, W_MIX, 3 * W_MIX, W_MIX, W_MIX, N_BRANCH * D_MODEL)

VMEM_LIMIT_BYTES = 48 * 1024 * 1024


def _mm_kernel(x_ref, w_ref, o_ref):
    o_ref[...] = jnp.dot(x_ref[...].astype(jnp.bfloat16), w_ref[...],
                         preferred_element_type=jnp.float32)


def _pick_tile(n, cap):
    best = None
    for t in range(128, cap + 1, 128):
        if n % t == 0:
            best = t
    return best if best is not None else n


def pmm(x, w):
    lead = x.shape[:-1]
    k = x.shape[-1]
    n = w.shape[-1]
    x2 = x.reshape(-1, k)
    m = x2.shape[0]
    tm = 512 if m % 512 == 0 else m
    if k > 2048 and m % 256 == 0:
        tm = 256
    tn = n if k * n * 2 <= 6 * 1024 * 1024 else _pick_tile(n, 1024)
    out = pl.pallas_call(
        _mm_kernel,
        grid=(m // tm, n // tn),
        in_specs=[pl.BlockSpec((tm, k), lambda i, j: (i, 0)),
                  pl.BlockSpec((k, tn), lambda i, j: (0, j))],
        out_specs=pl.BlockSpec((tm, tn), lambda i, j: (i, j)),
        out_shape=jax.ShapeDtypeStruct((m, n), jnp.float32),
        compiler_params=pltpu.CompilerParams(
            dimension_semantics=("arbitrary", "arbitrary"),
            vmem_limit_bytes=VMEM_LIMIT_BYTES),
    )(x2, w.astype(jnp.bfloat16))
    return out.reshape(lead + (n,))


def _mm_multi_kernel(*refs, transposed):
    o_ref = refs[-1]
    n = (len(refs) - 1) // 2
    acc = None
    for i in range(n):
        dims = (((1,), (1 if transposed[i] else 0,)), ((), ()))
        t = lax.dot_general(refs[i][...].astype(jnp.bfloat16), refs[n + i][...], dims,
                            preferred_element_type=jnp.float32)
        acc = t if acc is None else acc + t
    o_ref[...] = acc


def pmm_multi(xs, ws, transposed, tm=256, tn=512):
    m = xs[0].shape[0]
    n = ws[0].shape[0 if transposed[0] else 1]
    tm = tm if m % tm == 0 else m
    tn = tn if n % tn == 0 else n
    in_specs = ([pl.BlockSpec((tm, x.shape[1]), lambda i, j: (i, 0)) for x in xs]
                + [pl.BlockSpec((tn, w.shape[1]), lambda i, j: (j, 0)) if t else
                   pl.BlockSpec((w.shape[0], tn), lambda i, j: (0, j)) for w, t in zip(ws, transposed)])
    return pl.pallas_call(
        functools.partial(_mm_multi_kernel, transposed=tuple(transposed)),
        grid=(m // tm, n // tn),
        in_specs=in_specs,
        out_specs=pl.BlockSpec((tm, tn), lambda i, j: (i, j)),
        out_shape=jax.ShapeDtypeStruct((m, n), jnp.float32),
        compiler_params=pltpu.CompilerParams(
            dimension_semantics=("arbitrary", "arbitrary"),
            vmem_limit_bytes=VMEM_LIMIT_BYTES),
    )(*xs, *[w.astype(jnp.bfloat16) for w in ws])


RG_SCAN_ROWS = 256


def _rg_scan_kernel(a_ref, b_ref, s0_ref, h_ref, fin_ref, st_ref, *, reverse, tb, nb):
    @pl.when(pl.program_id(0) == 0)
    def _():
        st_ref[...] = s0_ref[...]

    def body(r, hs):
        rr = (tb - 1 - r) if reverse else r
        out = []
        for i in range(nb):
            h = a_ref[i, pl.ds(rr, 1), :] * hs[i] + b_ref[i, pl.ds(rr, 1), :]
            h_ref[i, pl.ds(rr, 1), :] = h
            out.append(h)
        return tuple(out)

    hs = lax.fori_loop(0, tb, body, tuple(st_ref[i:i + 1, :] for i in range(nb)), unroll=8)
    for i in range(nb):
        st_ref[i:i + 1, :] = hs[i]
        fin_ref[i:i + 1, :] = hs[i]


def rg_scan(a, b, s0, reverse):
    nb, n_tok, ch = a.shape
    tb = min(RG_SCAN_ROWS, n_tok)
    nblk = n_tok // tb
    imap = (lambda k: (0, nblk - 1 - k, 0)) if reverse else (lambda k: (0, k, 0))
    return pl.pallas_call(
        functools.partial(_rg_scan_kernel, reverse=reverse, tb=tb, nb=nb),
        grid=(nblk,),
        in_specs=[pl.BlockSpec((nb, tb, ch), imap), pl.BlockSpec((nb, tb, ch), imap),
                  pl.BlockSpec((nb, ch), lambda k: (0, 0))],
        out_specs=[pl.BlockSpec((nb, tb, ch), imap), pl.BlockSpec((nb, ch), lambda k: (0, 0))],
        out_shape=[jax.ShapeDtypeStruct((nb, n_tok, ch), jnp.float32),
                   jax.ShapeDtypeStruct((nb, ch), jnp.float32)],
        scratch_shapes=[pltpu.VMEM((nb, ch), jnp.float32)],
        compiler_params=pltpu.CompilerParams(dimension_semantics=("arbitrary",)),
        name="rg_scan",
    )(a, b, s0)


RG_HALO = 8


def _rg_pre_kernel(x_ref, prev_ref, next_ref, wc_ref, wg_ref, bg_ref, c_ref, af_ref, bf_ref, ab_ref, bb_ref,
                   *, tb):
    k = pl.program_id(1)
    prev = jnp.where(k > 0, prev_ref[0], 0.0)
    nxt = jnp.where(k < pl.num_programs(1) - 1, next_ref[0], 0.0)
    ext = jnp.concatenate([prev, x_ref[0], nxt], axis=0)
    xc = sum(ext[RG_HALO - 2 + j:RG_HALO - 2 + j + tb, :] * wc_ref[j:j + 1, :] for j in range(RG_CONV))
    m = jnp.dot(xc.astype(jnp.bfloat16), wg_ref[...], preferred_element_type=jnp.float32) + bg_ref[...]
    for d, (a_ref, b_ref) in enumerate(((af_ref, bf_ref), (ab_ref, bb_ref))):
        r = jax.nn.sigmoid(m[:, (2 * d) * W_MIX:(2 * d + 1) * W_MIX])
        i = jax.nn.sigmoid(m[:, (2 * d + 1) * W_MIX:(2 * d + 2) * W_MIX])
        log_a = -r * c_ref[d:d + 1, :]
        a_ref[0] = jnp.exp(log_a)
        u = jnp.tanh(log_a)
        b_ref[0] = jnp.sqrt(-2.0 * u / (1.0 - u)) * (i * xc)


def rg_pre(p, w_conv, w_a, b_a, w_x, b_x, lam):
    nb, n_tok, _ = p.shape
    tb = min(ROW_TILE, n_tok)
    nblk = n_tok // tb
    hb = tb // RG_HALO
    n_halo = n_tok // RG_HALO
    col = P_RGX
    blockdiag = lambda w: jax.scipy.linalg.block_diag(*[w[i] for i in range(RG_BLOCKS)])
    wg = jnp.concatenate([blockdiag(w_a[0]), blockdiag(w_x[0]), blockdiag(w_a[1]), blockdiag(w_x[1])],
                         axis=1).astype(jnp.bfloat16)
    bg = jnp.concatenate([b_a[0].reshape(-1), b_x[0].reshape(-1), b_a[1].reshape(-1), b_x[1].reshape(-1)]
                         ).reshape(1, 4 * W_MIX)
    c = RG_C * jax.nn.softplus(-lam)
    blk = pl.BlockSpec((1, tb, W_MIX), lambda b, k: (b, k, 0))
    full = lambda a: pl.BlockSpec(a.shape, lambda b, k: (0,) * a.ndim)
    out = jax.ShapeDtypeStruct((nb, n_tok, W_MIX), jnp.float32)
    return pl.pallas_call(
        functools.partial(_rg_pre_kernel, tb=tb),
        grid=(nb, nblk),
        in_specs=[pl.BlockSpec((1, tb, W_MIX), lambda b, k: (b, k, col)),
                  pl.BlockSpec((1, RG_HALO, W_MIX), lambda b, k: (b, jnp.maximum(k * hb - 1, 0), col)),
                  pl.BlockSpec((1, RG_HALO, W_MIX), lambda b, k: (b, jnp.minimum((k + 1) * hb, n_halo - 1), col)),
                  full(w_conv), full(wg), full(bg), full(c)],
        out_specs=[blk, blk, blk, blk],
        out_shape=[out, out, out, out],
        compiler_params=pltpu.CompilerParams(dimension_semantics=("arbitrary", "arbitrary")),
        name="rg_pre",
    )(p, p, p, w_conv, wg, bg, c)


S5_T = 16
S5_NS = S5_GROUPS * S5_STATE
S5_SCAN_CHUNKS = 64


def _s5_scan_kernel(d_ref, s0_ref, a_ref, h_ref, fin_ref, st_ref, *, rc, nb):
    d = pl.program_id(0)

    @pl.when(pl.program_id(1) == 0)
    def _():
        st_ref[...] = s0_ref[0]

    ar = jnp.broadcast_to(a_ref[0, :, 0:S5_NS], (nb, S5_NS))
    ai = jnp.broadcast_to(a_ref[0, :, S5_NS:2 * S5_NS], (nb, S5_NS))

    def body(r, carry):
        hr, hi = carry
        rr = r + d * (rc - 1 - 2 * r)
        h_ref[rr, :, 0:S5_NS] = hr
        h_ref[rr, :, S5_NS:2 * S5_NS] = hi
        dr = d_ref[rr, :, 0:S5_NS]
        di = d_ref[rr, :, S5_NS:2 * S5_NS]
        return ar * hr - ai * hi + dr, ar * hi + ai * hr + di

    hr, hi = lax.fori_loop(0, rc, body, (st_ref[:, 0:S5_NS], st_ref[:, S5_NS:2 * S5_NS]))
    st_ref[:, 0:S5_NS] = hr
    st_ref[:, S5_NS:2 * S5_NS] = hi
    fin_ref[0, :, 0:S5_NS] = hr
    fin_ref[0, :, S5_NS:2 * S5_NS] = hi


def s5_scan(dmat, s0, a_t):
    n, nb, _ = dmat.shape
    rc = min(S5_SCAN_CHUNKS, n)
    nblk = n // rc
    w = 2 * S5_NS
    imap = lambda d, k: (k + d * (nblk - 1 - 2 * k), 0, d)
    return pl.pallas_call(
        functools.partial(_s5_scan_kernel, rc=rc, nb=nb),
        grid=(2, nblk),
        in_specs=[pl.BlockSpec((rc, nb, w), imap),
                  pl.BlockSpec((1, nb, w), lambda d, k: (d, 0, 0)),
                  pl.BlockSpec((1, 1, w), lambda d, k: (d, 0, 0))],
        out_specs=[pl.BlockSpec((rc, nb, w), imap),
                   pl.BlockSpec((1, nb, w), lambda d, k: (d, 0, 0))],
        out_shape=[jax.ShapeDtypeStruct((n, nb, 2 * w), jnp.float32),
                   jax.ShapeDtypeStruct((2, nb, w), jnp.float32)],
        scratch_shapes=[pltpu.VMEM((nb, w), jnp.float32)],
        compiler_params=pltpu.CompilerParams(dimension_semantics=("arbitrary", "arbitrary"),
                                             vmem_limit_bytes=VMEM_LIMIT_BYTES),
    )(dmat, s0, a_t)


def _cmul(ar, ai, br, bi):
    return ar * br - ai * bi, ar * bi + ai * br


def s5_prepare(lam_re, lam_im, log_dt, b_re, b_im, c_re, c_im, d_skip):
    f32 = jnp.float32
    hp = lax.Precision.HIGHEST
    t_len, g_n, p_n, h_n = S5_T, S5_GROUPS, S5_STATE, S5_GROUP
    bf16 = jnp.bfloat16
    eye_g = jnp.eye(g_n, dtype=f32)
    mask_gp = jnp.repeat(eye_g, p_n, axis=1).astype(bf16)[None, :, None, :]
    ar_t = jnp.arange(t_len)
    wd, wc, kk, a_t = [], [], [], []
    for d in range(2):
        lr = jnp.minimum(lam_re[d], S5_MAX_RE)
        li = lam_im[d]
        dt = jnp.exp(log_dt[d])[:, None]
        tt = jnp.arange(t_len + 1, dtype=f32)[:, None, None]
        mag = jnp.exp(lr * dt * tt)
        ang = li * dt * tt
        pr, pi = mag * jnp.cos(ang), mag * jnp.sin(ang)
        nr, ni = pr[1] - 1.0, pi[1]
        den = lr * lr + li * li
        qr, qi = (nr * lr + ni * li) / den, (ni * lr - nr * li) / den
        bbr, bbi = _cmul(qr[..., None], qi[..., None], b_re[d], b_im[d])
        cr, ci = c_re[d], c_im[d]

        idx = (t_len - 1 - ar_t) if d == 0 else ar_t
        wr, wi = _cmul(pr[idx][..., None], pi[idx][..., None], bbr[None], bbi[None])

        def place_d(w):
            wt = w.transpose(0, 3, 1, 2).reshape(t_len, 1, h_n, g_n * p_n).astype(bf16)
            return (wt * mask_gp).reshape(t_len * W_MIX, g_n * p_n)

        wd.append(jnp.concatenate([place_d(wr), place_d(wi)], axis=1))

        idx2 = (ar_t + 1) if d == 0 else (t_len - ar_t)
        cwr, cwi = _cmul(cr[None], ci[None], pr[idx2][:, :, None, :], pi[idx2][:, :, None, :])

        def place_c(w):
            wt = w.transpose(0, 2, 1, 3).reshape(t_len, 1, h_n, g_n * p_n).astype(bf16)
            return (wt * mask_gp).reshape(t_len * W_MIX, g_n * p_n)

        wc.append(jnp.concatenate([place_c(cwr), place_c(-cwi)], axis=1))

        er, ei = _cmul(pr[:t_len][:, :, None, :], pi[:t_len][:, :, None, :], cr[None], ci[None])
        kk.append(jnp.einsum('tghp,gpk->tghk', er, bbr, precision=hp)
                  - jnp.einsum('tghp,gpk->tghk', ei, bbi, precision=hp))
        a_t.append(jnp.concatenate([pr[t_len].reshape(1, -1), pi[t_len].reshape(1, -1)], axis=1))

    lag = ar_t[None, :] - ar_t[:, None]
    mf = jnp.where((lag >= 0)[..., None, None, None], kk[0][jnp.clip(lag, 0, t_len - 1)], 0.0)
    mb = jnp.where((lag <= 0)[..., None, None, None], kk[1][jnp.clip(-lag, 0, t_len - 1)], 0.0)
    skip = (jnp.eye(t_len, dtype=f32)[:, :, None, None, None] * d_skip[None, None, :, :, None]
            * jnp.eye(h_n, dtype=f32)[None, None, None])
    m = mf + mb + skip
    mt = m.transpose(0, 4, 1, 2, 3).reshape(t_len, 1, h_n, t_len * W_MIX).astype(bf16)
    mask_igh = jnp.tile(jnp.repeat(eye_g, h_n, axis=1), (1, t_len)).astype(bf16)[None, :, None, :]
    wk = (mt * mask_igh).reshape(t_len * W_MIX, t_len * W_MIX)
    return (jnp.concatenate(wd, axis=1), wk, jnp.concatenate(wc, axis=1), jnp.stack(a_t))


GLA_BLOCK = 1024


def _gla_kernel(q_ref, k_ref, v_ref, g_ref, wup_ref, bup_ref, s0_ref, o_ref, fin_ref, st_ref,
                *, reverse, tb):
    f32, bf16 = jnp.float32, jnp.bfloat16
    cc = GLA_CHUNK

    @pl.when(pl.program_id(1) == 0)
    def _():
        st_ref[...] = s0_ref[0]

    r_i = lax.broadcasted_iota(jnp.int32, (cc, cc), 0)
    c_i = lax.broadcasted_iota(jnp.int32, (cc, cc), 1)
    keep = (c_i >= r_i) if reverse else (c_i <= r_i)
    tri = keep.astype(bf16)
    nt = (((1,), (1,)), ((), ()))
    tn = (((0,), (0,)), ((), ()))
    n_ch = tb // cc
    chunks = range(n_ch)
    heads = range(GLA_HEADS)
    hsl = [slice(h * GLA_DK, (h + 1) * GLA_DK) for h in heads]
    rows = [slice(c * cc, (c + 1) * cc) for c in chunks]

    z = _dot3(*_split_bf16(g_ref[0]), *_split_bf16(wup_ref[...])) + bup_ref[...]
    la = (jnp.minimum(z, 0.0) - jnp.log1p(jnp.exp(-jnp.abs(z)))) * (1.0 / GLA_TAU)
    la_hi, la_lo = _split_bf16(la)
    cum = [jnp.dot(tri, la_hi[rows[c]], preferred_element_type=f32)
           + jnp.dot(tri, la_lo[rows[c]], preferred_element_type=f32) for c in chunks]
    last = [cm[0:1, :] if reverse else cm[cc - 1:cc, :] for cm in cum]
    k = [k_ref[0, rows[c], :] for c in chunks]
    q_in = [(q_ref[0, rows[c], :] * (GLA_DK ** -0.5) * jnp.exp(cum[c])).astype(bf16) for c in chunks]
    k_in = [(k[c] * jnp.exp(-cum[c])).astype(bf16) for c in chunks]
    k_out = [(k[c] * jnp.exp(last[c] - cum[c])).astype(bf16) for c in chunks]
    dec = [jnp.exp(last[c]) for c in chunks]
    vb = [v_ref[0, rows[c], :].astype(bf16) for c in chunks]
    att = [[jnp.where(keep, lax.dot_general(q_in[c][:, s], k_in[c][:, s], nt, preferred_element_type=f32),
                      0.0).astype(bf16) for s in hsl] for c in chunks]
    o_intra = [[jnp.dot(att[c][h], vb[c][:, hsl[h]], preferred_element_type=f32) for h in heads]
               for c in chunks]
    d_state = [[lax.dot_general(vb[c][:, s], k_out[c][:, s], tn, preferred_element_type=f32) for s in hsl]
               for c in chunks]

    st = [st_ref[h] for h in heads]
    st_in = [None] * n_ch
    for c in (reversed(chunks) if reverse else chunks):
        st_in[c] = [s.astype(bf16) for s in st]
        st = [st[h] * dec[c][:, hsl[h]] + d_state[c][h] for h in heads]
    for h in heads:
        st_ref[h] = st[h]
        fin_ref[0, h] = st[h]

    for c in chunks:
        o_ref[0, rows[c], :] = jnp.concatenate(
            [o_intra[c][h] + lax.dot_general(q_in[c][:, hsl[h]], st_in[c][h], nt, preferred_element_type=f32)
             for h in heads], axis=1)


def gla_dir(p, gd, w_up, b_up, s0, reverse):
    nb, n_tok, _ = p.shape
    tb = min(GLA_BLOCK, n_tok)
    nblk = n_tok // tb
    blk = (lambda k: nblk - 1 - k) if reverse else (lambda k: k)
    col = lambda c: pl.BlockSpec((1, tb, W_MIX), lambda b, k: (b, blk(k), c))
    st_shape = (GLA_HEADS, GLA_DV, GLA_DK)
    return pl.pallas_call(
        functools.partial(_gla_kernel, reverse=reverse, tb=tb),
        grid=(nb, nblk),
        in_specs=[col(P_Q), col(P_Q + 1), col(P_Q + 2),
                  pl.BlockSpec((1, tb, GLA_RANK), lambda b, k: (b, blk(k), 0)),
                  pl.BlockSpec((GLA_RANK, W_MIX), lambda b, k: (0, 0)),
                  pl.BlockSpec((1, W_MIX), lambda b, k: (0, 0)),
                  pl.BlockSpec((1,) + st_shape, lambda b, k: (b, 0, 0, 0))],
        out_specs=[pl.BlockSpec((1, tb, W_MIX), lambda b, k: (b, blk(k), 0)),
                   pl.BlockSpec((1,) + st_shape, lambda b, k: (b, 0, 0, 0))],
        out_shape=[jax.ShapeDtypeStruct((nb, n_tok, W_MIX), jnp.float32),
                   jax.ShapeDtypeStruct((nb,) + st_shape, jnp.float32)],
        scratch_shapes=[pltpu.VMEM(st_shape, jnp.float32)],
        compiler_params=pltpu.CompilerParams(dimension_semantics=("arbitrary", "arbitrary"),
                                             vmem_limit_bytes=VMEM_LIMIT_BYTES),
        name="gla",
    )(p, p, p, gd, w_up, b_up.reshape(1, W_MIX), s0)


FFN_TM = 512
CAST_ROWS = 256


def _swiglu_kernel(te_ref, nu_ref, x_ref, wg_ref, wu_ref, wd_ref, *rest, nf, residual):
    del te_ref
    o_ref, acc_ref = rest[-2:]
    j = pl.program_id(1)

    @pl.when(pl.program_id(0) < nu_ref[0])
    def _():
        x = x_ref[...]
        g = jnp.dot(x, wg_ref[0], preferred_element_type=jnp.float32)
        u = jnp.dot(x, wu_ref[0], preferred_element_type=jnp.float32)
        a = (g * jax.nn.sigmoid(g) * u).astype(jnp.bfloat16)
        part = jnp.dot(a, wd_ref[0], preferred_element_type=jnp.float32)

        @pl.when(j == 0)
        def _():
            acc_ref[...] = part

        @pl.when(j > 0)
        def _():
            acc_ref[...] += part

        @pl.when(j == nf - 1)
        def _():
            if residual:
                xres_ref, gm_ref = rest[:2]
                o_ref[...] = (xres_ref[...] + gm_ref[0] * acc_ref[...]).astype(o_ref.dtype)
            else:
                o_ref[...] = acc_ref[...].astype(o_ref.dtype)


def _cast_split_kernel(a_ref, b_ref, oa_ref, ob_ref):
    oa_ref[...] = a_ref[...].astype(oa_ref.dtype)
    ob_ref[...] = b_ref[...].astype(ob_ref.dtype)


def cast_split_bf16(w):
    e, r, c2 = w.shape
    c = c2 // 2
    rows = min(CAST_ROWS, r)
    half = lambda h: pl.BlockSpec((1, rows, c), lambda i, j: (i, j, h))
    out = jax.ShapeDtypeStruct((e, r, c), jnp.bfloat16)
    return pl.pallas_call(
        _cast_split_kernel,
        grid=(e, r // rows),
        in_specs=[half(0), half(1)],
        out_specs=[half(0), half(0)],
        out_shape=[out, out],
        compiler_params=pltpu.CompilerParams(dimension_semantics=("arbitrary", "arbitrary"),
                                             vmem_limit_bytes=VMEM_LIMIT_BYTES),
        name="cast_split_bf16",
    )(w, w)


def _cast_rows_kernel(a_ref, b_ref, o_ref):
    o_ref[0, 0] = a_ref[0, 0].astype(o_ref.dtype)
    o_ref[0, 1] = b_ref[0, 0].astype(o_ref.dtype)


def cast_rows_bf16(w):
    e, r, c = w.shape
    rows = min(CAST_ROWS, r // 2)
    w4 = w.reshape(e, 2, r // 2, c)
    half = lambda h: pl.BlockSpec((1, 1, rows, c), lambda i, j: (i, h, j, 0))
    out = pl.pallas_call(
        _cast_rows_kernel,
        grid=(e, r // 2 // rows),
        in_specs=[half(0), half(1)],
        out_specs=pl.BlockSpec((1, 2, rows, c), lambda i, j: (i, 0, j, 0)),
        out_shape=jax.ShapeDtypeStruct(w4.shape, jnp.bfloat16),
        compiler_params=pltpu.CompilerParams(dimension_semantics=("arbitrary", "arbitrary"),
                                             vmem_limit_bytes=VMEM_LIMIT_BYTES),
        name="cast_rows_bf16",
    )(w4, w4)
    return out.reshape(e, r, c)


def grouped_swiglu(tile_expert, n_used, xs, w_g, w_u, w_down, nf, residual=None, out_dtype=jnp.float32):
    m, d = xs.shape
    f = w_down.shape[1]
    tf = f // nf
    n_tiles = m // FFN_TM
    in_specs = [pl.BlockSpec((FFN_TM, d), lambda t, j, te, nu: (t, 0)),
                pl.BlockSpec((1, d, tf), lambda t, j, te, nu: (te[t], 0, j)),
                pl.BlockSpec((1, d, tf), lambda t, j, te, nu: (te[t], 0, j)),
                pl.BlockSpec((1, tf, d), lambda t, j, te, nu: (te[t], j, 0))]
    extra = ()
    if residual is not None:
        rows_per_mod = m // residual[1].shape[0]
        in_specs += [pl.BlockSpec((FFN_TM, d), lambda t, j, te, nu: (t, 0)),
                     pl.BlockSpec((1, 1, d), lambda t, j, te, nu: ((t * FFN_TM) // rows_per_mod, 0, 0))]
        extra = tuple(residual)
    grid_spec = pltpu.PrefetchScalarGridSpec(
        num_scalar_prefetch=2,
        grid=(n_tiles, nf),
        in_specs=in_specs,
        out_specs=pl.BlockSpec((FFN_TM, d), lambda t, j, te, nu: (t, 0)),
        scratch_shapes=[pltpu.VMEM((FFN_TM, d), jnp.float32)])
    return pl.pallas_call(
        functools.partial(_swiglu_kernel, nf=nf, residual=residual is not None),
        grid_spec=grid_spec,
        out_shape=jax.ShapeDtypeStruct((m, d), out_dtype),
        compiler_params=pltpu.CompilerParams(
            dimension_semantics=("arbitrary", "arbitrary"),
            vmem_limit_bytes=VMEM_LIMIT_BYTES),
        name="grouped_swiglu",
    )(tile_expert, n_used, xs, w_g, w_u, w_down, *extra)


ROW_TILE = 512
P_HY, P_Q, P_OG, P_S5, P_RGX, P_RGG = 0, 3, 6, 7, 8, 9
P_GDN = 10 * W_MIX
P_COLS = P_GDN + 128


def _rms_mod(x, g, shift, scale):
    y = x * lax.rsqrt(jnp.mean(x * x, axis=-1, keepdims=True) + EPS) * g
    return y * (1.0 + scale) + shift


def _front_kernel(x_ref, sh_ref, sc_ref, g_ref, w_ref, *rest):
    o_ref = rest[-1]
    x = x_ref[...] + rest[0][...] if len(rest) == 2 else x_ref[...]
    h = _rms_mod(x, g_ref[...], sh_ref[0], sc_ref[0])
    o_ref[...] = jnp.dot(h.astype(jnp.bfloat16), w_ref[...], preferred_element_type=jnp.float32)


def _pos_spec(pos, tile):
    if pos is None:
        return []
    n_blk = pos.shape[0] // tile
    return [pl.BlockSpec((tile, D_MODEL), lambda i: (i % n_blk, 0))]


def _mod_spec(rows_per_mod):
    return pl.BlockSpec((1, 1, D_MODEL), lambda i: ((i * ROW_TILE) // rows_per_mod, 0, 0))


def front(x, shift, scale, g, w, pos=None):
    m = x.shape[0]
    n_mod = shift.shape[0]
    n_out = w.shape[1]
    mod = _mod_spec(m // n_mod)
    return pl.pallas_call(
        _front_kernel,
        grid=(m // ROW_TILE,),
        in_specs=[pl.BlockSpec((ROW_TILE, D_MODEL), lambda i: (i, 0)), mod, mod,
                  pl.BlockSpec((1, D_MODEL), lambda i: (0, 0)),
                  pl.BlockSpec((D_MODEL, n_out), lambda i: (0, 0))] + _pos_spec(pos, ROW_TILE),
        out_specs=pl.BlockSpec((ROW_TILE, n_out), lambda i: (i, 0)),
        out_shape=jax.ShapeDtypeStruct((m, n_out), jnp.float32),
        compiler_params=pltpu.CompilerParams(dimension_semantics=("arbitrary",),
                                             vmem_limit_bytes=VMEM_LIMIT_BYTES),
        name="front",
    )(x, shift, scale, g.reshape(1, D_MODEL), w, *([] if pos is None else [pos]))


MERGE_TILE = 256
ROUTER_PAD = 128
N_MERGE_IN = 25


def _merge_kernel(*refs, with_router, with_pos):
    (x_ref, sh1_ref, sc1_ref, gm_ref, sh2_ref, sc2_ref, g1_ref, g2_ref,
     of_ref, ob_ref, og_ref, s5_ref, hc_ref, hz_ref, hx0_ref, rf_ref, rb_ref, rgg_ref,
     wg_ref, wb_ref, wo_ref, wglu_ref, bglu_ref, havg_ref, hbias_ref) = refs[:N_MERGE_IN]
    f32, bf16 = jnp.float32, jnp.bfloat16
    hp = lax.Precision.HIGHEST
    rest = list(refs[N_MERGE_IN:])
    router_refs = [rest.pop(0), rest.pop(0)] if with_router else None
    x = x_ref[...] + rest.pop(0)[...] if with_pos else x_ref[...]
    hb = _rms_mod(x, g1_ref[...], sh1_ref[0], sc1_ref[0]).astype(bf16)

    o = of_ref[...] + ob_ref[...]
    ms = jnp.dot(o * o, havg_ref[...], precision=hp, preferred_element_type=f32)
    og = og_ref[...]
    gla = o * lax.rsqrt(ms + EPS) * (og * jax.nn.sigmoid(og))
    g5 = jax.nn.gelu(s5_ref[...])
    s5o = g5 * jax.nn.sigmoid(jnp.dot(g5.astype(bf16), wglu_ref[...], preferred_element_type=f32)
                              + bglu_ref[...])
    rgo = (rf_ref[...] + rb_ref[...]) * jax.nn.gelu(rgg_ref[...])
    hyo = hx0_ref[...] * (hc_ref[...] + hz_ref[...] * hbias_ref[...])
    branches = (gla, s5o, hyo, rgo)

    y = None
    for k in range(N_BRANCH):
        gate = jax.nn.sigmoid(jnp.dot(hb, wg_ref[:, k * D_MODEL:(k + 1) * D_MODEL],
                                      preferred_element_type=f32))
        t = gate * jnp.dot(branches[k].astype(bf16), wb_ref[k], preferred_element_type=f32)
        y = t if y is None else y + t
    out = jnp.dot(y.astype(bf16), wo_ref[...], preferred_element_type=f32)
    xn = x + gm_ref[0] * out
    h2 = _rms_mod(xn, g2_ref[...], sh2_ref[0], sc2_ref[0])
    if with_router:
        (rw_ref, rb2_ref), (xo_ref, h2_ref, lg_ref) = router_refs, rest
        lg_ref[...] = _dot3(*_split_bf16(h2), *_split_bf16(rw_ref[...])) + rb2_ref[...]
    else:
        xo_ref, h2_ref = rest
    xo_ref[...] = xn
    h2_ref[...] = h2.astype(bf16)


def merge(x, mods, g1, g2, o_f, o_b, p, s5y, hy, r_f, r_b, wg, wb, wo, wglu, bglu, hbias, router=None,
          pos=None):
    m = x.shape[0]
    tm = MERGE_TILE
    n_mod = mods[0].shape[0]
    rows_per_mod = m // n_mod
    mod = pl.BlockSpec((1, 1, D_MODEL), lambda i: ((i * tm) // rows_per_mod, 0, 0))
    row = pl.BlockSpec((tm, D_MODEL), lambda i: (i, 0))
    br = pl.BlockSpec((tm, W_MIX), lambda i: (i, 0))
    pcol = lambda c: pl.BlockSpec((tm, W_MIX), lambda i: (i, c))
    full = lambda a: pl.BlockSpec(a.shape, lambda i: (0,) * a.ndim)
    head = jnp.arange(W_MIX) // GLA_DV
    havg = (head[:, None] == head[None, :]).astype(jnp.float32) / GLA_DV
    vec = lambda v: v.reshape(1, -1)
    consts = [wg, wb, wo, wglu, vec(bglu), havg, vec(hbias)]
    out_specs = [row, pl.BlockSpec((tm, D_MODEL), lambda i: (i, 0))]
    out_shape = [jax.ShapeDtypeStruct((m, D_MODEL), jnp.float32),
                 jax.ShapeDtypeStruct((m, D_MODEL), jnp.bfloat16)]
    if router is not None:
        rw, rbias = router
        pad = ROUTER_PAD - rw.shape[1]
        consts += [jnp.pad(rw, ((0, 0), (0, pad))), jnp.pad(rbias, (0, pad)).reshape(1, -1)]
        out_specs.append(pl.BlockSpec((tm, ROUTER_PAD), lambda i: (i, 0)))
        out_shape.append(jax.ShapeDtypeStruct((m, ROUTER_PAD), jnp.float32))
    in_specs = ([row] + [mod] * 5 + [full(vec(g1)), full(vec(g2)), br, br, pcol(P_OG), br, br, br, br, br, br,
                                     pcol(P_RGG)] + [full(a) for a in consts] + _pos_spec(pos, tm))
    return pl.pallas_call(
        functools.partial(_merge_kernel, with_router=router is not None, with_pos=pos is not None),
        grid=(m // tm,),
        in_specs=in_specs,
        out_specs=out_specs,
        out_shape=out_shape,
        compiler_params=pltpu.CompilerParams(dimension_semantics=("arbitrary",),
                                             vmem_limit_bytes=VMEM_LIMIT_BYTES),
        name="merge",
    )(x, *mods, vec(g1), vec(g2), o_f, o_b, p, s5y, *hy, r_f, r_b, p, *consts, *([] if pos is None else [pos]))


def _combine_kernel(x_ref, y0_ref, y1_ref, w_ref, gm_ref, g_ref, o_ref, *, final_norm):
    w = w_ref[...]
    y = w[:, 0:1] * y0_ref[...].astype(jnp.float32) + w[:, 1:2] * y1_ref[...].astype(jnp.float32)
    xn = x_ref[...] + gm_ref[0] * y
    if final_norm:
        xn = xn * lax.rsqrt(jnp.mean(xn * xn, axis=-1, keepdims=True) + EPS) * g_ref[...]
    o_ref[...] = xn


def moe_combine(x, yk, w, gate_mod, final_g):
    m = x.shape[0]
    n_mod = gate_mod.shape[0]
    g = jnp.ones((1, D_MODEL), jnp.float32) if final_g is None else final_g.reshape(1, D_MODEL)
    return pl.pallas_call(
        functools.partial(_combine_kernel, final_norm=final_g is not None),
        grid=(m // ROW_TILE,),
        in_specs=[pl.BlockSpec((ROW_TILE, D_MODEL), lambda i: (i, 0)),
                  pl.BlockSpec((ROW_TILE, D_MODEL), lambda i: (i, 0)),
                  pl.BlockSpec((ROW_TILE, D_MODEL), lambda i: (i + m // ROW_TILE, 0)),
                  pl.BlockSpec((ROW_TILE, TOP_K), lambda i: (i, 0)),
                  _mod_spec(m // n_mod),
                  pl.BlockSpec((1, D_MODEL), lambda i: (0, 0))],
        out_specs=pl.BlockSpec((ROW_TILE, D_MODEL), lambda i: (i, 0)),
        out_shape=jax.ShapeDtypeStruct((m, D_MODEL), jnp.float32),
        compiler_params=pltpu.CompilerParams(dimension_semantics=("arbitrary",)),
        name="moe_combine",
    )(x, yk, yk, w, gate_mod, g)


HY_LANES = 128
HY_STEP_ROWS = 1024
HY_MIN_LEN = 1024


def _hyena_pre_kernel(x_ref, prev_ref, next_ref, w_ref, zt_ref, z_ref, x0_ref, *, tb):
    k = pl.program_id(1)
    prev = jnp.where(k > 0, prev_ref[0], 0.0)
    nxt = jnp.where(k < pl.num_programs(1) - 1, next_ref[0], 0.0)
    ext = jnp.concatenate([prev, x_ref[0], nxt], axis=0)
    pc = sum(ext[RG_HALO - 1 + j:RG_HALO - 1 + j + tb, :] * w_ref[j:j + 1, :] for j in range(HY_SHORT))
    v, x0, x1 = pc[:, :W_MIX], pc[:, W_MIX:2 * W_MIX], pc[:, 2 * W_MIX:]
    z = x1 * v
    z_ref[0] = z
    x0_ref[0] = x0
    zt_ref[0] = z.T


def hyena_pre(p, w_short):
    nb, n_tok, _ = p.shape
    tb = min(ROW_TILE, n_tok)
    hb = tb // RG_HALO
    n_halo = n_tok // RG_HALO
    wide = 3 * W_MIX
    tok = pl.BlockSpec((1, tb, W_MIX), lambda b, k: (b, k, 0))
    tok_out = jax.ShapeDtypeStruct((nb, n_tok, W_MIX), jnp.float32)
    return pl.pallas_call(
        functools.partial(_hyena_pre_kernel, tb=tb),
        grid=(nb, n_tok // tb),
        in_specs=[pl.BlockSpec((1, tb, wide), lambda b, k: (b, k, P_HY)),
                  pl.BlockSpec((1, RG_HALO, wide), lambda b, k: (b, jnp.maximum(k * hb - 1, 0), P_HY)),
                  pl.BlockSpec((1, RG_HALO, wide), lambda b, k: (b, jnp.minimum((k + 1) * hb, n_halo - 1), P_HY)),
                  pl.BlockSpec(w_short.shape, lambda b, k: (0, 0))],
        out_specs=[pl.BlockSpec((1, W_MIX, tb), lambda b, k: (b, 0, k)), tok, tok],
        out_shape=[jax.ShapeDtypeStruct((nb, W_MIX, n_tok), jnp.float32), tok_out, tok_out],
        compiler_params=pltpu.CompilerParams(dimension_semantics=("arbitrary", "arbitrary"),
                                             vmem_limit_bytes=VMEM_LIMIT_BYTES),
        name="hyena_pre",
    )(p, p, p, w_short)


def _split_bf16(a):
    hi = a.astype(jnp.bfloat16)
    lo = (a - hi.astype(jnp.float32)).astype(jnp.bfloat16)
    return hi, lo


def _dot3(a_hi, a_lo, b_hi, b_lo):
    d = functools.partial(jnp.dot, preferred_element_type=jnp.float32)
    return d(a_hi, b_hi) + (d(a_lo, b_hi) + d(a_hi, b_lo))


def _hyena_dft_consts(n1):
    n = n1 * HY_LANES
    ka = np.arange(n1, dtype=np.float64)[:, None]
    f1_ang = 2.0 * np.pi * ka * np.arange(n1 // 2, dtype=np.float64)[None, :] / n1
    f1r, f1i = np.cos(f1_ang), -np.sin(f1_ang)
    tw_ang = 2.0 * np.pi * ka * np.arange(HY_LANES, dtype=np.float64)[None, :] / n
    lo = np.arange(HY_LANES, dtype=np.float64)
    f2_ang = 2.0 * np.pi * lo[:, None] * lo[None, :] / HY_LANES
    f2r, f2i = np.cos(f2_ang), -np.sin(f2_ang)
    fwd_rows = np.concatenate([f1r, f1i], axis=0)
    fwd_lanes = np.block([[f2r, f2i], [-f2i, f2r]])
    inv_lanes = np.block([[f2r, -f2i], [f2i, f2r]])
    inv_rows = np.concatenate([f1r.T, f1i.T], axis=1) / n
    out = []
    for m in (fwd_rows, fwd_lanes, inv_lanes, inv_rows):
        m32 = jnp.asarray(m, jnp.float32)
        out.extend(_split_bf16(m32))
    return out + [jnp.asarray(np.cos(tw_ang), jnp.float32), jnp.asarray(-np.sin(tw_ang), jnp.float32)]


def _hyena_fft_kernel(*refs, n1, ns, spectrum):
    bf16 = jnp.bfloat16
    if spectrum:
        z_ref, f1h, f1l, f2h, f2l, twr_ref, twi_ref, o_ref = refs
        rows_dft = lambda t: _dot3(f1h[...], f1l[...], *_split_bf16(t))
        lanes_dft = lambda t: _dot3(*_split_bf16(t), f2h[...], f2l[...])
    else:
        z_ref, hf_ref, f1h, f2h, g2h, fih, twr_ref, twi_ref, o_ref = refs
        mm = lambda a, b: jnp.dot(a.astype(bf16), b.astype(bf16), preferred_element_type=jnp.float32)
        rows_dft = lambda t: mm(f1h[...], t)
        lanes_dft = lambda t: mm(t, f2h[...])
    w = HY_LANES
    twr, twi = twr_ref[...], twi_ref[...]
    z2 = jnp.concatenate([z_ref[0, s] for s in range(ns)], axis=1)
    a2 = rows_dft(z2)
    rows = []
    for s in range(ns):
        r, i = a2[:n1, s * w:(s + 1) * w], a2[n1:, s * w:(s + 1) * w]
        rows.append(jnp.concatenate([r * twr - i * twi, r * twi + i * twr], axis=1))
    x = lanes_dft(jnp.concatenate(rows, axis=0))
    if spectrum:
        o_ref[...] = x.reshape(ns, n1, 2 * w)
        return
    h = hf_ref[...].reshape(ns * n1, 2 * w)
    xr, xi, hr, hi = x[:, :w], x[:, w:], h[:, :w], h[:, w:]
    y = jnp.concatenate([xr * hr - xi * hi, xr * hi + xi * hr], axis=1)
    g = mm(y, g2h[...])
    cr, ci = [], []
    for s in range(ns):
        gr, gi = g[s * n1:(s + 1) * n1, :w], g[s * n1:(s + 1) * n1, w:]
        cr.append(gr * twr + gi * twi)
        ci.append(gi * twr - gr * twi)
    gc = jnp.concatenate([jnp.concatenate(cr, axis=1), jnp.concatenate(ci, axis=1)], axis=0)
    y2 = mm(fih[...], gc)
    for s in range(ns):
        o_ref[0, s] = y2[:, s * w:(s + 1) * w]


def hyena_fft(zt, hf=None):
    nb, ch, half, w = zt.shape
    n1 = 2 * half
    ns = HY_STEP_ROWS // n1
    f1h, f1l, f2h, f2l, g2h, _, fih, _, twr, twi = _hyena_dft_consts(n1)
    full = lambda a: pl.BlockSpec(a.shape, lambda b, c: (0,) * a.ndim)
    zspec = pl.BlockSpec((1, ns, half, w), lambda b, c: (b, c, 0, 0))
    if hf is None:
        consts = [f1h, f1l, f2h, f2l, twr, twi]
        in_specs, args = [zspec], [zt]
        out_spec = pl.BlockSpec((ns, n1, 2 * w), lambda b, c: (b * (ch // ns) + c, 0, 0))
        out_shape = jax.ShapeDtypeStruct((nb * ch, n1, 2 * w), jnp.float32)
    else:
        consts = [f1h, f2h, g2h, fih, twr, twi]
        in_specs = [zspec, pl.BlockSpec((ns, n1, 2 * w), lambda b, c: (c, 0, 0))]
        args = [zt, hf]
        out_spec = zspec
        out_shape = jax.ShapeDtypeStruct(zt.shape, jnp.float32)
    return pl.pallas_call(
        functools.partial(_hyena_fft_kernel, n1=n1, ns=ns, spectrum=hf is None),
        grid=(nb, ch // ns),
        in_specs=in_specs + [full(a) for a in consts],
        out_specs=out_spec,
        out_shape=out_shape,
        compiler_params=pltpu.CompilerParams(dimension_semantics=("arbitrary", "arbitrary"),
                                             vmem_limit_bytes=VMEM_LIMIT_BYTES),
        name="hyena_fft",
    )(*args, *consts)


def rmsnorm(x, g):
    y = x * lax.rsqrt(jnp.mean(x * x, axis=-1, keepdims=True) + EPS)
    return y * g


def adaln(cond, w, b):
    return jax.nn.silu(cond) @ w + b


def grid_pos_embed(n_tokens, dim):
    rows = n_tokens // GRID_W
    q = dim // 4
    omega = 1.0 / (10000.0 ** (jnp.arange(q, dtype=jnp.float32) / q))
    r = jnp.arange(rows, dtype=jnp.float32)[:, None] * omega
    cc = jnp.arange(GRID_W, dtype=jnp.float32)[:, None] * omega
    er = jnp.concatenate([jnp.sin(r), jnp.cos(r)], axis=-1)
    ec = jnp.concatenate([jnp.sin(cc), jnp.cos(cc)], axis=-1)
    emb = jnp.concatenate([jnp.broadcast_to(er[:, None], (rows, GRID_W, dim // 2)),
                           jnp.broadcast_to(ec[None], (rows, GRID_W, dim // 2))], axis=-1)
    return emb.reshape(rows * GRID_W, dim)


def gla_mixer(p, w_up, b_up, s0):
    gdn = p[..., P_GDN:P_GDN + 2 * GLA_RANK]
    outs, finals = [], []
    for d in range(2):
        od, sd = gla_dir(p, gdn[..., d * GLA_RANK:(d + 1) * GLA_RANK], w_up[d], b_up[d], s0[d], d == 1)
        outs.append(od)
        finals.append(sd)
    return outs, jnp.stack(finals)


def s5_mixer(u, prep, s0):
    wd, wk, wc, a_t = prep
    b_, n_tok, _ = u.shape
    n = n_tok // S5_T
    u2 = u.reshape(b_, n, S5_T * W_MIX).transpose(1, 0, 2).reshape(n * b_, S5_T * W_MIX)
    dmat = pmm(u2, wd)
    hmat, fin = s5_scan(dmat.reshape(n, b_, 4 * S5_NS), s0, a_t)
    y2 = pmm_multi([u2, hmat.reshape(n * b_, 4 * S5_NS)], [wk, wc], (False, True))
    y = y2.reshape(n, b_, S5_T, W_MIX).transpose(1, 0, 2, 3).reshape(b_, n_tok, W_MIX)
    return y, fin


def hyena_filters(n_tok, w1, b1, w2, b2, w3, freq):
    f32 = jnp.float32
    t = jnp.arange(n_tok, dtype=f32)[:, None]
    bands = jnp.linspace(1e-4, HY_BANDS - 1, HY_BANDS, dtype=f32)[None]
    ang = 2.0 * math.pi * bands * t / n_tok
    z = jnp.concatenate([t / n_tok, jnp.cos(ang), jnp.sin(ang)], axis=-1)
    hp = lax.Precision.HIGHEST
    h = jnp.sin(freq * (jnp.dot(z, w1, precision=hp) + b1))
    h = jnp.sin(freq * (jnp.dot(h, w2, precision=hp) + b2))
    h = jnp.dot(h, w3, precision=hp)
    t01 = t / max(n_tok - 1, 1)
    deltas = jnp.abs(jnp.linspace(math.log(HY_TARGET) / HY_DECAY_SHORT,
                                  math.log(HY_TARGET) / HY_DECAY_LONG, W_MIX, dtype=f32))
    h = h * jnp.exp(-t01 * jnp.tile(deltas, 2))
    return h / (jnp.sum(jnp.abs(h), axis=0, keepdims=True) + EPS)


def hyena_mixer(p, w_short, w1, b1, w2, b2, w3, freq):
    nb, n_tok, _ = p.shape
    zt, z, x0 = hyena_pre(p, w_short)
    n_pad = max(n_tok, HY_MIN_LEN)
    half = n_pad // HY_LANES

    def frames(t):
        t = jnp.pad(t, [(0, 0)] * (t.ndim - 1) + [(0, n_pad - n_tok)])
        return t.reshape(t.shape[:-1] + (half, HY_LANES))

    filt = hyena_filters(n_tok, w1, b1, w2, b2, w3, freq)
    spec = hyena_fft(frames(filt.T)[None])
    sf, sb = spec[:W_MIX], spec[W_MIX:]
    hfreq = jnp.concatenate([sf[..., :HY_LANES] + sb[..., :HY_LANES],
                             sf[..., HY_LANES:] - sb[..., HY_LANES:]], axis=-1)
    conv = hyena_fft(frames(zt), hfreq)
    conv = conv.reshape(nb, W_MIX, n_pad)[:, :, :n_tok].transpose(0, 2, 1)
    return conv, z, x0


def rglru_mixer(p, w_conv, w_a, b_a, w_x, b_x, lam, s0):
    a_f, b_f, a_b, b_b = rg_pre(p, w_conv, w_a, b_a, w_x, b_x, lam)
    h_f, fin_f = rg_scan(a_f, b_f, s0[0], False)
    h_b, fin_b = rg_scan(a_b, b_b, s0[1], True)
    return [h_f, h_b], jnp.stack([fin_f, fin_b])


def ffn_dense(h2, x, gate_mod, w_gu, w_down):
    n_tiles = h2.shape[0] // FFN_TM
    d_ff = w_down.shape[0]
    return grouped_swiglu(jnp.zeros((n_tiles,), jnp.int32), jnp.full((1,), n_tiles, jnp.int32), h2,
                          w_gu[None, :, :d_ff].astype(jnp.bfloat16), w_gu[None, :, d_ff:].astype(jnp.bfloat16),
                          w_down[None].astype(jnp.bfloat16), nf=2,
                          residual=(x, gate_mod))


def ffn_moe(h2, logits, x, gate_mod, w_gu, w_down, final_g):
    n_tok = h2.shape[0]
    n_slot = TOP_K * n_tok
    top_v, top_i = lax.top_k(logits, TOP_K)
    w = jax.nn.softmax(top_v, axis=-1)
    e_flat = top_i.T.reshape(-1).astype(jnp.int32)
    onehot = (e_flat[:, None] == jnp.arange(N_EXPERTS, dtype=jnp.int32)[None]).astype(jnp.int32)
    csum = jnp.cumsum(onehot, axis=0)
    cnt = csum[-1]
    rank = jnp.sum(csum * onehot, axis=1) - 1
    padded = ((cnt + FFN_TM - 1) // FFN_TM) * FFN_TM
    ends = jnp.cumsum(padded)
    dest = (ends - padded)[e_flat] + rank
    n_rows = n_slot + N_EXPERTS * FFN_TM
    n_tiles = n_rows // FFN_TM
    tile_start = jnp.arange(n_tiles, dtype=jnp.int32) * FFN_TM
    tile_expert = jnp.minimum(jnp.sum((tile_start[:, None] >= ends[None, :]).astype(jnp.int32), axis=1),
                              N_EXPERTS - 1)
    n_used = (ends[-1:] // FFN_TM).astype(jnp.int32)
    order = jnp.argsort(e_flat, stable=True).astype(jnp.int32)
    row_expert = jnp.repeat(tile_expert, FFN_TM)
    shift = (ends - padded) - (jnp.cumsum(cnt) - cnt)
    q = jnp.arange(n_rows, dtype=jnp.int32) - shift[row_expert]
    src = order[jnp.clip(q, 0, n_slot - 1)] % n_tok
    xs = h2.at[src].get(mode="promise_in_bounds")
    wg, wu = cast_split_bf16(w_gu)
    ys = grouped_swiglu(tile_expert, n_used, xs, wg, wu, cast_rows_bf16(w_down), nf=2,
                        out_dtype=jnp.bfloat16)
    yk = ys.at[dest].get(mode="promise_in_bounds")
    return moe_combine(x, yk, w, gate_mod, final_g)


def kernel(x, c, ctx, c_ctx, mod_w, mod_b, norm1_g, norm2_g, w_in, gla_w_up, gla_b_up,
           s5_lam_re, s5_lam_im, s5_log_dt, s5_b_re, s5_b_im, s5_c_re, s5_c_im, s5_d,
           s5_w_glu, s5_b_glu, hy_w_short, hy_w1, hy_b1, hy_w2, hy_b2, hy_w3, hy_freq,
           hy_bias, rg_w_conv, rg_w_a, rg_b_a, rg_w_x, rg_b_x, rg_lam, w_branch, w_out,
           ffn_w_gu, ffn_w_down, moe_router, moe_router_b, moe_w_gu, moe_w_down, final_g):
    f32, bf16 = jnp.float32, jnp.bfloat16
    n_b, n_lat, _ = x.shape
    n_ctx = ctx.shape[1]
    xs = x.reshape(n_b * n_lat, D_MODEL)
    pos = grid_pos_embed(n_lat, D_MODEL)
    cs = ctx.reshape(n_b * n_ctx, D_MODEL)
    offs = [0]
    for n in IN_SIZES:
        offs.append(offs[-1] + n)
    for l in range(DEPTH):
        last = l == DEPTH - 1
        dense = l % 2 == 0
        j = l // 2
        m_lat = [t[:, None, :] for t in jnp.split(adaln(c, mod_w[l], mod_b[l]), 6, axis=-1)]
        m_ctx = [t[None, None, :] for t in jnp.split(adaln(c_ctx, mod_w[l], mod_b[l]), 6, axis=-1)]
        wl = w_in[l]
        w_mix = jnp.concatenate([wl[:, offs[6]:offs[7]], wl[:, offs[0]:offs[4]], wl[:, offs[5]:offs[6]],
                                 wl[:, offs[7]:offs[9]], wl[:, offs[4]:offs[5]],
                                 jnp.zeros((D_MODEL, P_COLS - P_GDN - 2 * GLA_RANK), f32)], axis=1).astype(bf16)
        w_gate = wl[:, offs[9]:offs[10]].astype(bf16)
        wb, wo, wglu = w_branch[l].astype(bf16), w_out[l].astype(bf16), s5_w_glu[l].astype(bf16)
        s5_ops = s5_prepare(s5_lam_re[l], s5_lam_im[l], s5_log_dt[l], s5_b_re[l], s5_b_im[l],
                            s5_c_re[l], s5_c_im[l], s5_d[l])
        hy_p = (hy_w_short[l], hy_w1[l], hy_b1[l], hy_w2[l], hy_b2[l], hy_w3[l], hy_freq[l])
        rg_p = (rg_w_conv[l], rg_w_a[l], rg_b_a[l], rg_w_x[l], rg_b_x[l], rg_lam[l])
        router = None if dense else (moe_router[j], moe_router_b[j])

        def mixers(p2, n_tok, states, with_hyena):
            p = p2.reshape(n_b, n_tok, P_COLS)
            blk = lambda i, n=1: p[..., i * W_MIX:(i + n) * W_MIX]
            flat = lambda t: t.reshape(n_b * n_tok, W_MIX)
            gla_o, gla_s = gla_mixer(p, gla_w_up[l], gla_b_up[l], states[0])
            s5_y, s5_s = s5_mixer(blk(P_S5), s5_ops, states[1])
            rg_o, rg_s = rglru_mixer(p, *rg_p, states[2])
            hy = tuple(flat(t) for t in hyena_mixer(p, *hy_p)) if with_hyena else None
            return ((flat(gla_o[0]), flat(gla_o[1]), flat(s5_y), hy, flat(rg_o[0]), flat(rg_o[1])),
                    (gla_s, s5_s, rg_s))

        def tail(stream, mods, p2, br, final, pos_add=None):
            outs = merge(stream, tuple(mods[:5]), norm1_g[l], norm2_g[l], br[0], br[1], p2, br[2], br[3],
                         br[4], br[5], w_gate, wb, wo, wglu, s5_b_glu[l], hy_bias[l], router, pos_add)
            if dense:
                return ffn_dense(outs[1], outs[0], mods[5], ffn_w_gu[j], ffn_w_down[j])
            return ffn_moe(outs[1], outs[2][:, :N_EXPERTS], outs[0], mods[5], moe_w_gu[j], moe_w_down[j],
                           final_g if final else None)

        zero_states = (jnp.zeros((2, n_b, GLA_HEADS, GLA_DV, GLA_DK), f32),
                       jnp.zeros((2, n_b, 2 * S5_NS), f32), jnp.zeros((2, n_b, W_MIX), f32))
        p_ctx = front(cs, m_ctx[0], m_ctx[1], norm1_g[l], w_mix)
        br_ctx, states = mixers(p_ctx, n_ctx, zero_states, not last)
        p_lat = front(xs, m_lat[0], m_lat[1], norm1_g[l], w_mix, pos if l == 0 else None)
        br_lat, _ = mixers(p_lat, n_lat, states, True)
        xs = tail(xs, m_lat, p_lat, br_lat, last, pos if l == 0 else None)
        if not last:
            cs = tail(cs, m_ctx, p_ctx, br_ctx, False)
    if (DEPTH - 1) % 2 == 0:
        xs = rmsnorm(xs, final_g)
    return xs.reshape(n_b, n_lat, D_MODEL)
```

```python
import functools
import math

import jax
import jax.numpy as jnp
import numpy as np
from jax import lax
from jax.experimental import pallas as pl
from jax.experimental.pallas import tpu as pltpu

D_MODEL = 1024
DEPTH = 2
GRID_W = 64
EPS = 1e-6
N_BRANCH = 4
W_MIX = D_MODEL // N_BRANCH
GLA_HEADS = 4
GLA_DK = W_MIX // GLA_HEADS
GLA_DV = W_MIX // GLA_HEADS
GLA_RANK = 16
GLA_TAU = 16.0
GLA_CHUNK = 64
S5_GROUP = 16
S5_GROUPS = W_MIX // S5_GROUP
S5_STATE = 64
S5_MAX_RE = -1e-4
HY_BANDS = 16
HY_SHORT = 3
HY_DECAY_SHORT = 0.3
HY_DECAY_LONG = 1.5
HY_TARGET = 1e-2
RG_BLOCKS = 4
RG_BLOCK = W_MIX // RG_BLOCKS
RG_CONV = 4
RG_C = 8.0
N_EXPERTS = 8
TOP_K = 2
IN_SIZES = (GLA_HEADS * GLA_DK, GLA_HEADS * GLA_DK, GLA_HEADS * GLA_DV, GLA_HEADS * GLA_DV,
            2 * GLA_RANK, W_MIX, 3 * W_MIX, W_MIX, W_MIX, N_BRANCH * D_MODEL)

VMEM_LIMIT_BYTES = 48 * 1024 * 1024


def _mm_kernel(x_ref, w_ref, o_ref):
    o_ref[...] = jnp.dot(x_ref[...].astype(jnp.bfloat16), w_ref[...],
                         preferred_element_type=jnp.float32)


def _pick_tile(n, cap):
    best = None
    for t in range(128, cap + 1, 128):
        if n % t == 0:
            best = t
    return best if best is not None else n


def pmm(x, w):
    lead = x.shape[:-1]
    k = x.shape[-1]
    n = w.shape[-1]
    x2 = x.reshape(-1, k)
    m = x2.shape[0]
    tm = 512 if m % 512 == 0 else m
    if k > 2048 and m % 256 == 0:
        tm = 256
    tn = n if k * n * 2 <= 6 * 1024 * 1024 else _pick_tile(n, 1024)
    out = pl.pallas_call(
        _mm_kernel,
        grid=(m // tm, n // tn),
        in_specs=[pl.BlockSpec((tm, k), lambda i, j: (i, 0)),
                  pl.BlockSpec((k, tn), lambda i, j: (0, j))],
        out_specs=pl.BlockSpec((tm, tn), lambda i, j: (i, j)),
        out_shape=jax.ShapeDtypeStruct((m, n), jnp.float32),
        compiler_params=pltpu.CompilerParams(
            dimension_semantics=("arbitrary", "arbitrary"),
            vmem_limit_bytes=VMEM_LIMIT_BYTES),
    )(x2, w.astype(jnp.bfloat16))
    return out.reshape(lead + (n,))


def _mm_multi_kernel(*refs, transposed):
    o_ref = refs[-1]
    n = (len(refs) - 1) // 2
    acc = None
    for i in range(n):
        dims = (((1,), (1 if transposed[i] else 0,)), ((), ()))
        t = lax.dot_general(refs[i][...].astype(jnp.bfloat16), refs[n + i][...], dims,
                            preferred_element_type=jnp.float32)
        acc = t if acc is None else acc + t
    o_ref[...] = acc


def pmm_multi(xs, ws, transposed, tm=256, tn=512):
    m = xs[0].shape[0]
    n = ws[0].shape[0 if transposed[0] else 1]
    tm = tm if m % tm == 0 else m
    tn = tn if n % tn == 0 else n
    in_specs = ([pl.BlockSpec((tm, x.shape[1]), lambda i, j: (i, 0)) for x in xs]
                + [pl.BlockSpec((tn, w.shape[1]), lambda i, j: (j, 0)) if t else
                   pl.BlockSpec((w.shape[0], tn), lambda i, j: (0, j)) for w, t in zip(ws, transposed)])
    return pl.pallas_call(
        functools.partial(_mm_multi_kernel, transposed=tuple(transposed)),
        grid=(m // tm, n // tn),
        in_specs=in_specs,
        out_specs=pl.BlockSpec((tm, tn), lambda i, j: (i, j)),
        out_shape=jax.ShapeDtypeStruct((m, n), jnp.float32),
        compiler_params=pltpu.CompilerParams(
            dimension_semantics=("arbitrary", "arbitrary"),
            vmem_limit_bytes=VMEM_LIMIT_BYTES),
    )(*xs, *[w.astype(jnp.bfloat16) for w in ws])


RG_SCAN_ROWS = 256


def _rg_scan_kernel(a_ref, b_ref, s0_ref, h_ref, fin_ref, st_ref, *, reverse, tb, nb):
    @pl.when(pl.program_id(0) == 0)
    def _():
        st_ref[...] = s0_ref[...]

    def body(r, hs):
        rr = (tb - 1 - r) if reverse else r
        out = []
        for i in range(nb):
            h = a_ref[i, pl.ds(rr, 1), :] * hs[i] + b_ref[i, pl.ds(rr, 1), :]
            h_ref[i, pl.ds(rr, 1), :] = h
            out.append(h)
        return tuple(out)

    hs = lax.fori_loop(0, tb, body, tuple(st_ref[i:i + 1, :] for i in range(nb)), unroll=8)
    for i in range(nb):
        st_ref[i:i + 1, :] = hs[i]
        fin_ref[i:i + 1, :] = hs[i]


def rg_scan(a, b, s0, reverse):
    nb, n_tok, ch = a.shape
    tb = min(RG_SCAN_ROWS, n_tok)
    nblk = n_tok // tb
    imap = (lambda k: (0, nblk - 1 - k, 0)) if reverse else (lambda k: (0, k, 0))
    return pl.pallas_call(
        functools.partial(_rg_scan_kernel, reverse=reverse, tb=tb, nb=nb),
        grid=(nblk,),
        in_specs=[pl.BlockSpec((nb, tb, ch), imap), pl.BlockSpec((nb, tb, ch), imap),
                  pl.BlockSpec((nb, ch), lambda k: (0, 0))],
        out_specs=[pl.BlockSpec((nb, tb, ch), imap), pl.BlockSpec((nb, ch), lambda k: (0, 0))],
        out_shape=[jax.ShapeDtypeStruct((nb, n_tok, ch), jnp.float32),
                   jax.ShapeDtypeStruct((nb, ch), jnp.float32)],
        scratch_shapes=[pltpu.VMEM((nb, ch), jnp.float32)],
        compiler_params=pltpu.CompilerParams(dimension_semantics=("arbitrary",)),
        name="rg_scan",
    )(a, b, s0)


RG_HALO = 8


def _rg_pre_kernel(x_ref, prev_ref, next_ref, wc_ref, wg_ref, bg_ref, c_ref, af_ref, bf_ref, ab_ref, bb_ref,
                   *, tb):
    k = pl.program_id(1)
    prev = jnp.where(k > 0, prev_ref[0], 0.0)
    nxt = jnp.where(k < pl.num_programs(1) - 1, next_ref[0], 0.0)
    ext = jnp.concatenate([prev, x_ref[0], nxt], axis=0)
    xc = sum(ext[RG_HALO - 2 + j:RG_HALO - 2 + j + tb, :] * wc_ref[j:j + 1, :] for j in range(RG_CONV))
    m = jnp.dot(xc.astype(jnp.bfloat16), wg_ref[...], preferred_element_type=jnp.float32) + bg_ref[...]
    for d, (a_ref, b_ref) in enumerate(((af_ref, bf_ref), (ab_ref, bb_ref))):
        r = jax.nn.sigmoid(m[:, (2 * d) * W_MIX:(2 * d + 1) * W_MIX])
        i = jax.nn.sigmoid(m[:, (2 * d + 1) * W_MIX:(2 * d + 2) * W_MIX])
        log_a = -r * c_ref[d:d + 1, :]
        a_ref[0] = jnp.exp(log_a)
        u = jnp.tanh(log_a)
        b_ref[0] = jnp.sqrt(-2.0 * u / (1.0 - u)) * (i * xc)


def rg_pre(p, w_conv, w_a, b_a, w_x, b_x, lam):
    nb, n_tok, _ = p.shape
    tb = min(ROW_TILE, n_tok)
    nblk = n_tok // tb
    hb = tb // RG_HALO
    n_halo = n_tok // RG_HALO
    col = P_RGX
    blockdiag = lambda w: jax.scipy.linalg.block_diag(*[w[i] for i in range(RG_BLOCKS)])
    wg = jnp.concatenate([blockdiag(w_a[0]), blockdiag(w_x[0]), blockdiag(w_a[1]), blockdiag(w_x[1])],
                         axis=1).astype(jnp.bfloat16)
    bg = jnp.concatenate([b_a[0].reshape(-1), b_x[0].reshape(-1), b_a[1].reshape(-1), b_x[1].reshape(-1)]
                         ).reshape(1, 4 * W_MIX)
    c = RG_C * jax.nn.softplus(-lam)
    blk = pl.BlockSpec((1, tb, W_MIX), lambda b, k: (b, k, 0))
    full = lambda a: pl.BlockSpec(a.shape, lambda b, k: (0,) * a.ndim)
    out = jax.ShapeDtypeStruct((nb, n_tok, W_MIX), jnp.float32)
    return pl.pallas_call(
        functools.partial(_rg_pre_kernel, tb=tb),
        grid=(nb, nblk),
        in_specs=[pl.BlockSpec((1, tb, W_MIX), lambda b, k: (b, k, col)),
                  pl.BlockSpec((1, RG_HALO, W_MIX), lambda b, k: (b, jnp.maximum(k * hb - 1, 0), col)),
                  pl.BlockSpec((1, RG_HALO, W_MIX), lambda b, k: (b, jnp.minimum((k + 1) * hb, n_halo - 1), col)),
                  full(w_conv), full(wg), full(bg), full(c)],
        out_specs=[blk, blk, blk, blk],
        out_shape=[out, out, out, out],
        compiler_params=pltpu.CompilerParams(dimension_semantics=("arbitrary", "arbitrary")),
        name="rg_pre",
    )(p, p, p, w_conv, wg, bg, c)


S5_T = 16
S5_NS = S5_GROUPS * S5_STATE
S5_SCAN_CHUNKS = 64


def _s5_scan_kernel(d_ref, s0_ref, a_ref, h_ref, fin_ref, st_ref, *, rc, nb):
    d = pl.program_id(0)

    @pl.when(pl.program_id(1) == 0)
    def _():
        st_ref[...] = s0_ref[0]

    ar = jnp.broadcast_to(a_ref[0, :, 0:S5_NS], (nb, S5_NS))
    ai = jnp.broadcast_to(a_ref[0, :, S5_NS:2 * S5_NS], (nb, S5_NS))

    def body(r, carry):
        hr, hi = carry
        rr = r + d * (rc - 1 - 2 * r)
        h_ref[rr, :, 0:S5_NS] = hr
        h_ref[rr, :, S5_NS:2 * S5_NS] = hi
        dr = d_ref[rr, :, 0:S5_NS]
        di = d_ref[rr, :, S5_NS:2 * S5_NS]
        return ar * hr - ai * hi + dr, ar * hi + ai * hr + di

    hr, hi = lax.fori_loop(0, rc, body, (st_ref[:, 0:S5_NS], st_ref[:, S5_NS:2 * S5_NS]))
    st_ref[:, 0:S5_NS] = hr
    st_ref[:, S5_NS:2 * S5_NS] = hi
    fin_ref[0, :, 0:S5_NS] = hr
    fin_ref[0, :, S5_NS:2 * S5_NS] = hi


def s5_scan(dmat, s0, a_t):
    n, nb, _ = dmat.shape
    rc = min(S5_SCAN_CHUNKS, n)
    nblk = n // rc
    w = 2 * S5_NS
    imap = lambda d, k: (k + d * (nblk - 1 - 2 * k), 0, d)
    return pl.pallas_call(
        functools.partial(_s5_scan_kernel, rc=rc, nb=nb),
        grid=(2, nblk),
        in_specs=[pl.BlockSpec((rc, nb, w), imap),
                  pl.BlockSpec((1, nb, w), lambda d, k: (d, 0, 0)),
                  pl.BlockSpec((1, 1, w), lambda d, k: (d, 0, 0))],
        out_specs=[pl.BlockSpec((rc, nb, w), imap),
                   pl.BlockSpec((1, nb, w), lambda d, k: (d, 0, 0))],
        out_shape=[jax.ShapeDtypeStruct((n, nb, 2 * w), jnp.float32),
                   jax.ShapeDtypeStruct((2, nb, w), jnp.float32)],
        scratch_shapes=[pltpu.VMEM((nb, w), jnp.float32)],
        compiler_params=pltpu.CompilerParams(dimension_semantics=("arbitrary", "arbitrary"),
                                             vmem_limit_bytes=VMEM_LIMIT_BYTES),
    )(dmat, s0, a_t)


def _cmul(ar, ai, br, bi):
    return ar * br - ai * bi, ar * bi + ai * br


def s5_prepare(lam_re, lam_im, log_dt, b_re, b_im, c_re, c_im, d_skip):
    f32 = jnp.float32
    hp = lax.Precision.HIGHEST
    t_len, g_n, p_n, h_n = S5_T, S5_GROUPS, S5_STATE, S5_GROUP
    bf16 = jnp.bfloat16
    eye_g = jnp.eye(g_n, dtype=f32)
    mask_gp = jnp.repeat(eye_g, p_n, axis=1).astype(bf16)[None, :, None, :]
    ar_t = jnp.arange(t_len)
    wd, wc, kk, a_t = [], [], [], []
    for d in range(2):
        lr = jnp.minimum(lam_re[d], S5_MAX_RE)
        li = lam_im[d]
        dt = jnp.exp(log_dt[d])[:, None]
        tt = jnp.arange(t_len + 1, dtype=f32)[:, None, None]
        mag = jnp.exp(lr * dt * tt)
        ang = li * dt * tt
        pr, pi = mag * jnp.cos(ang), mag * jnp.sin(ang)
        nr, ni = pr[1] - 1.0, pi[1]
        den = lr * lr + li * li
        qr, qi = (nr * lr + ni * li) / den, (ni * lr - nr * li) / den
        bbr, bbi = _cmul(qr[..., None], qi[..., None], b_re[d], b_im[d])
        cr, ci = c_re[d], c_im[d]

        idx = (t_len - 1 - ar_t) if d == 0 else ar_t
        wr, wi = _cmul(pr[idx][..., None], pi[idx][..., None], bbr[None], bbi[None])

        def place_d(w):
            wt = w.transpose(0, 3, 1, 2).reshape(t_len, 1, h_n, g_n * p_n).astype(bf16)
            return (wt * mask_gp).reshape(t_len * W_MIX, g_n * p_n)

        wd.append(jnp.concatenate([place_d(wr), place_d(wi)], axis=1))

        idx2 = (ar_t + 1) if d == 0 else (t_len - ar_t)
        cwr, cwi = _cmul(cr[None], ci[None], pr[idx2][:, :, None, :], pi[idx2][:, :, None, :])

        def place_c(w):
            wt = w.transpose(0, 2, 1, 3).reshape(t_len, 1, h_n, g_n * p_n).astype(bf16)
            return (wt * mask_gp).reshape(t_len * W_MIX, g_n * p_n)

        wc.append(jnp.concatenate([place_c(cwr), place_c(-cwi)], axis=1))

        er, ei = _cmul(pr[:t_len][:, :, None, :], pi[:t_len][:, :, None, :], cr[None], ci[None])
        kk.append(jnp.einsum('tghp,gpk->tghk', er, bbr, precision=hp)
                  - jnp.einsum('tghp,gpk->tghk', ei, bbi, precision=hp))
        a_t.append(jnp.concatenate([pr[t_len].reshape(1, -1), pi[t_len].reshape(1, -1)], axis=1))

    lag = ar_t[None, :] - ar_t[:, None]
    mf = jnp.where((lag >= 0)[..., None, None, None], kk[0][jnp.clip(lag, 0, t_len - 1)], 0.0)
    mb = jnp.where((lag <= 0)[..., None, None, None], kk[1][jnp.clip(-lag, 0, t_len - 1)], 0.0)
    skip = (jnp.eye(t_len, dtype=f32)[:, :, None, None, None] * d_skip[None, None, :, :, None]
            * jnp.eye(h_n, dtype=f32)[None, None, None])
    m = mf + mb + skip
    mt = m.transpose(0, 4, 1, 2, 3).reshape(t_len, 1, h_n, t_len * W_MIX).astype(bf16)
    mask_igh = jnp.tile(jnp.repeat(eye_g, h_n, axis=1), (1, t_len)).astype(bf16)[None, :, None, :]
    wk = (mt * mask_igh).reshape(t_len * W_MIX, t_len * W_MIX)
    return (jnp.concatenate(wd, axis=1), wk, jnp.concatenate(wc, axis=1), jnp.stack(a_t))


GLA_BLOCK = 1024


def _gla_kernel(q_ref, k_ref, v_ref, g_ref, wup_ref, bup_ref, s0_ref, o_ref, fin_ref, st_ref,
                *, reverse, tb):
    f32, bf16 = jnp.float32, jnp.bfloat16
    cc = GLA_CHUNK

    @pl.when(pl.program_id(1) == 0)
    def _():
        st_ref[...] = s0_ref[0]

    r_i = lax.broadcasted_iota(jnp.int32, (cc, cc), 0)
    c_i = lax.broadcasted_iota(jnp.int32, (cc, cc), 1)
    keep = (c_i >= r_i) if reverse else (c_i <= r_i)
    tri = keep.astype(bf16)
    nt = (((1,), (1,)), ((), ()))
    tn = (((0,), (0,)), ((), ()))
    n_ch = tb // cc
    chunks = range(n_ch)
    heads = range(GLA_HEADS)
    hsl = [slice(h * GLA_DK, (h + 1) * GLA_DK) for h in heads]
    rows = [slice(c * cc, (c + 1) * cc) for c in chunks]

    z = _dot3(*_split_bf16(g_ref[0]), *_split_bf16(wup_ref[...])) + bup_ref[...]
    la = (jnp.minimum(z, 0.0) - jnp.log1p(jnp.exp(-jnp.abs(z)))) * (1.0 / GLA_TAU)
    la_hi, la_lo = _split_bf16(la)
    cum = [jnp.dot(tri, la_hi[rows[c]], preferred_element_type=f32)
           + jnp.dot(tri, la_lo[rows[c]], preferred_element_type=f32) for c in chunks]
    last = [cm[0:1, :] if reverse else cm[cc - 1:cc, :] for cm in cum]
    k = [k_ref[0, rows[c], :] for c in chunks]
    q_in = [(q_ref[0, rows[c], :] * (GLA_DK ** -0.5) * jnp.exp(cum[c])).astype(bf16) for c in chunks]
    k_in = [(k[c] * jnp.exp(-cum[c])).astype(bf16) for c in chunks]
    k_out = [(k[c] * jnp.exp(last[c] - cum[c])).astype(bf16) for c in chunks]
    dec = [jnp.exp(last[c]) for c in chunks]
    vb = [v_ref[0, rows[c], :].astype(bf16) for c in chunks]
    att = [[jnp.where(keep, lax.dot_general(q_in[c][:, s], k_in[c][:, s], nt, preferred_element_type=f32),
                      0.0).astype(bf16) for s in hsl] for c in chunks]
    o_intra = [[jnp.dot(att[c][h], vb[c][:, hsl[h]], preferred_element_type=f32) for h in heads]
               for c in chunks]
    d_state = [[lax.dot_general(vb[c][:, s], k_out[c][:, s], tn, preferred_element_type=f32) for s in hsl]
               for c in chunks]

    st = [st_ref[h] for h in heads]
    st_in = [None] * n_ch
    for c in (reversed(chunks) if reverse else chunks):
        st_in[c] = [s.astype(bf16) for s in st]
        st = [st[h] * dec[c][:, hsl[h]] + d_state[c][h] for h in heads]
    for h in heads:
        st_ref[h] = st[h]
        fin_ref[0, h] = st[h]

    for c in chunks:
        o_ref[0, rows[c], :] = jnp.concatenate(
            [o_intra[c][h] + lax.dot_general(q_in[c][:, hsl[h]], st_in[c][h], nt, preferred_element_type=f32)
             for h in heads], axis=1)


def gla_dir(p, gd, w_up, b_up, s0, reverse):
    nb, n_tok, _ = p.shape
    tb = min(GLA_BLOCK, n_tok)
    nblk = n_tok // tb
    blk = (lambda k: nblk - 1 - k) if reverse else (lambda k: k)
    col = lambda c: pl.BlockSpec((1, tb, W_MIX), lambda b, k: (b, blk(k), c))
    st_shape = (GLA_HEADS, GLA_DV, GLA_DK)
    return pl.pallas_call(
        functools.partial(_gla_kernel, reverse=reverse, tb=tb),
        grid=(nb, nblk),
        in_specs=[col(P_Q), col(P_Q + 1), col(P_Q + 2),
                  pl.BlockSpec((1, tb, GLA_RANK), lambda b, k: (b, blk(k), 0)),
                  pl.BlockSpec((GLA_RANK, W_MIX), lambda b, k: (0, 0)),
                  pl.BlockSpec((1, W_MIX), lambda b, k: (0, 0)),
                  pl.BlockSpec((1,) + st_shape, lambda b, k: (b, 0, 0, 0))],
        out_specs=[pl.BlockSpec((1, tb, W_MIX), lambda b, k: (b, blk(k), 0)),
                   pl.BlockSpec((1,) + st_shape, lambda b, k: (b, 0, 0, 0))],
        out_shape=[jax.ShapeDtypeStruct((nb, n_tok, W_MIX), jnp.float32),
                   jax.ShapeDtypeStruct((nb,) + st_shape, jnp.float32)],
        scratch_shapes=[pltpu.VMEM(st_shape, jnp.float32)],
        compiler_params=pltpu.CompilerParams(dimension_semantics=("arbitrary", "arbitrary"),
                                             vmem_limit_bytes=VMEM_LIMIT_BYTES),
        name="gla",
    )(p, p, p, gd, w_up, b_up.reshape(1, W_MIX), s0)


FFN_TM = 512
CAST_ROWS = 256


def _swiglu_kernel(te_ref, nu_ref, x_ref, wg_ref, wu_ref, wd_ref, *rest, nf, residual):
    del te_ref
    o_ref, acc_ref = rest[-2:]
    j = pl.program_id(1)

    @pl.when(jnp.logical_and(pl.program_id(0) >= nu_ref[0], j == nf - 1))
    def _():
        o_ref[...] = jnp.zeros_like(o_ref)

    @pl.when(pl.program_id(0) < nu_ref[0])
    def _():
        x = x_ref[...]
        g = jnp.dot(x, wg_ref[0], preferred_element_type=jnp.float32)
        u = jnp.dot(x, wu_ref[0], preferred_element_type=jnp.float32)
        a = (g * jax.nn.sigmoid(g) * u).astype(jnp.bfloat16)
        part = jnp.dot(a, wd_ref[0], preferred_element_type=jnp.float32)

        @pl.when(j == 0)
        def _():
            acc_ref[...] = part

        @pl.when(j > 0)
        def _():
            acc_ref[...] += part

        @pl.when(j == nf - 1)
        def _():
            if residual:
                xres_ref, gm_ref = rest[:2]
                o_ref[...] = (xres_ref[...] + gm_ref[0] * acc_ref[...]).astype(o_ref.dtype)
            else:
                o_ref[...] = acc_ref[...].astype(o_ref.dtype)


def _cast_split_kernel(a_ref, b_ref, oa_ref, ob_ref):
    oa_ref[...] = a_ref[...].astype(oa_ref.dtype)
    ob_ref[...] = b_ref[...].astype(ob_ref.dtype)


def cast_split_bf16(w):
    e, r, c2 = w.shape
    c = c2 // 2
    rows = min(CAST_ROWS, r)
    half = lambda h: pl.BlockSpec((1, rows, c), lambda i, j: (i, j, h))
    out = jax.ShapeDtypeStruct((e, r, c), jnp.bfloat16)
    return pl.pallas_call(
        _cast_split_kernel,
        grid=(e, r // rows),
        in_specs=[half(0), half(1)],
        out_specs=[half(0), half(0)],
        out_shape=[out, out],
        compiler_params=pltpu.CompilerParams(dimension_semantics=("arbitrary", "arbitrary"),
                                             vmem_limit_bytes=VMEM_LIMIT_BYTES),
        name="cast_split_bf16",
    )(w, w)


def _cast_rows_kernel(a_ref, b_ref, o_ref):
    o_ref[0, 0] = a_ref[0, 0].astype(o_ref.dtype)
    o_ref[0, 1] = b_ref[0, 0].astype(o_ref.dtype)


def cast_rows_bf16(w):
    e, r, c = w.shape
    rows = min(CAST_ROWS, r // 2)
    w4 = w.reshape(e, 2, r // 2, c)
    half = lambda h: pl.BlockSpec((1, 1, rows, c), lambda i, j: (i, h, j, 0))
    out = pl.pallas_call(
        _cast_rows_kernel,
        grid=(e, r // 2 // rows),
        in_specs=[half(0), half(1)],
        out_specs=pl.BlockSpec((1, 2, rows, c), lambda i, j: (i, 0, j, 0)),
        out_shape=jax.ShapeDtypeStruct(w4.shape, jnp.bfloat16),
        compiler_params=pltpu.CompilerParams(dimension_semantics=("arbitrary", "arbitrary"),
                                             vmem_limit_bytes=VMEM_LIMIT_BYTES),
        name="cast_rows_bf16",
    )(w4, w4)
    return out.reshape(e, r, c)


def grouped_swiglu(tile_expert, n_used, xs, w_g, w_u, w_down, nf, residual=None, out_dtype=jnp.float32):
    m, d = xs.shape
    f = w_down.shape[1]
    tf = f // nf
    n_tiles = m // FFN_TM
    in_specs = [pl.BlockSpec((FFN_TM, d), lambda t, j, te, nu: (t, 0)),
                pl.BlockSpec((1, d, tf), lambda t, j, te, nu: (te[t], 0, j)),
                pl.BlockSpec((1, d, tf), lambda t, j, te, nu: (te[t], 0, j)),
                pl.BlockSpec((1, tf, d), lambda t, j, te, nu: (te[t], j, 0))]
    extra = ()
    if residual is not None:
        rows_per_mod = m // residual[1].shape[0]
        in_specs += [pl.BlockSpec((FFN_TM, d), lambda t, j, te, nu: (t, 0)),
                     pl.BlockSpec((1, 1, d), lambda t, j, te, nu: ((t * FFN_TM) // rows_per_mod, 0, 0))]
        extra = tuple(residual)
    grid_spec = pltpu.PrefetchScalarGridSpec(
        num_scalar_prefetch=2,
        grid=(n_tiles, nf),
        in_specs=in_specs,
        out_specs=pl.BlockSpec((FFN_TM, d), lambda t, j, te, nu: (t, 0)),
        scratch_shapes=[pltpu.VMEM((FFN_TM, d), jnp.float32)])
    return pl.pallas_call(
        functools.partial(_swiglu_kernel, nf=nf, residual=residual is not None),
        grid_spec=grid_spec,
        out_shape=jax.ShapeDtypeStruct((m, d), out_dtype),
        compiler_params=pltpu.CompilerParams(
            dimension_semantics=("arbitrary", "arbitrary"),
            vmem_limit_bytes=VMEM_LIMIT_BYTES),
        name="grouped_swiglu",
    )(tile_expert, n_used, xs, w_g, w_u, w_down, *extra)


ROW_TILE = 512
P_HY, P_Q, P_OG, P_S5, P_RGX, P_RGG = 0, 3, 6, 7, 8, 9
P_GDN = 10 * W_MIX
P_COLS = P_GDN + 128


def _rms_mod(x, g, shift, scale):
    y = x * lax.rsqrt(jnp.mean(x * x, axis=-1, keepdims=True) + EPS) * g
    return y * (1.0 + scale) + shift


def _front_kernel(x_ref, sh_ref, sc_ref, g_ref, w_ref, *rest):
    o_ref = rest[-1]
    x = x_ref[...] + rest[0][...] if len(rest) == 2 else x_ref[...]
    h = _rms_mod(x, g_ref[...], sh_ref[0], sc_ref[0])
    o_ref[...] = jnp.dot(h.astype(jnp.bfloat16), w_ref[...], preferred_element_type=jnp.float32)


def _pos_spec(pos, tile):
    if pos is None:
        return []
    n_blk = pos.shape[0] // tile
    return [pl.BlockSpec((tile, D_MODEL), lambda i: (i % n_blk, 0))]


def _mod_spec(rows_per_mod):
    return pl.BlockSpec((1, 1, D_MODEL), lambda i: ((i * ROW_TILE) // rows_per_mod, 0, 0))


def front(x, shift, scale, g, w, pos=None):
    m = x.shape[0]
    n_mod = shift.shape[0]
    n_out = w.shape[1]
    mod = _mod_spec(m // n_mod)
    return pl.pallas_call(
        _front_kernel,
        grid=(m // ROW_TILE,),
        in_specs=[pl.BlockSpec((ROW_TILE, D_MODEL), lambda i: (i, 0)), mod, mod,
                  pl.BlockSpec((1, D_MODEL), lambda i: (0, 0)),
                  pl.BlockSpec((D_MODEL, n_out), lambda i: (0, 0))] + _pos_spec(pos, ROW_TILE),
        out_specs=pl.BlockSpec((ROW_TILE, n_out), lambda i: (i, 0)),
        out_shape=jax.ShapeDtypeStruct((m, n_out), jnp.float32),
        compiler_params=pltpu.CompilerParams(dimension_semantics=("arbitrary",),
                                             vmem_limit_bytes=VMEM_LIMIT_BYTES),
        name="front",
    )(x, shift, scale, g.reshape(1, D_MODEL), w, *([] if pos is None else [pos]))


MERGE_TILE = 256
ROUTER_PAD = 128
N_MERGE_IN = 25


def _merge_kernel(*refs, with_router, with_pos):
    (x_ref, sh1_ref, sc1_ref, gm_ref, sh2_ref, sc2_ref, g1_ref, g2_ref,
     of_ref, ob_ref, og_ref, s5_ref, hc_ref, hz_ref, hx0_ref, rf_ref, rb_ref, rgg_ref,
     wg_ref, wb_ref, wo_ref, wglu_ref, bglu_ref, havg_ref, hbias_ref) = refs[:N_MERGE_IN]
    f32, bf16 = jnp.float32, jnp.bfloat16
    hp = lax.Precision.HIGHEST
    rest = list(refs[N_MERGE_IN:])
    router_refs = [rest.pop(0), rest.pop(0)] if with_router else None
    x = x_ref[...] + rest.pop(0)[...] if with_pos else x_ref[...]
    hb = _rms_mod(x, g1_ref[...], sh1_ref[0], sc1_ref[0]).astype(bf16)

    o = of_ref[...] + ob_ref[...]
    ms = jnp.dot(o * o, havg_ref[...], precision=hp, preferred_element_type=f32)
    og = og_ref[...]
    gla = o * lax.rsqrt(ms + EPS) * (og * jax.nn.sigmoid(og))
    g5 = jax.nn.gelu(s5_ref[...])
    s5o = g5 * jax.nn.sigmoid(jnp.dot(g5.astype(bf16), wglu_ref[...], preferred_element_type=f32)
                              + bglu_ref[...])
    rgo = (rf_ref[...] + rb_ref[...]) * jax.nn.gelu(rgg_ref[...])
    hyo = hx0_ref[...] * (hc_ref[...] + hz_ref[...] * hbias_ref[...])
    branches = (gla, s5o, hyo, rgo)

    y = None
    for k in range(N_BRANCH):
        gate = jax.nn.sigmoid(jnp.dot(hb, wg_ref[:, k * D_MODEL:(k + 1) * D_MODEL],
                                      preferred_element_type=f32))
        t = gate * jnp.dot(branches[k].astype(bf16), wb_ref[k], preferred_element_type=f32)
        y = t if y is None else y + t
    out = jnp.dot(y.astype(bf16), wo_ref[...], preferred_element_type=f32)
    xn = x + gm_ref[0] * out
    h2 = _rms_mod(xn, g2_ref[...], sh2_ref[0], sc2_ref[0])
    if with_router:
        (rw_ref, rb2_ref), (xo_ref, h2_ref, lg_ref) = router_refs, rest
        lg_ref[...] = _dot3(*_split_bf16(h2), *_split_bf16(rw_ref[...])) + rb2_ref[...]
    else:
        xo_ref, h2_ref = rest
    xo_ref[...] = xn
    h2_ref[...] = h2.astype(bf16)


def merge(x, mods, g1, g2, o_f, o_b, p, s5y, hy, r_f, r_b, wg, wb, wo, wglu, bglu, hbias, router=None,
          pos=None):
    m = x.shape[0]
    tm = MERGE_TILE
    n_mod = mods[0].shape[0]
    rows_per_mod = m // n_mod
    mod = pl.BlockSpec((1, 1, D_MODEL), lambda i: ((i * tm) // rows_per_mod, 0, 0))
    row = pl.BlockSpec((tm, D_MODEL), lambda i: (i, 0))
    br = pl.BlockSpec((tm, W_MIX), lambda i: (i, 0))
    pcol = lambda c: pl.BlockSpec((tm, W_MIX), lambda i: (i, c))
    full = lambda a: pl.BlockSpec(a.shape, lambda i: (0,) * a.ndim)
    head = jnp.arange(W_MIX) // GLA_DV
    havg = (head[:, None] == head[None, :]).astype(jnp.float32) / GLA_DV
    vec = lambda v: v.reshape(1, -1)
    consts = [wg, wb, wo, wglu, vec(bglu), havg, vec(hbias)]
    out_specs = [row, pl.BlockSpec((tm, D_MODEL), lambda i: (i, 0))]
    out_shape = [jax.ShapeDtypeStruct((m, D_MODEL), jnp.float32),
                 jax.ShapeDtypeStruct((m, D_MODEL), jnp.bfloat16)]
    if router is not None:
        rw, rbias = router
        pad = ROUTER_PAD - rw.shape[1]
        consts += [jnp.pad(rw, ((0, 0), (0, pad))), jnp.pad(rbias, (0, pad)).reshape(1, -1)]
        out_specs.append(pl.BlockSpec((tm, ROUTER_PAD), lambda i: (i, 0)))
        out_shape.append(jax.ShapeDtypeStruct((m, ROUTER_PAD), jnp.float32))
    in_specs = ([row] + [mod] * 5 + [full(vec(g1)), full(vec(g2)), br, br, pcol(P_OG), br, br, br, br, br, br,
                                     pcol(P_RGG)] + [full(a) for a in consts] + _pos_spec(pos, tm))
    return pl.pallas_call(
        functools.partial(_merge_kernel, with_router=router is not None, with_pos=pos is not None),
        grid=(m // tm,),
        in_specs=in_specs,
        out_specs=out_specs,
        out_shape=out_shape,
        compiler_params=pltpu.CompilerParams(dimension_semantics=("arbitrary",),
                                             vmem_limit_bytes=VMEM_LIMIT_BYTES),
        name="merge",
    )(x, *mods, vec(g1), vec(g2), o_f, o_b, p, s5y, *hy, r_f, r_b, p, *consts, *([] if pos is None else [pos]))


def _combine_kernel(x_ref, y0_ref, y1_ref, w_ref, gm_ref, g_ref, o_ref, *, final_norm):
    w = w_ref[...]
    y = w[:, 0:1] * y0_ref[...].astype(jnp.float32) + w[:, 1:2] * y1_ref[...].astype(jnp.float32)
    xn = x_ref[...] + gm_ref[0] * y
    if final_norm:
        xn = xn * lax.rsqrt(jnp.mean(xn * xn, axis=-1, keepdims=True) + EPS) * g_ref[...]
    o_ref[...] = xn


def moe_combine(x, yk, w, gate_mod, final_g):
    m = x.shape[0]
    n_mod = gate_mod.shape[0]
    g = jnp.ones((1, D_MODEL), jnp.float32) if final_g is None else final_g.reshape(1, D_MODEL)
    return pl.pallas_call(
        functools.partial(_combine_kernel, final_norm=final_g is not None),
        grid=(m // ROW_TILE,),
        in_specs=[pl.BlockSpec((ROW_TILE, D_MODEL), lambda i: (i, 0)),
                  pl.BlockSpec((ROW_TILE, D_MODEL), lambda i: (i, 0)),
                  pl.BlockSpec((ROW_TILE, D_MODEL), lambda i: (i + m // ROW_TILE, 0)),
                  pl.BlockSpec((ROW_TILE, TOP_K), lambda i: (i, 0)),
                  _mod_spec(m // n_mod),
                  pl.BlockSpec((1, D_MODEL), lambda i: (0, 0))],
        out_specs=pl.BlockSpec((ROW_TILE, D_MODEL), lambda i: (i, 0)),
        out_shape=jax.ShapeDtypeStruct((m, D_MODEL), jnp.float32),
        compiler_params=pltpu.CompilerParams(dimension_semantics=("arbitrary",)),
        name="moe_combine",
    )(x, yk, yk, w, gate_mod, g)


HY_LANES = 128
HY_STEP_ROWS = 1024
HY_MIN_LEN = 1024


def _hyena_pre_kernel(x_ref, prev_ref, next_ref, w_ref, zt_ref, z_ref, x0_ref, *, tb):
    k = pl.program_id(1)
    prev = jnp.where(k > 0, prev_ref[0], 0.0)
    nxt = jnp.where(k < pl.num_programs(1) - 1, next_ref[0], 0.0)
    ext = jnp.concatenate([prev, x_ref[0], nxt], axis=0)
    pc = sum(ext[RG_HALO - 1 + j:RG_HALO - 1 + j + tb, :] * w_ref[j:j + 1, :] for j in range(HY_SHORT))
    v, x0, x1 = pc[:, :W_MIX], pc[:, W_MIX:2 * W_MIX], pc[:, 2 * W_MIX:]
    z = x1 * v
    z_ref[0] = z
    x0_ref[0] = x0
    zt_ref[0] = z.T


def hyena_pre(p, w_short):
    nb, n_tok, _ = p.shape
    tb = min(ROW_TILE, n_tok)
    hb = tb // RG_HALO
    n_halo = n_tok // RG_HALO
    wide = 3 * W_MIX
    tok = pl.BlockSpec((1, tb, W_MIX), lambda b, k: (b, k, 0))
    tok_out = jax.ShapeDtypeStruct((nb, n_tok, W_MIX), jnp.float32)
    return pl.pallas_call(
        functools.partial(_hyena_pre_kernel, tb=tb),
        grid=(nb, n_tok // tb),
        in_specs=[pl.BlockSpec((1, tb, wide), lambda b, k: (b, k, P_HY)),
                  pl.BlockSpec((1, RG_HALO, wide), lambda b, k: (b, jnp.maximum(k * hb - 1, 0), P_HY)),
                  pl.BlockSpec((1, RG_HALO, wide), lambda b, k: (b, jnp.minimum((k + 1) * hb, n_halo - 1), P_HY)),
                  pl.BlockSpec(w_short.shape, lambda b, k: (0, 0))],
        out_specs=[pl.BlockSpec((1, W_MIX, tb), lambda b, k: (b, 0, k)), tok, tok],
        out_shape=[jax.ShapeDtypeStruct((nb, W_MIX, n_tok), jnp.float32), tok_out, tok_out],
        compiler_params=pltpu.CompilerParams(dimension_semantics=("arbitrary", "arbitrary"),
                                             vmem_limit_bytes=VMEM_LIMIT_BYTES),
        name="hyena_pre",
    )(p, p, p, w_short)


def _split_bf16(a):
    hi = a.astype(jnp.bfloat16)
    lo = (a - hi.astype(jnp.float32)).astype(jnp.bfloat16)
    return hi, lo


def _dot3(a_hi, a_lo, b_hi, b_lo):
    d = functools.partial(jnp.dot, preferred_element_type=jnp.float32)
    return d(a_hi, b_hi) + (d(a_lo, b_hi) + d(a_hi, b_lo))


def _hyena_dft_consts(n1):
    n = n1 * HY_LANES
    ka = np.arange(n1, dtype=np.float64)[:, None]
    f1_ang = 2.0 * np.pi * ka * np.arange(n1 // 2, dtype=np.float64)[None, :] / n1
    f1r, f1i = np.cos(f1_ang), -np.sin(f1_ang)
    tw_ang = 2.0 * np.pi * ka * np.arange(HY_LANES, dtype=np.float64)[None, :] / n
    lo = np.arange(HY_LANES, dtype=np.float64)
    f2_ang = 2.0 * np.pi * lo[:, None] * lo[None, :] / HY_LANES
    f2r, f2i = np.cos(f2_ang), -np.sin(f2_ang)
    fwd_rows = np.concatenate([f1r, f1i], axis=0)
    fwd_lanes = np.block([[f2r, f2i], [-f2i, f2r]])
    inv_lanes = np.block([[f2r, -f2i], [f2i, f2r]])
    inv_rows = np.concatenate([f1r.T, f1i.T], axis=1) / n
    out = []
    for m in (fwd_rows, fwd_lanes, inv_lanes, inv_rows):
        m32 = jnp.asarray(m, jnp.float32)
        out.extend(_split_bf16(m32))
    return out + [jnp.asarray(np.cos(tw_ang), jnp.float32), jnp.asarray(-np.sin(tw_ang), jnp.float32)]


def _hyena_fft_kernel(*refs, n1, ns, spectrum):
    bf16 = jnp.bfloat16
    if spectrum:
        z_ref, f1h, f1l, f2h, f2l, twr_ref, twi_ref, o_ref = refs
        rows_dft = lambda t: _dot3(f1h[...], f1l[...], *_split_bf16(t))
        lanes_dft = lambda t: _dot3(*_split_bf16(t), f2h[...], f2l[...])
    else:
        z_ref, hf_ref, f1h, f2h, g2h, fih, twr_ref, twi_ref, o_ref = refs
        mm = lambda a, b: jnp.dot(a.astype(bf16), b.astype(bf16), preferred_element_type=jnp.float32)
        rows_dft = lambda t: mm(f1h[...], t)
        lanes_dft = lambda t: mm(t, f2h[...])
    w = HY_LANES
    twr, twi = twr_ref[...], twi_ref[...]
    z2 = jnp.concatenate([z_ref[0, s] for s in range(ns)], axis=1)
    a2 = rows_dft(z2)
    rows = []
    for s in range(ns):
        r, i = a2[:n1, s * w:(s + 1) * w], a2[n1:, s * w:(s + 1) * w]
        rows.append(jnp.concatenate([r * twr - i * twi, r * twi + i * twr], axis=1))
    x = lanes_dft(jnp.concatenate(rows, axis=0))
    if spectrum:
        o_ref[...] = x.reshape(ns, n1, 2 * w)
        return
    h = hf_ref[...].reshape(ns * n1, 2 * w)
    xr, xi, hr, hi = x[:, :w], x[:, w:], h[:, :w], h[:, w:]
    y = jnp.concatenate([xr * hr - xi * hi, xr * hi + xi * hr], axis=1)
    g = mm(y, g2h[...])
    cr, ci = [], []
    for s in range(ns):
        gr, gi = g[s * n1:(s + 1) * n1, :w], g[s * n1:(s + 1) * n1, w:]
        cr.append(gr * twr + gi * twi)
        ci.append(gi * twr - gr * twi)
    gc = jnp.concatenate([jnp.concatenate(cr, axis=1), jnp.concatenate(ci, axis=1)], axis=0)
    y2 = mm(fih[...], gc)
    for s in range(ns):
        o_ref[0, s] = y2[:, s * w:(s + 1) * w]


def hyena_fft(zt, hf=None):
    nb, ch, half, w = zt.shape
    n1 = 2 * half
    ns = HY_STEP_ROWS // n1
    f1h, f1l, f2h, f2l, g2h, _, fih, _, twr, twi = _hyena_dft_consts(n1)
    full = lambda a: pl.BlockSpec(a.shape, lambda b, c: (0,) * a.ndim)
    zspec = pl.BlockSpec((1, ns, half, w), lambda b, c: (b, c, 0, 0))
    if hf is None:
        consts = [f1h, f1l, f2h, f2l, twr, twi]
        in_specs, args = [zspec], [zt]
        out_spec = pl.BlockSpec((ns, n1, 2 * w), lambda b, c: (b * (ch // ns) + c, 0, 0))
        out_shape = jax.ShapeDtypeStruct((nb * ch, n1, 2 * w), jnp.float32)
    else:
        consts = [f1h, f2h, g2h, fih, twr, twi]
        in_specs = [zspec, pl.BlockSpec((ns, n1, 2 * w), lambda b, c: (c, 0, 0))]
        args = [zt, hf]
        out_spec = zspec
        out_shape = jax.ShapeDtypeStruct(zt.shape, jnp.float32)
    return pl.pallas_call(
        functools.partial(_hyena_fft_kernel, n1=n1, ns=ns, spectrum=hf is None),
        grid=(nb, ch // ns),
        in_specs=in_specs + [full(a) for a in consts],
        out_specs=out_spec,
        out_shape=out_shape,
        compiler_params=pltpu.CompilerParams(dimension_semantics=("arbitrary", "arbitrary"),
                                             vmem_limit_bytes=VMEM_LIMIT_BYTES),
        name="hyena_fft",
    )(*args, *consts)


def rmsnorm(x, g):
    y = x * lax.rsqrt(jnp.mean(x * x, axis=-1, keepdims=True) + EPS)
    return y * g


def adaln(cond, w, b):
    return jax.nn.silu(cond) @ w + b


def grid_pos_embed(n_tokens, dim):
    rows = n_tokens // GRID_W
    q = dim // 4
    omega = 1.0 / (10000.0 ** (jnp.arange(q, dtype=jnp.float32) / q))
    r = jnp.arange(rows, dtype=jnp.float32)[:, None] * omega
    cc = jnp.arange(GRID_W, dtype=jnp.float32)[:, None] * omega
    er = jnp.concatenate([jnp.sin(r), jnp.cos(r)], axis=-1)
    ec = jnp.concatenate([jnp.sin(cc), jnp.cos(cc)], axis=-1)
    emb = jnp.concatenate([jnp.broadcast_to(er[:, None], (rows, GRID_W, dim // 2)),
                           jnp.broadcast_to(ec[None], (rows, GRID_W, dim // 2))], axis=-1)
    return emb.reshape(rows * GRID_W, dim)


def gla_mixer(p, w_up, b_up, s0):
    gdn = p[..., P_GDN:P_GDN + 2 * GLA_RANK]
    outs, finals = [], []
    for d in range(2):
        od, sd = gla_dir(p, gdn[..., d * GLA_RANK:(d + 1) * GLA_RANK], w_up[d], b_up[d], s0[d], d == 1)
        outs.append(od)
        finals.append(sd)
    return outs, jnp.stack(finals)


def s5_mixer(u, prep, s0):
    wd, wk, wc, a_t = prep
    b_, n_tok, _ = u.shape
    n = n_tok // S5_T
    u2 = u.reshape(b_, n, S5_T * W_MIX).transpose(1, 0, 2).reshape(n * b_, S5_T * W_MIX)
    dmat = pmm(u2, wd)
    hmat, fin = s5_scan(dmat.reshape(n, b_, 4 * S5_NS), s0, a_t)
    y2 = pmm_multi([u2, hmat.reshape(n * b_, 4 * S5_NS)], [wk, wc], (False, True))
    y = y2.reshape(n, b_, S5_T, W_MIX).transpose(1, 0, 2, 3).reshape(b_, n_tok, W_MIX)
    return y, fin


def hyena_filters(n_tok, w1, b1, w2, b2, w3, freq):
    f32 = jnp.float32
    t = jnp.arange(n_tok, dtype=f32)[:, None]
    bands = jnp.linspace(1e-4, HY_BANDS - 1, HY_BANDS, dtype=f32)[None]
    ang = 2.0 * math.pi * bands * t / n_tok
    z = jnp.concatenate([t / n_tok, jnp.cos(ang), jnp.sin(ang)], axis=-1)
    hp = lax.Precision.HIGHEST
    h = jnp.sin(freq * (jnp.dot(z, w1, precision=hp) + b1))
    h = jnp.sin(freq * (jnp.dot(h, w2, precision=hp) + b2))
    h = jnp.dot(h, w3, precision=hp)
    t01 = t / max(n_tok - 1, 1)
    deltas = jnp.abs(jnp.linspace(math.log(HY_TARGET) / HY_DECAY_SHORT,
                                  math.log(HY_TARGET) / HY_DECAY_LONG, W_MIX, dtype=f32))
    h = h * jnp.exp(-t01 * jnp.tile(deltas, 2))
    return h / (jnp.sum(jnp.abs(h), axis=0, keepdims=True) + EPS)


def hyena_mixer(p, w_short, w1, b1, w2, b2, w3, freq):
    nb, n_tok, _ = p.shape
    zt, z, x0 = hyena_pre(p, w_short)
    n_pad = max(n_tok, HY_MIN_LEN)
    half = n_pad // HY_LANES

    def frames(t):
        t = jnp.pad(t, [(0, 0)] * (t.ndim - 1) + [(0, n_pad - n_tok)])
        return t.reshape(t.shape[:-1] + (half, HY_LANES))

    filt = hyena_filters(n_tok, w1, b1, w2, b2, w3, freq)
    spec = hyena_fft(frames(filt.T)[None])
    sf, sb = spec[:W_MIX], spec[W_MIX:]
    hfreq = jnp.concatenate([sf[..., :HY_LANES] + sb[..., :HY_LANES],
                             sf[..., HY_LANES:] - sb[..., HY_LANES:]], axis=-1)
    conv = hyena_fft(frames(zt), hfreq)
    conv = conv.reshape(nb, W_MIX, n_pad)[:, :, :n_tok].transpose(0, 2, 1)
    return conv, z, x0


def rglru_mixer(p, w_conv, w_a, b_a, w_x, b_x, lam, s0):
    a_f, b_f, a_b, b_b = rg_pre(p, w_conv, w_a, b_a, w_x, b_x, lam)
    h_f, fin_f = rg_scan(a_f, b_f, s0[0], False)
    h_b, fin_b = rg_scan(a_b, b_b, s0[1], True)
    return [h_f, h_b], jnp.stack([fin_f, fin_b])


def ffn_dense(h2, x, gate_mod, w_gu, w_down):
    n_tiles = h2.shape[0] // FFN_TM
    d_ff = w_down.shape[0]
    return grouped_swiglu(jnp.zeros((n_tiles,), jnp.int32), jnp.full((1,), n_tiles, jnp.int32), h2,
                          w_gu[None, :, :d_ff].astype(jnp.bfloat16), w_gu[None, :, d_ff:].astype(jnp.bfloat16),
                          w_down[None].astype(jnp.bfloat16), nf=2,
                          residual=(x, gate_mod))


def ffn_moe(h2, logits, x, gate_mod, w_gu, w_down, final_g):
    n_tok = h2.shape[0]
    n_slot = TOP_K * n_tok
    top_v, top_i = lax.top_k(logits, TOP_K)
    w = jax.nn.softmax(top_v, axis=-1)
    e_flat = top_i.T.reshape(-1).astype(jnp.int32)
    onehot = (e_flat[:, None] == jnp.arange(N_EXPERTS, dtype=jnp.int32)[None]).astype(jnp.int32)
    csum = jnp.cumsum(onehot, axis=0)
    cnt = csum[-1]
    rank = jnp.sum(csum * onehot, axis=1) - 1
    padded = ((cnt + FFN_TM - 1) // FFN_TM) * FFN_TM
    ends = jnp.cumsum(padded)
    dest = (ends - padded)[e_flat] + rank
    n_rows = n_slot + N_EXPERTS * FFN_TM
    n_tiles = n_rows // FFN_TM
    tile_start = jnp.arange(n_tiles, dtype=jnp.int32) * FFN_TM
    tile_expert = jnp.minimum(jnp.sum((tile_start[:, None] >= ends[None, :]).astype(jnp.int32), axis=1),
                              N_EXPERTS - 1)
    n_used = (ends[-1:] // FFN_TM).astype(jnp.int32)
    order = jnp.argsort(e_flat, stable=True).astype(jnp.int32)
    row_expert = jnp.repeat(tile_expert, FFN_TM)
    shift = (ends - padded) - (jnp.cumsum(cnt) - cnt)
    q = jnp.arange(n_rows, dtype=jnp.int32) - shift[row_expert]
    src = order[jnp.clip(q, 0, n_slot - 1)] % n_tok
    xs = h2.at[src].get(mode="promise_in_bounds")
    wg, wu = cast_split_bf16(w_gu)
    ys = grouped_swiglu(tile_expert, n_used, xs, wg, wu, cast_rows_bf16(w_down), nf=2,
                        out_dtype=jnp.bfloat16)
    yk = ys.at[dest].get(mode="promise_in_bounds")
    return moe_combine(x, yk, w, gate_mod, final_g)


def kernel(x, c, ctx, c_ctx, mod_w, mod_b, norm1_g, norm2_g, w_in, gla_w_up, gla_b_up,
           s5_lam_re, s5_lam_im, s5_log_dt, s5_b_re, s5_b_im, s5_c_re, s5_c_im, s5_d,
           s5_w_glu, s5_b_glu, hy_w_short, hy_w1, hy_b1, hy_w2, hy_b2, hy_w3, hy_freq,
           hy_bias, rg_w_conv, rg_w_a, rg_b_a, rg_w_x, rg_b_x, rg_lam, w_branch, w_out,
           ffn_w_gu, ffn_w_down, moe_router, moe_router_b, moe_w_gu, moe_w_down, final_g):
    f32, bf16 = jnp.float32, jnp.bfloat16
    n_b, n_lat, _ = x.shape
    n_ctx = ctx.shape[1]
    xs = x.reshape(n_b * n_lat, D_MODEL)
    pos = grid_pos_embed(n_lat, D_MODEL)
    cs = ctx.reshape(n_b * n_ctx, D_MODEL)
    offs = [0]
    for n in IN_SIZES:
        offs.append(offs[-1] + n)
    for l in range(DEPTH):
        last = l == DEPTH - 1
        dense = l % 2 == 0
        j = l // 2
        m_lat = [t[:, None, :] for t in jnp.split(adaln(c, mod_w[l], mod_b[l]), 6, axis=-1)]
        m_ctx = [t[None, None, :] for t in jnp.split(adaln(c_ctx, mod_w[l], mod_b[l]), 6, axis=-1)]
        wl = w_in[l]
        w_mix = jnp.concatenate([wl[:, offs[6]:offs[7]], wl[:, offs[0]:offs[4]], wl[:, offs[5]:offs[6]],
                                 wl[:, offs[7]:offs[9]], wl[:, offs[4]:offs[5]],
                                 jnp.zeros((D_MODEL, P_COLS - P_GDN - 2 * GLA_RANK), f32)], axis=1).astype(bf16)
        w_gate = wl[:, offs[9]:offs[10]].astype(bf16)
        wb, wo, wglu = w_branch[l].astype(bf16), w_out[l].astype(bf16), s5_w_glu[l].astype(bf16)
        s5_ops = s5_prepare(s5_lam_re[l], s5_lam_im[l], s5_log_dt[l], s5_b_re[l], s5_b_im[l],
                            s5_c_re[l], s5_c_im[l], s5_d[l])
        hy_p = (hy_w_short[l], hy_w1[l], hy_b1[l], hy_w2[l], hy_b2[l], hy_w3[l], hy_freq[l])
        rg_p = (rg_w_conv[l], rg_w_a[l], rg_b_a[l], rg_w_x[l], rg_b_x[l], rg_lam[l])
        router = None if dense else (moe_router[j], moe_router_b[j])

        def mixers(p2, n_tok, states, with_hyena):
            p = p2.reshape(n_b, n_tok, P_COLS)
            blk = lambda i, n=1: p[..., i * W_MIX:(i + n) * W_MIX]
            flat = lambda t: t.reshape(n_b * n_tok, W_MIX)
            gla_o, gla_s = gla_mixer(p, gla_w_up[l], gla_b_up[l], states[0])
            s5_y, s5_s = s5_mixer(blk(P_S5), s5_ops, states[1])
            rg_o, rg_s = rglru_mixer(p, *rg_p, states[2])
            hy = tuple(flat(t) for t in hyena_mixer(p, *hy_p)) if with_hyena else None
            return ((flat(gla_o[0]), flat(gla_o[1]), flat(s5_y), hy, flat(rg_o[0]), flat(rg_o[1])),
                    (gla_s, s5_s, rg_s))

        def tail(stream, mods, p2, br, final, pos_add=None):
            outs = merge(stream, tuple(mods[:5]), norm1_g[l], norm2_g[l], br[0], br[1], p2, br[2], br[3],
                         br[4], br[5], w_gate, wb, wo, wglu, s5_b_glu[l], hy_bias[l], router, pos_add)
            if dense:
                return ffn_dense(outs[1], outs[0], mods[5], ffn_w_gu[j], ffn_w_down[j])
            return ffn_moe(outs[1], outs[2][:, :N_EXPERTS], outs[0], mods[5], moe_w_gu[j], moe_w_down[j],
                           final_g if final else None)

        zero_states = (jnp.zeros((2, n_b, GLA_HEADS, GLA_DV, GLA_DK), f32),
                       jnp.zeros((2, n_b, 2 * S5_NS), f32), jnp.zeros((2, n_b, W_MIX), f32))
        p_ctx = front(cs, m_ctx[0], m_ctx[1], norm1_g[l], w_mix)
        br_ctx, states = mixers(p_ctx, n_ctx, zero_states, not last)
        p_lat = front(xs, m_lat[0], m_lat[1], norm1_g[l], w_mix, pos if l == 0 else None)
        br_lat, _ = mixers(p_lat, n_lat, states, True)
        xs = tail(xs, m_lat, p_lat, br_lat, last, pos if l == 0 else None)
        if not last:
            cs = tail(cs, m_ctx, p_ctx, br_ctx, False)
    if (DEPTH - 1) % 2 == 0:
        xs = rmsnorm(xs, final_g)
    return xs.reshape(n_b, n_lat, D_MODEL)
```

```python
import functools
import math

import jax
import jax.numpy as jnp
import numpy as np
from jax import lax
from jax.experimental import pallas as pl
from jax.experimental.pallas import tpu as pltpu

D_MODEL = 1024
DEPTH = 2
GRID_W = 64
EPS = 1e-6
N_BRANCH = 4
W_MIX = D_MODEL // N_BRANCH
GLA_HEADS = 4
GLA_DK = W_MIX // GLA_HEADS
GLA_DV = W_MIX // GLA_HEADS
GLA_RANK = 16
GLA_TAU = 16.0
GLA_CHUNK = 64
S5_GROUP = 16
S5_GROUPS = W_MIX // S5_GROUP
S5_STATE = 64
S5_MAX_RE = -1e-4
HY_BANDS = 16
HY_SHORT = 3
HY_DECAY_SHORT = 0.3
HY_DECAY_LONG = 1.5
HY_TARGET = 1e-2
RG_BLOCKS = 4
RG_BLOCK = W_MIX // RG_BLOCKS
RG_CONV = 4
RG_C = 8.0
N_EXPERTS = 8
TOP_K = 2
IN_SIZES = (GLA_HEADS * GLA_DK, GLA_HEADS * GLA_DK, GLA_HEADS * GLA_DV, GLA_HEADS * GLA_DV,
            2 * GLA_RANK, W_MIX, 3 * W_MIX, W_MIX, W_MIX, N_BRANCH * D_MODEL)

VMEM_LIMIT_BYTES = 48 * 1024 * 1024


def _mm_kernel(x_ref, w_ref, o_ref):
    o_ref[...] = jnp.dot(x_ref[...].astype(jnp.bfloat16), w_ref[...],
                         preferred_element_type=jnp.float32)


def _pick_tile(n, cap):
    best = None
    for t in range(128, cap + 1, 128):
        if n % t == 0:
            best = t
    return best if best is not None else n


def pmm(x, w):
    lead = x.shape[:-1]
    k = x.shape[-1]
    n = w.shape[-1]
    x2 = x.reshape(-1, k)
    m = x2.shape[0]
    tm = 512 if m % 512 == 0 else m
    if k > 2048 and m % 256 == 0:
        tm = 256
    tn = n if k * n * 2 <= 6 * 1024 * 1024 else _pick_tile(n, 1024)
    out = pl.pallas_call(
        _mm_kernel,
        grid=(m // tm, n // tn),
        in_specs=[pl.BlockSpec((tm, k), lambda i, j: (i, 0)),
                  pl.BlockSpec((k, tn), lambda i, j: (0, j))],
        out_specs=pl.BlockSpec((tm, tn), lambda i, j: (i, j)),
        out_shape=jax.ShapeDtypeStruct((m, n), jnp.float32),
        compiler_params=pltpu.CompilerParams(
            dimension_semantics=("arbitrary", "arbitrary"),
            vmem_limit_bytes=VMEM_LIMIT_BYTES),
    )(x2, w.astype(jnp.bfloat16))
    return out.reshape(lead + (n,))


def _mm_multi_kernel(*refs, transposed):
    o_ref = refs[-1]
    n = (len(refs) - 1) // 2
    acc = None
    for i in range(n):
        dims = (((1,), (1 if transposed[i] else 0,)), ((), ()))
        t = lax.dot_general(refs[i][...].astype(jnp.bfloat16), refs[n + i][...], dims,
                            preferred_element_type=jnp.float32)
        acc = t if acc is None else acc + t
    o_ref[...] = acc


def pmm_multi(xs, ws, transposed, tm=256, tn=512):
    m = xs[0].shape[0]
    n = ws[0].shape[0 if transposed[0] else 1]
    tm = tm if m % tm == 0 else m
    tn = tn if n % tn == 0 else n
    in_specs = ([pl.BlockSpec((tm, x.shape[1]), lambda i, j: (i, 0)) for x in xs]
                + [pl.BlockSpec((tn, w.shape[1]), lambda i, j: (j, 0)) if t else
                   pl.BlockSpec((w.shape[0], tn), lambda i, j: (0, j)) for w, t in zip(ws, transposed)])
    return pl.pallas_call(
        functools.partial(_mm_multi_kernel, transposed=tuple(transposed)),
        grid=(m // tm, n // tn),
        in_specs=in_specs,
        out_specs=pl.BlockSpec((tm, tn), lambda i, j: (i, j)),
        out_shape=jax.ShapeDtypeStruct((m, n), jnp.float32),
        compiler_params=pltpu.CompilerParams(
            dimension_semantics=("arbitrary", "arbitrary"),
            vmem_limit_bytes=VMEM_LIMIT_BYTES),
    )(*xs, *[w.astype(jnp.bfloat16) for w in ws])


RG_SCAN_ROWS = 256


def _rg_scan_kernel(a_ref, b_ref, s0_ref, h_ref, fin_ref, st_ref, *, reverse, tb, nb):
    @pl.when(pl.program_id(0) == 0)
    def _():
        st_ref[...] = s0_ref[...]

    def body(r, hs):
        rr = (tb - 1 - r) if reverse else r
        out = []
        for i in range(nb):
            h = a_ref[i, pl.ds(rr, 1), :] * hs[i] + b_ref[i, pl.ds(rr, 1), :]
            h_ref[i, pl.ds(rr, 1), :] = h
            out.append(h)
        return tuple(out)

    hs = lax.fori_loop(0, tb, body, tuple(st_ref[i:i + 1, :] for i in range(nb)), unroll=8)
    for i in range(nb):
        st_ref[i:i + 1, :] = hs[i]
        fin_ref[i:i + 1, :] = hs[i]


def rg_scan(a, b, s0, reverse):
    nb, n_tok, ch = a.shape
    tb = min(RG_SCAN_ROWS, n_tok)
    nblk = n_tok // tb
    imap = (lambda k: (0, nblk - 1 - k, 0)) if reverse else (lambda k: (0, k, 0))
    return pl.pallas_call(
        functools.partial(_rg_scan_kernel, reverse=reverse, tb=tb, nb=nb),
        grid=(nblk,),
        in_specs=[pl.BlockSpec((nb, tb, ch), imap), pl.BlockSpec((nb, tb, ch), imap),
                  pl.BlockSpec((nb, ch), lambda k: (0, 0))],
        out_specs=[pl.BlockSpec((nb, tb, ch), imap), pl.BlockSpec((nb, ch), lambda k: (0, 0))],
        out_shape=[jax.ShapeDtypeStruct((nb, n_tok, ch), jnp.float32),
                   jax.ShapeDtypeStruct((nb, ch), jnp.float32)],
        scratch_shapes=[pltpu.VMEM((nb, ch), jnp.float32)],
        compiler_params=pltpu.CompilerParams(dimension_semantics=("arbitrary",)),
        name="rg_scan",
    )(a, b, s0)


RG_HALO = 8


def _rg_pre_kernel(x_ref, prev_ref, next_ref, wc_ref, wg_ref, bg_ref, c_ref, af_ref, bf_ref, ab_ref, bb_ref,
                   *, tb):
    k = pl.program_id(1)
    prev = jnp.where(k > 0, prev_ref[0], 0.0)
    nxt = jnp.where(k < pl.num_programs(1) - 1, next_ref[0], 0.0)
    ext = jnp.concatenate([prev, x_ref[0], nxt], axis=0)
    xc = sum(ext[RG_HALO - 2 + j:RG_HALO - 2 + j + tb, :] * wc_ref[j:j + 1, :] for j in range(RG_CONV))
    m = jnp.dot(xc.astype(jnp.bfloat16), wg_ref[...], preferred_element_type=jnp.float32) + bg_ref[...]
    for d, (a_ref, b_ref) in enumerate(((af_ref, bf_ref), (ab_ref, bb_ref))):
        r = jax.nn.sigmoid(m[:, (2 * d) * W_MIX:(2 * d + 1) * W_MIX])
        i = jax.nn.sigmoid(m[:, (2 * d + 1) * W_MIX:(2 * d + 2) * W_MIX])
        log_a = -r * c_ref[d:d + 1, :]
        a_ref[0] = jnp.exp(log_a)
        u = jnp.tanh(log_a)
        b_ref[0] = jnp.sqrt(-2.0 * u / (1.0 - u)) * (i * xc)


def rg_pre(p, w_conv, w_a, b_a, w_x, b_x, lam):
    nb, n_tok, _ = p.shape
    tb = min(ROW_TILE, n_tok)
    nblk = n_tok // tb
    hb = tb // RG_HALO
    n_halo = n_tok // RG_HALO
    col = P_RGX
    blockdiag = lambda w: jax.scipy.linalg.block_diag(*[w[i] for i in range(RG_BLOCKS)])
    wg = jnp.concatenate([blockdiag(w_a[0]), blockdiag(w_x[0]), blockdiag(w_a[1]), blockdiag(w_x[1])],
                         axis=1).astype(jnp.bfloat16)
    bg = jnp.concatenate([b_a[0].reshape(-1), b_x[0].reshape(-1), b_a[1].reshape(-1), b_x[1].reshape(-1)]
                         ).reshape(1, 4 * W_MIX)
    c = RG_C * jax.nn.softplus(-lam)
    blk = pl.BlockSpec((1, tb, W_MIX), lambda b, k: (b, k, 0))
    full = lambda a: pl.BlockSpec(a.shape, lambda b, k: (0,) * a.ndim)
    out = jax.ShapeDtypeStruct((nb, n_tok, W_MIX), jnp.float32)
    return pl.pallas_call(
        functools.partial(_rg_pre_kernel, tb=tb),
        grid=(nb, nblk),
        in_specs=[pl.BlockSpec((1, tb, W_MIX), lambda b, k: (b, k, col)),
                  pl.BlockSpec((1, RG_HALO, W_MIX), lambda b, k: (b, jnp.maximum(k * hb - 1, 0), col)),
                  pl.BlockSpec((1, RG_HALO, W_MIX), lambda b, k: (b, jnp.minimum((k + 1) * hb, n_halo - 1), col)),
                  full(w_conv), full(wg), full(bg), full(c)],
        out_specs=[blk, blk, blk, blk],
        out_shape=[out, out, out, out],
        compiler_params=pltpu.CompilerParams(dimension_semantics=("arbitrary", "arbitrary")),
        name="rg_pre",
    )(p, p, p, w_conv, wg, bg, c)


S5_T = 16
S5_NS = S5_GROUPS * S5_STATE
S5_SCAN_CHUNKS = 64


def _s5_scan_kernel(d_ref, s0_ref, a_ref, h_ref, fin_ref, st_ref, *, rc, nb):
    d = pl.program_id(0)

    @pl.when(pl.program_id(1) == 0)
    def _():
        st_ref[...] = s0_ref[0]

    ar = jnp.broadcast_to(a_ref[0, :, 0:S5_NS], (nb, S5_NS))
    ai = jnp.broadcast_to(a_ref[0, :, S5_NS:2 * S5_NS], (nb, S5_NS))

    def body(r, carry):
        hr, hi = carry
        rr = r + d * (rc - 1 - 2 * r)
        h_ref[rr, :, 0:S5_NS] = hr
        h_ref[rr, :, S5_NS:2 * S5_NS] = hi
        dr = d_ref[rr, :, 0:S5_NS]
        di = d_ref[rr, :, S5_NS:2 * S5_NS]
        return ar * hr - ai * hi + dr, ar * hi + ai * hr + di

    hr, hi = lax.fori_loop(0, rc, body, (st_ref[:, 0:S5_NS], st_ref[:, S5_NS:2 * S5_NS]))
    st_ref[:, 0:S5_NS] = hr
    st_ref[:, S5_NS:2 * S5_NS] = hi
    fin_ref[0, :, 0:S5_NS] = hr
    fin_ref[0, :, S5_NS:2 * S5_NS] = hi


def s5_scan(dmat, s0, a_t):
    n, nb, _ = dmat.shape
    rc = min(S5_SCAN_CHUNKS, n)
    nblk = n // rc
    w = 2 * S5_NS
    imap = lambda d, k: (k + d * (nblk - 1 - 2 * k), 0, d)
    return pl.pallas_call(
        functools.partial(_s5_scan_kernel, rc=rc, nb=nb),
        grid=(2, nblk),
        in_specs=[pl.BlockSpec((rc, nb, w), imap),
                  pl.BlockSpec((1, nb, w), lambda d, k: (d, 0, 0)),
                  pl.BlockSpec((1, 1, w), lambda d, k: (d, 0, 0))],
        out_specs=[pl.BlockSpec((rc, nb, w), imap),
                   pl.BlockSpec((1, nb, w), lambda d, k: (d, 0, 0))],
        out_shape=[jax.ShapeDtypeStruct((n, nb, 2 * w), jnp.float32),
                   jax.ShapeDtypeStruct((2, nb, w), jnp.float32)],
        scratch_shapes=[pltpu.VMEM((nb, w), jnp.float32)],
        compiler_params=pltpu.CompilerParams(dimension_semantics=("arbitrary", "arbitrary"),
                                             vmem_limit_bytes=VMEM_LIMIT_BYTES),
    )(dmat, s0, a_t)


def _cmul(ar, ai, br, bi):
    return ar * br - ai * bi, ar * bi + ai * br


def s5_prepare(lam_re, lam_im, log_dt, b_re, b_im, c_re, c_im, d_skip):
    f32 = jnp.float32
    hp = lax.Precision.HIGHEST
    t_len, g_n, p_n, h_n = S5_T, S5_GROUPS, S5_STATE, S5_GROUP
    bf16 = jnp.bfloat16
    eye_g = jnp.eye(g_n, dtype=f32)
    mask_gp = jnp.repeat(eye_g, p_n, axis=1).astype(bf16)[None, :, None, :]
    ar_t = jnp.arange(t_len)
    wd, wc, kk, a_t = [], [], [], []
    for d in range(2):
        lr = jnp.minimum(lam_re[d], S5_MAX_RE)
        li = lam_im[d]
        dt = jnp.exp(log_dt[d])[:, None]
        tt = jnp.arange(t_len + 1, dtype=f32)[:, None, None]
        mag = jnp.exp(lr * dt * tt)
        ang = li * dt * tt
        pr, pi = mag * jnp.cos(ang), mag * jnp.sin(ang)
        nr, ni = pr[1] - 1.0, pi[1]
        den = lr * lr + li * li
        qr, qi = (nr * lr + ni * li) / den, (ni * lr - nr * li) / den
        bbr, bbi = _cmul(qr[..., None], qi[..., None], b_re[d], b_im[d])
        cr, ci = c_re[d], c_im[d]

        idx = (t_len - 1 - ar_t) if d == 0 else ar_t
        wr, wi = _cmul(pr[idx][..., None], pi[idx][..., None], bbr[None], bbi[None])

        def place_d(w):
            wt = w.transpose(0, 3, 1, 2).reshape(t_len, 1, h_n, g_n * p_n).astype(bf16)
            return (wt * mask_gp).reshape(t_len * W_MIX, g_n * p_n)

        wd.append(jnp.concatenate([place_d(wr), place_d(wi)], axis=1))

        idx2 = (ar_t + 1) if d == 0 else (t_len - ar_t)
        cwr, cwi = _cmul(cr[None], ci[None], pr[idx2][:, :, None, :], pi[idx2][:, :, None, :])

        def place_c(w):
            wt = w.transpose(0, 2, 1, 3).reshape(t_len, 1, h_n, g_n * p_n).astype(bf16)
            return (wt * mask_gp).reshape(t_len * W_MIX, g_n * p_n)

        wc.append(jnp.concatenate([place_c(cwr), place_c(-cwi)], axis=1))

        er, ei = _cmul(pr[:t_len][:, :, None, :], pi[:t_len][:, :, None, :], cr[None], ci[None])
        kk.append(jnp.einsum('tghp,gpk->tghk', er, bbr, precision=hp)
                  - jnp.einsum('tghp,gpk->tghk', ei, bbi, precision=hp))
        a_t.append(jnp.concatenate([pr[t_len].reshape(1, -1), pi[t_len].reshape(1, -1)], axis=1))

    lag = ar_t[None, :] - ar_t[:, None]
    mf = jnp.where((lag >= 0)[..., None, None, None], kk[0][jnp.clip(lag, 0, t_len - 1)], 0.0)
    mb = jnp.where((lag <= 0)[..., None, None, None], kk[1][jnp.clip(-lag, 0, t_len - 1)], 0.0)
    skip = (jnp.eye(t_len, dtype=f32)[:, :, None, None, None] * d_skip[None, None, :, :, None]
            * jnp.eye(h_n, dtype=f32)[None, None, None])
    m = mf + mb + skip
    mt = m.transpose(0, 4, 1, 2, 3).reshape(t_len, 1, h_n, t_len * W_MIX).astype(bf16)
    mask_igh = jnp.tile(jnp.repeat(eye_g, h_n, axis=1), (1, t_len)).astype(bf16)[None, :, None, :]
    wk = (mt * mask_igh).reshape(t_len * W_MIX, t_len * W_MIX)
    return (jnp.concatenate(wd, axis=1), wk, jnp.concatenate(wc, axis=1), jnp.stack(a_t))


GLA_BLOCK = 1024


def _gla_kernel(q_ref, k_ref, v_ref, g_ref, wup_ref, bup_ref, s0_ref, o_ref, fin_ref, st_ref,
                *, reverse, tb):
    f32, bf16 = jnp.float32, jnp.bfloat16
    cc = GLA_CHUNK

    @pl.when(pl.program_id(1) == 0)
    def _():
        st_ref[...] = s0_ref[0]

    r_i = lax.broadcasted_iota(jnp.int32, (cc, cc), 0)
    c_i = lax.broadcasted_iota(jnp.int32, (cc, cc), 1)
    keep = (c_i >= r_i) if reverse else (c_i <= r_i)
    tri = keep.astype(bf16)
    nt = (((1,), (1,)), ((), ()))
    tn = (((0,), (0,)), ((), ()))
    n_ch = tb // cc
    chunks = range(n_ch)
    heads = range(GLA_HEADS)
    hsl = [slice(h * GLA_DK, (h + 1) * GLA_DK) for h in heads]
    rows = [slice(c * cc, (c + 1) * cc) for c in chunks]

    z = _dot3(*_split_bf16(g_ref[0]), *_split_bf16(wup_ref[...])) + bup_ref[...]
    la = (jnp.minimum(z, 0.0) - jnp.log1p(jnp.exp(-jnp.abs(z)))) * (1.0 / GLA_TAU)
    la_hi, la_lo = _split_bf16(la)
    cum = [jnp.dot(tri, la_hi[rows[c]], preferred_element_type=f32)
           + jnp.dot(tri, la_lo[rows[c]], preferred_element_type=f32) for c in chunks]
    last = [cm[0:1, :] if reverse else cm[cc - 1:cc, :] for cm in cum]
    k = [k_ref[0, rows[c], :] for c in chunks]
    q_in = [(q_ref[0, rows[c], :] * (GLA_DK ** -0.5) * jnp.exp(cum[c])).astype(bf16) for c in chunks]
    k_in = [(k[c] * jnp.exp(-cum[c])).astype(bf16) for c in chunks]
    k_out = [(k[c] * jnp.exp(last[c] - cum[c])).astype(bf16) for c in chunks]
    dec = [jnp.exp(last[c]) for c in chunks]
    vb = [v_ref[0, rows[c], :].astype(bf16) for c in chunks]
    att = [[jnp.where(keep, lax.dot_general(q_in[c][:, s], k_in[c][:, s], nt, preferred_element_type=f32),
                      0.0).astype(bf16) for s in hsl] for c in chunks]
    o_intra = [[jnp.dot(att[c][h], vb[c][:, hsl[h]], preferred_element_type=f32) for h in heads]
               for c in chunks]
    d_state = [[lax.dot_general(vb[c][:, s], k_out[c][:, s], tn, preferred_element_type=f32) for s in hsl]
               for c in chunks]

    st = [st_ref[h] for h in heads]
    st_in = [None] * n_ch
    for c in (reversed(chunks) if reverse else chunks):
        st_in[c] = [s.astype(bf16) for s in st]
        st = [st[h] * dec[c][:, hsl[h]] + d_state[c][h] for h in heads]
    for h in heads:
        st_ref[h] = st[h]
        fin_ref[0, h] = st[h]

    for c in chunks:
        o_ref[0, rows[c], :] = jnp.concatenate(
            [o_intra[c][h] + lax.dot_general(q_in[c][:, hsl[h]], st_in[c][h], nt, preferred_element_type=f32)
             for h in heads], axis=1)


def gla_dir(p, gd, w_up, b_up, s0, reverse):
    nb, n_tok, _ = p.shape
    tb = min(GLA_BLOCK, n_tok)
    nblk = n_tok // tb
    blk = (lambda k: nblk - 1 - k) if reverse else (lambda k: k)
    col = lambda c: pl.BlockSpec((1, tb, W_MIX), lambda b, k: (b, blk(k), c))
    st_shape = (GLA_HEADS, GLA_DV, GLA_DK)
    return pl.pallas_call(
        functools.partial(_gla_kernel, reverse=reverse, tb=tb),
        grid=(nb, nblk),
        in_specs=[col(P_Q), col(P_Q + 1), col(P_Q + 2),
                  pl.BlockSpec((1, tb, GLA_RANK), lambda b, k: (b, blk(k), 0)),
                  pl.BlockSpec((GLA_RANK, W_MIX), lambda b, k: (0, 0)),
                  pl.BlockSpec((1, W_MIX), lambda b, k: (0, 0)),
                  pl.BlockSpec((1,) + st_shape, lambda b, k: (b, 0, 0, 0))],
        out_specs=[pl.BlockSpec((1, tb, W_MIX), lambda b, k: (b, blk(k), 0)),
                   pl.BlockSpec((1,) + st_shape, lambda b, k: (b, 0, 0, 0))],
        out_shape=[jax.ShapeDtypeStruct((nb, n_tok, W_MIX), jnp.float32),
                   jax.ShapeDtypeStruct((nb,) + st_shape, jnp.float32)],
        scratch_shapes=[pltpu.VMEM(st_shape, jnp.float32)],
        compiler_params=pltpu.CompilerParams(dimension_semantics=("arbitrary", "arbitrary"),
                                             vmem_limit_bytes=VMEM_LIMIT_BYTES),
        name="gla",
    )(p, p, p, gd, w_up, b_up.reshape(1, W_MIX), s0)


FFN_TM = 512
CAST_ROWS = 256


def _swiglu_kernel(te_ref, nu_ref, x_ref, wg_ref, wu_ref, wd_ref, *rest, nf, residual):
    del te_ref
    o_ref, acc_ref = rest[-2:]
    j = pl.program_id(1)

    @pl.when(jnp.logical_and(pl.program_id(0) >= nu_ref[0], j == nf - 1))
    def _():
        o_ref[...] = jnp.zeros_like(o_ref)

    @pl.when(pl.program_id(0) < nu_ref[0])
    def _():
        x = x_ref[...]
        g = jnp.dot(x, wg_ref[0], preferred_element_type=jnp.float32)
        u = jnp.dot(x, wu_ref[0], preferred_element_type=jnp.float32)
        a = (g * jax.nn.sigmoid(g) * u).astype(jnp.bfloat16)
        part = jnp.dot(a, wd_ref[0], preferred_element_type=jnp.float32)

        @pl.when(j == 0)
        def _():
            acc_ref[...] = part

        @pl.when(j > 0)
        def _():
            acc_ref[...] += part

        @pl.when(j == nf - 1)
        def _():
            if residual:
                xres_ref, gm_ref = rest[:2]
                o_ref[...] = (xres_ref[...] + gm_ref[0] * acc_ref[...]).astype(o_ref.dtype)
            else:
                o_ref[...] = acc_ref[...].astype(o_ref.dtype)


def _cast_split_kernel(a_ref, b_ref, oa_ref, ob_ref):
    oa_ref[...] = a_ref[...].astype(oa_ref.dtype)
    ob_ref[...] = b_ref[...].astype(ob_ref.dtype)


def cast_split_bf16(w):
    e, r, c2 = w.shape
    c = c2 // 2
    rows = min(CAST_ROWS, r)
    half = lambda h: pl.BlockSpec((1, rows, c), lambda i, j: (i, j, h))
    out = jax.ShapeDtypeStruct((e, r, c), jnp.bfloat16)
    return pl.pallas_call(
        _cast_split_kernel,
        grid=(e, r // rows),
        in_specs=[half(0), half(1)],
        out_specs=[half(0), half(0)],
        out_shape=[out, out],
        compiler_params=pltpu.CompilerParams(dimension_semantics=("arbitrary", "arbitrary"),
                                             vmem_limit_bytes=VMEM_LIMIT_BYTES),
        name="cast_split_bf16",
    )(w, w)


def _cast_rows_kernel(a_ref, b_ref, o_ref):
    o_ref[0, 0] = a_ref[0, 0].astype(o_ref.dtype)
    o_ref[0, 1] = b_ref[0, 0].astype(o_ref.dtype)


def cast_rows_bf16(w):
    e, r, c = w.shape
    rows = min(CAST_ROWS, r // 2)
    w4 = w.reshape(e, 2, r // 2, c)
    half = lambda h: pl.BlockSpec((1, 1, rows, c), lambda i, j: (i, h, j, 0))
    out = pl.pallas_call(
        _cast_rows_kernel,
        grid=(e, r // 2 // rows),
        in_specs=[half(0), half(1)],
        out_specs=pl.BlockSpec((1, 2, rows, c), lambda i, j: (i, 0, j, 0)),
        out_shape=jax.ShapeDtypeStruct(w4.shape, jnp.bfloat16),
        compiler_params=pltpu.CompilerParams(dimension_semantics=("arbitrary", "arbitrary"),
                                             vmem_limit_bytes=VMEM_LIMIT_BYTES),
        name="cast_rows_bf16",
    )(w4, w4)
    return out.reshape(e, r, c)


def grouped_swiglu(tile_expert, n_used, xs, w_g, w_u, w_down, nf, residual=None, out_dtype=jnp.float32):
    m, d = xs.shape
    f = w_down.shape[1]
    tf = f // nf
    n_tiles = m // FFN_TM
    in_specs = [pl.BlockSpec((FFN_TM, d), lambda t, j, te, nu: (t, 0)),
                pl.BlockSpec((1, d, tf), lambda t, j, te, nu: (te[t], 0, j)),
                pl.BlockSpec((1, d, tf), lambda t, j, te, nu: (te[t], 0, j)),
                pl.BlockSpec((1, tf, d), lambda t, j, te, nu: (te[t], j, 0))]
    extra = ()
    if residual is not None:
        rows_per_mod = m // residual[1].shape[0]
        in_specs += [pl.BlockSpec((FFN_TM, d), lambda t, j, te, nu: (t, 0)),
                     pl.BlockSpec((1, 1, d), lambda t, j, te, nu: ((t * FFN_TM) // rows_per_mod, 0, 0))]
        extra = tuple(residual)
    grid_spec = pltpu.PrefetchScalarGridSpec(
        num_scalar_prefetch=2,
        grid=(n_tiles, nf),
        in_specs=in_specs,
        out_specs=pl.BlockSpec((FFN_TM, d), lambda t, j, te, nu: (t, 0)),
        scratch_shapes=[pltpu.VMEM((FFN_TM, d), jnp.float32)])
    return pl.pallas_call(
        functools.partial(_swiglu_kernel, nf=nf, residual=residual is not None),
        grid_spec=grid_spec,
        out_shape=jax.ShapeDtypeStruct((m, d), out_dtype),
        compiler_params=pltpu.CompilerParams(
            dimension_semantics=("arbitrary", "arbitrary"),
            vmem_limit_bytes=VMEM_LIMIT_BYTES),
        name="grouped_swiglu",
    )(tile_expert, n_used, xs, w_g, w_u, w_down, *extra)


ROW_TILE = 512
P_HY, P_Q, P_OG, P_S5, P_RGX, P_RGG = 0, 3, 6, 7, 8, 9
P_GDN = 10 * W_MIX
P_COLS = P_GDN + 128


def _rms_mod(x, g, shift, scale):
    y = x * lax.rsqrt(jnp.mean(x * x, axis=-1, keepdims=True) + EPS) * g
    return y * (1.0 + scale) + shift


def _front_kernel(x_ref, sh_ref, sc_ref, g_ref, w_ref, *rest):
    o_ref = rest[-1]
    x = x_ref[...] + rest[0][...] if len(rest) == 2 else x_ref[...]
    h = _rms_mod(x, g_ref[...], sh_ref[0], sc_ref[0])
    o_ref[...] = jnp.dot(h.astype(jnp.bfloat16), w_ref[...], preferred_element_type=jnp.float32)


def _pos_spec(pos, tile):
    if pos is None:
        return []
    n_blk = pos.shape[0] // tile
    return [pl.BlockSpec((tile, D_MODEL), lambda i: (i % n_blk, 0))]


def _mod_spec(rows_per_mod):
    return pl.BlockSpec((1, 1, D_MODEL), lambda i: ((i * ROW_TILE) // rows_per_mod, 0, 0))


def front(x, shift, scale, g, w, pos=None):
    m = x.shape[0]
    n_mod = shift.shape[0]
    n_out = w.shape[1]
    mod = _mod_spec(m // n_mod)
    return pl.pallas_call(
        _front_kernel,
        grid=(m // ROW_TILE,),
        in_specs=[pl.BlockSpec((ROW_TILE, D_MODEL), lambda i: (i, 0)), mod, mod,
                  pl.BlockSpec((1, D_MODEL), lambda i: (0, 0)),
                  pl.BlockSpec((D_MODEL, n_out), lambda i: (0, 0))] + _pos_spec(pos, ROW_TILE),
        out_specs=pl.BlockSpec((ROW_TILE, n_out), lambda i: (i, 0)),
        out_shape=jax.ShapeDtypeStruct((m, n_out), jnp.float32),
        compiler_params=pltpu.CompilerParams(dimension_semantics=("arbitrary",),
                                             vmem_limit_bytes=VMEM_LIMIT_BYTES),
        name="front",
    )(x, shift, scale, g.reshape(1, D_MODEL), w, *([] if pos is None else [pos]))


MERGE_TILE = 256
ROUTER_PAD = 128
N_MERGE_IN = 25


def _merge_kernel(*refs, with_router, with_pos):
    (x_ref, sh1_ref, sc1_ref, gm_ref, sh2_ref, sc2_ref, g1_ref, g2_ref,
     of_ref, ob_ref, og_ref, s5_ref, hc_ref, hz_ref, hx0_ref, rf_ref, rb_ref, rgg_ref,
     wg_ref, wb_ref, wo_ref, wglu_ref, bglu_ref, havg_ref, hbias_ref) = refs[:N_MERGE_IN]
    f32, bf16 = jnp.float32, jnp.bfloat16
    hp = lax.Precision.HIGHEST
    rest = list(refs[N_MERGE_IN:])
    router_refs = [rest.pop(0), rest.pop(0)] if with_router else None
    x = x_ref[...] + rest.pop(0)[...] if with_pos else x_ref[...]
    hb = _rms_mod(x, g1_ref[...], sh1_ref[0], sc1_ref[0]).astype(bf16)

    o = of_ref[...] + ob_ref[...]
    ms = jnp.dot(o * o, havg_ref[...], precision=hp, preferred_element_type=f32)
    og = og_ref[...]
    gla = o * lax.rsqrt(ms + EPS) * (og * jax.nn.sigmoid(og))
    g5 = jax.nn.gelu(s5_ref[...])
    s5o = g5 * jax.nn.sigmoid(jnp.dot(g5.astype(bf16), wglu_ref[...], preferred_element_type=f32)
                              + bglu_ref[...])
    rgo = (rf_ref[...] + rb_ref[...]) * jax.nn.gelu(rgg_ref[...])
    hyo = hx0_ref[...] * (hc_ref[...] + hz_ref[...] * hbias_ref[...])
    branches = (gla, s5o, hyo, rgo)

    y = None
    for k in range(N_BRANCH):
        gate = jax.nn.sigmoid(jnp.dot(hb, wg_ref[:, k * D_MODEL:(k + 1) * D_MODEL],
                                      preferred_element_type=f32))
        t = gate * jnp.dot(branches[k].astype(bf16), wb_ref[k], preferred_element_type=f32)
        y = t if y is None else y + t
    out = jnp.dot(y.astype(bf16), wo_ref[...], preferred_element_type=f32)
    xn = x + gm_ref[0] * out
    h2 = _rms_mod(xn, g2_ref[...], sh2_ref[0], sc2_ref[0])
    if with_router:
        (rw_ref, rb2_ref), (xo_ref, h2_ref, lg_ref) = router_refs, rest
        lg_ref[...] = _dot3(*_split_bf16(h2), *_split_bf16(rw_ref[...])) + rb2_ref[...]
    else:
        xo_ref, h2_ref = rest
    xo_ref[...] = xn
    h2_ref[...] = h2.astype(bf16)


def merge(x, mods, g1, g2, o_f, o_b, p, s5y, hy, r_f, r_b, wg, wb, wo, wglu, bglu, hbias, router=None,
          pos=None):
    m = x.shape[0]
    tm = MERGE_TILE
    n_mod = mods[0].shape[0]
    rows_per_mod = m // n_mod
    mod = pl.BlockSpec((1, 1, D_MODEL), lambda i: ((i * tm) // rows_per_mod, 0, 0))
    row = pl.BlockSpec((tm, D_MODEL), lambda i: (i, 0))
    br = pl.BlockSpec((tm, W_MIX), lambda i: (i, 0))
    pcol = lambda c: pl.BlockSpec((tm, W_MIX), lambda i: (i, c))
    full = lambda a: pl.BlockSpec(a.shape, lambda i: (0,) * a.ndim)
    head = jnp.arange(W_MIX) // GLA_DV
    havg = (head[:, None] == head[None, :]).astype(jnp.float32) / GLA_DV
    vec = lambda v: v.reshape(1, -1)
    consts = [wg, wb, wo, wglu, vec(bglu), havg, vec(hbias)]
    out_specs = [row, pl.BlockSpec((tm, D_MODEL), lambda i: (i, 0))]
    out_shape = [jax.ShapeDtypeStruct((m, D_MODEL), jnp.float32),
                 jax.ShapeDtypeStruct((m, D_MODEL), jnp.bfloat16)]
    if router is not None:
        rw, rbias = router
        pad = ROUTER_PAD - rw.shape[1]
        consts += [jnp.pad(rw, ((0, 0), (0, pad))), jnp.pad(rbias, (0, pad)).reshape(1, -1)]
        out_specs.append(pl.BlockSpec((tm, ROUTER_PAD), lambda i: (i, 0)))
        out_shape.append(jax.ShapeDtypeStruct((m, ROUTER_PAD), jnp.float32))
    in_specs = ([row] + [mod] * 5 + [full(vec(g1)), full(vec(g2)), br, br, pcol(P_OG), br, br, br, br, br, br,
                                     pcol(P_RGG)] + [full(a) for a in consts] + _pos_spec(pos, tm))
    return pl.pallas_call(
        functools.partial(_merge_kernel, with_router=router is not None, with_pos=pos is not None),
        grid=(m // tm,),
        in_specs=in_specs,
        out_specs=out_specs,
        out_shape=out_shape,
        compiler_params=pltpu.CompilerParams(dimension_semantics=("arbitrary",),
                                             vmem_limit_bytes=VMEM_LIMIT_BYTES),
        name="merge",
    )(x, *mods, vec(g1), vec(g2), o_f, o_b, p, s5y, *hy, r_f, r_b, p, *consts, *([] if pos is None else [pos]))


def _combine_kernel(x_ref, y0_ref, y1_ref, w_ref, gm_ref, g_ref, o_ref, *, final_norm):
    w = w_ref[...]
    y = w[:, 0:1] * y0_ref[...].astype(jnp.float32) + w[:, 1:2] * y1_ref[...].astype(jnp.float32)
    xn = x_ref[...] + gm_ref[0] * y
    if final_norm:
        xn = xn * lax.rsqrt(jnp.mean(xn * xn, axis=-1, keepdims=True) + EPS) * g_ref[...]
    o_ref[...] = xn


def moe_combine(x, yk, w, gate_mod, final_g):
    m = x.shape[0]
    n_mod = gate_mod.shape[0]
    g = jnp.ones((1, D_MODEL), jnp.float32) if final_g is None else final_g.reshape(1, D_MODEL)
    return pl.pallas_call(
        functools.partial(_combine_kernel, final_norm=final_g is not None),
        grid=(m // ROW_TILE,),
        in_specs=[pl.BlockSpec((ROW_TILE, D_MODEL), lambda i: (i, 0)),
                  pl.BlockSpec((ROW_TILE, D_MODEL), lambda i: (i, 0)),
                  pl.BlockSpec((ROW_TILE, D_MODEL), lambda i: (i + m // ROW_TILE, 0)),
                  pl.BlockSpec((ROW_TILE, TOP_K), lambda i: (i, 0)),
                  _mod_spec(m // n_mod),
                  pl.BlockSpec((1, D_MODEL), lambda i: (0, 0))],
        out_specs=pl.BlockSpec((ROW_TILE, D_MODEL), lambda i: (i, 0)),
        out_shape=jax.ShapeDtypeStruct((m, D_MODEL), jnp.float32),
        compiler_params=pltpu.CompilerParams(dimension_semantics=("arbitrary",)),
        name="moe_combine",
    )(x, yk, yk, w, gate_mod, g)


HY_LANES = 128
HY_STEP_ROWS = 1024
HY_MIN_LEN = 1024


def _hyena_pre_kernel(x_ref, prev_ref, next_ref, w_ref, zt_ref, z_ref, x0_ref, *, tb):
    k = pl.program_id(1)
    prev = jnp.where(k > 0, prev_ref[0], 0.0)
    nxt = jnp.where(k < pl.num_programs(1) - 1, next_ref[0], 0.0)
    ext = jnp.concatenate([prev, x_ref[0], nxt], axis=0)
    pc = sum(ext[RG_HALO - 1 + j:RG_HALO - 1 + j + tb, :] * w_ref[j:j + 1, :] for j in range(HY_SHORT))
    v, x0, x1 = pc[:, :W_MIX], pc[:, W_MIX:2 * W_MIX], pc[:, 2 * W_MIX:]
    z = x1 * v
    z_ref[0] = z
    x0_ref[0] = x0
    zt_ref[0] = z.T


def hyena_pre(p, w_short):
    nb, n_tok, _ = p.shape
    tb = min(ROW_TILE, n_tok)
    hb = tb // RG_HALO
    n_halo = n_tok // RG_HALO
    wide = 3 * W_MIX
    tok = pl.BlockSpec((1, tb, W_MIX), lambda b, k: (b, k, 0))
    tok_out = jax.ShapeDtypeStruct((nb, n_tok, W_MIX), jnp.float32)
    return pl.pallas_call(
        functools.partial(_hyena_pre_kernel, tb=tb),
        grid=(nb, n_tok // tb),
        in_specs=[pl.BlockSpec((1, tb, wide), lambda b, k: (b, k, P_HY)),
                  pl.BlockSpec((1, RG_HALO, wide), lambda b, k: (b, jnp.maximum(k * hb - 1, 0), P_HY)),
                  pl.BlockSpec((1, RG_HALO, wide), lambda b, k: (b, jnp.minimum((k + 1) * hb, n_halo - 1), P_HY)),
                  pl.BlockSpec(w_short.shape, lambda b, k: (0, 0))],
        out_specs=[pl.BlockSpec((1, W_MIX, tb), lambda b, k: (b, 0, k)), tok, tok],
        out_shape=[jax.ShapeDtypeStruct((nb, W_MIX, n_tok), jnp.float32), tok_out, tok_out],
        compiler_params=pltpu.CompilerParams(dimension_semantics=("arbitrary", "arbitrary"),
                                             vmem_limit_bytes=VMEM_LIMIT_BYTES),
        name="hyena_pre",
    )(p, p, p, w_short)


def _split_bf16(a):
    hi = a.astype(jnp.bfloat16)
    lo = (a - hi.astype(jnp.float32)).astype(jnp.bfloat16)
    return hi, lo


def _dot3(a_hi, a_lo, b_hi, b_lo):
    d = functools.partial(jnp.dot, preferred_element_type=jnp.float32)
    return d(a_hi, b_hi) + (d(a_lo, b_hi) + d(a_hi, b_lo))


def _hyena_dft_consts(n1):
    n = n1 * HY_LANES
    ka = np.arange(n1, dtype=np.float64)[:, None]
    f1_ang = 2.0 * np.pi * ka * np.arange(n1 // 2, dtype=np.float64)[None, :] / n1
    f1r, f1i = np.cos(f1_ang), -np.sin(f1_ang)
    tw_ang = 2.0 * np.pi * ka * np.arange(HY_LANES, dtype=np.float64)[None, :] / n
    lo = np.arange(HY_LANES, dtype=np.float64)
    f2_ang = 2.0 * np.pi * lo[:, None] * lo[None, :] / HY_LANES
    f2r, f2i = np.cos(f2_ang), -np.sin(f2_ang)
    fwd_rows = np.concatenate([f1r, f1i], axis=0)
    fwd_lanes = np.block([[f2r, f2i], [-f2i, f2r]])
    inv_lanes = np.block([[f2r, -f2i], [f2i, f2r]])
    inv_rows = np.concatenate([f1r.T, f1i.T], axis=1) / n
    out = []
    for m in (fwd_rows, fwd_lanes, inv_lanes, inv_rows):
        m32 = jnp.asarray(m, jnp.float32)
        out.extend(_split_bf16(m32))
    return out + [jnp.asarray(np.cos(tw_ang), jnp.float32), jnp.asarray(-np.sin(tw_ang), jnp.float32)]


def _hyena_fft_kernel(*refs, n1, ns, spectrum):
    bf16 = jnp.bfloat16
    if spectrum:
        z_ref, f1h, f1l, f2h, f2l, twr_ref, twi_ref, o_ref = refs
        rows_dft = lambda t: _dot3(f1h[...], f1l[...], *_split_bf16(t))
        lanes_dft = lambda t: _dot3(*_split_bf16(t), f2h[...], f2l[...])
    else:
        z_ref, hf_ref, f1h, f2h, g2h, fih, twr_ref, twi_ref, o_ref = refs
        mm = lambda a, b: jnp.dot(a.astype(bf16), b.astype(bf16), preferred_element_type=jnp.float32)
        rows_dft = lambda t: mm(f1h[...], t)
        lanes_dft = lambda t: mm(t, f2h[...])
    w = HY_LANES
    twr, twi = twr_ref[...], twi_ref[...]
    z2 = jnp.concatenate([z_ref[0, s] for s in range(ns)], axis=1)
    a2 = rows_dft(z2)
    rows = []
    for s in range(ns):
        r, i = a2[:n1, s * w:(s + 1) * w], a2[n1:, s * w:(s + 1) * w]
        rows.append(jnp.concatenate([r * twr - i * twi, r * twi + i * twr], axis=1))
    x = lanes_dft(jnp.concatenate(rows, axis=0))
    if spectrum:
        o_ref[...] = x.reshape(ns, n1, 2 * w)
        return
    h = hf_ref[...].reshape(ns * n1, 2 * w)
    xr, xi, hr, hi = x[:, :w], x[:, w:], h[:, :w], h[:, w:]
    y = jnp.concatenate([xr * hr - xi * hi, xr * hi + xi * hr], axis=1)
    g = mm(y, g2h[...])
    cr, ci = [], []
    for s in range(ns):
        gr, gi = g[s * n1:(s + 1) * n1, :w], g[s * n1:(s + 1) * n1, w:]
        cr.append(gr * twr + gi * twi)
        ci.append(gi * twr - gr * twi)
    gc = jnp.concatenate([jnp.concatenate(cr, axis=1), jnp.concatenate(ci, axis=1)], axis=0)
    y2 = mm(fih[...], gc)
    for s in range(ns):
        o_ref[0, s] = y2[:, s * w:(s + 1) * w]


def hyena_fft(zt, hf=None):
    nb, ch, half, w = zt.shape
    n1 = 2 * half
    ns = HY_STEP_ROWS // n1
    f1h, f1l, f2h, f2l, g2h, _, fih, _, twr, twi = _hyena_dft_consts(n1)
    full = lambda a: pl.BlockSpec(a.shape, lambda b, c: (0,) * a.ndim)
    zspec = pl.BlockSpec((1, ns, half, w), lambda b, c: (b, c, 0, 0))
    if hf is None:
        consts = [f1h, f1l, f2h, f2l, twr, twi]
        in_specs, args = [zspec], [zt]
        out_spec = pl.BlockSpec((ns, n1, 2 * w), lambda b, c: (b * (ch // ns) + c, 0, 0))
        out_shape = jax.ShapeDtypeStruct((nb * ch, n1, 2 * w), jnp.float32)
    else:
        consts = [f1h, f2h, g2h, fih, twr, twi]
        in_specs = [zspec, pl.BlockSpec((ns, n1, 2 * w), lambda b, c: (c, 0, 0))]
        args = [zt, hf]
        out_spec = zspec
        out_shape = jax.ShapeDtypeStruct(zt.shape, jnp.float32)
    return pl.pallas_call(
        functools.partial(_hyena_fft_kernel, n1=n1, ns=ns, spectrum=hf is None),
        grid=(nb, ch // ns),
        in_specs=in_specs + [full(a) for a in consts],
        out_specs=out_spec,
        out_shape=out_shape,
        compiler_params=pltpu.CompilerParams(dimension_semantics=("arbitrary", "arbitrary"),
                                             vmem_limit_bytes=VMEM_LIMIT_BYTES),
        name="hyena_fft",
    )(*args, *consts)


def rmsnorm(x, g):
    y = x * lax.rsqrt(jnp.mean(x * x, axis=-1, keepdims=True) + EPS)
    return y * g


def adaln(cond, w, b):
    return jax.nn.silu(cond) @ w + b


def grid_pos_embed(n_tokens, dim):
    rows = n_tokens // GRID_W
    q = dim // 4
    omega = 1.0 / (10000.0 ** (jnp.arange(q, dtype=jnp.float32) / q))
    r = jnp.arange(rows, dtype=jnp.float32)[:, None] * omega
    cc = jnp.arange(GRID_W, dtype=jnp.float32)[:, None] * omega
    er = jnp.concatenate([jnp.sin(r), jnp.cos(r)], axis=-1)
    ec = jnp.concatenate([jnp.sin(cc), jnp.cos(cc)], axis=-1)
    emb = jnp.concatenate([jnp.broadcast_to(er[:, None], (rows, GRID_W, dim // 2)),
                           jnp.broadcast_to(ec[None], (rows, GRID_W, dim // 2))], axis=-1)
    return emb.reshape(rows * GRID_W, dim)


def gla_mixer(p, w_up, b_up, s0):
    gdn = p[..., P_GDN:P_GDN + 2 * GLA_RANK]
    outs, finals = [], []
    for d in range(2):
        od, sd = gla_dir(p, gdn[..., d * GLA_RANK:(d + 1) * GLA_RANK], w_up[d], b_up[d], s0[d], d == 1)
        outs.append(od)
        finals.append(sd)
    return outs, jnp.stack(finals)


def s5_mixer(u, prep, s0):
    wd, wk, wc, a_t = prep
    b_, n_tok, _ = u.shape
    n = n_tok // S5_T
    u2 = u.astype(jnp.bfloat16).reshape(b_, n, S5_T * W_MIX).transpose(1, 0, 2).reshape(n * b_, S5_T * W_MIX)
    dmat = pmm(u2, wd)
    hmat, fin = s5_scan(dmat.reshape(n, b_, 4 * S5_NS), s0, a_t)
    y2 = pmm_multi([u2, hmat.reshape(n * b_, 4 * S5_NS)], [wk, wc], (False, True))
    y = y2.reshape(n, b_, S5_T, W_MIX).transpose(1, 0, 2, 3).reshape(b_, n_tok, W_MIX)
    return y, fin


def hyena_filters(n_tok, w1, b1, w2, b2, w3, freq):
    f32 = jnp.float32
    t = jnp.arange(n_tok, dtype=f32)[:, None]
    bands = jnp.linspace(1e-4, HY_BANDS - 1, HY_BANDS, dtype=f32)[None]
    ang = 2.0 * math.pi * bands * t / n_tok
    z = jnp.concatenate([t / n_tok, jnp.cos(ang), jnp.sin(ang)], axis=-1)
    hp = lax.Precision.HIGHEST
    h = jnp.sin(freq * (jnp.dot(z, w1, precision=hp) + b1))
    h = jnp.sin(freq * (jnp.dot(h, w2, precision=hp) + b2))
    h = jnp.dot(h, w3, precision=hp)
    t01 = t / max(n_tok - 1, 1)
    deltas = jnp.abs(jnp.linspace(math.log(HY_TARGET) / HY_DECAY_SHORT,
                                  math.log(HY_TARGET) / HY_DECAY_LONG, W_MIX, dtype=f32))
    h = h * jnp.exp(-t01 * jnp.tile(deltas, 2))
    return h / (jnp.sum(jnp.abs(h), axis=0, keepdims=True) + EPS)


def hyena_mixer(p, w_short, w1, b1, w2, b2, w3, freq):
    nb, n_tok, _ = p.shape
    zt, z, x0 = hyena_pre(p, w_short)
    n_pad = max(n_tok, HY_MIN_LEN)
    half = n_pad // HY_LANES

    def frames(t):
        t = jnp.pad(t, [(0, 0)] * (t.ndim - 1) + [(0, n_pad - n_tok)])
        return t.reshape(t.shape[:-1] + (half, HY_LANES))

    filt = hyena_filters(n_tok, w1, b1, w2, b2, w3, freq)
    spec = hyena_fft(frames(filt.T)[None])
    sf, sb = spec[:W_MIX], spec[W_MIX:]
    hfreq = jnp.concatenate([sf[..., :HY_LANES] + sb[..., :HY_LANES],
                             sf[..., HY_LANES:] - sb[..., HY_LANES:]], axis=-1)
    conv = hyena_fft(frames(zt), hfreq)
    conv = conv.reshape(nb, W_MIX, n_pad)[:, :, :n_tok].transpose(0, 2, 1)
    return conv, z, x0


def rglru_mixer(p, w_conv, w_a, b_a, w_x, b_x, lam, s0):
    a_f, b_f, a_b, b_b = rg_pre(p, w_conv, w_a, b_a, w_x, b_x, lam)
    h_f, fin_f = rg_scan(a_f, b_f, s0[0], False)
    h_b, fin_b = rg_scan(a_b, b_b, s0[1], True)
    return [h_f, h_b], jnp.stack([fin_f, fin_b])


def ffn_dense(h2, x, gate_mod, w_gu, w_down):
    n_tiles = h2.shape[0] // FFN_TM
    d_ff = w_down.shape[0]
    return grouped_swiglu(jnp.zeros((n_tiles,), jnp.int32), jnp.full((1,), n_tiles, jnp.int32), h2,
                          w_gu[None, :, :d_ff].astype(jnp.bfloat16), w_gu[None, :, d_ff:].astype(jnp.bfloat16),
                          w_down[None].astype(jnp.bfloat16), nf=2,
                          residual=(x, gate_mod))


def ffn_moe(h2, logits, x, gate_mod, w_gu, w_down, final_g):
    n_tok = h2.shape[0]
    n_slot = TOP_K * n_tok
    top_v, top_i = lax.top_k(logits, TOP_K)
    w = jax.nn.softmax(top_v, axis=-1)
    e_flat = top_i.T.reshape(-1).astype(jnp.int32)
    onehot = (e_flat[:, None] == jnp.arange(N_EXPERTS, dtype=jnp.int32)[None]).astype(jnp.int32)
    csum = jnp.cumsum(onehot, axis=0)
    cnt = csum[-1]
    rank = jnp.sum(csum * onehot, axis=1) - 1
    padded = ((cnt + FFN_TM - 1) // FFN_TM) * FFN_TM
    ends = jnp.cumsum(padded)
    dest = (ends - padded)[e_flat] + rank
    n_rows = n_slot + N_EXPERTS * FFN_TM
    n_tiles = n_rows // FFN_TM
    tile_start = jnp.arange(n_tiles, dtype=jnp.int32) * FFN_TM
    tile_expert = jnp.minimum(jnp.sum((tile_start[:, None] >= ends[None, :]).astype(jnp.int32), axis=1),
                              N_EXPERTS - 1)
    n_used = (ends[-1:] // FFN_TM).astype(jnp.int32)
    order = jnp.argsort(e_flat, stable=True).astype(jnp.int32)
    row_expert = jnp.repeat(tile_expert, FFN_TM)
    shift = (ends - padded) - (jnp.cumsum(cnt) - cnt)
    q = jnp.arange(n_rows, dtype=jnp.int32) - shift[row_expert]
    src = order[jnp.clip(q, 0, n_slot - 1)] % n_tok
    xs = h2.at[src].get(mode="promise_in_bounds")
    wg, wu = cast_split_bf16(w_gu)
    ys = grouped_swiglu(tile_expert, n_used, xs, wg, wu, cast_rows_bf16(w_down), nf=2,
                        out_dtype=jnp.bfloat16)
    yk = ys.at[dest].get(mode="promise_in_bounds")
    return moe_combine(x, yk, w, gate_mod, final_g)


def kernel(x, c, ctx, c_ctx, mod_w, mod_b, norm1_g, norm2_g, w_in, gla_w_up, gla_b_up,
           s5_lam_re, s5_lam_im, s5_log_dt, s5_b_re, s5_b_im, s5_c_re, s5_c_im, s5_d,
           s5_w_glu, s5_b_glu, hy_w_short, hy_w1, hy_b1, hy_w2, hy_b2, hy_w3, hy_freq,
           hy_bias, rg_w_conv, rg_w_a, rg_b_a, rg_w_x, rg_b_x, rg_lam, w_branch, w_out,
           ffn_w_gu, ffn_w_down, moe_router, moe_router_b, moe_w_gu, moe_w_down, final_g):
    f32, bf16 = jnp.float32, jnp.bfloat16
    n_b, n_lat, _ = x.shape
    n_ctx = ctx.shape[1]
    xs = x.reshape(n_b * n_lat, D_MODEL)
    pos = grid_pos_embed(n_lat, D_MODEL)
    cs = ctx.reshape(n_b * n_ctx, D_MODEL)
    offs = [0]
    for n in IN_SIZES:
        offs.append(offs[-1] + n)
    for l in range(DEPTH):
        last = l == DEPTH - 1
        dense = l % 2 == 0
        j = l // 2
        m_lat = [t[:, None, :] for t in jnp.split(adaln(c, mod_w[l], mod_b[l]), 6, axis=-1)]
        m_ctx = [t[None, None, :] for t in jnp.split(adaln(c_ctx, mod_w[l], mod_b[l]), 6, axis=-1)]
        wl = w_in[l]
        w_mix = jnp.concatenate([wl[:, offs[6]:offs[7]], wl[:, offs[0]:offs[4]], wl[:, offs[5]:offs[6]],
                                 wl[:, offs[7]:offs[9]], wl[:, offs[4]:offs[5]],
                                 jnp.zeros((D_MODEL, P_COLS - P_GDN - 2 * GLA_RANK), f32)], axis=1).astype(bf16)
        w_gate = wl[:, offs[9]:offs[10]].astype(bf16)
        wb, wo, wglu = w_branch[l].astype(bf16), w_out[l].astype(bf16), s5_w_glu[l].astype(bf16)
        s5_ops = s5_prepare(s5_lam_re[l], s5_lam_im[l], s5_log_dt[l], s5_b_re[l], s5_b_im[l],
                            s5_c_re[l], s5_c_im[l], s5_d[l])
        hy_p = (hy_w_short[l], hy_w1[l], hy_b1[l], hy_w2[l], hy_b2[l], hy_w3[l], hy_freq[l])
        rg_p = (rg_w_conv[l], rg_w_a[l], rg_b_a[l], rg_w_x[l], rg_b_x[l], rg_lam[l])
        router = None if dense else (moe_router[j], moe_router_b[j])

        def mixers(p2, n_tok, states, with_hyena):
            p = p2.reshape(n_b, n_tok, P_COLS)
            blk = lambda i, n=1: p[..., i * W_MIX:(i + n) * W_MIX]
            flat = lambda t: t.reshape(n_b * n_tok, W_MIX)
            gla_o, gla_s = gla_mixer(p, gla_w_up[l], gla_b_up[l], states[0])
            s5_y, s5_s = s5_mixer(blk(P_S5), s5_ops, states[1])
            rg_o, rg_s = rglru_mixer(p, *rg_p, states[2])
            hy = tuple(flat(t) for t in hyena_mixer(p, *hy_p)) if with_hyena else None
            return ((flat(gla_o[0]), flat(gla_o[1]), flat(s5_y), hy, flat(rg_o[0]), flat(rg_o[1])),
                    (gla_s, s5_s, rg_s))

        def tail(stream, mods, p2, br, final, pos_add=None):
            outs = merge(stream, tuple(mods[:5]), norm1_g[l], norm2_g[l], br[0], br[1], p2, br[2], br[3],
                         br[4], br[5], w_gate, wb, wo, wglu, s5_b_glu[l], hy_bias[l], router, pos_add)
            if dense:
                return ffn_dense(outs[1], outs[0], mods[5], ffn_w_gu[j], ffn_w_down[j])
            return ffn_moe(outs[1], outs[2][:, :N_EXPERTS], outs[0], mods[5], moe_w_gu[j], moe_w_down[j],
                           final_g if final else None)

        zero_states = (jnp.zeros((2, n_b, GLA_HEADS, GLA_DV, GLA_DK), f32),
                       jnp.zeros((2, n_b, 2 * S5_NS), f32), jnp.zeros((2, n_b, W_MIX), f32))
        p_ctx = front(cs, m_ctx[0], m_ctx[1], norm1_g[l], w_mix)
        br_ctx, states = mixers(p_ctx, n_ctx, zero_states, not last)
        p_lat = front(xs, m_lat[0], m_lat[1], norm1_g[l], w_mix, pos if l == 0 else None)
        br_lat, _ = mixers(p_lat, n_lat, states, True)
        xs = tail(xs, m_lat, p_lat, br_lat, last, pos if l == 0 else None)
        if not last:
            cs = tail(cs, m_ctx, p_ctx, br_ctx, False)
    if (DEPTH - 1) % 2 == 0:
        xs = rmsnorm(xs, final_g)
    return xs.reshape(n_b, n_lat, D_MODEL)
```
